```python
import jax, jax.numpy as jnp
from jax import lax
import numpy as np

D_MODEL = 4096
BATCH = 1
SEQ = 8192
DEPTH = 1

CHUNK = 64

FOX_HEAD_DIM = 128
FOX_WIDTH = D_MODEL // 2
FOX_HEADS = FOX_WIDTH // FOX_HEAD_DIM
Q_BLOCK = 128

POOL_WINDOWS = (2, 4, 8, 16)
POOL_GROUPS = len(POOL_WINDOWS)
POOL_WIDTH = D_MODEL // 2
POOL_GROUP_DIM = POOL_WIDTH // POOL_GROUPS
POOL_OUT_GROUP = D_MODEL // POOL_GROUPS

N_BRANCHES = 2
IN_COLS = 3 * FOX_WIDTH + FOX_HEADS + POOL_WIDTH + N_BRANCHES * D_MODEL

FFN_HIDDEN = -(-(8 * D_MODEL) // (3 * 256)) * 256

ALPHA = (2 * DEPTH) ** 0.25
BETA = (8 * DEPTH) ** -0.25
LN_EPS = 1e-5

kernel_name = "hybrid_fox_pool_deepnorm_adaln_block"


def _layer_norm(x, gain=None, bias=None):
    xf = x.astype(jnp.float32)
    mu = jnp.mean(xf, axis=-1, keepdims=True)
    var = jnp.mean(jnp.square(xf - mu), axis=-1, keepdims=True)
    y = (xf - mu) * lax.rsqrt(var + LN_EPS)
    if gain is not None:
        y = y * gain.astype(jnp.float32) + bias.astype(jnp.float32)
    return y.astype(x.dtype)


def _fox_attention(q, k, v, fcum):
    B, S, H, Dh = q.shape
    nb = S // Q_BLOCK
    scale = Dh ** -0.5
    kh = k.transpose(0, 2, 1, 3)
    vh = v.transpose(0, 2, 1, 3)
    fk = fcum.transpose(0, 2, 1)
    qb = q.reshape(B, nb, Q_BLOCK, H, Dh).transpose(1, 0, 3, 2, 4)
    fq = fk.reshape(B, H, nb, Q_BLOCK).transpose(2, 0, 1, 3)
    k_pos = jnp.arange(S)

    def one_block(args):
        q_blk, fq_blk, blk = args
        q_pos = blk * Q_BLOCK + jnp.arange(Q_BLOCK)
        logits = jnp.einsum('bhqd,bhkd->bhqk', q_blk, kh).astype(jnp.float32) * scale
        logits = logits + (fq_blk[..., :, None] - fk[..., None, :])
        mask = k_pos[None, :] <= q_pos[:, None]
        logits = jnp.where(mask, logits, -jnp.inf)
        probs = jax.nn.softmax(logits, axis=-1)
        return jnp.einsum('bhqk,bhkd->bhqd', probs.astype(vh.dtype), vh)

    out = lax.map(one_block, (qb, fq, jnp.arange(nb)))
    return out.transpose(1, 0, 3, 2, 4).reshape(B, S, H * Dh)


def _multiscale_pool(p, w_pool, pool_scale):
    B, S, _ = p.shape
    pg = p.reshape(B, S, POOL_GROUPS, POOL_GROUP_DIM).astype(jnp.float32)
    csum = jnp.cumsum(pg, axis=1)
    t = jnp.arange(S)
    outs = []
    for g, w in enumerate(POOL_WINDOWS):
        cg = csum[:, :, g]
        lag = jnp.pad(cg, ((0, 0), (w, 0), (0, 0)))[:, :S]
        cnt = jnp.minimum(t + 1, w).astype(jnp.float32)[None, :, None]
        outs.append((cg - lag) / cnt - pg[:, :, g])
    pooled = jnp.stack(outs, axis=2).astype(p.dtype)
    y = jnp.einsum('bsgc,gce->bsge', pooled, w_pool).reshape(B, S, D_MODEL)
    return y * pool_scale


def _token_mixer(u, w_in, b_forget, w_attn_out, w_pool, pool_scale, w_out):
    B, S, _ = u.shape
    z = u @ w_in
    F, H = FOX_WIDTH, FOX_HEADS
    o1 = 3 * F
    o2 = o1 + H
    o3 = o2 + POOL_WIDTH
    o4 = o3 + D_MODEL
    q = z[..., 0:F].reshape(B, S, H, FOX_HEAD_DIM)
    k = z[..., F:2 * F].reshape(B, S, H, FOX_HEAD_DIM)
    v = z[..., 2 * F:o1].reshape(B, S, H, FOX_HEAD_DIM)
    f_logit = z[..., o1:o2].astype(jnp.float32) + b_forget.astype(jnp.float32)
    p = z[..., o2:o3]
    gate_a = jax.nn.sigmoid(z[..., o3:o4])
    gate_p = jax.nn.sigmoid(z[..., o4:])

    fcum = jnp.cumsum(jax.nn.log_sigmoid(f_logit), axis=1)
    y_a = _fox_attention(q, k, v, fcum) @ w_attn_out
    y_p = _multiscale_pool(p, w_pool, pool_scale)
    return (gate_a * y_a + gate_p * y_p) @ w_out


def _swiglu(u, w_gate_up, w_down):
    h = u @ w_gate_up
    g, up = h[..., :FFN_HIDDEN], h[..., FFN_HIDDEN:]
    return (jax.nn.silu(g) * up) @ w_down


def setup_inputs(seed: int = 0) -> dict:
    key = jax.random.key(seed)
    ks = jax.random.split(key, 17)
    L, D = DEPTH, D_MODEL
    nrm = lambda k, shape: jax.random.normal(k, shape, jnp.float32)
    x = nrm(ks[0], (BATCH, SEQ, D))
    c = nrm(ks[1], (BATCH, D))
    w_ada = nrm(ks[2], (L, D, 6 * D)) * (0.5 * D ** -0.5)
    b_ada = nrm(ks[3], (L, 6 * D)) * 0.02
    col_scale = jnp.ones((IN_COLS,), jnp.float32).at[2 * FOX_WIDTH:3 * FOX_WIDTH].set(BETA)
    w_in = nrm(ks[4], (L, D, IN_COLS)) * (D ** -0.5) * col_scale
    b_forget = (jnp.linspace(1.0, 6.0, FOX_HEADS, dtype=jnp.float32)[None, :]
                + 0.1 * nrm(ks[5], (L, FOX_HEADS)))
    w_attn_out = nrm(ks[6], (L, FOX_WIDTH, D)) * FOX_WIDTH ** -0.5
    w_pool = nrm(ks[7], (L, POOL_GROUPS, POOL_GROUP_DIM, POOL_OUT_GROUP)) * POOL_GROUP_DIM ** -0.5
    pool_scale = 1.0 + 0.1 * nrm(ks[8], (L, D))
    w_out = nrm(ks[9], (L, D, D)) * (D ** -0.5) * BETA
    ln1_g = 1.0 + 0.02 * nrm(ks[10], (L, D))
    ln1_b = 0.02 * nrm(ks[11], (L, D))
    w_gate_up = nrm(ks[12], (L, D, 2 * FFN_HIDDEN)) * D ** -0.5
    w_down = nrm(ks[13], (L, FFN_HIDDEN, D)) * (FFN_HIDDEN ** -0.5) * BETA
    ln2_g = 1.0 + 0.02 * nrm(ks[14], (L, D))
    ln2_b = 0.02 * nrm(ks[15], (L, D))
    return {"x": x, "c": c, "w_ada": w_ada, "b_ada": b_ada, "w_in": w_in,
            "b_forget": b_forget, "w_attn_out": w_attn_out, "w_pool": w_pool,
            "pool_scale": pool_scale, "w_out": w_out, "ln1_g": ln1_g, "ln1_b": ln1_b,
            "w_gate_up": w_gate_up, "w_down": w_down, "ln2_g": ln2_g, "ln2_b": ln2_b}


def reference(x, c, w_ada, b_ada, w_in, b_forget, w_attn_out, w_pool, pool_scale,
              w_out, ln1_g, ln1_b, w_gate_up, w_down, ln2_g, ln2_b):
    D = D_MODEL
    for l in range(DEPTH):
        mod = (jax.nn.silu(c) @ w_ada[l] + b_ada[l])[:, None, :]
        sh1, sc1, g1 = mod[..., 0:D], mod[..., D:2 * D], mod[..., 2 * D:3 * D]
        sh2, sc2, g2 = mod[..., 3 * D:4 * D], mod[..., 4 * D:5 * D], mod[..., 5 * D:]
        u = _layer_norm(x) * (1.0 + sc1) + sh1
        m = _token_mixer(u, w_in[l], b_forget[l], w_attn_out[l], w_pool[l],
                         pool_scale[l], w_out[l])
        x = _layer_norm(ALPHA * x + g1 * m, ln1_g[l], ln1_b[l])
        u = _layer_norm(x) * (1.0 + sc2) + sh2
        f = _swiglu(u, w_gate_up[l], w_down[l])
        x = _layer_norm(ALPHA * x + g2 * f, ln2_g[l], ln2_b[l])
    return x
```

```python
import functools

import jax
import jax.numpy as jnp
from jax import lax
from jax.experimental import pallas as pl
from jax.experimental.pallas import tpu as pltpu

F32 = jnp.float32
BF16 = jnp.bfloat16

FOX_HEAD_DIM = 128
POOL_WINDOWS = (2, 4, 8, 16)
POOL_HALO = 16
DEPTH = 1
ALPHA = (2 * DEPTH) ** 0.25
LN_EPS = 1e-5

V7X_VMEM_BYTES = 64 * 1024 * 1024
VMEM_CAP_BYTES = V7X_VMEM_BYTES - 6 * 1024 * 1024

_NT = (((1,), (1,)), ((), ()))


def _nbytes(shape, dtype):
    n = jnp.dtype(dtype).itemsize
    for s in shape:
        n *= s
    return n


def _params(semantics, pipelined_bytes, resident_bytes=0):
    need = 2 * pipelined_bytes + resident_bytes
    assert need <= VMEM_CAP_BYTES, (need, VMEM_CAP_BYTES)
    return pltpu.CompilerParams(dimension_semantics=semantics,
                                vmem_limit_bytes=min(need + (4 << 20), VMEM_CAP_BYTES))


def _ln_rows(x):
    mu = jnp.mean(x, axis=-1, keepdims=True)
    xc = x - mu
    var = jnp.mean(xc * xc, axis=-1, keepdims=True)
    return xc * lax.rsqrt(var + LN_EPS)


def _ada_kernel(c_ref, w_ref, b_ref, o_ref):
    k = pl.program_id(1)

    @pl.when(k == 0)
    def _():
        o_ref[...] = b_ref[...]

    cc = c_ref[...]
    s = cc * jax.nn.sigmoid(cc)
    o_ref[...] += jnp.sum(s * w_ref[...], axis=0, keepdims=True)


def _ada(c_col, w_ada, b_ada, *, tk=512, tn=4096):
    d, n = w_ada.shape
    return pl.pallas_call(
        _ada_kernel,
        out_shape=jax.ShapeDtypeStruct((1, n), F32),
        grid=(n // tn, d // tk),
        in_specs=[pl.BlockSpec((tk, 1), lambda j, k: (k, 0)),
                  pl.BlockSpec((tk, tn), lambda j, k: (k, j)),
                  pl.BlockSpec((1, tn), lambda j, k: (0, j))],
        out_specs=pl.BlockSpec((1, tn), lambda j, k: (0, j)),
        compiler_params=_params(("parallel", "arbitrary"),
                                _nbytes((tk, tn), F32) + _nbytes((tk, 128), F32),
                                _nbytes((tk, tn), F32)),
        name="ada",
    )(c_col, w_ada, b_ada)


def _ln_mod_kernel(x_ref, sh_ref, sc_ref, o_ref):
    y = _ln_rows(x_ref[...])
    o_ref[...] = (y * (1.0 + sc_ref[...]) + sh_ref[...]).astype(o_ref.dtype)


def _ln_mod(x, mod, shift_idx, scale_idx, *, tm=256):
    s, d = x.shape
    return pl.pallas_call(
        _ln_mod_kernel,
        out_shape=jax.ShapeDtypeStruct((s, d), BF16),
        grid=(s // tm,),
        in_specs=[pl.BlockSpec((tm, d), lambda i: (i, 0)),
                  pl.BlockSpec((1, d), lambda i: (0, shift_idx)),
                  pl.BlockSpec((1, d), lambda i: (0, scale_idx))],
        out_specs=pl.BlockSpec((tm, d), lambda i: (i, 0)),
        compiler_params=_params(("parallel",),
                                _nbytes((tm, d), F32) + _nbytes((tm, d), BF16),
                                3 * _nbytes((tm, d), F32)),
        name="ln_mod",
    )(x, mod, mod)


def _mm_kernel(a_ref, w_ref, o_ref, *, scale, sigmoid):
    acc = jnp.dot(a_ref[...], w_ref[...], preferred_element_type=F32)
    if scale is not None:
        acc = acc * scale
    if sigmoid:
        acc = jax.nn.sigmoid(acc)
    o_ref[...] = acc.astype(o_ref.dtype)


def _matmul(a, w, out_dtype, *, tm, tn, scale=None, sigmoid=False, name):
    m, k = a.shape
    _, n = w.shape
    return pl.pallas_call(
        functools.partial(_mm_kernel, scale=scale, sigmoid=sigmoid),
        out_shape=jax.ShapeDtypeStruct((m, n), out_dtype),
        grid=(m // tm, n // tn),
        in_specs=[pl.BlockSpec((tm, k), lambda i, j: (i, 0)),
                  pl.BlockSpec((k, tn), lambda i, j: (0, j))],
        out_specs=pl.BlockSpec((tm, tn), lambda i, j: (i, j)),
        compiler_params=_params(("parallel", "parallel"),
                                _nbytes((tm, k), a.dtype) + _nbytes((k, tn), w.dtype)
                                + _nbytes((tm, tn), out_dtype),
                                2 * _nbytes((tm, tn), F32)),
        name=name,
    )(a, w)


def _fcum_kernel(wh_ref, wl_ref, b_ref, u_ref, o_ref, carry_ref, *, tm):
    @pl.when(pl.program_id(0) == 0)
    def _():
        carry_ref[...] = jnp.zeros_like(carry_ref)

    u = u_ref[...]
    f = (lax.dot_general(wh_ref[...], u, _NT, preferred_element_type=F32)
         + lax.dot_general(wl_ref[...], u, _NT, preferred_element_type=F32)
         + b_ref[...])
    ls = jnp.minimum(f, 0.0) - jnp.log1p(jnp.exp(-jnp.abs(f)))
    row = lax.broadcasted_iota(jnp.int32, (tm, tm), 0)
    col = lax.broadcasted_iota(jnp.int32, (tm, tm), 1)
    tri = jnp.where(row <= col, 1.0, 0.0).astype(BF16)
    hi = ls.astype(BF16)
    r1 = ls - hi.astype(F32)
    mid = r1.astype(BF16)
    lo = (r1 - mid.astype(F32)).astype(BF16)
    cs = (jnp.dot(hi, tri, preferred_element_type=F32)
          + jnp.dot(mid, tri, preferred_element_type=F32)
          + jnp.dot(lo, tri, preferred_element_type=F32))
    out = cs + carry_ref[...]
    o_ref[...] = out
    carry_ref[...] = out[:, tm - 1:tm]


def _fcum(wf_hi, wf_lo, b_col, u, *, tm=512):
    s, d = u.shape
    h = wf_hi.shape[0]
    return pl.pallas_call(
        functools.partial(_fcum_kernel, tm=tm),
        out_shape=jax.ShapeDtypeStruct((h, s), F32),
        grid=(s // tm,),
        in_specs=[pl.BlockSpec((h, d), lambda i: (0, 0)),
                  pl.BlockSpec((h, d), lambda i: (0, 0)),
                  pl.BlockSpec((h, 1), lambda i: (0, 0)),
                  pl.BlockSpec((tm, d), lambda i: (i, 0))],
        out_specs=pl.BlockSpec((h, tm), lambda i: (0, i)),
        scratch_shapes=[pltpu.VMEM((h, 1), F32)],
        compiler_params=_params(("arbitrary",),
                                _nbytes((tm, d), BF16) + 2 * _nbytes((h, d), BF16),
                                4 * _nbytes((tm, tm), F32)),
        name="fcum",
    )(wf_hi, wf_lo, b_col, u)


def _attn_kernel(q_ref, k_ref, v_ref, f_ref, o_ref, *, tq):
    i = pl.program_id(1)
    q = q_ref[...]
    r = f_ref[pl.ds(i, 1), :][:, 0:1]

    def block(j):
        rows = pl.ds(pl.multiple_of(j * tq, tq), tq)
        s = lax.dot_general(q, k_ref[rows, :], _NT, preferred_element_type=F32)
        return s + (r - f_ref[pl.ds(j, 1), :]), v_ref[rows, :]

    def update(s, v, carry):
        m, l, acc = carry
        m_new = jnp.maximum(m, jnp.max(s, axis=-1, keepdims=True))
        p = jnp.exp(s - m_new)
        a = jnp.exp(m - m_new)
        l = a * l + jnp.sum(p, axis=-1, keepdims=True)
        acc = a * acc + jnp.dot(p.astype(BF16), v, preferred_element_type=F32)
        return m_new, l, acc

    def body(j, carry):
        s, v = block(j)
        return update(s, v, carry)

    init = (jnp.full((tq, 1), -jnp.inf, F32), jnp.zeros((tq, 1), F32),
            jnp.zeros((tq, q.shape[1]), F32))
    carry = lax.fori_loop(0, i, body, init)
    s, v = block(i)
    row = lax.broadcasted_iota(jnp.int32, (tq, tq), 0)
    col = lax.broadcasted_iota(jnp.int32, (tq, tq), 1)
    s = jnp.where(col <= row, s, -jnp.inf)
    _, l, acc = update(s, v, carry)
    o_ref[...] = (acc / l).astype(o_ref.dtype)


def _attention(q, kv, fcum_blocks, *, tq):
    s, width = q.shape
    dh = FOX_HEAD_DIM
    h = width // dh
    nq = s // tq
    return pl.pallas_call(
        functools.partial(_attn_kernel, tq=tq),
        out_shape=jax.ShapeDtypeStruct((s, width), BF16),
        grid=(h, nq),
        in_specs=[pl.BlockSpec((tq, dh), lambda hh, i: (i, hh)),
                  pl.BlockSpec((s, dh), lambda hh, i: (0, hh)),
                  pl.BlockSpec((s, dh), lambda hh, i: (0, h + hh)),
                  pl.BlockSpec((None, nq, tq), lambda hh, i: (hh, 0, 0))],
        out_specs=pl.BlockSpec((tq, dh), lambda hh, i: (i, hh)),
        compiler_params=_params(("parallel", "arbitrary"),
                                2 * _nbytes((s, dh), BF16) + 2 * _nbytes((tq, dh), BF16)
                                + _nbytes((nq, tq), F32),
                                6 * _nbytes((tq, tq), F32)),
        name="fox_attention",
    )(q, kv, kv, fcum_blocks)


def _pool_kernel(halo_ref, p_ref, o_ref, *, tm, group_dim):
    i = pl.program_id(0)
    halo = jnp.where(i == 0, 0.0, halo_ref[...])
    ext = jnp.concatenate([halo, p_ref[...]], axis=0)
    t = i * tm + lax.broadcasted_iota(jnp.int32, (tm, 1), 0)
    for g, w in enumerate(POOL_WINDOWS):
        cols = slice(g * group_dim, (g + 1) * group_dim)
        e = ext[:, cols]
        acc = e
        d = 1
        while d < w:
            acc = acc + pltpu.roll(acc, d, 0)
            d *= 2
        cnt = jnp.minimum(t + 1, w).astype(F32)
        o_ref[:, cols] = (acc[POOL_HALO:] / cnt - e[POOL_HALO:]).astype(o_ref.dtype)


def _pool(p, *, tm=512):
    s, width = p.shape
    group_dim = width // len(POOL_WINDOWS)
    hb = tm // POOL_HALO
    return pl.pallas_call(
        functools.partial(_pool_kernel, tm=tm, group_dim=group_dim),
        out_shape=jax.ShapeDtypeStruct((s, width), BF16),
        grid=(s // tm,),
        in_specs=[pl.BlockSpec((POOL_HALO, width), lambda i: (jnp.maximum(i * hb - 1, 0), 0)),
                  pl.BlockSpec((tm, width), lambda i: (i, 0))],
        out_specs=pl.BlockSpec((tm, width), lambda i: (i, 0)),
        compiler_params=_params(("parallel",),
                                _nbytes((tm + POOL_HALO, width), F32) + _nbytes((tm, width), BF16),
                                3 * _nbytes((tm + POOL_HALO, width), F32)),
        name="pool",
    )(p, p)


def _mix_kernel(a_ref, wa_ref, p_ref, wp_ref, ps_ref, ga_ref, gp_ref, o_ref):
    ya = jnp.dot(a_ref[...], wa_ref[...], preferred_element_type=F32)
    yp = jnp.dot(p_ref[...], wp_ref[...], preferred_element_type=F32) * ps_ref[...]
    o_ref[...] = (ga_ref[...] * ya + gp_ref[...] * yp).astype(o_ref.dtype)


def _mix(attn, w_a, pooled, w_pool, pool_scale, gates, *, tm=512):
    s, fw = attn.shape
    groups, gd, tn = w_pool.shape
    d = w_a.shape[1]
    return pl.pallas_call(
        _mix_kernel,
        out_shape=jax.ShapeDtypeStruct((s, d), BF16),
        grid=(s // tm, groups),
        in_specs=[pl.BlockSpec((tm, fw), lambda i, j: (i, 0)),
                  pl.BlockSpec((fw, tn), lambda i, j: (0, j)),
                  pl.BlockSpec((tm, gd), lambda i, j: (i, j)),
                  pl.BlockSpec((None, gd, tn), lambda i, j: (j, 0, 0)),
                  pl.BlockSpec((1, tn), lambda i, j: (0, j)),
                  pl.BlockSpec((tm, tn), lambda i, j: (i, j)),
                  pl.BlockSpec((tm, tn), lambda i, j: (i, groups + j))],
        out_specs=pl.BlockSpec((tm, tn), lambda i, j: (i, j)),
        compiler_params=_params(("parallel", "parallel"),
                                _nbytes((tm, fw), BF16) + _nbytes((fw, tn), BF16)
                                + _nbytes((tm, gd), BF16) + _nbytes((gd, tn), BF16)
                                + 2 * _nbytes((tm, tn), F32) + _nbytes((tm, tn), BF16),
                                3 * _nbytes((tm, tn), F32)),
        name="branch_mix",
    )(attn, w_a, pooled, w_pool, pool_scale, gates, gates)


def _resid_mm_kernel(a_ref, w_ref, x_ref, g_ref, o_ref):
    m = jnp.dot(a_ref[...], w_ref[...], preferred_element_type=F32)
    o_ref[...] = ALPHA * x_ref[...] + g_ref[...] * m


def _resid_matmul(a, w, x, mod, gate_idx, *, tm, tn, name):
    m, k = a.shape
    n = w.shape[1]
    nb = n // tn
    return pl.pallas_call(
        _resid_mm_kernel,
        out_shape=jax.ShapeDtypeStruct((m, n), F32),
        grid=(m // tm, nb),
        in_specs=[pl.BlockSpec((tm, k), lambda i, j: (i, 0)),
                  pl.BlockSpec((k, tn), lambda i, j: (0, j)),
                  pl.BlockSpec((tm, tn), lambda i, j: (i, j)),
                  pl.BlockSpec((1, tn), lambda i, j: (0, gate_idx * nb + j))],
        out_specs=pl.BlockSpec((tm, tn), lambda i, j: (i, j)),
        compiler_params=_params(("parallel", "parallel"),
                                _nbytes((tm, k), BF16) + _nbytes((k, tn), BF16)
                                + 2 * _nbytes((tm, tn), F32),
                                2 * _nbytes((tm, tn), F32)),
        name=name,
    )(a, w, x, mod)


def _ln_ln_mod_kernel(r_ref, g_ref, b_ref, sh_ref, sc_ref, x_ref, u_ref):
    x1 = _ln_rows(r_ref[...]) * g_ref[...] + b_ref[...]
    x_ref[...] = x1
    u_ref[...] = (_ln_rows(x1) * (1.0 + sc_ref[...]) + sh_ref[...]).astype(u_ref.dtype)


def _ln_ln_mod(r, gain, bias, mod, shift_idx, scale_idx, *, tm=256):
    s, d = r.shape
    row = pl.BlockSpec((tm, d), lambda i: (i, 0))
    vec = pl.BlockSpec((1, d), lambda i: (0, 0))
    return pl.pallas_call(
        _ln_ln_mod_kernel,
        out_shape=(jax.ShapeDtypeStruct((s, d), F32), jax.ShapeDtypeStruct((s, d), BF16)),
        grid=(s // tm,),
        in_specs=[row, vec, vec,
                  pl.BlockSpec((1, d), lambda i: (0, shift_idx)),
                  pl.BlockSpec((1, d), lambda i: (0, scale_idx))],
        out_specs=(row, row),
        compiler_params=_params(("parallel",),
                                2 * _nbytes((tm, d), F32) + _nbytes((tm, d), BF16),
                                4 * _nbytes((tm, d), F32)),
        name="ln1_ln_mod",
    )(r, gain, bias, mod, mod)


def _ln_affine_kernel(r_ref, g_ref, b_ref, o_ref):
    o_ref[...] = _ln_rows(r_ref[...]) * g_ref[...] + b_ref[...]


def _ln_affine(r, gain, bias, *, tm=256):
    s, d = r.shape
    row = pl.BlockSpec((tm, d), lambda i: (i, 0))
    vec = pl.BlockSpec((1, d), lambda i: (0, 0))
    return pl.pallas_call(
        _ln_affine_kernel,
        out_shape=jax.ShapeDtypeStruct((s, d), F32),
        grid=(s // tm,),
        in_specs=[row, vec, vec],
        out_specs=row,
        compiler_params=_params(("parallel",), 2 * _nbytes((tm, d), F32),
                                3 * _nbytes((tm, d), F32)),
        name="ln2",
    )(r, gain, bias)


def _ffn_up_kernel(u_ref, wg_ref, wu_ref, o_ref):
    u = u_ref[...]
    g = jnp.dot(u, wg_ref[...], preferred_element_type=F32)
    up = jnp.dot(u, wu_ref[...], preferred_element_type=F32)
    o_ref[...] = (g * jax.nn.sigmoid(g) * up).astype(o_ref.dtype)


def _ffn_up(u, w_gate_up, *, tm=1024, tn=256):
    s, d = u.shape
    hidden = w_gate_up.shape[1] // 2
    nb = hidden // tn
    return pl.pallas_call(
        _ffn_up_kernel,
        out_shape=jax.ShapeDtypeStruct((s, hidden), BF16),
        grid=(s // tm, nb),
        in_specs=[pl.BlockSpec((tm, d), lambda i, j: (i, 0)),
                  pl.BlockSpec((d, tn), lambda i, j: (0, j)),
                  pl.BlockSpec((d, tn), lambda i, j: (0, nb + j))],
        out_specs=pl.BlockSpec((tm, tn), lambda i, j: (i, j)),
        compiler_params=_params(("parallel", "parallel"),
                                _nbytes((tm, d), BF16) + 2 * _nbytes((d, tn), BF16)
                                + _nbytes((tm, tn), BF16),
                                4 * _nbytes((tm, tn), F32)),
        name="ffn_up",
    )(u, w_gate_up, w_gate_up)


def kernel(x, c, w_ada, b_ada, w_in, b_forget, w_attn_out, w_pool, pool_scale, w_out,
           ln1_g, ln1_b, w_gate_up, w_down, ln2_g, ln2_b):
    batch, seq, d = x.shape
    assert batch == 1 and w_ada.shape[0] == DEPTH == 1
    fox_w = w_attn_out.shape[1]
    heads = fox_w // FOX_HEAD_DIM
    pool_w = w_pool.shape[1] * w_pool.shape[2]
    o_f = 3 * fox_w
    o_p = o_f + heads
    o_g = o_p + pool_w

    xs = x[0]
    w_in0 = w_in[0]
    w_q = w_in0[:, :fox_w].astype(BF16)
    w_kv = w_in0[:, fox_w:o_f].astype(BF16)
    w_f = w_in0[:, o_f:o_p].T
    w_f_hi = w_f.astype(BF16)
    w_f_lo = (w_f - w_f_hi.astype(F32)).astype(BF16)
    w_p = w_in0[:, o_p:o_g].astype(BF16)
    w_g = w_in0[:, o_g:].astype(BF16)

    mod = _ada(c.reshape(d, 1), w_ada[0], b_ada)

    u1 = _ln_mod(xs, mod, 0, 1)
    q = _matmul(u1, w_q, BF16, tm=1024, tn=1024, scale=FOX_HEAD_DIM ** -0.5, name="proj_q")
    kv = _matmul(u1, w_kv, BF16, tm=1024, tn=1024, name="proj_kv")
    p = _matmul(u1, w_p, F32, tm=1024, tn=1024, name="proj_pool")
    gates = _matmul(u1, w_g, F32, tm=1024, tn=1024, sigmoid=True, name="proj_gates")
    fcum = _fcum(w_f_hi, w_f_lo, b_forget[0].reshape(heads, 1), u1)
    tq = 512
    attn = _attention(q, kv, fcum.reshape(heads, seq // tq, tq), tq=tq)
    pooled = _pool(p)
    mix = _mix(attn, w_attn_out[0].astype(BF16), pooled, w_pool[0].astype(BF16),
               pool_scale, gates)
    r1 = _resid_matmul(mix, w_out[0].astype(BF16), xs, mod, 2, tm=512, tn=1024, name="out_proj")
    x1, u2 = _ln_ln_mod(r1, ln1_g, ln1_b, mod, 3, 4)

    act = _ffn_up(u2, w_gate_up[0].astype(BF16))
    r2 = _resid_matmul(act, w_down[0].astype(BF16), x1, mod, 5, tm=512, tn=512, name="ffn_down")
    out = _ln_affine(r2, ln2_g, ln2_b)
    return out[None]
```

```python
import functools

import jax
import jax.numpy as jnp
from jax import lax
from jax.experimental import pallas as pl
from jax.experimental.pallas import tpu as pltpu

F32 = jnp.float32
BF16 = jnp.bfloat16

FOX_HEAD_DIM = 128
POOL_WINDOWS = (2, 4, 8, 16)
POOL_HALO = 16
DEPTH = 1
ALPHA = (2 * DEPTH) ** 0.25
LN_EPS = 1e-5
LOG2E = 1.4426950408889634
LANES = 128
BIAS_PIECES = 3

V7X_VMEM_BYTES = 64 * 1024 * 1024
VMEM_CAP_BYTES = V7X_VMEM_BYTES - 6 * 1024 * 1024

_NT = (((1,), (1,)), ((), ()))


def _nbytes(shape, dtype):
    n = jnp.dtype(dtype).itemsize
    for s in shape:
        n *= s
    return n


def _cast_bytes(shape, dtype):
    return 0 if dtype == BF16 else _nbytes(shape, BF16)


def _params(semantics, pipelined_bytes, resident_bytes=0):
    need = 2 * pipelined_bytes + resident_bytes
    assert need <= VMEM_CAP_BYTES, (need, VMEM_CAP_BYTES)
    return pltpu.CompilerParams(dimension_semantics=semantics,
                                vmem_limit_bytes=min(need + (4 << 20), VMEM_CAP_BYTES))


def _ln_rows(x):
    mu = jnp.mean(x, axis=-1, keepdims=True)
    xc = x - mu
    var = jnp.mean(xc * xc, axis=-1, keepdims=True)
    return xc * lax.rsqrt(var + LN_EPS)


def _as_bf16(w):
    return w if w.dtype == BF16 else w.astype(BF16)


def _split3(v):
    hi = v.astype(BF16)
    r1 = v - hi.astype(F32)
    mid = r1.astype(BF16)
    lo = (r1 - mid.astype(F32)).astype(BF16)
    return hi, mid, lo


def _ada_kernel(c_ref, w_ref, b_ref, o_ref):
    k = pl.program_id(1)

    @pl.when(k == 0)
    def _():
        o_ref[...] = b_ref[...]

    cc = c_ref[...]
    s = cc * jax.nn.sigmoid(cc)
    o_ref[...] += jnp.sum(s * w_ref[...], axis=0, keepdims=True)


def _ada(c_col, w_ada, b_ada, *, tk=512, tn=4096):
    d, n = w_ada.shape
    return pl.pallas_call(
        _ada_kernel,
        out_shape=jax.ShapeDtypeStruct((1, n), F32),
        grid=(n // tn, d // tk),
        in_specs=[pl.BlockSpec((tk, 1), lambda j, k: (k, 0)),
                  pl.BlockSpec((tk, tn), lambda j, k: (k, j)),
                  pl.BlockSpec((1, tn), lambda j, k: (0, j))],
        out_specs=pl.BlockSpec((1, tn), lambda j, k: (0, j)),
        compiler_params=_params(("parallel", "arbitrary"),
                                _nbytes((tk, tn), F32) + _nbytes((tk, LANES), F32),
                                _nbytes((tk, tn), F32)),
        name="ada",
    )(c_col, w_ada, b_ada)


def _ln_mod_kernel(x_ref, sh_ref, sc_ref, o_ref):
    y = _ln_rows(x_ref[...])
    o_ref[...] = (y * (1.0 + sc_ref[...]) + sh_ref[...]).astype(o_ref.dtype)


def _ln_mod(x, mod, shift_idx, scale_idx, *, tm=256):
    s, d = x.shape
    return pl.pallas_call(
        _ln_mod_kernel,
        out_shape=jax.ShapeDtypeStruct((s, d), BF16),
        grid=(s // tm,),
        in_specs=[pl.BlockSpec((tm, d), lambda i: (i, 0)),
                  pl.BlockSpec((1, d), lambda i: (0, shift_idx)),
                  pl.BlockSpec((1, d), lambda i: (0, scale_idx))],
        out_specs=pl.BlockSpec((tm, d), lambda i: (i, 0)),
        compiler_params=_params(("parallel",),
                                _nbytes((tm, d), F32) + _nbytes((tm, d), BF16),
                                3 * _nbytes((tm, d), F32)),
        name="ln_mod",
    )(x, mod, mod)


def _mm_kernel(a_ref, w_ref, o_ref, *, scale, n_scaled, sigmoid):
    acc = jnp.dot(a_ref[...], _as_bf16(w_ref[...]), preferred_element_type=F32)
    if scale is not None:
        acc = acc * jnp.where(pl.program_id(1) < n_scaled, scale, 1.0)
    if sigmoid:
        acc = jax.nn.sigmoid(acc)
    o_ref[...] = acc.astype(o_ref.dtype)


def _matmul(a, w, n, out_dtype, *, tm, tn, col_block=0, scale=None, n_scaled=0,
            sigmoid=False, name):
    m, k = a.shape
    return pl.pallas_call(
        functools.partial(_mm_kernel, scale=scale, n_scaled=n_scaled, sigmoid=sigmoid),
        out_shape=jax.ShapeDtypeStruct((m, n), out_dtype),
        grid=(m // tm, n // tn),
        in_specs=[pl.BlockSpec((tm, k), lambda i, j: (i, 0)),
                  pl.BlockSpec((k, tn), lambda i, j: (0, col_block + j))],
        out_specs=pl.BlockSpec((tm, tn), lambda i, j: (i, j)),
        compiler_params=_params(("parallel", "parallel"),
                                _nbytes((tm, k), a.dtype) + _nbytes((k, tn), w.dtype)
                                + _nbytes((tm, tn), out_dtype),
                                _cast_bytes((k, tn), w.dtype) + 2 * _nbytes((tm, tn), F32)),
        name=name,
    )(a, w)


def _fcum_kernel(u_ref, wh_ref, wl_ref, b_ref, e_ref, c_ref, carry_ref, *, tm, heads):
    @pl.when(pl.program_id(0) == 0)
    def _():
        carry_ref[...] = jnp.zeros_like(carry_ref)

    u = u_ref[...]
    f = (jnp.dot(u, wh_ref[...], preferred_element_type=F32)
         + jnp.dot(u, wl_ref[...], preferred_element_type=F32) + b_ref[...])
    ls = (jnp.minimum(f, 0.0) - jnp.log1p(jnp.exp(-jnp.abs(f)))) * LOG2E
    row = lax.broadcasted_iota(jnp.int32, (tm, tm), 0)
    col = lax.broadcasted_iota(jnp.int32, (tm, tm), 1)
    tri = jnp.where(col <= row, 1.0, 0.0).astype(BF16)
    cs = sum(jnp.dot(tri, piece, preferred_element_type=F32) for piece in _split3(ls))
    carry = carry_ref[...]
    c_ref[...] = carry
    carry_ref[...] = carry + cs[tm - 1:tm, :]
    r = lax.broadcasted_iota(jnp.int32, (LANES, heads * LANES), 0)
    c = lax.broadcasted_iota(jnp.int32, (LANES, heads * LANES), 1)
    e = sum(jnp.dot(piece, jnp.where((c == r * LANES + p) & (r < heads), 1.0, 0.0).astype(BF16),
                    preferred_element_type=F32)
            for p, piece in enumerate(_split3(-cs)))
    e_ref[...] = e.astype(e_ref.dtype)


def _fcum(u, wf_hi, wf_lo, b_row, *, heads, tm):
    s, d = u.shape
    nb = s // tm
    return pl.pallas_call(
        functools.partial(_fcum_kernel, tm=tm, heads=heads),
        out_shape=(jax.ShapeDtypeStruct((s, heads * LANES), BF16),
                   jax.ShapeDtypeStruct((nb, 1, LANES), F32)),
        grid=(nb,),
        in_specs=[pl.BlockSpec((tm, d), lambda i: (i, 0)),
                  pl.BlockSpec((d, LANES), lambda i: (0, 0)),
                  pl.BlockSpec((d, LANES), lambda i: (0, 0)),
                  pl.BlockSpec((1, LANES), lambda i: (0, 0))],
        out_specs=(pl.BlockSpec((tm, heads * LANES), lambda i: (i, 0)),
                   pl.BlockSpec((None, 1, LANES), lambda i: (i, 0, 0))),
        scratch_shapes=[pltpu.VMEM((1, LANES), F32)],
        compiler_params=_params(("arbitrary",),
                                _nbytes((tm, d), BF16) + 2 * _nbytes((d, LANES), BF16)
                                + _nbytes((tm, heads * LANES), BF16),
                                2 * _nbytes((tm, tm), F32) + 2 * _nbytes((tm, heads * LANES), F32)),
        name="fcum",
    )(u, wf_hi, wf_lo, b_row)


def _attn_kernel(cb_ref, q_ref, k_ref, e_ref, vt_ref, o_ref, s0_ref, s1_ref, acc_ref, *, tk, tn):
    h = pl.program_id(0)
    i = pl.program_id(1)
    tq, dh = q_ref.shape
    nt = tq // tn
    lane = lax.broadcasted_iota(jnp.int32, (tq, dh), 1)
    ones_cols = jnp.where(lane < BIAS_PIECES, 1.0, 0.0).astype(BF16)
    q_aug = jnp.concatenate([q_ref[...], ones_cols], axis=1)
    c_q = cb_ref[h, 2 * i]
    acc_ref[...] = jnp.zeros_like(acc_ref)

    def logits_to(buf_ref, j, first_tile=0):
        rows = pl.ds(pl.multiple_of(j * tk, tk), tk)
        k_aug = jnp.concatenate([k_ref[rows, :], e_ref[rows, :]], axis=1)
        cms = []
        for n in range(first_tile, nt):
            cols = slice(n * tn, (n + 1) * tn)
            s = lax.dot_general(k_aug, q_aug[cols, :], _NT, preferred_element_type=F32)
            buf_ref[:, cols] = s
            cms.append(jnp.max(s, axis=0, keepdims=True))
        return jnp.concatenate(cms, axis=1)

    def masked(buf_ref, n, first_tile):
        key = lax.broadcasted_iota(jnp.int32, (tk, tn), 0)
        qry = lax.broadcasted_iota(jnp.int32, (tk, tn), 1) + (n - first_tile) * tn
        return jnp.where(key <= qry, buf_ref[:, n * tn:(n + 1) * tn], -jnp.inf)

    def softmax_pv(buf_ref, j, cmax, m, l, first_tile=0, diagonal=False):
        lo = first_tile * tn
        c = c_q - cb_ref[h, j]
        m_old = m[:, lo:]
        m_new = jnp.maximum(m_old, cmax + c)
        a = jnp.exp2(m_old - m_new)
        off = m_new - c
        sums = []
        for n in range(first_tile, nt):
            cols = slice(n * tn, (n + 1) * tn)
            rel = slice(n * tn - lo, (n + 1) * tn - lo)
            s = masked(buf_ref, n, first_tile) if diagonal else buf_ref[:, cols]
            p = jnp.exp2(s - off[:, rel])
            sums.append(jnp.sum(p, axis=0, keepdims=True))
            acc_ref[:, cols] = a[:, rel] * acc_ref[:, cols] + jnp.dot(
                vt_ref[j], p.astype(BF16), preferred_element_type=F32)
        l_new = a * l[:, lo:] + jnp.concatenate(sums, axis=1)
        if first_tile:
            m_new = jnp.concatenate([m[:, :lo], m_new], axis=1)
            l_new = jnp.concatenate([l[:, :lo], l_new], axis=1)
        return m_new, l_new

    def pair(p, carry):
        m, l, cm0 = carry
        cm1 = logits_to(s1_ref, 2 * p + 1)
        m, l = softmax_pv(s0_ref, 2 * p, cm0, m, l)
        cm0 = logits_to(s0_ref, 2 * p + 2)
        m, l = softmax_pv(s1_ref, 2 * p + 1, cm1, m, l)
        return m, l, cm0

    init = (jnp.full((1, tq), -jnp.inf, F32), jnp.zeros((1, tq), F32), logits_to(s0_ref, 0))
    m, l, _ = lax.fori_loop(0, i, pair, init)
    half = nt // 2
    logits_to(s1_ref, 2 * i + 1, first_tile=half)
    cm = jnp.concatenate([jnp.max(masked(s0_ref, n, 0), axis=0, keepdims=True) for n in range(nt)],
                         axis=1)
    m, l = softmax_pv(s0_ref, 2 * i, cm, m, l, diagonal=True)
    cm = jnp.concatenate([jnp.max(masked(s1_ref, n, half), axis=0, keepdims=True)
                          for n in range(half, nt)], axis=1)
    m, l = softmax_pv(s1_ref, 2 * i + 1, cm, m, l, first_tile=half, diagonal=True)
    o_ref[...] = (acc_ref[...] / l).T.astype(o_ref.dtype)


def _attention(cb, qkv, e, vt, *, heads, tk, tn=256):
    s = qkv.shape[0]
    dh = FOX_HEAD_DIM
    tq = 2 * tk
    nq = s // tq
    grid_spec = pltpu.PrefetchScalarGridSpec(
        num_scalar_prefetch=1,
        grid=(heads, nq),
        in_specs=[pl.BlockSpec((tq, dh), lambda h, i, cb_ref: (i, h)),
                  pl.BlockSpec((s, dh), lambda h, i, cb_ref: (0, heads + h)),
                  pl.BlockSpec((s, dh), lambda h, i, cb_ref: (0, h)),
                  pl.BlockSpec((None, s // tk, dh, tk), lambda h, i, cb_ref: (h, 0, 0, 0))],
        out_specs=pl.BlockSpec((tq, dh), lambda h, i, cb_ref: (i, h)),
        scratch_shapes=[pltpu.VMEM((tk, tq), F32), pltpu.VMEM((tk, tq), F32),
                        pltpu.VMEM((dh, tq), F32)],
    )
    return pl.pallas_call(
        functools.partial(_attn_kernel, tk=tk, tn=tn),
        out_shape=jax.ShapeDtypeStruct((s, heads * dh), BF16),
        grid_spec=grid_spec,
        compiler_params=_params(("parallel", "arbitrary"),
                                3 * _nbytes((s, dh), BF16) + 2 * _nbytes((tq, dh), BF16),
                                6 * _nbytes((tk, tq), F32)),
        name="fox_attention",
    )(cb, qkv, qkv, e, vt)


def _pool_kernel(halo_ref, p_ref, o_ref, *, tm, group_dim):
    i = pl.program_id(0)
    halo = jnp.where(i == 0, 0.0, halo_ref[...])
    ext = jnp.concatenate([halo, p_ref[...]], axis=0)
    t = i * tm + lax.broadcasted_iota(jnp.int32, (tm, 1), 0)
    for g, w in enumerate(POOL_WINDOWS):
        cols = slice(g * group_dim, (g + 1) * group_dim)
        e = ext[:, cols]
        acc = e
        d = 1
        while d < w:
            acc = acc + pltpu.roll(acc, d, 0)
            d *= 2
        cnt = jnp.minimum(t + 1, w).astype(F32)
        o_ref[:, cols] = (acc[POOL_HALO:] / cnt - e[POOL_HALO:]).astype(o_ref.dtype)


def _pool(p, *, tm=512):
    s, width = p.shape
    group_dim = width // len(POOL_WINDOWS)
    hb = tm // POOL_HALO
    return pl.pallas_call(
        functools.partial(_pool_kernel, tm=tm, group_dim=group_dim),
        out_shape=jax.ShapeDtypeStruct((s, width), BF16),
        grid=(s // tm,),
        in_specs=[pl.BlockSpec((POOL_HALO, width), lambda i: (jnp.maximum(i * hb - 1, 0), 0)),
                  pl.BlockSpec((tm, width), lambda i: (i, 0))],
        out_specs=pl.BlockSpec((tm, width), lambda i: (i, 0)),
        compiler_params=_params(("parallel",),
                                _nbytes((tm + POOL_HALO, width), F32) + _nbytes((tm, width), BF16),
                                3 * _nbytes((tm + POOL_HALO, width), F32)),
        name="pool",
    )(p, p)


def _mix_kernel(a_ref, wa_ref, p_ref, wp_ref, ps_ref, ga_ref, gp_ref, o_ref):
    ya = jnp.dot(a_ref[...], _as_bf16(wa_ref[...]), preferred_element_type=F32)
    yp = jnp.dot(p_ref[...], _as_bf16(wp_ref[...]), preferred_element_type=F32) * ps_ref[...]
    o_ref[...] = (ga_ref[...] * ya + gp_ref[...] * yp).astype(o_ref.dtype)


def _mix(attn, w_a, pooled, w_pool, pool_scale, gates, *, tm=512):
    s, fw = attn.shape
    groups, gd, tn = w_pool.shape
    d = w_a.shape[1]
    return pl.pallas_call(
        _mix_kernel,
        out_shape=jax.ShapeDtypeStruct((s, d), BF16),
        grid=(s // tm, groups),
        in_specs=[pl.BlockSpec((tm, fw), lambda i, j: (i, 0)),
                  pl.BlockSpec((fw, tn), lambda i, j: (0, j)),
                  pl.BlockSpec((tm, gd), lambda i, j: (i, j)),
                  pl.BlockSpec((None, gd, tn), lambda i, j: (j, 0, 0)),
                  pl.BlockSpec((1, tn), lambda i, j: (0, j)),
                  pl.BlockSpec((tm, tn), lambda i, j: (i, j)),
                  pl.BlockSpec((tm, tn), lambda i, j: (i, groups + j))],
        out_specs=pl.BlockSpec((tm, tn), lambda i, j: (i, j)),
        compiler_params=_params(("parallel", "parallel"),
                                _nbytes((tm, fw), BF16) + _nbytes((fw, tn), w_a.dtype)
                                + _nbytes((tm, gd), BF16) + _nbytes((gd, tn), w_pool.dtype)
                                + 2 * _nbytes((tm, tn), F32) + _nbytes((tm, tn), BF16),
                                _nbytes((fw + gd, tn), BF16) + 3 * _nbytes((tm, tn), F32)),
        name="branch_mix",
    )(attn, w_a, pooled, w_pool, pool_scale, gates, gates)


def _resid_mm_kernel(a_ref, w_ref, x_ref, g_ref, o_ref):
    m = jnp.dot(a_ref[...], _as_bf16(w_ref[...]), preferred_element_type=F32)
    o_ref[...] = ALPHA * x_ref[...] + g_ref[...] * m


def _resid_matmul(a, w, x, mod, gate_idx, *, tm, tn, name):
    m, k = a.shape
    n = w.shape[1]
    nb = n // tn
    return pl.pallas_call(
        _resid_mm_kernel,
        out_shape=jax.ShapeDtypeStruct((m, n), F32),
        grid=(m // tm, nb),
        in_specs=[pl.BlockSpec((tm, k), lambda i, j: (i, 0)),
                  pl.BlockSpec((k, tn), lambda i, j: (0, j)),
                  pl.BlockSpec((tm, tn), lambda i, j: (i, j)),
                  pl.BlockSpec((1, tn), lambda i, j: (0, gate_idx * nb + j))],
        out_specs=pl.BlockSpec((tm, tn), lambda i, j: (i, j)),
        compiler_params=_params(("parallel", "parallel"),
                                _nbytes((tm, k), BF16) + _nbytes((k, tn), w.dtype)
                                + 2 * _nbytes((tm, tn), F32),
                                _cast_bytes((k, tn), w.dtype) + 2 * _nbytes((tm, tn), F32)),
        name=name,
    )(a, w, x, mod)


def _ln_ln_mod_kernel(r_ref, g_ref, b_ref, sh_ref, sc_ref, x_ref, u_ref):
    x1 = _ln_rows(r_ref[...]) * g_ref[...] + b_ref[...]
    x_ref[...] = x1
    u_ref[...] = (_ln_rows(x1) * (1.0 + sc_ref[...]) + sh_ref[...]).astype(u_ref.dtype)


def _ln_ln_mod(r, gain, bias, mod, shift_idx, scale_idx, *, tm=256):
    s, d = r.shape
    row = pl.BlockSpec((tm, d), lambda i: (i, 0))
    vec = pl.BlockSpec((1, d), lambda i: (0, 0))
    return pl.pallas_call(
        _ln_ln_mod_kernel,
        out_shape=(jax.ShapeDtypeStruct((s, d), F32), jax.ShapeDtypeStruct((s, d), BF16)),
        grid=(s // tm,),
        in_specs=[row, vec, vec,
                  pl.BlockSpec((1, d), lambda i: (0, shift_idx)),
                  pl.BlockSpec((1, d), lambda i: (0, scale_idx))],
        out_specs=(row, row),
        compiler_params=_params(("parallel",),
                                2 * _nbytes((tm, d), F32) + _nbytes((tm, d), BF16),
                                4 * _nbytes((tm, d), F32)),
        name="ln1_ln_mod",
    )(r, gain, bias, mod, mod)


def _ln_affine_kernel(r_ref, g_ref, b_ref, o_ref):
    o_ref[...] = _ln_rows(r_ref[...]) * g_ref[...] + b_ref[...]


def _ln_affine(r, gain, bias, *, tm=256):
    s, d = r.shape
    row = pl.BlockSpec((tm, d), lambda i: (i, 0))
    vec = pl.BlockSpec((1, d), lambda i: (0, 0))
    return pl.pallas_call(
        _ln_affine_kernel,
        out_shape=jax.ShapeDtypeStruct((s, d), F32),
        grid=(s // tm,),
        in_specs=[row, vec, vec],
        out_specs=row,
        compiler_params=_params(("parallel",), 2 * _nbytes((tm, d), F32),
                                3 * _nbytes((tm, d), F32)),
        name="ln2",
    )(r, gain, bias)


def _ffn_up_kernel(u_ref, wg_ref, wu_ref, o_ref):
    u = u_ref[...]
    g = jnp.dot(u, _as_bf16(wg_ref[...]), preferred_element_type=F32)
    up = jnp.dot(u, _as_bf16(wu_ref[...]), preferred_element_type=F32)
    o_ref[...] = (g * jax.nn.sigmoid(g) * up).astype(o_ref.dtype)


def _ffn_up(u, w_gate_up, *, tm=1024, tn=256):
    s, d = u.shape
    hidden = w_gate_up.shape[1] // 2
    nb = hidden // tn
    return pl.pallas_call(
        _ffn_up_kernel,
        out_shape=jax.ShapeDtypeStruct((s, hidden), BF16),
        grid=(s // tm, nb),
        in_specs=[pl.BlockSpec((tm, d), lambda i, j: (i, 0)),
                  pl.BlockSpec((d, tn), lambda i, j: (0, j)),
                  pl.BlockSpec((d, tn), lambda i, j: (0, nb + j))],
        out_specs=pl.BlockSpec((tm, tn), lambda i, j: (i, j)),
        compiler_params=_params(("parallel", "parallel"),
                                _nbytes((tm, d), BF16) + 2 * _nbytes((d, tn), w_gate_up.dtype)
                                + _nbytes((tm, tn), BF16),
                                2 * _nbytes((d, tn), BF16) + 4 * _nbytes((tm, tn), F32)),
        name="ffn_up",
    )(u, w_gate_up, w_gate_up)


def kernel(x, c, w_ada, b_ada, w_in, b_forget, w_attn_out, w_pool, pool_scale, w_out,
           ln1_g, ln1_b, w_gate_up, w_down, ln2_g, ln2_b):
    batch, seq, d = x.shape
    assert batch == 1 and w_ada.shape[0] == DEPTH == 1
    fox_w = w_attn_out.shape[1]
    heads = fox_w // FOX_HEAD_DIM
    pool_w = w_pool.shape[1] * w_pool.shape[2]
    o_f = 3 * fox_w
    o_p = o_f + heads
    tk = 512

    xs = x[0]
    w_in0 = w_in[0]
    w_f = jnp.pad(w_in0[:, o_f:o_p], ((0, 0), (0, LANES - heads)))
    w_f_hi = w_f.astype(BF16)
    w_f_lo = (w_f - w_f_hi.astype(F32)).astype(BF16)
    b_f = jnp.pad(b_forget, ((0, 0), (0, LANES - heads)))
    w_pg = w_in0[:, o_p:].astype(BF16)

    mod = _ada(c.reshape(d, 1), w_ada[0], b_ada)

    u1 = _ln_mod(xs, mod, 0, 1)
    qkv = _matmul(u1, w_in0, o_f, BF16, tm=1024, tn=512, scale=FOX_HEAD_DIM ** -0.5 * LOG2E,
                  n_scaled=fox_w // 512, name="proj_qkv")
    p = _matmul(u1, w_pg, pool_w, F32, tm=1024, tn=1024, name="proj_pool")
    gates = _matmul(u1, w_pg, 2 * d, F32, tm=1024, tn=1024, col_block=pool_w // 1024,
                    sigmoid=True, name="proj_gates")
    e, c_blk = _fcum(u1, w_f_hi, w_f_lo, b_f, heads=heads, tm=tk)
    cb = c_blk[:, 0, :heads].T
    vt = qkv[:, 2 * fox_w:].reshape(seq // tk, tk, heads, FOX_HEAD_DIM).transpose(2, 0, 3, 1)
    attn = _attention(cb, qkv, e, vt, heads=heads, tk=tk)
    pooled = _pool(p)
    mix = _mix(attn, w_attn_out[0], pooled, w_pool[0], pool_scale, gates)
    r1 = _resid_matmul(mix, w_out[0], xs, mod, 2, tm=1024, tn=512, name="out_proj")
    x1, u2 = _ln_ln_mod(r1, ln1_g, ln1_b, mod, 3, 4)

    act = _ffn_up(u2, w_gate_up[0])
    r2 = _resid_matmul(act, w_down[0].astype(BF16), x1, mod, 5, tm=512, tn=512, name="ffn_down")
    out = _ln_affine(r2, ln2_g, ln2_b)
    return out[None]
```

```python
import functools

import jax
import jax.numpy as jnp
from jax import lax
from jax.experimental import pallas as pl
from jax.experimental.pallas import tpu as pltpu

F32 = jnp.float32
BF16 = jnp.bfloat16

FOX_HEAD_DIM = 128
POOL_WINDOWS = (2, 4, 8, 16)
POOL_HALO = 16
DEPTH = 1
ALPHA = (2 * DEPTH) ** 0.25
LN_EPS = 1e-5
LOG2E = 1.4426950408889634
LANES = 128
BIAS_PIECES = 3
ONES_ROWS = 16

V7X_VMEM_BYTES = 64 * 1024 * 1024
VMEM_CAP_BYTES = V7X_VMEM_BYTES - 6 * 1024 * 1024


def _nbytes(shape, dtype):
    n = jnp.dtype(dtype).itemsize
    for s in shape:
        n *= s
    return n


def _cast_bytes(shape, dtype):
    return 0 if dtype == BF16 else _nbytes(shape, BF16)


def _params(semantics, pipelined_bytes, resident_bytes=0):
    need = 2 * pipelined_bytes + resident_bytes
    assert need <= VMEM_CAP_BYTES, (need, VMEM_CAP_BYTES)
    return pltpu.CompilerParams(dimension_semantics=semantics,
                                vmem_limit_bytes=min(need + (4 << 20), VMEM_CAP_BYTES))


def _ln_rows(x):
    mu = jnp.mean(x, axis=-1, keepdims=True)
    xc = x - mu
    var = jnp.mean(xc * xc, axis=-1, keepdims=True)
    return xc * lax.rsqrt(var + LN_EPS)


def _as_bf16(w):
    return w if w.dtype == BF16 else w.astype(BF16)


def _split3(v):
    hi = v.astype(BF16)
    r1 = v - hi.astype(F32)
    mid = r1.astype(BF16)
    lo = (r1 - mid.astype(F32)).astype(BF16)
    return hi, mid, lo


def _ada_kernel(c_ref, w_ref, b_ref, o_ref):
    k = pl.program_id(1)

    @pl.when(k == 0)
    def _():
        o_ref[...] = b_ref[...]

    cc = c_ref[...]
    s = cc * jax.nn.sigmoid(cc)
    o_ref[...] += jnp.sum(s * w_ref[...], axis=0, keepdims=True)


def _ada(c_col, w_ada, b_ada, *, tk=512, tn=4096):
    d, n = w_ada.shape
    return pl.pallas_call(
        _ada_kernel,
        out_shape=jax.ShapeDtypeStruct((1, n), F32),
        grid=(n // tn, d // tk),
        in_specs=[pl.BlockSpec((tk, 1), lambda j, k: (k, 0)),
                  pl.BlockSpec((tk, tn), lambda j, k: (k, j)),
                  pl.BlockSpec((1, tn), lambda j, k: (0, j))],
        out_specs=pl.BlockSpec((1, tn), lambda j, k: (0, j)),
        compiler_params=_params(("parallel", "arbitrary"),
                                _nbytes((tk, tn), F32) + _nbytes((tk, LANES), F32),
                                _nbytes((tk, tn), F32)),
        name="ada",
    )(c_col, w_ada, b_ada)


def _ln_mod_fcum_kernel(x_ref, sh_ref, sc_ref, wf_ref, b_ref, u_ref, e_ref, c_ref,
                        carry_ref, f_ref, *, tm, rows, heads):
    @pl.when(pl.program_id(0) == 0)
    def _():
        carry_ref[...] = jnp.zeros_like(carry_ref)

    gain = 1.0 + sc_ref[...]
    shift = sh_ref[...]
    for r in range(0, tm, rows):
        u = (_ln_rows(x_ref[r:r + rows, :]) * gain + shift).astype(BF16)
        u_ref[r:r + rows, :] = u
        f_ref[r:r + rows, :] = jnp.dot(u, wf_ref[...], preferred_element_type=F32)
    f = f_ref[...] + b_ref[...]
    ls = (jnp.minimum(f, 0.0) - jnp.log1p(jnp.exp(-jnp.abs(f)))) * LOG2E
    row = lax.broadcasted_iota(jnp.int32, (tm, tm), 0)
    col = lax.broadcasted_iota(jnp.int32, (tm, tm), 1)
    tri = jnp.where(col <= row, 1.0, 0.0).astype(BF16)
    cs = sum(jnp.dot(tri, piece, preferred_element_type=F32) for piece in _split3(ls))
    carry = carry_ref[...]
    c_ref[...] = carry
    carry_ref[...] = carry + cs[tm - 1:tm, :]
    head_lane = lax.broadcasted_iota(jnp.int32, (tm, LANES), 1) < heads
    e = sum(pltpu.roll(jnp.where(head_lane, piece.astype(F32), 0.0), p * heads, 1) if p else
            jnp.where(head_lane, piece.astype(F32), 0.0)
            for p, piece in enumerate(_split3(-cs)))
    e_ref[...] = e.astype(e_ref.dtype)


def _ln_mod_fcum(x, mod, shift_idx, scale_idx, wf, b_row, *, heads, tm, rows=128):
    s, d = x.shape
    nb = s // tm
    assert BIAS_PIECES * heads <= LANES
    vec = lambda idx: pl.BlockSpec((1, d), lambda i: (0, idx))
    const = lambda shape: pl.BlockSpec(shape, lambda i: (0, 0))
    return pl.pallas_call(
        functools.partial(_ln_mod_fcum_kernel, tm=tm, rows=rows, heads=heads),
        out_shape=(jax.ShapeDtypeStruct((s, d), BF16),
                   jax.ShapeDtypeStruct((s, LANES), BF16),
                   jax.ShapeDtypeStruct((nb, 1, LANES), F32)),
        grid=(nb,),
        in_specs=[pl.BlockSpec((tm, d), lambda i: (i, 0)), vec(shift_idx), vec(scale_idx),
                  const((d, LANES)), const((1, LANES))],
        out_specs=(pl.BlockSpec((tm, d), lambda i: (i, 0)),
                   pl.BlockSpec((tm, LANES), lambda i: (i, 0)),
                   pl.BlockSpec((None, 1, LANES), lambda i: (i, 0, 0))),
        scratch_shapes=[pltpu.VMEM((1, LANES), F32), pltpu.VMEM((tm, LANES), F32)],
        compiler_params=_params(("arbitrary",),
                                _nbytes((tm, d), F32) + _nbytes((tm, d), BF16)
                                + _nbytes((tm, LANES), BF16) + _nbytes((d, LANES), BF16),
                                4 * _nbytes((rows, d), F32) + 2 * _nbytes((tm, tm), F32)),
        name="ln_mod_fcum",
    )(x, mod, mod, wf, b_row)


def _realign_kernel(a_ref, b_ref, o_ref, *, shift):
    tc = a_ref.shape[1]
    ar = pltpu.roll(a_ref[...], tc - shift, 1)
    br = pltpu.roll(b_ref[...], LANES - shift, 1)
    lane = lax.broadcasted_iota(jnp.int32, br.shape, 1)
    o_ref[:, :tc - LANES] = ar[:, :tc - LANES].astype(o_ref.dtype)
    o_ref[:, tc - LANES:] = jnp.where(lane >= LANES - shift, br, ar[:, tc - LANES:]).astype(o_ref.dtype)


def _realign_cast(w, col0, n, *, tr=512, tc=1024):
    rows = w.shape[0]
    shift = col0 % LANES
    base = col0 - shift
    assert 0 < shift and base % tc == 0 and n % tc == 0 and col0 + n <= w.shape[1]
    return pl.pallas_call(
        functools.partial(_realign_kernel, shift=shift),
        out_shape=jax.ShapeDtypeStruct((rows, n), BF16),
        grid=(rows // tr, n // tc),
        in_specs=[pl.BlockSpec((tr, tc), lambda r, j: (r, base // tc + j)),
                  pl.BlockSpec((tr, LANES), lambda r, j: (r, (base + tc * (j + 1)) // LANES))],
        out_specs=pl.BlockSpec((tr, tc), lambda r, j: (r, j)),
        compiler_params=_params(("parallel", "parallel"),
                                _nbytes((tr, tc + LANES), F32) + _nbytes((tr, tc), BF16),
                                3 * _nbytes((tr, tc), F32)),
        name="realign_cast",
    )(w, w)


def _mm_kernel(a_ref, w_ref, o_ref, *, scale, n_scaled, sigmoid):
    acc = jnp.dot(a_ref[...], _as_bf16(w_ref[...]), preferred_element_type=F32)
    if scale is not None:
        acc = acc * jnp.where(pl.program_id(1) < n_scaled, scale, 1.0)
    if sigmoid:
        acc = jax.nn.sigmoid(acc)
    o_ref[...] = acc.astype(o_ref.dtype)


def _matmul(a, w, n, out_dtype, *, tm, tn, col_block=0, scale=None, n_scaled=0,
            sigmoid=False, name):
    m, k = a.shape
    return pl.pallas_call(
        functools.partial(_mm_kernel, scale=scale, n_scaled=n_scaled, sigmoid=sigmoid),
        out_shape=jax.ShapeDtypeStruct((m, n), out_dtype),
        grid=(m // tm, n // tn),
        in_specs=[pl.BlockSpec((tm, k), lambda i, j: (i, 0)),
                  pl.BlockSpec((k, tn), lambda i, j: (0, col_block + j))],
        out_specs=pl.BlockSpec((tm, tn), lambda i, j: (i, j)),
        compiler_params=_params(("parallel", "parallel"),
                                _nbytes((tm, k), a.dtype) + _nbytes((k, tn), w.dtype)
                                + _nbytes((tm, tn), out_dtype),
                                _cast_bytes((k, tn), w.dtype) + 2 * _nbytes((tm, tn), F32)),
        name=name,
    )(a, w)


def _mm_t_kernel(a_ref, w_ref, o_ref):
    acc = jnp.dot(a_ref[...], _as_bf16(w_ref[...]), preferred_element_type=F32)
    acc_t = acc.T
    tk = o_ref.shape[2]
    for kb in range(o_ref.shape[0]):
        o_ref[kb] = acc_t[:, kb * tk:(kb + 1) * tk].astype(o_ref.dtype)


def _matmul_t(a, w, n, *, tm, tn, tk, col_block, name):
    m, k = a.shape
    return pl.pallas_call(
        _mm_t_kernel,
        out_shape=jax.ShapeDtypeStruct((m // tk, n, tk), BF16),
        grid=(m // tm, n // tn),
        in_specs=[pl.BlockSpec((tm, k), lambda i, j: (i, 0)),
                  pl.BlockSpec((k, tn), lambda i, j: (0, col_block + j))],
        out_specs=pl.BlockSpec((tm // tk, tn, tk), lambda i, j: (i, j, 0)),
        compiler_params=_params(("parallel", "parallel"),
                                _nbytes((tm, k), a.dtype) + _nbytes((k, tn), w.dtype)
                                + _nbytes((tm, tn), BF16),
                                _cast_bytes((k, tn), w.dtype) + 3 * _nbytes((tm, tn), F32)),
        name=name,
    )(a, w)


def _attn_kernel(cb_ref, q_ref, k_ref, e_ref, vt_ref, o_ref, s0_ref, s1_ref, acc_ref, *,
                 tk, tn, heads):
    h = pl.program_id(0)
    i = pl.program_id(1)
    tq, dh = q_ref.shape
    nt = tq // tn
    r = lax.broadcasted_iota(jnp.int32, (dh, tq), 0) - h
    ones_rows = sum(jnp.where(r == p * heads, 1.0, 0.0) for p in range(BIAS_PIECES))
    qt_aug = jnp.concatenate([q_ref[...].astype(F32).T.astype(BF16), ones_rows.astype(BF16)],
                             axis=0)
    ones_v = jnp.ones((acc_ref.shape[0] - dh, tk), BF16)
    c_q = cb_ref[h, 2 * i]
    acc_ref[...] = jnp.zeros_like(acc_ref)

    def logits_to(buf_ref, j, first_tile=0):
        rows = pl.ds(pl.multiple_of(j * tk, tk), tk)
        k_aug = jnp.concatenate([k_ref[rows, :], e_ref[rows, :]], axis=1)
        cms = []
        for n in range(first_tile, nt):
            cols = slice(n * tn, (n + 1) * tn)
            s = jnp.dot(k_aug, qt_aug[:, cols], preferred_element_type=F32)
            buf_ref[:, cols] = s
            cms.append(jnp.max(s, axis=0, keepdims=True))
        return jnp.concatenate(cms, axis=1)

    def masked(buf_ref, n, first_tile):
        key = lax.broadcasted_iota(jnp.int32, (tk, tn), 0)
        qry = lax.broadcasted_iota(jnp.int32, (tk, tn), 1) + (n - first_tile) * tn
        return jnp.where(key <= qry, buf_ref[:, n * tn:(n + 1) * tn], -jnp.inf)

    def softmax_pv(buf_ref, j, cmax, m, first_tile=0, diagonal=False):
        lo = first_tile * tn
        c = c_q - cb_ref[h, j]
        m_old = m[:, lo:]
        m_new = jnp.maximum(m_old, cmax + c)
        a = jnp.exp2(m_old - m_new)
        off = m_new - c
        v_aug = jnp.concatenate([vt_ref[j], ones_v], axis=0)
        for n in range(first_tile, nt):
            cols = slice(n * tn, (n + 1) * tn)
            rel = slice(n * tn - lo, (n + 1) * tn - lo)
            s = masked(buf_ref, n, first_tile) if diagonal else buf_ref[:, cols]
            p = jnp.exp2(s - off[:, rel]).astype(BF16)
            acc_ref[:, cols] = a[:, rel] * acc_ref[:, cols] + jnp.dot(
                v_aug, p, preferred_element_type=F32)
        if first_tile:
            m_new = jnp.concatenate([m[:, :lo], m_new], axis=1)
        return m_new

    def pair(p, carry):
        m, cm0 = carry
        cm1 = logits_to(s1_ref, 2 * p + 1)
        m = softmax_pv(s0_ref, 2 * p, cm0, m)
        cm0 = logits_to(s0_ref, 2 * p + 2)
        m = softmax_pv(s1_ref, 2 * p + 1, cm1, m)
        return m, cm0

    init = (jnp.full((1, tq), -jnp.inf, F32), logits_to(s0_ref, 0))
    m, _ = lax.fori_loop(0, i, pair, init)
    half = nt // 2
    logits_to(s1_ref, 2 * i + 1, first_tile=half)
    cm = jnp.concatenate([jnp.max(masked(s0_ref, n, 0), axis=0, keepdims=True) for n in range(nt)],
                         axis=1)
    m = softmax_pv(s0_ref, 2 * i, cm, m, diagonal=True)
    cm = jnp.concatenate([jnp.max(masked(s1_ref, n, half), axis=0, keepdims=True)
                          for n in range(half, nt)], axis=1)
    softmax_pv(s1_ref, 2 * i + 1, cm, m, first_tile=half, diagonal=True)
    o_ref[...] = (acc_ref[:dh, :] / acc_ref[dh:dh + 1, :]).T.astype(o_ref.dtype)


def _attention(cb, qk, e, vt, *, heads, tk, tn=256):
    s = qk.shape[0]
    dh = FOX_HEAD_DIM
    tq = 2 * tk
    nq = s // tq
    grid_spec = pltpu.PrefetchScalarGridSpec(
        num_scalar_prefetch=1,
        grid=(heads, nq),
        in_specs=[pl.BlockSpec((tq, dh), lambda h, i, cb_ref: (i, h)),
                  pl.BlockSpec((s, dh), lambda h, i, cb_ref: (0, heads + h)),
                  pl.BlockSpec((s, LANES), lambda h, i, cb_ref: (0, 0)),
                  pl.BlockSpec((s // tk, dh, tk), lambda h, i, cb_ref: (0, h, 0))],
        out_specs=pl.BlockSpec((tq, dh), lambda h, i, cb_ref: (i, h)),
        scratch_shapes=[pltpu.VMEM((tk, tq), F32), pltpu.VMEM((tk, tq), F32),
                        pltpu.VMEM((dh + ONES_ROWS, tq), F32)],
    )
    return pl.pallas_call(
        functools.partial(_attn_kernel, tk=tk, tn=tn, heads=heads),
        out_shape=jax.ShapeDtypeStruct((s, heads * dh), BF16),
        grid_spec=grid_spec,
        compiler_params=_params(("parallel", "arbitrary"),
                                3 * _nbytes((s, dh), BF16) + 2 * _nbytes((tq, dh), BF16),
                                6 * _nbytes((tk, tq), F32)),
        name="fox_attention",
    )(cb, qk, qk, e, vt)


def _mix_kernel(a_ref, wa_ref, halo_ref, p_ref, wp_ref, ps_ref, ga_ref, gp_ref, o_ref,
                wa_b, wp_b, *, tm):
    j = pl.program_id(0)
    i = pl.program_id(1)

    @pl.when(i == 0)
    def _():
        wa_b[...] = wa_ref[...].astype(BF16)
        wp_b[...] = wp_ref[...].astype(BF16)

    halo = jnp.where(i == 0, 0.0, halo_ref[...])
    ext = jnp.concatenate([halo, p_ref[...]], axis=0)
    acc = ext
    win = ext
    for g, w in enumerate(POOL_WINDOWS):
        acc = acc + pltpu.roll(acc, w // 2, 0)
        win = jnp.where(j == g, acc, win)
    t = i * tm + lax.broadcasted_iota(jnp.int32, (tm, 1), 0)
    cnt = jnp.minimum(t + 1, jnp.left_shift(2, j)).astype(F32)
    pooled = (win[POOL_HALO:] / cnt - ext[POOL_HALO:]).astype(BF16)

    ya = jnp.dot(a_ref[...], wa_b[...], preferred_element_type=F32)
    yp = jnp.dot(pooled, wp_b[...], preferred_element_type=F32) * ps_ref[...]
    o_ref[...] = (ga_ref[...].astype(F32) * ya + gp_ref[...].astype(F32) * yp).astype(o_ref.dtype)


def _mix(attn, w_a, p, w_pool, pool_scale, gates, *, tm=512):
    s, fw = attn.shape
    groups, gd, tn = w_pool.shape
    assert POOL_WINDOWS == tuple(2 << g for g in range(groups))
    d = w_a.shape[1]
    hb = tm // POOL_HALO
    return pl.pallas_call(
        functools.partial(_mix_kernel, tm=tm),
        out_shape=jax.ShapeDtypeStruct((s, d), BF16),
        grid=(groups, s // tm),
        in_specs=[pl.BlockSpec((tm, fw), lambda j, i: (i, 0)),
                  pl.BlockSpec((fw, tn), lambda j, i: (0, j)),
                  pl.BlockSpec((POOL_HALO, gd), lambda j, i: (jnp.maximum(i * hb - 1, 0), j)),
                  pl.BlockSpec((tm, gd), lambda j, i: (i, j)),
                  pl.BlockSpec((None, gd, tn), lambda j, i: (j, 0, 0)),
                  pl.BlockSpec((1, tn), lambda j, i: (0, j)),
                  pl.BlockSpec((tm, tn), lambda j, i: (i, j)),
                  pl.BlockSpec((tm, tn), lambda j, i: (i, groups + j))],
        out_specs=pl.BlockSpec((tm, tn), lambda j, i: (i, j)),
        scratch_shapes=[pltpu.VMEM((fw, tn), BF16), pltpu.VMEM((gd, tn), BF16)],
        compiler_params=_params(("parallel", "arbitrary"),
                                _nbytes((tm, fw), BF16) + _nbytes((fw, tn), w_a.dtype)
                                + _nbytes((tm + POOL_HALO, gd), F32) + _nbytes((gd, tn), w_pool.dtype)
                                + 3 * _nbytes((tm, tn), BF16),
                                _nbytes((fw + gd, tn), BF16) + 3 * _nbytes((tm, tn), F32)
                                + 4 * _nbytes((tm + POOL_HALO, gd), F32)),
        name="branch_mix",
    )(attn, w_a, p, p, w_pool, pool_scale, gates, gates)


def _resid_mm_kernel(a_ref, w_ref, x_ref, g_ref, o_ref):
    m = jnp.dot(a_ref[...], _as_bf16(w_ref[...]), preferred_element_type=F32)
    o_ref[...] = ALPHA * x_ref[...] + g_ref[...] * m


def _resid_matmul(a, w, x, mod, gate_idx, *, tm, tn, name):
    m, k = a.shape
    n = w.shape[1]
    nb = n // tn
    return pl.pallas_call(
        _resid_mm_kernel,
        out_shape=jax.ShapeDtypeStruct((m, n), F32),
        grid=(m // tm, nb),
        in_specs=[pl.BlockSpec((tm, k), lambda i, j: (i, 0)),
                  pl.BlockSpec((k, tn), lambda i, j: (0, j)),
                  pl.BlockSpec((tm, tn), lambda i, j: (i, j)),
                  pl.BlockSpec((1, tn), lambda i, j: (0, gate_idx * nb + j))],
        out_specs=pl.BlockSpec((tm, tn), lambda i, j: (i, j)),
        compiler_params=_params(("parallel", "parallel"),
                                _nbytes((tm, k), BF16) + _nbytes((k, tn), w.dtype)
                                + 2 * _nbytes((tm, tn), F32),
                                _cast_bytes((k, tn), w.dtype) + 2 * _nbytes((tm, tn), F32)),
        name=name,
    )(a, w, x, mod)


def _ln_ln_mod_kernel(r_ref, g_ref, b_ref, sh_ref, sc_ref, x_ref, u_ref):
    x1 = _ln_rows(r_ref[...]) * g_ref[...] + b_ref[...]
    x_ref[...] = x1
    u_ref[...] = (_ln_rows(x1) * (1.0 + sc_ref[...]) + sh_ref[...]).astype(u_ref.dtype)


def _ln_ln_mod(r, gain, bias, mod, shift_idx, scale_idx, *, tm=256):
    s, d = r.shape
    row = pl.BlockSpec((tm, d), lambda i: (i, 0))
    vec = pl.BlockSpec((1, d), lambda i: (0, 0))
    return pl.pallas_call(
        _ln_ln_mod_kernel,
        out_shape=(jax.ShapeDtypeStruct((s, d), F32), jax.ShapeDtypeStruct((s, d), BF16)),
        grid=(s // tm,),
        in_specs=[row, vec, vec,
                  pl.BlockSpec((1, d), lambda i: (0, shift_idx)),
                  pl.BlockSpec((1, d), lambda i: (0, scale_idx))],
        out_specs=(row, row),
        compiler_params=_params(("parallel",),
                                2 * _nbytes((tm, d), F32) + _nbytes((tm, d), BF16),
                                4 * _nbytes((tm, d), F32)),
        name="ln1_ln_mod",
    )(r, gain, bias, mod, mod)


def _ln_affine_kernel(r_ref, g_ref, b_ref, o_ref):
    o_ref[...] = _ln_rows(r_ref[...]) * g_ref[...] + b_ref[...]


def _ln_affine(r, gain, bias, *, tm=256):
    s, d = r.shape
    row = pl.BlockSpec((tm, d), lambda i: (i, 0))
    vec = pl.BlockSpec((1, d), lambda i: (0, 0))
    return pl.pallas_call(
        _ln_affine_kernel,
        out_shape=jax.ShapeDtypeStruct((s, d), F32),
        grid=(s // tm,),
        in_specs=[row, vec, vec],
        out_specs=row,
        compiler_params=_params(("parallel",), 2 * _nbytes((tm, d), F32),
                                3 * _nbytes((tm, d), F32)),
        name="ln2",
    )(r, gain, bias)


def _ffn_up_kernel(u_ref, wg_ref, wu_ref, o_ref):
    u = u_ref[...]
    g = jnp.dot(u, _as_bf16(wg_ref[...]), preferred_element_type=F32)
    up = jnp.dot(u, _as_bf16(wu_ref[...]), preferred_element_type=F32)
    o_ref[...] = (g * jax.nn.sigmoid(g) * up).astype(o_ref.dtype)


def _ffn_up(u, w_gate_up, *, tm=1024, tn=256):
    s, d = u.shape
    hidden = w_gate_up.shape[1] // 2
    nb = hidden // tn
    return pl.pallas_call(
        _ffn_up_kernel,
        out_shape=jax.ShapeDtypeStruct((s, hidden), BF16),
        grid=(s // tm, nb),
        in_specs=[pl.BlockSpec((tm, d), lambda i, j: (i, 0)),
                  pl.BlockSpec((d, tn), lambda i, j: (0, j)),
                  pl.BlockSpec((d, tn), lambda i, j: (0, nb + j))],
        out_specs=pl.BlockSpec((tm, tn), lambda i, j: (i, j)),
        compiler_params=_params(("parallel", "parallel"),
                                _nbytes((tm, d), BF16) + 2 * _nbytes((d, tn), w_gate_up.dtype)
                                + _nbytes((tm, tn), BF16),
                                2 * _nbytes((d, tn), BF16) + 4 * _nbytes((tm, tn), F32)),
        name="ffn_up",
    )(u, w_gate_up, w_gate_up)


def kernel(x, c, w_ada, b_ada, w_in, b_forget, w_attn_out, w_pool, pool_scale, w_out,
           ln1_g, ln1_b, w_gate_up, w_down, ln2_g, ln2_b):
    batch, seq, d = x.shape
    assert batch == 1 and w_ada.shape[0] == DEPTH == 1
    fox_w = w_attn_out.shape[1]
    heads = fox_w // FOX_HEAD_DIM
    pool_w = w_pool.shape[1] * w_pool.shape[2]
    o_f = 3 * fox_w
    o_p = o_f + heads
    tk = 512

    xs = x[0]
    w_in0 = w_in[0]
    w_f = jnp.pad(w_in0[:, o_f:o_p], ((0, 0), (0, LANES - heads))).astype(BF16)
    b_f = jnp.pad(b_forget, ((0, 0), (0, LANES - heads)))

    mod = _ada(c.reshape(d, 1), w_ada[0], b_ada)
    w_pg = _realign_cast(w_in0, o_p, pool_w + 2 * d)

    u1, e, c_blk = _ln_mod_fcum(xs, mod, 0, 1, w_f, b_f, heads=heads, tm=tk)
    cb = c_blk[:, 0, :heads].T
    qk = _matmul(u1, w_in0, 2 * fox_w, BF16, tm=1024, tn=512, scale=FOX_HEAD_DIM ** -0.5 * LOG2E,
                 n_scaled=fox_w // 512, name="proj_qk")
    vt = _matmul_t(u1, w_in0, fox_w, tm=1024, tn=512, tk=tk, col_block=2 * fox_w // 512,
                   name="proj_vt")
    p = _matmul(u1, w_pg, pool_w, F32, tm=1024, tn=1024, name="proj_pool")
    gates = _matmul(u1, w_pg, 2 * d, BF16, tm=1024, tn=1024, col_block=pool_w // 1024,
                    sigmoid=True, name="proj_gates")
    attn = _attention(cb, qk, e, vt, heads=heads, tk=tk)
    mix = _mix(attn, w_attn_out[0], p, w_pool[0], pool_scale, gates)
    r1 = _resid_matmul(mix, w_out[0], xs, mod, 2, tm=1024, tn=512, name="out_proj")
    x1, u2 = _ln_ln_mod(r1, ln1_g, ln1_b, mod, 3, 4)

    act = _ffn_up(u2, w_gate_up[0])
    r2 = _resid_matmul(act, w_down[0].astype(BF16), x1, mod, 5, tm=512, tn=512, name="ffn_down")
    out = _ln_affine(r2, ln2_g, ln2_b)
    return out[None]
```

```python
import functools

import jax
import jax.numpy as jnp
from jax import lax
from jax.experimental import pallas as pl
from jax.experimental.pallas import tpu as pltpu

F32 = jnp.float32
BF16 = jnp.bfloat16

FOX_HEAD_DIM = 128
POOL_WINDOWS = (2, 4, 8, 16)
POOL_HALO = 16
DEPTH = 1
ALPHA = (2 * DEPTH) ** 0.25
LN_EPS = 1e-5
LOG2E = 1.4426950408889634
LANES = 128
SUBLANES = 8
BIAS_PIECES = 3
ONES_ROWS = 16

V7X_VMEM_BYTES = 64 * 1024 * 1024
VMEM_CAP_BYTES = V7X_VMEM_BYTES - 6 * 1024 * 1024

_NT = (((1,), (1,)), ((), ()))


def _nbytes(shape, dtype):
    n = jnp.dtype(dtype).itemsize
    for s in shape:
        n *= s
    return n


def _cast_bytes(shape, dtype):
    return 0 if dtype == BF16 else _nbytes(shape, BF16)


def _params(semantics, pipelined_bytes, resident_bytes=0):
    need = 2 * pipelined_bytes + resident_bytes
    assert need <= VMEM_CAP_BYTES, (need, VMEM_CAP_BYTES)
    return pltpu.CompilerParams(dimension_semantics=semantics,
                                vmem_limit_bytes=min(need + (4 << 20), VMEM_CAP_BYTES))


def _ln_rows(x):
    mu = jnp.mean(x, axis=-1, keepdims=True)
    xc = x - mu
    var = jnp.mean(xc * xc, axis=-1, keepdims=True)
    return xc * lax.rsqrt(var + LN_EPS)


def _as_bf16(w):
    return w if w.dtype == BF16 else w.astype(BF16)


def _split3(v):
    hi = v.astype(BF16)
    r1 = v - hi.astype(F32)
    mid = r1.astype(BF16)
    lo = (r1 - mid.astype(F32)).astype(BF16)
    return hi, mid, lo


def _ada_kernel(c_ref, w_ref, b_ref, o_ref):
    k = pl.program_id(1)

    @pl.when(k == 0)
    def _():
        o_ref[...] = b_ref[...]

    cc = c_ref[...]
    s = cc * jax.nn.sigmoid(cc)
    o_ref[...] += jnp.sum(s * w_ref[...], axis=0, keepdims=True)


def _ada(c_col, w_ada, b_ada, *, tk=512, tn=4096):
    d, n = w_ada.shape
    return pl.pallas_call(
        _ada_kernel,
        out_shape=jax.ShapeDtypeStruct((1, n), F32),
        grid=(n // tn, d // tk),
        in_specs=[pl.BlockSpec((tk, 1), lambda j, k: (k, 0)),
                  pl.BlockSpec((tk, tn), lambda j, k: (k, j)),
                  pl.BlockSpec((1, tn), lambda j, k: (0, j))],
        out_specs=pl.BlockSpec((1, tn), lambda j, k: (0, j)),
        compiler_params=_params(("parallel", "arbitrary"),
                                _nbytes((tk, tn), F32) + _nbytes((tk, LANES), F32),
                                _nbytes((tk, tn), F32)),
        name="ada",
    )(c_col, w_ada, b_ada)


def _ln_mod_fcum_kernel(x_ref, sh_ref, sc_ref, wf_ref, b_ref, u_ref, e_ref, c_ref,
                        carry_ref, f_ref, *, tm, rows, heads):
    @pl.when(pl.program_id(0) == 0)
    def _():
        carry_ref[...] = jnp.zeros_like(carry_ref)

    gain = 1.0 + sc_ref[...]
    shift = sh_ref[...]
    for r in range(0, tm, rows):
        u = (_ln_rows(x_ref[r:r + rows, :]) * gain + shift).astype(BF16)
        u_ref[r:r + rows, :] = u
        f_ref[r:r + rows, :] = lax.dot_general(u, wf_ref[...], _NT, preferred_element_type=F32)
    f = f_ref[...] + b_ref[...]
    ls = (jnp.minimum(f, 0.0) - jnp.log1p(jnp.exp(-jnp.abs(f)))) * LOG2E
    row = lax.broadcasted_iota(jnp.int32, (tm, tm), 0)
    col = lax.broadcasted_iota(jnp.int32, (tm, tm), 1)
    tri = jnp.where(col <= row, 1.0, 0.0).astype(BF16)
    cs = sum(jnp.dot(tri, piece, preferred_element_type=F32) for piece in _split3(ls))
    carry = carry_ref[...]
    c_ref[...] = carry
    carry_ref[...] = carry + cs[tm - 1:tm, :]
    head_lane = lax.broadcasted_iota(jnp.int32, (tm, LANES), 1) < heads
    e = sum(pltpu.roll(jnp.where(head_lane, piece.astype(F32), 0.0), p * heads, 1) if p else
            jnp.where(head_lane, piece.astype(F32), 0.0)
            for p, piece in enumerate(_split3(-cs)))
    e_ref[...] = e.astype(e_ref.dtype)


def _ln_mod_fcum(x, mod, shift_idx, scale_idx, wf, b_row, *, heads, tm, rows=128):
    s, d = x.shape
    nb = s // tm
    assert BIAS_PIECES * heads <= LANES
    vec = lambda idx: pl.BlockSpec((1, d), lambda i: (0, idx))
    const = lambda shape: pl.BlockSpec(shape, lambda i: (0, 0))
    return pl.pallas_call(
        functools.partial(_ln_mod_fcum_kernel, tm=tm, rows=rows, heads=heads),
        out_shape=(jax.ShapeDtypeStruct((s, d), BF16),
                   jax.ShapeDtypeStruct((s, LANES), BF16),
                   jax.ShapeDtypeStruct((nb, 1, LANES), F32)),
        grid=(nb,),
        in_specs=[pl.BlockSpec((tm, d), lambda i: (i, 0)), vec(shift_idx), vec(scale_idx),
                  const((LANES, d)), const((1, LANES))],
        out_specs=(pl.BlockSpec((tm, d), lambda i: (i, 0)),
                   pl.BlockSpec((tm, LANES), lambda i: (i, 0)),
                   pl.BlockSpec((None, 1, LANES), lambda i: (i, 0, 0))),
        scratch_shapes=[pltpu.VMEM((1, LANES), F32), pltpu.VMEM((tm, LANES), F32)],
        compiler_params=_params(("arbitrary",),
                                _nbytes((tm, d), F32) + _nbytes((tm, d), BF16)
                                + _nbytes((tm, LANES), BF16) + _nbytes((d, LANES), BF16),
                                4 * _nbytes((rows, d), F32) + 2 * _nbytes((tm, tm), F32)),
        name="ln_mod_fcum",
    )(x, mod, mod, wf, b_row)


def _wt_spec(tn, k, row0):
    assert row0 % SUBLANES == 0 and tn % SUBLANES == 0
    return pl.BlockSpec((pl.Element(tn), pl.Element(k)),
                        lambda i, j: ((row0 // SUBLANES + j * (tn // SUBLANES)) * SUBLANES, 0))


def _mm_nt_kernel(a_ref, w_ref, o_ref, *, scale, n_scaled, sigmoid):
    w = _as_bf16(w_ref[...])
    tm = a_ref.shape[0]
    chunk = tm // 4 if sigmoid else tm
    for r in range(0, tm, chunk):
        acc = lax.dot_general(a_ref[r:r + chunk, :], w, _NT, preferred_element_type=F32)
        if scale is not None:
            acc = acc * jnp.where(pl.program_id(1) < n_scaled, scale, 1.0)
        if sigmoid:
            acc = jax.nn.sigmoid(acc)
        o_ref[r:r + chunk, :] = acc.astype(o_ref.dtype)


def _matmul_nt(a, wt, row0, n, out_dtype, *, tm, tn, scale=None, n_scaled=0, sigmoid=False, name):
    m, k = a.shape
    return pl.pallas_call(
        functools.partial(_mm_nt_kernel, scale=scale, n_scaled=n_scaled, sigmoid=sigmoid),
        out_shape=jax.ShapeDtypeStruct((m, n), out_dtype),
        grid=(m // tm, n // tn),
        in_specs=[pl.BlockSpec((tm, k), lambda i, j: (i, 0)), _wt_spec(tn, k, row0)],
        out_specs=pl.BlockSpec((tm, tn), lambda i, j: (i, j)),
        compiler_params=_params(("parallel", "parallel"),
                                _nbytes((tm, k), a.dtype) + _nbytes((tn, k), wt.dtype)
                                + _nbytes((tm, tn), out_dtype),
                                _cast_bytes((tn, k), wt.dtype) + 2 * _nbytes((tm, tn), F32)),
        name=name,
    )(a, wt)


def _mm_tn_kernel(a_ref, w_ref, o_ref):
    acc_t = lax.dot_general(_as_bf16(w_ref[...]), a_ref[...], _NT,
                            preferred_element_type=F32)
    tk = o_ref.shape[2]
    for kb in range(o_ref.shape[0]):
        o_ref[kb] = acc_t[:, kb * tk:(kb + 1) * tk].astype(o_ref.dtype)


def _matmul_tn(a, wt, row0, n, *, tm, tn, tk, name):
    m, k = a.shape
    return pl.pallas_call(
        _mm_tn_kernel,
        out_shape=jax.ShapeDtypeStruct((m // tk, n, tk), BF16),
        grid=(m // tm, n // tn),
        in_specs=[pl.BlockSpec((tm, k), lambda i, j: (i, 0)), _wt_spec(tn, k, row0)],
        out_specs=pl.BlockSpec((tm // tk, tn, tk), lambda i, j: (i, j, 0)),
        compiler_params=_params(("parallel", "parallel"),
                                _nbytes((tm, k), a.dtype) + _nbytes((tn, k), wt.dtype)
                                + _nbytes((tm, tn), BF16),
                                _cast_bytes((tn, k), wt.dtype) + 3 * _nbytes((tm, tn), F32)),
        name=name,
    )(a, wt)


def _attn_kernel(cb_ref, q_ref, k_ref, e_ref, vt_ref, o_ref, s0_ref, s1_ref, acc_ref, *,
                 tk, tn, heads):
    h = pl.program_id(0)
    i = pl.program_id(1)
    tq, dh = q_ref.shape
    nt = tq // tn
    r = lax.broadcasted_iota(jnp.int32, (dh, tq), 0) - h
    ones_rows = sum(jnp.where(r == p * heads, 1.0, 0.0) for p in range(BIAS_PIECES))
    qt_aug = jnp.concatenate([q_ref[...].astype(F32).T.astype(BF16), ones_rows.astype(BF16)],
                             axis=0)
    ones_v = jnp.ones((acc_ref.shape[0] - dh, tk), BF16)
    c_q = cb_ref[h, 2 * i]
    acc_ref[...] = jnp.zeros_like(acc_ref)

    def logits_to(buf_ref, j, first_tile=0):
        rows = pl.ds(pl.multiple_of(j * tk, tk), tk)
        k_aug = jnp.concatenate([k_ref[rows, :], e_ref[rows, :]], axis=1)
        cms = []
        for n in range(first_tile, nt):
            cols = slice(n * tn, (n + 1) * tn)
            s = jnp.dot(k_aug, qt_aug[:, cols], preferred_element_type=F32)
            buf_ref[:, cols] = s
            cms.append(jnp.max(s, axis=0, keepdims=True))
        return jnp.concatenate(cms, axis=1)

    def masked(buf_ref, n, first_tile):
        key = lax.broadcasted_iota(jnp.int32, (tk, tn), 0)
        qry = lax.broadcasted_iota(jnp.int32, (tk, tn), 1) + (n - first_tile) * tn
        return jnp.where(key <= qry, buf_ref[:, n * tn:(n + 1) * tn], -jnp.inf)

    def softmax_pv(buf_ref, j, cmax, m, first_tile=0, diagonal=False):
        lo = first_tile * tn
        c = c_q - cb_ref[h, j]
        m_old = m[:, lo:]
        m_new = jnp.maximum(m_old, cmax + c)
        a = jnp.exp2(m_old - m_new)
        off = m_new - c
        v_aug = jnp.concatenate([vt_ref[j], ones_v], axis=0)
        for n in range(first_tile, nt):
            cols = slice(n * tn, (n + 1) * tn)
            rel = slice(n * tn - lo, (n + 1) * tn - lo)
            s = masked(buf_ref, n, first_tile) if diagonal else buf_ref[:, cols]
            p = jnp.exp2(s - off[:, rel]).astype(BF16)
            acc_ref[:, cols] = a[:, rel] * acc_ref[:, cols] + jnp.dot(
                v_aug, p, preferred_element_type=F32)
        if first_tile:
            m_new = jnp.concatenate([m[:, :lo], m_new], axis=1)
        return m_new

    def pair(p, carry):
        m, cm0 = carry
        cm1 = logits_to(s1_ref, 2 * p + 1)
        m = softmax_pv(s0_ref, 2 * p, cm0, m)
        cm0 = logits_to(s0_ref, 2 * p + 2)
        m = softmax_pv(s1_ref, 2 * p + 1, cm1, m)
        return m, cm0

    init = (jnp.full((1, tq), -jnp.inf, F32), logits_to(s0_ref, 0))
    m, _ = lax.fori_loop(0, i, pair, init)
    half = nt // 2
    logits_to(s1_ref, 2 * i + 1, first_tile=half)
    cm = jnp.concatenate([jnp.max(masked(s0_ref, n, 0), axis=0, keepdims=True) for n in range(nt)],
                         axis=1)
    m = softmax_pv(s0_ref, 2 * i, cm, m, diagonal=True)
    cm = jnp.concatenate([jnp.max(masked(s1_ref, n, half), axis=0, keepdims=True)
                          for n in range(half, nt)], axis=1)
    softmax_pv(s1_ref, 2 * i + 1, cm, m, first_tile=half, diagonal=True)
    o_ref[...] = (acc_ref[:dh, :] / acc_ref[dh:dh + 1, :]).T.astype(o_ref.dtype)


def _attention(cb, qk, e, vt, *, heads, tk, tn=256):
    s = qk.shape[0]
    dh = FOX_HEAD_DIM
    tq = 2 * tk
    nq = s // tq
    grid_spec = pltpu.PrefetchScalarGridSpec(
        num_scalar_prefetch=1,
        grid=(heads, nq),
        in_specs=[pl.BlockSpec((tq, dh), lambda h, i, cb_ref: (i, h)),
                  pl.BlockSpec((s, dh), lambda h, i, cb_ref: (0, heads + h)),
                  pl.BlockSpec((s, LANES), lambda h, i, cb_ref: (0, 0)),
                  pl.BlockSpec((s // tk, dh, tk), lambda h, i, cb_ref: (0, h, 0))],
        out_specs=pl.BlockSpec((tq, dh), lambda h, i, cb_ref: (i, h)),
        scratch_shapes=[pltpu.VMEM((tk, tq), F32), pltpu.VMEM((tk, tq), F32),
                        pltpu.VMEM((dh + ONES_ROWS, tq), F32)],
    )
    return pl.pallas_call(
        functools.partial(_attn_kernel, tk=tk, tn=tn, heads=heads),
        out_shape=jax.ShapeDtypeStruct((s, heads * dh), BF16),
        grid_spec=grid_spec,
        compiler_params=_params(("parallel", "arbitrary"),
                                3 * _nbytes((s, dh), BF16) + 2 * _nbytes((tq, dh), BF16),
                                6 * _nbytes((tk, tq), F32)),
        name="fox_attention",
    )(cb, qk, qk, e, vt)


def _mix_kernel(a_ref, wa_ref, halo_ref, p_ref, wp_ref, ps_ref, ga_ref, gp_ref, o_ref,
                wa_b, wp_b, *, tm):
    j = pl.program_id(0)
    i = pl.program_id(1)

    @pl.when(i == 0)
    def _():
        wa_b[...] = wa_ref[...].astype(BF16)
        wp_b[...] = wp_ref[...].astype(BF16)

    halo = jnp.where(i == 0, 0.0, halo_ref[...])
    ext = jnp.concatenate([halo, p_ref[...]], axis=0)
    acc = ext
    win = ext
    for g, w in enumerate(POOL_WINDOWS):
        acc = acc + pltpu.roll(acc, w // 2, 0)
        win = jnp.where(j == g, acc, win)
    t = i * tm + lax.broadcasted_iota(jnp.int32, (tm, 1), 0)
    cnt = jnp.minimum(t + 1, jnp.left_shift(2, j)).astype(F32)
    pooled = (win[POOL_HALO:] / cnt - ext[POOL_HALO:]).astype(BF16)

    ya = jnp.dot(a_ref[...], wa_b[...], preferred_element_type=F32)
    yp = jnp.dot(pooled, wp_b[...], preferred_element_type=F32) * ps_ref[...]
    o_ref[...] = (ga_ref[...].astype(F32) * ya + gp_ref[...].astype(F32) * yp).astype(o_ref.dtype)


def _mix(attn, w_a, p, w_pool, pool_scale, gates, *, tm=512):
    s, fw = attn.shape
    groups, gd, tn = w_pool.shape
    assert POOL_WINDOWS == tuple(2 << g for g in range(groups))
    d = w_a.shape[1]
    hb = tm // POOL_HALO
    return pl.pallas_call(
        functools.partial(_mix_kernel, tm=tm),
        out_shape=jax.ShapeDtypeStruct((s, d), BF16),
        grid=(groups, s // tm),
        in_specs=[pl.BlockSpec((tm, fw), lambda j, i: (i, 0)),
                  pl.BlockSpec((fw, tn), lambda j, i: (0, j)),
                  pl.BlockSpec((POOL_HALO, gd), lambda j, i: (jnp.maximum(i * hb - 1, 0), j)),
                  pl.BlockSpec((tm, gd), lambda j, i: (i, j)),
                  pl.BlockSpec((None, gd, tn), lambda j, i: (j, 0, 0)),
                  pl.BlockSpec((1, tn), lambda j, i: (0, j)),
                  pl.BlockSpec((tm, tn), lambda j, i: (i, j)),
                  pl.BlockSpec((tm, tn), lambda j, i: (i, groups + j))],
        out_specs=pl.BlockSpec((tm, tn), lambda j, i: (i, j)),
        scratch_shapes=[pltpu.VMEM((fw, tn), BF16), pltpu.VMEM((gd, tn), BF16)],
        compiler_params=_params(("parallel", "arbitrary"),
                                _nbytes((tm, fw), BF16) + _nbytes((fw, tn), w_a.dtype)
                                + _nbytes((tm + POOL_HALO, gd), F32) + _nbytes((gd, tn), w_pool.dtype)
                                + 3 * _nbytes((tm, tn), BF16),
                                _nbytes((fw + gd, tn), BF16) + 3 * _nbytes((tm, tn), F32)
                                + 4 * _nbytes((tm + POOL_HALO, gd), F32)),
        name="branch_mix",
    )(attn, w_a, p, p, w_pool, pool_scale, gates, gates)


def _resid_mm_kernel(a_ref, w_ref, x_ref, g_ref, o_ref):
    m = jnp.dot(a_ref[...], _as_bf16(w_ref[...]), preferred_element_type=F32)
    o_ref[...] = ALPHA * x_ref[...] + g_ref[...] * m


def _resid_matmul(a, w, x, mod, gate_idx, *, tm, tn, name):
    m, k = a.shape
    n = w.shape[1]
    nb = n // tn
    return pl.pallas_call(
        _resid_mm_kernel,
        out_shape=jax.ShapeDtypeStruct((m, n), F32),
        grid=(m // tm, nb),
        in_specs=[pl.BlockSpec((tm, k), lambda i, j: (i, 0)),
                  pl.BlockSpec((k, tn), lambda i, j: (0, j)),
                  pl.BlockSpec((tm, tn), lambda i, j: (i, j)),
                  pl.BlockSpec((1, tn), lambda i, j: (0, gate_idx * nb + j))],
        out_specs=pl.BlockSpec((tm, tn), lambda i, j: (i, j)),
        compiler_params=_params(("parallel", "parallel"),
                                _nbytes((tm, k), BF16) + _nbytes((k, tn), w.dtype)
                                + 2 * _nbytes((tm, tn), F32),
                                _cast_bytes((k, tn), w.dtype) + 2 * _nbytes((tm, tn), F32)),
        name=name,
    )(a, w, x, mod)


def _ln_ln_mod_kernel(r_ref, g_ref, b_ref, sh_ref, sc_ref, x_ref, u_ref):
    x1 = _ln_rows(r_ref[...]) * g_ref[...] + b_ref[...]
    x_ref[...] = x1
    u_ref[...] = (_ln_rows(x1) * (1.0 + sc_ref[...]) + sh_ref[...]).astype(u_ref.dtype)


def _ln_ln_mod(r, gain, bias, mod, shift_idx, scale_idx, *, tm=256):
    s, d = r.shape
    row = pl.BlockSpec((tm, d), lambda i: (i, 0))
    vec = pl.BlockSpec((1, d), lambda i: (0, 0))
    return pl.pallas_call(
        _ln_ln_mod_kernel,
        out_shape=(jax.ShapeDtypeStruct((s, d), F32), jax.ShapeDtypeStruct((s, d), BF16)),
        grid=(s // tm,),
        in_specs=[row, vec, vec,
                  pl.BlockSpec((1, d), lambda i: (0, shift_idx)),
                  pl.BlockSpec((1, d), lambda i: (0, scale_idx))],
        out_specs=(row, row),
        compiler_params=_params(("parallel",),
                                2 * _nbytes((tm, d), F32) + _nbytes((tm, d), BF16),
                                4 * _nbytes((tm, d), F32)),
        name="ln1_ln_mod",
    )(r, gain, bias, mod, mod)


def _ln_affine_kernel(r_ref, g_ref, b_ref, o_ref):
    o_ref[...] = _ln_rows(r_ref[...]) * g_ref[...] + b_ref[...]


def _ln_affine(r, gain, bias, *, tm=256):
    s, d = r.shape
    row = pl.BlockSpec((tm, d), lambda i: (i, 0))
    vec = pl.BlockSpec((1, d), lambda i: (0, 0))
    return pl.pallas_call(
        _ln_affine_kernel,
        out_shape=jax.ShapeDtypeStruct((s, d), F32),
        grid=(s // tm,),
        in_specs=[row, vec, vec],
        out_specs=row,
        compiler_params=_params(("parallel",), 2 * _nbytes((tm, d), F32),
                                3 * _nbytes((tm, d), F32)),
        name="ln2",
    )(r, gain, bias)


def _ffn_up_kernel(u_ref, wg_ref, wu_ref, o_ref):
    u = u_ref[...]
    g = jnp.dot(u, _as_bf16(wg_ref[...]), preferred_element_type=F32)
    up = jnp.dot(u, _as_bf16(wu_ref[...]), preferred_element_type=F32)
    o_ref[...] = (g * jax.nn.sigmoid(g) * up).astype(o_ref.dtype)


def _ffn_up(u, w_gate_up, *, tm=1024, tn=256):
    s, d = u.shape
    hidden = w_gate_up.shape[1] // 2
    nb = hidden // tn
    return pl.pallas_call(
        _ffn_up_kernel,
        out_shape=jax.ShapeDtypeStruct((s, hidden), BF16),
        grid=(s // tm, nb),
        in_specs=[pl.BlockSpec((tm, d), lambda i, j: (i, 0)),
                  pl.BlockSpec((d, tn), lambda i, j: (0, j)),
                  pl.BlockSpec((d, tn), lambda i, j: (0, nb + j))],
        out_specs=pl.BlockSpec((tm, tn), lambda i, j: (i, j)),
        compiler_params=_params(("parallel", "parallel"),
                                _nbytes((tm, d), BF16) + 2 * _nbytes((d, tn), w_gate_up.dtype)
                                + _nbytes((tm, tn), BF16),
                                2 * _nbytes((d, tn), BF16) + 4 * _nbytes((tm, tn), F32)),
        name="ffn_up",
    )(u, w_gate_up, w_gate_up)


def kernel(x, c, w_ada, b_ada, w_in, b_forget, w_attn_out, w_pool, pool_scale, w_out,
           ln1_g, ln1_b, w_gate_up, w_down, ln2_g, ln2_b):
    batch, seq, d = x.shape
    assert batch == 1 and w_ada.shape[0] == DEPTH == 1
    fox_w = w_attn_out.shape[1]
    heads = fox_w // FOX_HEAD_DIM
    pool_w = w_pool.shape[1] * w_pool.shape[2]
    o_f = 3 * fox_w
    o_p = o_f + heads
    tk = 512

    xs = x[0]
    w_in_t = jnp.swapaxes(w_in, 1, 2)[0]
    w_f = jnp.pad(w_in_t[o_f:o_p], ((0, LANES - heads), (0, 0))).astype(BF16)
    b_f = jnp.pad(b_forget, ((0, 0), (0, LANES - heads)))

    mod = _ada(c.reshape(d, 1), w_ada[0], b_ada)

    u1, e, c_blk = _ln_mod_fcum(xs, mod, 0, 1, w_f, b_f, heads=heads, tm=tk)
    cb = c_blk[:, 0, :heads].T
    qk = _matmul_nt(u1, w_in_t, 0, 2 * fox_w, BF16, tm=1024, tn=512,
                    scale=FOX_HEAD_DIM ** -0.5 * LOG2E, n_scaled=fox_w // 512, name="proj_qk")
    vt = _matmul_tn(u1, w_in_t, 2 * fox_w, fox_w, tm=1024, tn=512, tk=tk, name="proj_vt")
    p = _matmul_nt(u1, w_in_t, o_p, pool_w, F32, tm=1024, tn=512, name="proj_pool")
    gates = _matmul_nt(u1, w_in_t, o_p + pool_w, 2 * d, BF16, tm=1024, tn=512, sigmoid=True,
                       name="proj_gates")
    attn = _attention(cb, qk, e, vt, heads=heads, tk=tk)
    mix = _mix(attn, w_attn_out[0], p, w_pool[0], pool_scale, gates)
    r1 = _resid_matmul(mix, w_out[0], xs, mod, 2, tm=1024, tn=512, name="out_proj")
    x1, u2 = _ln_ln_mod(r1, ln1_g, ln1_b, mod, 3, 4)

    act = _ffn_up(u2, w_gate_up[0])
    r2 = _resid_matmul(act, w_down[0].astype(BF16), x1, mod, 5, tm=512, tn=512, name="ffn_down")
    out = _ln_affine(r2, ln2_g, ln2_b)
    return out[None]
```

```python
import functools

import jax
import jax.numpy as jnp
from jax import lax
from jax.experimental import pallas as pl
from jax.experimental.pallas import tpu as pltpu

F32 = jnp.float32
BF16 = jnp.bfloat16

FOX_HEAD_DIM = 128
POOL_WINDOWS = (2, 4, 8, 16)
POOL_HALO = 16
DEPTH = 1
ALPHA = (2 * DEPTH) ** 0.25
LN_EPS = 1e-5
LOG2E = 1.4426950408889634
LANES = 128
SUBLANES = 8
BIAS_PIECES = 3
ONES_ROWS = 16
SKIP_LOG2 = 48.0
NORM_SLACK = 1.02

V7X_VMEM_BYTES = 64 * 1024 * 1024
VMEM_CAP_BYTES = V7X_VMEM_BYTES - 6 * 1024 * 1024

_NT = (((1,), (1,)), ((), ()))


def _nbytes(shape, dtype):
    n = jnp.dtype(dtype).itemsize
    for s in shape:
        n *= s
    return n


def _cast_bytes(shape, dtype):
    return 0 if dtype == BF16 else _nbytes(shape, BF16)


def _params(semantics, pipelined_bytes, resident_bytes=0):
    need = 2 * pipelined_bytes + resident_bytes
    assert need <= VMEM_CAP_BYTES, (need, VMEM_CAP_BYTES)
    return pltpu.CompilerParams(dimension_semantics=semantics,
                                vmem_limit_bytes=min(need + (4 << 20), VMEM_CAP_BYTES))


def _ln_rows(x):
    mu = jnp.mean(x, axis=-1, keepdims=True)
    xc = x - mu
    var = jnp.mean(xc * xc, axis=-1, keepdims=True)
    return xc * lax.rsqrt(var + LN_EPS)


def _as_bf16(w):
    return w if w.dtype == BF16 else w.astype(BF16)


def _split3(v):
    hi = v.astype(BF16)
    r1 = v - hi.astype(F32)
    mid = r1.astype(BF16)
    lo = (r1 - mid.astype(F32)).astype(BF16)
    return hi, mid, lo


def _ada_kernel(c_ref, w_ref, b_ref, o_ref):
    k = pl.program_id(1)

    @pl.when(k == 0)
    def _():
        o_ref[...] = b_ref[...]

    cc = c_ref[...]
    s = cc * jax.nn.sigmoid(cc)
    o_ref[...] += jnp.sum(s * w_ref[...], axis=0, keepdims=True)


def _ada(c_col, w_ada, b_ada, *, tk=512, tn=4096):
    d, n = w_ada.shape
    return pl.pallas_call(
        _ada_kernel,
        out_shape=jax.ShapeDtypeStruct((1, n), F32),
        grid=(n // tn, d // tk),
        in_specs=[pl.BlockSpec((tk, 1), lambda j, k: (k, 0)),
                  pl.BlockSpec((tk, tn), lambda j, k: (k, j)),
                  pl.BlockSpec((1, tn), lambda j, k: (0, j))],
        out_specs=pl.BlockSpec((1, tn), lambda j, k: (0, j)),
        compiler_params=_params(("parallel", "arbitrary"),
                                _nbytes((tk, tn), F32) + _nbytes((tk, LANES), F32),
                                _nbytes((tk, tn), F32)),
        name="ada",
    )(c_col, w_ada, b_ada)


def _ln_mod_fcum_kernel(x_ref, sh_ref, sc_ref, wf_ref, b_ref, u_ref, e_ref, c_ref,
                        carry_ref, f_ref, *, tm, rows, heads):
    @pl.when(pl.program_id(0) == 0)
    def _():
        carry_ref[...] = jnp.zeros_like(carry_ref)

    gain = 1.0 + sc_ref[...]
    shift = sh_ref[...]
    for r in range(0, tm, rows):
        u = (_ln_rows(x_ref[r:r + rows, :]) * gain + shift).astype(BF16)
        u_ref[r:r + rows, :] = u
        f_ref[r:r + rows, :] = lax.dot_general(u, wf_ref[...], _NT, preferred_element_type=F32)
    f = f_ref[...] + b_ref[...]
    ls = (jnp.minimum(f, 0.0) - jnp.log1p(jnp.exp(-jnp.abs(f)))) * LOG2E
    row = lax.broadcasted_iota(jnp.int32, (tm, tm), 0)
    col = lax.broadcasted_iota(jnp.int32, (tm, tm), 1)
    tri = jnp.where(col <= row, 1.0, 0.0).astype(BF16)
    cs = sum(jnp.dot(tri, piece, preferred_element_type=F32) for piece in _split3(ls))
    carry = carry_ref[...]
    c_ref[...] = carry
    carry_ref[...] = carry + cs[tm - 1:tm, :]
    head_lane = lax.broadcasted_iota(jnp.int32, (tm, LANES), 1) < heads
    e = sum(pltpu.roll(jnp.where(head_lane, piece.astype(F32), 0.0), p * heads, 1) if p else
            jnp.where(head_lane, piece.astype(F32), 0.0)
            for p, piece in enumerate(_split3(-cs)))
    e_ref[...] = e.astype(e_ref.dtype)


def _ln_mod_fcum(x, mod, shift_idx, scale_idx, wf, b_row, *, heads, tm, rows=128):
    s, d = x.shape
    nb = s // tm
    assert BIAS_PIECES * heads <= LANES
    vec = lambda idx: pl.BlockSpec((1, d), lambda i: (0, idx))
    const = lambda shape: pl.BlockSpec(shape, lambda i: (0, 0))
    return pl.pallas_call(
        functools.partial(_ln_mod_fcum_kernel, tm=tm, rows=rows, heads=heads),
        out_shape=(jax.ShapeDtypeStruct((s, d), BF16),
                   jax.ShapeDtypeStruct((s, LANES), BF16),
                   jax.ShapeDtypeStruct((nb, 1, LANES), F32)),
        grid=(nb,),
        in_specs=[pl.BlockSpec((tm, d), lambda i: (i, 0)), vec(shift_idx), vec(scale_idx),
                  const((LANES, d)), const((1, LANES))],
        out_specs=(pl.BlockSpec((tm, d), lambda i: (i, 0)),
                   pl.BlockSpec((tm, LANES), lambda i: (i, 0)),
                   pl.BlockSpec((None, 1, LANES), lambda i: (i, 0, 0))),
        scratch_shapes=[pltpu.VMEM((1, LANES), F32), pltpu.VMEM((tm, LANES), F32)],
        compiler_params=_params(("arbitrary",),
                                _nbytes((tm, d), F32) + _nbytes((tm, d), BF16)
                                + _nbytes((tm, LANES), BF16) + _nbytes((d, LANES), BF16),
                                4 * _nbytes((rows, d), F32) + 2 * _nbytes((tm, tm), F32)),
        name="ln_mod_fcum",
    )(x, mod, mod, wf, b_row)


def _wt_spec(tn, k, row0):
    assert row0 % SUBLANES == 0 and tn % SUBLANES == 0
    return pl.BlockSpec((pl.Element(tn), pl.Element(k)),
                        lambda i, j: ((row0 // SUBLANES + j * (tn // SUBLANES)) * SUBLANES, 0))


def _mm_nt_kernel(a_ref, w_ref, o_ref, *rest, scale, n_scaled, sigmoid):
    w = _as_bf16(w_ref[...])
    tm = a_ref.shape[0]
    chunk = tm // 4 if sigmoid or rest else tm
    lane = lax.broadcasted_iota(jnp.int32, (1, LANES), 1)
    nrm = jnp.zeros((1, LANES), F32)
    for r in range(0, tm, chunk):
        acc = lax.dot_general(a_ref[r:r + chunk, :], w, _NT, preferred_element_type=F32)
        if scale is not None:
            acc = acc * jnp.where(pl.program_id(1) < n_scaled, scale, 1.0)
        if sigmoid:
            acc = jax.nn.sigmoid(acc)
        o_ref[r:r + chunk, :] = acc.astype(o_ref.dtype)
        if rest:
            for g in range(acc.shape[1] // LANES):
                sq = jnp.sum(jnp.square(acc[:, g * LANES:(g + 1) * LANES]), axis=1, keepdims=True)
                nrm = jnp.maximum(nrm, jnp.where(lane == g, jnp.max(sq, axis=0, keepdims=True), 0.0))
    if rest:
        rest[0][...] = nrm


def _matmul_nt(a, wt, row0, n, out_dtype, *, tm, tn, scale=None, n_scaled=0, sigmoid=False,
               group_norms=False, name):
    m, k = a.shape
    assert not (group_norms and sigmoid)
    out_shape = [jax.ShapeDtypeStruct((m, n), out_dtype)]
    out_specs = [pl.BlockSpec((tm, tn), lambda i, j: (i, j))]
    if group_norms:
        out_shape.append(jax.ShapeDtypeStruct((m // tm, n // tn, 1, LANES), F32))
        out_specs.append(pl.BlockSpec((None, None, 1, LANES), lambda i, j: (i, j, 0, 0)))
    out = pl.pallas_call(
        functools.partial(_mm_nt_kernel, scale=scale, n_scaled=n_scaled, sigmoid=sigmoid),
        out_shape=out_shape,
        grid=(m // tm, n // tn),
        in_specs=[pl.BlockSpec((tm, k), lambda i, j: (i, 0)), _wt_spec(tn, k, row0)],
        out_specs=out_specs,
        compiler_params=_params(("parallel", "parallel"),
                                _nbytes((tm, k), a.dtype) + _nbytes((tn, k), wt.dtype)
                                + _nbytes((tm, tn), out_dtype),
                                _cast_bytes((tn, k), wt.dtype) + 3 * _nbytes((tm, tn), F32)),
        name=name,
    )(a, wt)
    return out if group_norms else out[0]


def _mm_tn_kernel(a_ref, w_ref, o_ref):
    acc_t = lax.dot_general(_as_bf16(w_ref[...]), a_ref[...], _NT,
                            preferred_element_type=F32)
    tk = o_ref.shape[2]
    for kb in range(o_ref.shape[0]):
        o_ref[kb] = acc_t[:, kb * tk:(kb + 1) * tk].astype(o_ref.dtype)


def _matmul_tn(a, wt, row0, n, *, tm, tn, tk, name):
    m, k = a.shape
    return pl.pallas_call(
        _mm_tn_kernel,
        out_shape=jax.ShapeDtypeStruct((m // tk, n, tk), BF16),
        grid=(m // tm, n // tn),
        in_specs=[pl.BlockSpec((tm, k), lambda i, j: (i, 0)), _wt_spec(tn, k, row0)],
        out_specs=pl.BlockSpec((tm // tk, tn, tk), lambda i, j: (i, j, 0)),
        compiler_params=_params(("parallel", "parallel"),
                                _nbytes((tm, k), a.dtype) + _nbytes((tn, k), wt.dtype)
                                + _nbytes((tm, tn), BF16),
                                _cast_bytes((tn, k), wt.dtype) + 3 * _nbytes((tm, tn), F32)),
        name=name,
    )(a, wt)


def _attn_kernel(cb_ref, thr_ref, q_ref, k_ref, e_ref, vt_ref, o_ref, s0_ref, s1_ref, acc_ref, *,
                 tk, tn, heads):
    h = pl.program_id(0)
    i = pl.program_id(1)
    tq, dh = q_ref.shape
    nt = tq // tn
    half = nt // 2
    r = lax.broadcasted_iota(jnp.int32, (dh, tq), 0) - h
    ones_rows = sum(jnp.where(r == p * heads, 1.0, 0.0) for p in range(BIAS_PIECES))
    qt_aug = jnp.concatenate([q_ref[...].astype(F32).T.astype(BF16), ones_rows.astype(BF16)],
                             axis=0)
    ones_v = jnp.ones((acc_ref.shape[0] - dh, tk), BF16)
    c_q = cb_ref[h, 2 * i]
    thr = thr_ref[h, i]
    acc_ref[...] = jnp.zeros_like(acc_ref)

    def logits_to(buf_ref, j, first_tile=0):
        rows = pl.ds(pl.multiple_of(j * tk, tk), tk)
        k_aug = jnp.concatenate([k_ref[rows, :], e_ref[rows, :]], axis=1)
        cms = []
        for n in range(first_tile, nt):
            cols = slice(n * tn, (n + 1) * tn)
            s = jnp.dot(k_aug, qt_aug[:, cols], preferred_element_type=F32)
            buf_ref[:, cols] = s
            cms.append(jnp.max(s, axis=0, keepdims=True))
        return jnp.concatenate(cms, axis=1)

    def causal(buf_ref, n, first_tile):
        key = lax.broadcasted_iota(jnp.int32, (tk, tn), 0)
        qry = lax.broadcasted_iota(jnp.int32, (tk, tn), 1) + (n - first_tile) * tn
        return jnp.where(key <= qry, buf_ref[:, n * tn:(n + 1) * tn], -jnp.inf)

    def softmax_pv(buf_ref, j, cmax, m, first_tile=0, causal_tiles=0):
        lo = first_tile * tn
        c = c_q - cb_ref[h, j]
        m_old = m[:, lo:]
        m_new = jnp.maximum(m_old, cmax + c)
        a = jnp.exp2(m_old - m_new)
        off = m_new - c
        v_aug = jnp.concatenate([vt_ref[j], ones_v], axis=0)
        for n in range(first_tile, nt):
            cols = slice(n * tn, (n + 1) * tn)
            rel = slice(n * tn - lo, (n + 1) * tn - lo)
            s = causal(buf_ref, n, first_tile) if n < first_tile + causal_tiles else buf_ref[:, cols]
            p = jnp.exp2(s - off[:, rel]).astype(BF16)
            acc_ref[:, cols] = a[:, rel] * acc_ref[:, cols] + jnp.dot(
                v_aug, p, preferred_element_type=F32)
        if first_tile:
            m_new = jnp.concatenate([m[:, :lo], m_new], axis=1)
        return m_new

    def causal_max(buf_ref, first_tile):
        return [jnp.max(causal(buf_ref, n, first_tile), axis=0, keepdims=True)
                for n in range(first_tile, first_tile + half)]

    def pair(p, carry):
        m, cm0 = carry
        cm1 = logits_to(s1_ref, 2 * p + 1)
        m = softmax_pv(s0_ref, 2 * p, cm0, m)
        cm0 = logits_to(s0_ref, 2 * p + 2)
        m = softmax_pv(s1_ref, 2 * p + 1, cm1, m)
        return m, cm0

    first = lax.while_loop(lambda p: (p < i) & (cb_ref[h, 2 * p + 2] - c_q > thr),
                           lambda p: p + 1, jnp.int32(0))
    init = (jnp.full((1, tq), -jnp.inf, F32), logits_to(s0_ref, 2 * first))
    m, cm0 = lax.fori_loop(first, i, pair, init)
    logits_to(s1_ref, 2 * i + 1, first_tile=half)
    cm = jnp.concatenate(causal_max(s0_ref, 0) + [cm0[:, half * tn:]], axis=1)
    m = softmax_pv(s0_ref, 2 * i, cm, m, causal_tiles=half)
    cm = jnp.concatenate(causal_max(s1_ref, half), axis=1)
    softmax_pv(s1_ref, 2 * i + 1, cm, m, first_tile=half, causal_tiles=half)
    o_ref[...] = (acc_ref[:dh, :] / acc_ref[dh:dh + 1, :]).T.astype(o_ref.dtype)


def _attention(cb, thr, qk, e, vt, *, heads, tk, tn=256):
    s = qk.shape[0]
    dh = FOX_HEAD_DIM
    tq = 2 * tk
    nq = s // tq
    grid_spec = pltpu.PrefetchScalarGridSpec(
        num_scalar_prefetch=2,
        grid=(heads, nq),
        in_specs=[pl.BlockSpec((tq, dh), lambda h, i, *_: (i, h)),
                  pl.BlockSpec((s, dh), lambda h, i, *_: (0, heads + h)),
                  pl.BlockSpec((s, LANES), lambda h, i, *_: (0, 0)),
                  pl.BlockSpec((s // tk, dh, tk), lambda h, i, *_: (0, h, 0))],
        out_specs=pl.BlockSpec((tq, dh), lambda h, i, *_: (i, h)),
        scratch_shapes=[pltpu.VMEM((tk, tq), F32), pltpu.VMEM((tk, tq), F32),
                        pltpu.VMEM((dh + ONES_ROWS, tq), F32)],
    )
    return pl.pallas_call(
        functools.partial(_attn_kernel, tk=tk, tn=tn, heads=heads),
        out_shape=jax.ShapeDtypeStruct((s, heads * dh), BF16),
        grid_spec=grid_spec,
        compiler_params=_params(("parallel", "arbitrary"),
                                3 * _nbytes((s, dh), BF16) + 2 * _nbytes((tq, dh), BF16),
                                6 * _nbytes((tk, tq), F32)),
        name="fox_attention",
    )(cb, thr, qk, qk, e, vt)


def _mix_kernel(a_ref, wa_ref, halo_ref, p_ref, wp_ref, ps_ref, ga_ref, gp_ref, o_ref,
                wa_b, wp_b, *, tm):
    j = pl.program_id(0)
    i = pl.program_id(1)

    @pl.when(i == 0)
    def _():
        wa_b[...] = wa_ref[...].astype(BF16)
        wp_b[...] = wp_ref[...].astype(BF16)

    halo = jnp.where(i == 0, 0.0, halo_ref[...])
    ext = jnp.concatenate([halo, p_ref[...]], axis=0)
    acc = ext
    win = ext
    for g, w in enumerate(POOL_WINDOWS):
        acc = acc + pltpu.roll(acc, w // 2, 0)
        win = jnp.where(j == g, acc, win)
    t = i * tm + lax.broadcasted_iota(jnp.int32, (tm, 1), 0)
    cnt = jnp.minimum(t + 1, jnp.left_shift(2, j)).astype(F32)
    pooled = (win[POOL_HALO:] / cnt - ext[POOL_HALO:]).astype(BF16)

    ya = jnp.dot(a_ref[...], wa_b[...], preferred_element_type=F32)
    yp = jnp.dot(pooled, wp_b[...], preferred_element_type=F32) * ps_ref[...]
    o_ref[...] = (ga_ref[...].astype(F32) * ya + gp_ref[...].astype(F32) * yp).astype(o_ref.dtype)


def _mix(attn, w_a, p, w_pool, pool_scale, gates, *, tm=512):
    s, fw = attn.shape
    groups, gd, tn = w_pool.shape
    assert POOL_WINDOWS == tuple(2 << g for g in range(groups))
    d = w_a.shape[1]
    hb = tm // POOL_HALO
    return pl.pallas_call(
        functools.partial(_mix_kernel, tm=tm),
        out_shape=jax.ShapeDtypeStruct((s, d), BF16),
        grid=(groups, s // tm),
        in_specs=[pl.BlockSpec((tm, fw), lambda j, i: (i, 0)),
                  pl.BlockSpec((fw, tn), lambda j, i: (0, j)),
                  pl.BlockSpec((POOL_HALO, gd), lambda j, i: (jnp.maximum(i * hb - 1, 0), j)),
                  pl.BlockSpec((tm, gd), lambda j, i: (i, j)),
                  pl.BlockSpec((None, gd, tn), lambda j, i: (j, 0, 0)),
                  pl.BlockSpec((1, tn), lambda j, i: (0, j)),
                  pl.BlockSpec((tm, tn), lambda j, i: (i, j)),
                  pl.BlockSpec((tm, tn), lambda j, i: (i, groups + j))],
        out_specs=pl.BlockSpec((tm, tn), lambda j, i: (i, j)),
        scratch_shapes=[pltpu.VMEM((fw, tn), BF16), pltpu.VMEM((gd, tn), BF16)],
        compiler_params=_params(("parallel", "arbitrary"),
                                _nbytes((tm, fw), BF16) + _nbytes((fw, tn), w_a.dtype)
                                + _nbytes((tm + POOL_HALO, gd), F32) + _nbytes((gd, tn), w_pool.dtype)
                                + 3 * _nbytes((tm, tn), BF16),
                                _nbytes((fw + gd, tn), BF16) + 3 * _nbytes((tm, tn), F32)
                                + 4 * _nbytes((tm + POOL_HALO, gd), F32)),
        name="branch_mix",
    )(attn, w_a, p, p, w_pool, pool_scale, gates, gates)


def _resid_mm_kernel(a_ref, w_ref, x_ref, g_ref, o_ref):
    m = jnp.dot(a_ref[...], _as_bf16(w_ref[...]), preferred_element_type=F32)
    o_ref[...] = ALPHA * x_ref[...] + g_ref[...] * m


def _resid_matmul(a, w, x, mod, gate_idx, *, tm, tn, name):
    m, k = a.shape
    n = w.shape[1]
    nb = n // tn
    return pl.pallas_call(
        _resid_mm_kernel,
        out_shape=jax.ShapeDtypeStruct((m, n), F32),
        grid=(m // tm, nb),
        in_specs=[pl.BlockSpec((tm, k), lambda i, j: (i, 0)),
                  pl.BlockSpec((k, tn), lambda i, j: (0, j)),
                  pl.BlockSpec((tm, tn), lambda i, j: (i, j)),
                  pl.BlockSpec((1, tn), lambda i, j: (0, gate_idx * nb + j))],
        out_specs=pl.BlockSpec((tm, tn), lambda i, j: (i, j)),
        compiler_params=_params(("parallel", "parallel"),
                                _nbytes((tm, k), BF16) + _nbytes((k, tn), w.dtype)
                                + 2 * _nbytes((tm, tn), F32),
                                _cast_bytes((k, tn), w.dtype) + 2 * _nbytes((tm, tn), F32)),
        name=name,
    )(a, w, x, mod)


def _ln_ln_mod_kernel(r_ref, g_ref, b_ref, sh_ref, sc_ref, x_ref, u_ref):
    x1 = _ln_rows(r_ref[...]) * g_ref[...] + b_ref[...]
    x_ref[...] = x1
    u_ref[...] = (_ln_rows(x1) * (1.0 + sc_ref[...]) + sh_ref[...]).astype(u_ref.dtype)


def _ln_ln_mod(r, gain, bias, mod, shift_idx, scale_idx, *, tm=256):
    s, d = r.shape
    row = pl.BlockSpec((tm, d), lambda i: (i, 0))
    vec = pl.BlockSpec((1, d), lambda i: (0, 0))
    return pl.pallas_call(
        _ln_ln_mod_kernel,
        out_shape=(jax.ShapeDtypeStruct((s, d), F32), jax.ShapeDtypeStruct((s, d), BF16)),
        grid=(s // tm,),
        in_specs=[row, vec, vec,
                  pl.BlockSpec((1, d), lambda i: (0, shift_idx)),
                  pl.BlockSpec((1, d), lambda i: (0, scale_idx))],
        out_specs=(row, row),
        compiler_params=_params(("parallel",),
                                2 * _nbytes((tm, d), F32) + _nbytes((tm, d), BF16),
                                4 * _nbytes((tm, d), F32)),
        name="ln1_ln_mod",
    )(r, gain, bias, mod, mod)


def _ln_affine_kernel(r_ref, g_ref, b_ref, o_ref):
    o_ref[...] = _ln_rows(r_ref[...]) * g_ref[...] + b_ref[...]


def _ln_affine(r, gain, bias, *, tm=256):
    s, d = r.shape
    row = pl.BlockSpec((tm, d), lambda i: (i, 0))
    vec = pl.BlockSpec((1, d), lambda i: (0, 0))
    return pl.pallas_call(
        _ln_affine_kernel,
        out_shape=jax.ShapeDtypeStruct((s, d), F32),
        grid=(s // tm,),
        in_specs=[row, vec, vec],
        out_specs=row,
        compiler_params=_params(("parallel",), 2 * _nbytes((tm, d), F32),
                                3 * _nbytes((tm, d), F32)),
        name="ln2",
    )(r, gain, bias)


def _ffn_up_kernel(u_ref, wg_ref, wu_ref, o_ref):
    u = u_ref[...]
    g = jnp.dot(u, _as_bf16(wg_ref[...]), preferred_element_type=F32)
    up = jnp.dot(u, _as_bf16(wu_ref[...]), preferred_element_type=F32)
    o_ref[...] = (g * jax.nn.sigmoid(g) * up).astype(o_ref.dtype)


def _ffn_up(u, w_gate_up, *, tm=1024, tn=256):
    s, d = u.shape
    hidden = w_gate_up.shape[1] // 2
    nb = hidden // tn
    return pl.pallas_call(
        _ffn_up_kernel,
        out_shape=jax.ShapeDtypeStruct((s, hidden), BF16),
        grid=(s // tm, nb),
        in_specs=[pl.BlockSpec((tm, d), lambda i, j: (i, 0)),
                  pl.BlockSpec((d, tn), lambda i, j: (0, j)),
                  pl.BlockSpec((d, tn), lambda i, j: (0, nb + j))],
        out_specs=pl.BlockSpec((tm, tn), lambda i, j: (i, j)),
        compiler_params=_params(("parallel", "parallel"),
                                _nbytes((tm, d), BF16) + 2 * _nbytes((d, tn), w_gate_up.dtype)
                                + _nbytes((tm, tn), BF16),
                                2 * _nbytes((d, tn), BF16) + 4 * _nbytes((tm, tn), F32)),
        name="ffn_up",
    )(u, w_gate_up, w_gate_up)


def kernel(x, c, w_ada, b_ada, w_in, b_forget, w_attn_out, w_pool, pool_scale, w_out,
           ln1_g, ln1_b, w_gate_up, w_down, ln2_g, ln2_b):
    batch, seq, d = x.shape
    assert batch == 1 and w_ada.shape[0] == DEPTH == 1
    fox_w = w_attn_out.shape[1]
    heads = fox_w // FOX_HEAD_DIM
    pool_w = w_pool.shape[1] * w_pool.shape[2]
    o_f = 3 * fox_w
    o_p = o_f + heads
    tk = 512

    xs = x[0]
    w_in_t = jnp.swapaxes(w_in, 1, 2)[0]
    w_f = jnp.pad(w_in_t[o_f:o_p], ((0, LANES - heads), (0, 0))).astype(BF16)
    b_f = jnp.pad(b_forget, ((0, 0), (0, LANES - heads)))

    mod = _ada(c.reshape(d, 1), w_ada[0], b_ada)

    u1, e, c_blk = _ln_mod_fcum(xs, mod, 0, 1, w_f, b_f, heads=heads, tm=tk)
    cb = c_blk[:, 0, :heads].T
    qk, sq = _matmul_nt(u1, w_in_t, 0, 2 * fox_w, BF16, tm=2 * tk, tn=512,
                        scale=FOX_HEAD_DIM ** -0.5 * LOG2E, n_scaled=fox_w // 512,
                        group_norms=True, name="proj_qk")
    nrm = jnp.sqrt(sq[:, :, 0, :512 // LANES].reshape(seq // (2 * tk), 2 * heads))
    thr = (SKIP_LOG2 + 2.0 * NORM_SLACK * nrm[:, :heads] * jnp.max(nrm[:, heads:], axis=0)).T
    vt = _matmul_tn(u1, w_in_t, 2 * fox_w, fox_w, tm=1024, tn=512, tk=tk, name="proj_vt")
    p = _matmul_nt(u1, w_in_t, o_p, pool_w, F32, tm=1024, tn=512, name="proj_pool")
    gates = _matmul_nt(u1, w_in_t, o_p + pool_w, 2 * d, BF16, tm=1024, tn=512, sigmoid=True,
                       name="proj_gates")
    attn = _attention(cb, thr, qk, e, vt, heads=heads, tk=tk)
    mix = _mix(attn, w_attn_out[0], p, w_pool[0], pool_scale, gates)
    r1 = _resid_matmul(mix, w_out[0], xs, mod, 2, tm=1024, tn=512, name="out_proj")
    x1, u2 = _ln_ln_mod(r1, ln1_g, ln1_b, mod, 3, 4)

    act = _ffn_up(u2, w_gate_up[0])
    r2 = _resid_matmul(act, w_down[0].astype(BF16), x1, mod, 5, tm=512, tn=512, name="ffn_down")
    out = _ln_affine(r2, ln2_g, ln2_b)
    return out[None]
```

```python
import functools

import jax
import jax.numpy as jnp
from jax import lax
from jax.experimental import pallas as pl
from jax.experimental.pallas import tpu as pltpu

F32 = jnp.float32
BF16 = jnp.bfloat16

FOX_HEAD_DIM = 128
POOL_WINDOWS = (2, 4, 8, 16)
POOL_HALO = 16
DEPTH = 1
ALPHA = (2 * DEPTH) ** 0.25
LN_EPS = 1e-5
LOG2E = 1.4426950408889634
LANES = 128
SUBLANES = 8
BIAS_PIECES = 3
ONES_ROWS = 16
SKIP_LOG2 = 48.0
NORM_SLACK = 1.02

V7X_VMEM_BYTES = 64 * 1024 * 1024
VMEM_CAP_BYTES = V7X_VMEM_BYTES - 6 * 1024 * 1024

_NT = (((1,), (1,)), ((), ()))


def _nbytes(shape, dtype):
    n = jnp.dtype(dtype).itemsize
    for s in shape:
        n *= s
    return n


def _cast_bytes(shape, dtype):
    return 0 if dtype == BF16 else _nbytes(shape, BF16)


def _params(semantics, pipelined_bytes, resident_bytes=0):
    need = 2 * pipelined_bytes + resident_bytes
    assert need <= VMEM_CAP_BYTES, (need, VMEM_CAP_BYTES)
    return pltpu.CompilerParams(dimension_semantics=semantics,
                                vmem_limit_bytes=min(need + (4 << 20), VMEM_CAP_BYTES))


def _held_rows_spec(tm, k):
    return pl.BlockSpec((tm, k), lambda i, j: (i, 0), pipeline_mode=pl.Buffered(1))


def _ln_rows(x):
    mu = jnp.mean(x, axis=-1, keepdims=True)
    xc = x - mu
    var = jnp.mean(xc * xc, axis=-1, keepdims=True)
    return xc * lax.rsqrt(var + LN_EPS)


def _as_bf16(w):
    return w if w.dtype == BF16 else w.astype(BF16)


def _split3(v):
    hi = v.astype(BF16)
    r1 = v - hi.astype(F32)
    mid = r1.astype(BF16)
    lo = (r1 - mid.astype(F32)).astype(BF16)
    return hi, mid, lo


def _ada_kernel(c_ref, w_ref, b_ref, o_ref):
    k = pl.program_id(1)

    @pl.when(k == 0)
    def _():
        o_ref[...] = b_ref[...]

    cc = c_ref[...]
    s = cc * jax.nn.sigmoid(cc)
    o_ref[...] += jnp.sum(s * w_ref[...], axis=0, keepdims=True)


def _ada(c_col, w_ada, b_ada, *, tk=512, tn=4096):
    d, n = w_ada.shape
    return pl.pallas_call(
        _ada_kernel,
        out_shape=jax.ShapeDtypeStruct((1, n), F32),
        grid=(n // tn, d // tk),
        in_specs=[pl.BlockSpec((tk, 1), lambda j, k: (k, 0)),
                  pl.BlockSpec((tk, tn), lambda j, k: (k, j)),
                  pl.BlockSpec((1, tn), lambda j, k: (0, j))],
        out_specs=pl.BlockSpec((1, tn), lambda j, k: (0, j)),
        compiler_params=_params(("parallel", "arbitrary"),
                                _nbytes((tk, tn), F32) + _nbytes((tk, LANES), F32),
                                _nbytes((tk, tn), F32)),
        name="ada",
    )(c_col, w_ada, b_ada)


def _ln_mod_fcum_kernel(x_ref, sh_ref, sc_ref, wf_ref, b_ref, u_ref, e_ref, c_ref,
                        carry_ref, f_ref, *, tm, rows, heads):
    @pl.when(pl.program_id(0) == 0)
    def _():
        carry_ref[...] = jnp.zeros_like(carry_ref)

    gain = 1.0 + sc_ref[...]
    shift = sh_ref[...]
    for r in range(0, tm, rows):
        u = (_ln_rows(x_ref[r:r + rows, :]) * gain + shift).astype(BF16)
        u_ref[r:r + rows, :] = u
        f_ref[r:r + rows, :] = lax.dot_general(u, wf_ref[...], _NT, preferred_element_type=F32)
    f = f_ref[...] + b_ref[...]
    ls = (jnp.minimum(f, 0.0) - jnp.log1p(jnp.exp(-jnp.abs(f)))) * LOG2E
    row = lax.broadcasted_iota(jnp.int32, (tm, tm), 0)
    col = lax.broadcasted_iota(jnp.int32, (tm, tm), 1)
    tri = jnp.where(col <= row, 1.0, 0.0).astype(BF16)
    cs = sum(jnp.dot(tri, piece, preferred_element_type=F32) for piece in _split3(ls))
    carry = carry_ref[...]
    c_ref[...] = carry
    carry_ref[...] = carry + cs[tm - 1:tm, :]
    head_lane = lax.broadcasted_iota(jnp.int32, (tm, LANES), 1) < heads
    e = sum(pltpu.roll(jnp.where(head_lane, piece.astype(F32), 0.0), p * heads, 1) if p else
            jnp.where(head_lane, piece.astype(F32), 0.0)
            for p, piece in enumerate(_split3(-cs)))
    e_ref[...] = e.astype(e_ref.dtype)


def _ln_mod_fcum(x, mod, shift_idx, scale_idx, wf, b_row, *, heads, tm, rows=128):
    s, d = x.shape
    nb = s // tm
    assert BIAS_PIECES * heads <= LANES
    vec = lambda idx: pl.BlockSpec((1, d), lambda i: (0, idx))
    const = lambda shape: pl.BlockSpec(shape, lambda i: (0, 0))
    return pl.pallas_call(
        functools.partial(_ln_mod_fcum_kernel, tm=tm, rows=rows, heads=heads),
        out_shape=(jax.ShapeDtypeStruct((s, d), BF16),
                   jax.ShapeDtypeStruct((s, LANES), BF16),
                   jax.ShapeDtypeStruct((nb, 1, LANES), F32)),
        grid=(nb,),
        in_specs=[pl.BlockSpec((tm, d), lambda i: (i, 0)), vec(shift_idx), vec(scale_idx),
                  const((LANES, d)), const((1, LANES))],
        out_specs=(pl.BlockSpec((tm, d), lambda i: (i, 0)),
                   pl.BlockSpec((tm, LANES), lambda i: (i, 0)),
                   pl.BlockSpec((None, 1, LANES), lambda i: (i, 0, 0))),
        scratch_shapes=[pltpu.VMEM((1, LANES), F32), pltpu.VMEM((tm, LANES), F32)],
        compiler_params=_params(("arbitrary",),
                                _nbytes((tm, d), F32) + _nbytes((tm, d), BF16)
                                + _nbytes((tm, LANES), BF16) + _nbytes((d, LANES), BF16),
                                4 * _nbytes((rows, d), F32) + 2 * _nbytes((tm, tm), F32)),
        name="ln_mod_fcum",
    )(x, mod, mod, wf, b_row)


def _wt_spec(tn, k, row0):
    assert row0 % SUBLANES == 0 and tn % SUBLANES == 0
    return pl.BlockSpec((pl.Element(tn), pl.Element(k)),
                        lambda i, j: ((row0 // SUBLANES + j * (tn // SUBLANES)) * SUBLANES, 0))


def _mm_nt_kernel(a_ref, w_ref, o_ref, *rest, scale, n_scaled, sigmoid):
    w = _as_bf16(w_ref[...])
    tm = a_ref.shape[0]
    chunk = tm // 4 if sigmoid or rest else tm
    lane = lax.broadcasted_iota(jnp.int32, (1, LANES), 1)
    nrm = jnp.zeros((1, LANES), F32)
    for r in range(0, tm, chunk):
        acc = lax.dot_general(a_ref[r:r + chunk, :], w, _NT, preferred_element_type=F32)
        if scale is not None:
            acc = acc * jnp.where(pl.program_id(1) < n_scaled, scale, 1.0)
        if sigmoid:
            acc = jax.nn.sigmoid(acc)
        o_ref[r:r + chunk, :] = acc.astype(o_ref.dtype)
        if rest:
            for g in range(acc.shape[1] // LANES):
                sq = jnp.sum(jnp.square(acc[:, g * LANES:(g + 1) * LANES]), axis=1, keepdims=True)
                nrm = jnp.maximum(nrm, jnp.where(lane == g, jnp.max(sq, axis=0, keepdims=True), 0.0))
    if rest:
        rest[0][...] = nrm


def _matmul_nt(a, wt, row0, n, out_dtype, *, tm, tn, scale=None, n_scaled=0, sigmoid=False,
               group_norms=False, name):
    m, k = a.shape
    assert not (group_norms and sigmoid)
    out_shape = [jax.ShapeDtypeStruct((m, n), out_dtype)]
    out_specs = [pl.BlockSpec((tm, tn), lambda i, j: (i, j))]
    if group_norms:
        out_shape.append(jax.ShapeDtypeStruct((m // tm, n // tn, 1, LANES), F32))
        out_specs.append(pl.BlockSpec((None, None, 1, LANES), lambda i, j: (i, j, 0, 0)))
    out = pl.pallas_call(
        functools.partial(_mm_nt_kernel, scale=scale, n_scaled=n_scaled, sigmoid=sigmoid),
        out_shape=out_shape,
        grid=(m // tm, n // tn),
        in_specs=[pl.BlockSpec((tm, k), lambda i, j: (i, 0)), _wt_spec(tn, k, row0)],
        out_specs=out_specs,
        compiler_params=_params(("parallel", "parallel"),
                                _nbytes((tm, k), a.dtype) + _nbytes((tn, k), wt.dtype)
                                + _nbytes((tm, tn), out_dtype),
                                _cast_bytes((tn, k), wt.dtype) + 3 * _nbytes((tm, tn), F32)),
        name=name,
    )(a, wt)
    return out if group_norms else out[0]


def _mm_tn_kernel(a_ref, w_ref, o_ref):
    acc_t = lax.dot_general(_as_bf16(w_ref[...]), a_ref[...], _NT,
                            preferred_element_type=F32)
    tk = o_ref.shape[2]
    for kb in range(o_ref.shape[0]):
        o_ref[kb] = acc_t[:, kb * tk:(kb + 1) * tk].astype(o_ref.dtype)


def _matmul_tn(a, wt, row0, n, *, tm, tn, tk, name):
    m, k = a.shape
    return pl.pallas_call(
        _mm_tn_kernel,
        out_shape=jax.ShapeDtypeStruct((m // tk, n, tk), BF16),
        grid=(m // tm, n // tn),
        in_specs=[pl.BlockSpec((tm, k), lambda i, j: (i, 0)), _wt_spec(tn, k, row0)],
        out_specs=pl.BlockSpec((tm // tk, tn, tk), lambda i, j: (i, j, 0)),
        compiler_params=_params(("parallel", "parallel"),
                                _nbytes((tm, k), a.dtype) + _nbytes((tn, k), wt.dtype)
                                + _nbytes((tm, tn), BF16),
                                _cast_bytes((tn, k), wt.dtype) + 3 * _nbytes((tm, tn), F32)),
        name=name,
    )(a, wt)


def _attn_kernel(cb_ref, thr_ref, q_ref, k_ref, e_ref, vt_ref, w_ref, o_ref, wb_ref,
                 s0_ref, s1_ref, acc_ref, *, tk, tn, heads, side_blocks):
    h = pl.program_id(0)
    i = pl.program_id(1)

    @pl.when(h * pl.num_programs(1) + i < side_blocks)
    def _():
        wb_ref[...] = w_ref[...].astype(wb_ref.dtype)

    tq, dh = q_ref.shape
    nt = tq // tn
    half = nt // 2
    r = lax.broadcasted_iota(jnp.int32, (dh, tq), 0) - h
    ones_rows = sum(jnp.where(r == p * heads, 1.0, 0.0) for p in range(BIAS_PIECES))
    qt_aug = jnp.concatenate([q_ref[...].astype(F32).T.astype(BF16), ones_rows.astype(BF16)],
                             axis=0)
    ones_v = jnp.ones((acc_ref.shape[0] - dh, tk), BF16)
    c_q = cb_ref[h, 2 * i]
    thr = thr_ref[h, i]
    acc_ref[...] = jnp.zeros_like(acc_ref)

    def logits_to(buf_ref, j, first_tile=0):
        rows = pl.ds(pl.multiple_of(j * tk, tk), tk)
        k_aug = jnp.concatenate([k_ref[rows, :], e_ref[rows, :]], axis=1)
        cms = []
        for n in range(first_tile, nt):
            cols = slice(n * tn, (n + 1) * tn)
            s = jnp.dot(k_aug, qt_aug[:, cols], preferred_element_type=F32)
            buf_ref[:, cols] = s
            cms.append(jnp.max(s, axis=0, keepdims=True))
        return jnp.concatenate(cms, axis=1)

    def causal(buf_ref, n, first_tile):
        key = lax.broadcasted_iota(jnp.int32, (tk, tn), 0)
        qry = lax.broadcasted_iota(jnp.int32, (tk, tn), 1) + (n - first_tile) * tn
        return jnp.where(key <= qry, buf_ref[:, n * tn:(n + 1) * tn], -jnp.inf)

    def softmax_pv(buf_ref, j, cmax, m, first_tile=0, causal_tiles=0):
        lo = first_tile * tn
        c = c_q - cb_ref[h, j]
        m_old = m[:, lo:]
        m_new = jnp.maximum(m_old, cmax + c)
        a = jnp.exp2(m_old - m_new)
        off = m_new - c
        v_aug = jnp.concatenate([vt_ref[j], ones_v], axis=0)
        for n in range(first_tile, nt):
            cols = slice(n * tn, (n + 1) * tn)
            rel = slice(n * tn - lo, (n + 1) * tn - lo)
            s = causal(buf_ref, n, first_tile) if n < first_tile + causal_tiles else buf_ref[:, cols]
            p = jnp.exp2(s - off[:, rel]).astype(BF16)
            acc_ref[:, cols] = a[:, rel] * acc_ref[:, cols] + jnp.dot(
                v_aug, p, preferred_element_type=F32)
        if first_tile:
            m_new = jnp.concatenate([m[:, :lo], m_new], axis=1)
        return m_new

    def causal_max(buf_ref, first_tile):
        return [jnp.max(causal(buf_ref, n, first_tile), axis=0, keepdims=True)
                for n in range(first_tile, first_tile + half)]

    def pair(p, carry):
        m, cm0 = carry
        cm1 = logits_to(s1_ref, 2 * p + 1)
        m = softmax_pv(s0_ref, 2 * p, cm0, m)
        cm0 = logits_to(s0_ref, 2 * p + 2)
        m = softmax_pv(s1_ref, 2 * p + 1, cm1, m)
        return m, cm0

    first = lax.while_loop(lambda p: (p < i) & (cb_ref[h, 2 * p + 2] - c_q > thr),
                           lambda p: p + 1, jnp.int32(0))
    init = (jnp.full((1, tq), -jnp.inf, F32), logits_to(s0_ref, 2 * first))
    m, cm0 = lax.fori_loop(first, i, pair, init)
    logits_to(s1_ref, 2 * i + 1, first_tile=half)
    cm = jnp.concatenate(causal_max(s0_ref, 0) + [cm0[:, half * tn:]], axis=1)
    m = softmax_pv(s0_ref, 2 * i, cm, m, causal_tiles=half)
    cm = jnp.concatenate(causal_max(s1_ref, half), axis=1)
    softmax_pv(s1_ref, 2 * i + 1, cm, m, first_tile=half, causal_tiles=half)
    o_ref[...] = (acc_ref[:dh, :] / acc_ref[dh:dh + 1, :]).T.astype(o_ref.dtype)


def _attention(cb, thr, qk, e, vt, w_side, *, heads, tk, tn=256, side_blocks=32):
    s = qk.shape[0]
    dh = FOX_HEAD_DIM
    tq = 2 * tk
    nq = s // tq
    side_rows = w_side.shape[0] // side_blocks
    assert side_rows * side_blocks == w_side.shape[0] and side_rows % (2 * SUBLANES) == 0
    assert side_blocks <= heads * nq
    side = lambda h, i, *_: (jnp.minimum(h * nq + i, side_blocks - 1), 0)
    grid_spec = pltpu.PrefetchScalarGridSpec(
        num_scalar_prefetch=2,
        grid=(heads, nq),
        in_specs=[pl.BlockSpec((tq, dh), lambda h, i, *_: (i, h)),
                  pl.BlockSpec((s, dh), lambda h, i, *_: (0, heads + h)),
                  pl.BlockSpec((s, LANES), lambda h, i, *_: (0, 0)),
                  pl.BlockSpec((s // tk, dh, tk), lambda h, i, *_: (0, h, 0)),
                  pl.BlockSpec((side_rows, w_side.shape[1]), side)],
        out_specs=[pl.BlockSpec((tq, dh), lambda h, i, *_: (i, h)),
                   pl.BlockSpec((side_rows, w_side.shape[1]), side)],
        scratch_shapes=[pltpu.VMEM((tk, tq), F32), pltpu.VMEM((tk, tq), F32),
                        pltpu.VMEM((dh + ONES_ROWS, tq), F32)],
    )
    return pl.pallas_call(
        functools.partial(_attn_kernel, tk=tk, tn=tn, heads=heads, side_blocks=side_blocks),
        out_shape=[jax.ShapeDtypeStruct((s, heads * dh), BF16),
                   jax.ShapeDtypeStruct(w_side.shape, BF16)],
        grid_spec=grid_spec,
        compiler_params=_params(("arbitrary", "arbitrary"),
                                3 * _nbytes((s, dh), BF16) + 2 * _nbytes((tq, dh), BF16)
                                + _nbytes((side_rows, w_side.shape[1]), F32)
                                + _nbytes((side_rows, w_side.shape[1]), BF16),
                                6 * _nbytes((tk, tq), F32)),
        name="fox_attention",
    )(cb, thr, qk, qk, e, vt, w_side)


def _mix_kernel(a_ref, wa_ref, halo_ref, p_ref, wp_ref, ps_ref, ga_ref, gp_ref, o_ref,
                wa_b, wp_b, *, tm):
    j = pl.program_id(0)
    i = pl.program_id(1)

    @pl.when(i == 0)
    def _():
        wa_b[...] = wa_ref[...].astype(BF16)
        wp_b[...] = wp_ref[...].astype(BF16)

    halo = jnp.where(i == 0, 0.0, halo_ref[...])
    ext = jnp.concatenate([halo, p_ref[...]], axis=0)
    acc = ext
    win = ext
    for g, w in enumerate(POOL_WINDOWS):
        acc = acc + pltpu.roll(acc, w // 2, 0)
        win = jnp.where(j == g, acc, win)
    t = i * tm + lax.broadcasted_iota(jnp.int32, (tm, 1), 0)
    cnt = jnp.minimum(t + 1, jnp.left_shift(2, j)).astype(F32)
    pooled = (win[POOL_HALO:] / cnt - ext[POOL_HALO:]).astype(BF16)

    ya = jnp.dot(a_ref[...], wa_b[...], preferred_element_type=F32)
    yp = jnp.dot(pooled, wp_b[...], preferred_element_type=F32) * ps_ref[...]
    o_ref[...] = (ga_ref[...].astype(F32) * ya + gp_ref[...].astype(F32) * yp).astype(o_ref.dtype)


def _mix(attn, w_a, p, w_pool, pool_scale, gates, *, tm=512):
    s, fw = attn.shape
    groups, gd, tn = w_pool.shape
    assert POOL_WINDOWS == tuple(2 << g for g in range(groups))
    d = w_a.shape[1]
    hb = tm // POOL_HALO
    return pl.pallas_call(
        functools.partial(_mix_kernel, tm=tm),
        out_shape=jax.ShapeDtypeStruct((s, d), BF16),
        grid=(groups, s // tm),
        in_specs=[pl.BlockSpec((tm, fw), lambda j, i: (i, 0)),
                  pl.BlockSpec((fw, tn), lambda j, i: (0, j)),
                  pl.BlockSpec((POOL_HALO, gd), lambda j, i: (jnp.maximum(i * hb - 1, 0), j)),
                  pl.BlockSpec((tm, gd), lambda j, i: (i, j)),
                  pl.BlockSpec((None, gd, tn), lambda j, i: (j, 0, 0)),
                  pl.BlockSpec((1, tn), lambda j, i: (0, j)),
                  pl.BlockSpec((tm, tn), lambda j, i: (i, j)),
                  pl.BlockSpec((tm, tn), lambda j, i: (i, groups + j))],
        out_specs=pl.BlockSpec((tm, tn), lambda j, i: (i, j)),
        scratch_shapes=[pltpu.VMEM((fw, tn), BF16), pltpu.VMEM((gd, tn), BF16)],
        compiler_params=_params(("parallel", "arbitrary"),
                                _nbytes((tm, fw), BF16) + _nbytes((fw, tn), w_a.dtype)
                                + _nbytes((tm + POOL_HALO, gd), F32) + _nbytes((gd, tn), w_pool.dtype)
                                + 3 * _nbytes((tm, tn), BF16),
                                _nbytes((fw + gd, tn), BF16) + 3 * _nbytes((tm, tn), F32)
                                + 4 * _nbytes((tm + POOL_HALO, gd), F32)),
        name="branch_mix",
    )(attn, w_a, p, p, w_pool, pool_scale, gates, gates)


def _resid_mm_kernel(a_ref, w_ref, x_ref, g_ref, o_ref, *, chunks):
    w = _as_bf16(w_ref[...])
    rows = a_ref.shape[0] // chunks
    for r in range(0, a_ref.shape[0], rows):
        m = jnp.dot(a_ref[r:r + rows, :], w, preferred_element_type=F32)
        o_ref[r:r + rows, :] = ALPHA * x_ref[r:r + rows, :] + g_ref[...] * m


def _resid_matmul(a, w, x, mod, gate_idx, *, tm, tn, hold_rows=False, chunks=1, name):
    m, k = a.shape
    n = w.shape[1]
    nb = n // tn
    a_spec = _held_rows_spec(tm, k) if hold_rows else pl.BlockSpec((tm, k), lambda i, j: (i, 0))
    return pl.pallas_call(
        functools.partial(_resid_mm_kernel, chunks=chunks),
        out_shape=jax.ShapeDtypeStruct((m, n), F32),
        grid=(m // tm, nb),
        in_specs=[a_spec,
                  pl.BlockSpec((k, tn), lambda i, j: (0, j)),
                  pl.BlockSpec((tm, tn), lambda i, j: (i, j)),
                  pl.BlockSpec((1, tn), lambda i, j: (0, gate_idx * nb + j))],
        out_specs=pl.BlockSpec((tm, tn), lambda i, j: (i, j)),
        compiler_params=_params(("parallel", "parallel"),
                                _nbytes((k, tn), w.dtype) + 2 * _nbytes((tm, tn), F32)
                                + (0 if hold_rows else _nbytes((tm, k), BF16)),
                                _cast_bytes((k, tn), w.dtype) + 2 * _nbytes((tm, tn), F32)
                                + (_nbytes((tm, k), BF16) if hold_rows else 0)),
        name=name,
    )(a, w, x, mod)


def _ln_ln_mod_kernel(r_ref, g_ref, b_ref, sh_ref, sc_ref, x_ref, u_ref):
    x1 = _ln_rows(r_ref[...]) * g_ref[...] + b_ref[...]
    x_ref[...] = x1
    u_ref[...] = (_ln_rows(x1) * (1.0 + sc_ref[...]) + sh_ref[...]).astype(u_ref.dtype)


def _ln_ln_mod(r, gain, bias, mod, shift_idx, scale_idx, *, tm=256):
    s, d = r.shape
    row = pl.BlockSpec((tm, d), lambda i: (i, 0))
    vec = pl.BlockSpec((1, d), lambda i: (0, 0))
    return pl.pallas_call(
        _ln_ln_mod_kernel,
        out_shape=(jax.ShapeDtypeStruct((s, d), F32), jax.ShapeDtypeStruct((s, d), BF16)),
        grid=(s // tm,),
        in_specs=[row, vec, vec,
                  pl.BlockSpec((1, d), lambda i: (0, shift_idx)),
                  pl.BlockSpec((1, d), lambda i: (0, scale_idx))],
        out_specs=(row, row),
        compiler_params=_params(("parallel",),
                                2 * _nbytes((tm, d), F32) + _nbytes((tm, d), BF16),
                                4 * _nbytes((tm, d), F32)),
        name="ln1_ln_mod",
    )(r, gain, bias, mod, mod)


def _ln_affine_kernel(r_ref, g_ref, b_ref, o_ref):
    o_ref[...] = _ln_rows(r_ref[...]) * g_ref[...] + b_ref[...]


def _ln_affine(r, gain, bias, *, tm=256):
    s, d = r.shape
    row = pl.BlockSpec((tm, d), lambda i: (i, 0))
    vec = pl.BlockSpec((1, d), lambda i: (0, 0))
    return pl.pallas_call(
        _ln_affine_kernel,
        out_shape=jax.ShapeDtypeStruct((s, d), F32),
        grid=(s // tm,),
        in_specs=[row, vec, vec],
        out_specs=row,
        compiler_params=_params(("parallel",), 2 * _nbytes((tm, d), F32),
                                3 * _nbytes((tm, d), F32)),
        name="ln2",
    )(r, gain, bias)


def _ffn_up_kernel(u_ref, wg_ref, wu_ref, o_ref):
    u = u_ref[...]
    g = jnp.dot(u, _as_bf16(wg_ref[...]), preferred_element_type=F32)
    up = jnp.dot(u, _as_bf16(wu_ref[...]), preferred_element_type=F32)
    o_ref[...] = (g * jax.nn.sigmoid(g) * up).astype(o_ref.dtype)


def _ffn_up(u, w_gate_up, *, tm=2048, tn=256):
    s, d = u.shape
    hidden = w_gate_up.shape[1] // 2
    nb = hidden // tn
    return pl.pallas_call(
        _ffn_up_kernel,
        out_shape=jax.ShapeDtypeStruct((s, hidden), BF16),
        grid=(s // tm, nb),
        in_specs=[_held_rows_spec(tm, d),
                  pl.BlockSpec((d, tn), lambda i, j: (0, j)),
                  pl.BlockSpec((d, tn), lambda i, j: (0, nb + j))],
        out_specs=pl.BlockSpec((tm, tn), lambda i, j: (i, j)),
        compiler_params=_params(("parallel", "parallel"),
                                2 * _nbytes((d, tn), w_gate_up.dtype) + _nbytes((tm, tn), BF16),
                                _nbytes((tm, d), BF16) + 2 * _nbytes((d, tn), BF16)
                                + 5 * _nbytes((tm, tn), F32)),
        name="ffn_up",
    )(u, w_gate_up, w_gate_up)


def kernel(x, c, w_ada, b_ada, w_in, b_forget, w_attn_out, w_pool, pool_scale, w_out,
           ln1_g, ln1_b, w_gate_up, w_down, ln2_g, ln2_b):
    batch, seq, d = x.shape
    assert batch == 1 and w_ada.shape[0] == DEPTH == 1
    fox_w = w_attn_out.shape[1]
    heads = fox_w // FOX_HEAD_DIM
    pool_w = w_pool.shape[1] * w_pool.shape[2]
    o_f = 3 * fox_w
    o_p = o_f + heads
    tk = 512

    xs = x[0]
    w_in_t = jnp.swapaxes(w_in, 1, 2)[0]
    w_f = jnp.pad(w_in_t[o_f:o_p], ((0, LANES - heads), (0, 0))).astype(BF16)
    b_f = jnp.pad(b_forget, ((0, 0), (0, LANES - heads)))

    mod = _ada(c.reshape(d, 1), w_ada[0], b_ada)

    u1, e, c_blk = _ln_mod_fcum(xs, mod, 0, 1, w_f, b_f, heads=heads, tm=tk)
    cb = c_blk[:, 0, :heads].T
    qk, sq = _matmul_nt(u1, w_in_t, 0, 2 * fox_w, BF16, tm=2 * tk, tn=512,
                        scale=FOX_HEAD_DIM ** -0.5 * LOG2E, n_scaled=fox_w // 512,
                        group_norms=True, name="proj_qk")
    nrm = jnp.sqrt(sq[:, :, 0, :512 // LANES].reshape(seq // (2 * tk), 2 * heads))
    thr = (SKIP_LOG2 + 2.0 * NORM_SLACK * nrm[:, :heads] * jnp.max(nrm[:, heads:], axis=0)).T
    vt = _matmul_tn(u1, w_in_t, 2 * fox_w, fox_w, tm=1024, tn=512, tk=tk, name="proj_vt")
    p = _matmul_nt(u1, w_in_t, o_p, pool_w, F32, tm=1024, tn=512, name="proj_pool")
    gates = _matmul_nt(u1, w_in_t, o_p + pool_w, 2 * d, BF16, tm=1024, tn=512, sigmoid=True,
                       name="proj_gates")
    hidden = w_down.shape[1]
    attn, w_down_bf16 = _attention(cb, thr, qk, e, vt, w_down[0].reshape(-1, d // 2), heads=heads, tk=tk)
    mix = _mix(attn, w_attn_out[0], p, w_pool[0], pool_scale, gates)
    r1 = _resid_matmul(mix, w_out[0], xs, mod, 2, tm=2048, tn=256, hold_rows=True, chunks=4, name="out_proj")
    x1, u2 = _ln_ln_mod(r1, ln1_g, ln1_b, mod, 3, 4)

    act = _ffn_up(u2, w_gate_up[0])
    r2 = _resid_matmul(act, w_down_bf16.reshape(hidden, d), x1, mod, 5, tm=512, tn=512, name="ffn_down")
    out = _ln_affine(r2, ln2_g, ln2_b)
    return out[None]
```

```python
import functools

import jax
import jax.numpy as jnp
from jax import lax
from jax.experimental import pallas as pl
from jax.experimental.pallas import tpu as pltpu

F32 = jnp.float32
BF16 = jnp.bfloat16

FOX_HEAD_DIM = 128
POOL_WINDOWS = (2, 4, 8, 16)
POOL_HALO = 16
DEPTH = 1
ALPHA = (2 * DEPTH) ** 0.25
LN_EPS = 1e-5
LOG2E = 1.4426950408889634
LANES = 128
SUBLANES = 8
BIAS_PIECES = 3
ONES_ROWS = 16
SKIP_LOG2 = 48.0
NORM_SLACK = 1.02

V7X_VMEM_BYTES = 64 * 1024 * 1024
VMEM_CAP_BYTES = V7X_VMEM_BYTES - 6 * 1024 * 1024

_NT = (((1,), (1,)), ((), ()))
_TN = (((0,), (0,)), ((), ()))


def _nbytes(shape, dtype):
    n = jnp.dtype(dtype).itemsize
    for s in shape:
        n *= s
    return n


def _cast_bytes(shape, dtype):
    return 0 if dtype == BF16 else _nbytes(shape, BF16)


def _params(semantics, pipelined_bytes, resident_bytes=0):
    need = 2 * pipelined_bytes + resident_bytes
    assert need <= VMEM_CAP_BYTES, (need, VMEM_CAP_BYTES)
    return pltpu.CompilerParams(dimension_semantics=semantics,
                                vmem_limit_bytes=min(need + (4 << 20), VMEM_CAP_BYTES))


def _held_rows_spec(tm, k):
    return pl.BlockSpec((tm, k), lambda i, j: (i, 0), pipeline_mode=pl.Buffered(1))


def _ln_rows(x):
    mu = jnp.mean(x, axis=-1, keepdims=True)
    xc = x - mu
    var = jnp.mean(xc * xc, axis=-1, keepdims=True)
    return xc * lax.rsqrt(var + LN_EPS)


def _as_bf16(w):
    return w if w.dtype == BF16 else w.astype(BF16)


def _split3(v):
    hi = v.astype(BF16)
    r1 = v - hi.astype(F32)
    mid = r1.astype(BF16)
    lo = (r1 - mid.astype(F32)).astype(BF16)
    return hi, mid, lo


def _ada_kernel(c_ref, w_ref, b_ref, o_ref):
    k = pl.program_id(1)

    @pl.when(k == 0)
    def _():
        o_ref[...] = b_ref[...]

    cc = c_ref[...]
    s = cc * jax.nn.sigmoid(cc)
    o_ref[...] += jnp.sum(s * w_ref[...], axis=0, keepdims=True)


def _ada(c_col, w_ada, b_ada, *, tk=512, tn=4096):
    d, n = w_ada.shape
    return pl.pallas_call(
        _ada_kernel,
        out_shape=jax.ShapeDtypeStruct((1, n), F32),
        grid=(n // tn, d // tk),
        in_specs=[pl.BlockSpec((tk, 1), lambda j, k: (k, 0)),
                  pl.BlockSpec((tk, tn), lambda j, k: (k, j)),
                  pl.BlockSpec((1, tn), lambda j, k: (0, j))],
        out_specs=pl.BlockSpec((1, tn), lambda j, k: (0, j)),
        compiler_params=_params(("parallel", "arbitrary"),
                                _nbytes((tk, tn), F32) + _nbytes((tk, LANES), F32),
                                _nbytes((tk, tn), F32)),
        name="ada",
    )(c_col, w_ada, b_ada)


def _ln_mod_fcum_kernel(x_ref, sh_ref, sc_ref, wf_ref, b_ref, u_ref, e_ref, c_ref,
                        carry_ref, f_ref, *, tm, rows, heads):
    @pl.when(pl.program_id(0) == 0)
    def _():
        carry_ref[...] = jnp.zeros_like(carry_ref)

    gain = 1.0 + sc_ref[...]
    shift = sh_ref[...]
    for r in range(0, tm, rows):
        u = (_ln_rows(x_ref[r:r + rows, :]) * gain + shift).astype(BF16)
        u_ref[r:r + rows, :] = u
        f_ref[r:r + rows, :] = lax.dot_general(u, wf_ref[...], _NT, preferred_element_type=F32)
    f = f_ref[...] + b_ref[...]
    ls = (jnp.minimum(f, 0.0) - jnp.log1p(jnp.exp(-jnp.abs(f)))) * LOG2E
    row = lax.broadcasted_iota(jnp.int32, (tm, tm), 0)
    col = lax.broadcasted_iota(jnp.int32, (tm, tm), 1)
    tri = jnp.where(col <= row, 1.0, 0.0).astype(BF16)
    cs = sum(jnp.dot(tri, piece, preferred_element_type=F32) for piece in _split3(ls))
    carry = carry_ref[...]
    c_ref[...] = carry
    carry_ref[...] = carry + cs[tm - 1:tm, :]
    head_lane = lax.broadcasted_iota(jnp.int32, (tm, LANES), 1) < heads
    e = sum(pltpu.roll(jnp.where(head_lane, piece.astype(F32), 0.0), p * heads, 1) if p else
            jnp.where(head_lane, piece.astype(F32), 0.0)
            for p, piece in enumerate(_split3(-cs)))
    e_ref[...] = e.astype(e_ref.dtype)


def _ln_mod_fcum(x, mod, shift_idx, scale_idx, wf, b_row, *, heads, tm, rows=128):
    s, d = x.shape
    nb = s // tm
    assert BIAS_PIECES * heads <= LANES
    vec = lambda idx: pl.BlockSpec((1, d), lambda i: (0, idx))
    const = lambda shape: pl.BlockSpec(shape, lambda i: (0, 0))
    return pl.pallas_call(
        functools.partial(_ln_mod_fcum_kernel, tm=tm, rows=rows, heads=heads),
        out_shape=(jax.ShapeDtypeStruct((s, d), BF16),
                   jax.ShapeDtypeStruct((s, LANES), BF16),
                   jax.ShapeDtypeStruct((nb, 1, LANES), F32)),
        grid=(nb,),
        in_specs=[pl.BlockSpec((tm, d), lambda i: (i, 0)), vec(shift_idx), vec(scale_idx),
                  const((LANES, d)), const((1, LANES))],
        out_specs=(pl.BlockSpec((tm, d), lambda i: (i, 0)),
                   pl.BlockSpec((tm, LANES), lambda i: (i, 0)),
                   pl.BlockSpec((None, 1, LANES), lambda i: (i, 0, 0))),
        scratch_shapes=[pltpu.VMEM((1, LANES), F32), pltpu.VMEM((tm, LANES), F32)],
        compiler_params=_params(("arbitrary",),
                                _nbytes((tm, d), F32) + _nbytes((tm, d), BF16)
                                + _nbytes((tm, LANES), BF16) + _nbytes((d, LANES), BF16),
                                4 * _nbytes((rows, d), F32) + 2 * _nbytes((tm, tm), F32)),
        name="ln_mod_fcum",
    )(x, mod, mod, wf, b_row)


def _wt_spec(tn, k, row0):
    assert row0 % SUBLANES == 0 and tn % SUBLANES == 0
    return pl.BlockSpec((pl.Element(tn), pl.Element(k)),
                        lambda i, j: ((row0 // SUBLANES + j * (tn // SUBLANES)) * SUBLANES, 0))


def _mm_nt_kernel(a_ref, w_ref, o_ref, *rest, sigmoid):
    w = _as_bf16(w_ref[...])
    tm = a_ref.shape[0]
    chunk = tm // 4 if sigmoid or rest else tm
    lane = lax.broadcasted_iota(jnp.int32, (1, LANES), 1)
    nrm = jnp.zeros((1, LANES), F32)
    for r in range(0, tm, chunk):
        acc = lax.dot_general(a_ref[r:r + chunk, :], w, _NT, preferred_element_type=F32)
        if sigmoid:
            acc = 0.5 * jnp.tanh(0.5 * acc) + 0.5
        o_ref[r:r + chunk, :] = acc.astype(o_ref.dtype)
        if rest:
            for g in range(acc.shape[1] // LANES):
                sq = jnp.sum(jnp.square(acc[:, g * LANES:(g + 1) * LANES]), axis=1, keepdims=True)
                nrm = jnp.maximum(nrm, jnp.where(lane == g, jnp.max(sq, axis=0, keepdims=True), 0.0))
    if rest:
        rest[0][...] = nrm


def _matmul_nt(a, wt, row0, n, out_dtype, *, tm, tn, sigmoid=False, group_norms=False, name):
    m, k = a.shape
    assert not (group_norms and sigmoid)
    out_shape = [jax.ShapeDtypeStruct((m, n), out_dtype)]
    out_specs = [pl.BlockSpec((tm, tn), lambda i, j: (i, j))]
    if group_norms:
        out_shape.append(jax.ShapeDtypeStruct((m // tm, n // tn, 1, LANES), F32))
        out_specs.append(pl.BlockSpec((None, None, 1, LANES), lambda i, j: (i, j, 0, 0)))
    out = pl.pallas_call(
        functools.partial(_mm_nt_kernel, sigmoid=sigmoid),
        out_shape=out_shape,
        grid=(m // tm, n // tn),
        in_specs=[pl.BlockSpec((tm, k), lambda i, j: (i, 0)), _wt_spec(tn, k, row0)],
        out_specs=out_specs,
        compiler_params=_params(("parallel", "parallel"),
                                _nbytes((tm, k), a.dtype) + _nbytes((tn, k), wt.dtype)
                                + _nbytes((tm, tn), out_dtype),
                                _cast_bytes((tn, k), wt.dtype) + 3 * _nbytes((tm, tn), F32)),
        name=name,
    )(a, wt)
    return out if group_norms else out[0]


def _mm_tn_kernel(a_ref, w_ref, o_ref, *rest, scale):
    acc_t = lax.dot_general(_as_bf16(w_ref[...]), a_ref[...], _NT,
                            preferred_element_type=F32)
    if scale is not None:
        acc_t = acc_t * scale
    tk = o_ref.shape[2]
    for kb in range(o_ref.shape[0]):
        o_ref[kb] = acc_t[:, kb * tk:(kb + 1) * tk].astype(o_ref.dtype)
    if rest:
        lane = lax.broadcasted_iota(jnp.int32, (1, LANES), 1)
        nrm = jnp.zeros((1, LANES), F32)
        for g in range(acc_t.shape[0] // LANES):
            sq = jnp.sum(jnp.square(acc_t[g * LANES:(g + 1) * LANES, :]), axis=0, keepdims=True)
            nrm = jnp.where(lane == g, jnp.max(sq, axis=1, keepdims=True), nrm)
        rest[0][...] = nrm


def _matmul_tn(a, wt, row0, n, *, tm, tn, tk, scale=None, group_norms=False, name):
    m, k = a.shape
    out_shape = [jax.ShapeDtypeStruct((m // tk, n, tk), BF16)]
    out_specs = [pl.BlockSpec((tm // tk, tn, tk), lambda i, j: (i, j, 0))]
    if group_norms:
        out_shape.append(jax.ShapeDtypeStruct((m // tm, n // tn, 1, LANES), F32))
        out_specs.append(pl.BlockSpec((None, None, 1, LANES), lambda i, j: (i, j, 0, 0)))
    out = pl.pallas_call(
        functools.partial(_mm_tn_kernel, scale=scale),
        out_shape=out_shape,
        grid=(m // tm, n // tn),
        in_specs=[pl.BlockSpec((tm, k), lambda i, j: (i, 0)), _wt_spec(tn, k, row0)],
        out_specs=out_specs,
        compiler_params=_params(("parallel", "parallel"),
                                _nbytes((tm, k), a.dtype) + _nbytes((tn, k), wt.dtype)
                                + _nbytes((tm, tn), BF16),
                                _cast_bytes((tn, k), wt.dtype) + 3 * _nbytes((tm, tn), F32)),
        name=name,
    )(a, wt)
    return out if group_norms else out[0]


def _attn_kernel(cb_ref, thr_ref, q_ref, k_ref, e_ref, vt_ref, w_ref, o_ref, wb_ref,
                 s0_ref, s1_ref, acc_ref, *, tk, tn, heads, side_blocks):
    h = pl.program_id(0)
    i = pl.program_id(1)

    @pl.when(h * pl.num_programs(1) + i < side_blocks)
    def _():
        wb_ref[...] = w_ref[...].astype(wb_ref.dtype)

    dh, tq = q_ref.shape
    nt = tq // tn
    half = nt // 2
    r = lax.broadcasted_iota(jnp.int32, (dh, tq), 0) - h
    ones_rows = sum(jnp.where(r == p * heads, 1.0, 0.0) for p in range(BIAS_PIECES))
    qt_aug = jnp.concatenate([q_ref[...], ones_rows.astype(BF16)], axis=0)
    ones_v = jnp.ones((acc_ref.shape[0] - dh, tk), BF16)
    c_q = cb_ref[h, 2 * i]
    thr = thr_ref[h, i]
    acc_ref[...] = jnp.zeros_like(acc_ref)

    def logits_to(buf_ref, j, first_tile=0):
        rows = pl.ds(pl.multiple_of(j * tk, tk), tk)
        k_aug = jnp.concatenate([k_ref[rows, :], e_ref[rows, :]], axis=1)
        cms = []
        for n in range(first_tile, nt):
            cols = slice(n * tn, (n + 1) * tn)
            s = jnp.dot(k_aug, qt_aug[:, cols], preferred_element_type=F32)
            buf_ref[:, cols] = s
            cms.append(jnp.max(s, axis=0, keepdims=True))
        return jnp.concatenate(cms, axis=1)

    def causal(buf_ref, n, first_tile):
        key = lax.broadcasted_iota(jnp.int32, (tk, tn), 0)
        qry = lax.broadcasted_iota(jnp.int32, (tk, tn), 1) + (n - first_tile) * tn
        return jnp.where(key <= qry, buf_ref[:, n * tn:(n + 1) * tn], -jnp.inf)

    def softmax_pv(buf_ref, j, cmax, m, first_tile=0, causal_tiles=0):
        lo = first_tile * tn
        c = c_q - cb_ref[h, j]
        m_old = m[:, lo:]
        m_new = jnp.maximum(m_old, cmax + c)
        a = jnp.exp2(m_old - m_new)
        off = m_new - c
        v_aug = jnp.concatenate([vt_ref[j], ones_v], axis=0)
        for n in range(first_tile, nt):
            cols = slice(n * tn, (n + 1) * tn)
            rel = slice(n * tn - lo, (n + 1) * tn - lo)
            s = causal(buf_ref, n, first_tile) if n < first_tile + causal_tiles else buf_ref[:, cols]
            p = jnp.exp2(s - off[:, rel]).astype(BF16)
            acc_ref[:, cols] = a[:, rel] * acc_ref[:, cols] + jnp.dot(
                v_aug, p, preferred_element_type=F32)
        if first_tile:
            m_new = jnp.concatenate([m[:, :lo], m_new], axis=1)
        return m_new

    def causal_max(buf_ref, first_tile):
        return [jnp.max(causal(buf_ref, n, first_tile), axis=0, keepdims=True)
                for n in range(first_tile, first_tile + half)]

    def pair(p, carry):
        m, cm0 = carry
        cm1 = logits_to(s1_ref, 2 * p + 1)
        m = softmax_pv(s0_ref, 2 * p, cm0, m)
        cm0 = logits_to(s0_ref, 2 * p + 2)
        m = softmax_pv(s1_ref, 2 * p + 1, cm1, m)
        return m, cm0

    first = lax.while_loop(lambda p: (p < i) & (cb_ref[h, 2 * p + 2] - c_q > thr),
                           lambda p: p + 1, jnp.int32(0))
    init = (jnp.full((1, tq), -jnp.inf, F32), logits_to(s0_ref, 2 * first))
    m, cm0 = lax.fori_loop(first, i, pair, init)
    logits_to(s1_ref, 2 * i + 1, first_tile=half)
    cm = jnp.concatenate(causal_max(s0_ref, 0) + [cm0[:, half * tn:]], axis=1)
    m = softmax_pv(s0_ref, 2 * i, cm, m, causal_tiles=half)
    cm = jnp.concatenate(causal_max(s1_ref, half), axis=1)
    softmax_pv(s1_ref, 2 * i + 1, cm, m, first_tile=half, causal_tiles=half)
    o_ref[...] = (acc_ref[:dh, :] / acc_ref[dh:dh + 1, :]).astype(o_ref.dtype)


def _attention(cb, thr, qt, k, e, vt, w_side, *, heads, tk, tn=256, side_blocks=32):
    s = k.shape[0]
    dh = FOX_HEAD_DIM
    tq = 2 * tk
    nq = s // tq
    side_cols = w_side.shape[1] // 2
    side_rows = 2 * w_side.shape[0] // side_blocks
    assert side_rows * side_blocks == 2 * w_side.shape[0] and side_rows % (2 * SUBLANES) == 0
    assert side_cols % LANES == 0 and side_blocks <= heads * nq

    def side(h, i, *_):
        blk = jnp.minimum(h * nq + i, side_blocks - 1)
        return blk // 2, blk % 2

    grid_spec = pltpu.PrefetchScalarGridSpec(
        num_scalar_prefetch=2,
        grid=(heads, nq),
        in_specs=[pl.BlockSpec((None, dh, tq), lambda h, i, *_: (i, h, 0)),
                  pl.BlockSpec((s, dh), lambda h, i, *_: (0, h)),
                  pl.BlockSpec((s, LANES), lambda h, i, *_: (0, 0)),
                  pl.BlockSpec((s // tk, dh, tk), lambda h, i, *_: (0, h, 0)),
                  pl.BlockSpec((side_rows, side_cols), side)],
        out_specs=[pl.BlockSpec((dh, tq), lambda h, i, *_: (h, i)),
                   pl.BlockSpec((side_rows, side_cols), side)],
        scratch_shapes=[pltpu.VMEM((tk, tq), F32), pltpu.VMEM((tk, tq), F32),
                        pltpu.VMEM((dh + ONES_ROWS, tq), F32)],
    )
    return pl.pallas_call(
        functools.partial(_attn_kernel, tk=tk, tn=tn, heads=heads, side_blocks=side_blocks),
        out_shape=[jax.ShapeDtypeStruct((heads * dh, s), BF16),
                   jax.ShapeDtypeStruct(w_side.shape, BF16)],
        grid_spec=grid_spec,
        compiler_params=_params(("arbitrary", "arbitrary"),
                                3 * _nbytes((s, dh), BF16) + 2 * _nbytes((tq, dh), BF16)
                                + _nbytes((side_rows, side_cols), F32)
                                + _nbytes((side_rows, side_cols), BF16),
                                6 * _nbytes((tk, tq), F32)),
        name="fox_attention",
    )(cb, thr, qt, k, e, vt, w_side)


def _mix_kernel(a_ref, wa_ref, halo_ref, p_ref, wp_ref, ps_ref, ga_ref, gp_ref, o_ref,
                wa_b, wp_b, *, tm):
    j = pl.program_id(0)
    i = pl.program_id(1)

    @pl.when(i == 0)
    def _():
        wa_b[...] = wa_ref[...].astype(BF16)
        wp_b[...] = wp_ref[...].astype(BF16)

    halo = jnp.where(i == 0, 0.0, halo_ref[...])
    ext = jnp.concatenate([halo, p_ref[...]], axis=0)
    acc = ext
    win = ext
    for g, w in enumerate(POOL_WINDOWS):
        acc = acc + pltpu.roll(acc, w // 2, 0)
        win = jnp.where(j == g, acc, win)
    t = i * tm + lax.broadcasted_iota(jnp.int32, (tm, 1), 0)
    cnt = jnp.minimum(t + 1, jnp.left_shift(2, j)).astype(F32)
    pooled = (win[POOL_HALO:] / cnt - ext[POOL_HALO:]).astype(BF16)

    ya = lax.dot_general(a_ref[...], wa_b[...], _TN, preferred_element_type=F32)
    yp = jnp.dot(pooled, wp_b[...], preferred_element_type=F32) * ps_ref[...]
    o_ref[...] = (ga_ref[...].astype(F32) * ya + gp_ref[...].astype(F32) * yp).astype(o_ref.dtype)


def _mix(attn_t, w_a, p, w_pool, pool_scale, gates, *, tm=512):
    fw, s = attn_t.shape
    groups, gd, tn = w_pool.shape
    assert POOL_WINDOWS == tuple(2 << g for g in range(groups))
    d = w_a.shape[1]
    hb = tm // POOL_HALO
    return pl.pallas_call(
        functools.partial(_mix_kernel, tm=tm),
        out_shape=jax.ShapeDtypeStruct((s, d), BF16),
        grid=(groups, s // tm),
        in_specs=[pl.BlockSpec((fw, tm), lambda j, i: (0, i)),
                  pl.BlockSpec((fw, tn), lambda j, i: (0, j)),
                  pl.BlockSpec((POOL_HALO, gd), lambda j, i: (jnp.maximum(i * hb - 1, 0), j)),
                  pl.BlockSpec((tm, gd), lambda j, i: (i, j)),
                  pl.BlockSpec((None, gd, tn), lambda j, i: (j, 0, 0)),
                  pl.BlockSpec((1, tn), lambda j, i: (0, j)),
                  pl.BlockSpec((tm, tn), lambda j, i: (i, j)),
                  pl.BlockSpec((tm, tn), lambda j, i: (i, groups + j))],
        out_specs=pl.BlockSpec((tm, tn), lambda j, i: (i, j)),
        scratch_shapes=[pltpu.VMEM((fw, tn), BF16), pltpu.VMEM((gd, tn), BF16)],
        compiler_params=_params(("parallel", "arbitrary"),
                                _nbytes((tm, fw), BF16) + _nbytes((fw, tn), w_a.dtype)
                                + _nbytes((tm + POOL_HALO, gd), F32) + _nbytes((gd, tn), w_pool.dtype)
                                + 3 * _nbytes((tm, tn), BF16),
                                _nbytes((fw + gd, tn), BF16) + 3 * _nbytes((tm, tn), F32)
                                + 4 * _nbytes((tm + POOL_HALO, gd), F32)),
        name="branch_mix",
    )(attn_t, w_a, p, p, w_pool, pool_scale, gates, gates)


def _resid_mm_kernel(a_ref, w_ref, x_ref, g_ref, o_ref, *, chunks):
    w = _as_bf16(w_ref[...])
    rows = a_ref.shape[0] // chunks
    for r in range(0, a_ref.shape[0], rows):
        m = jnp.dot(a_ref[r:r + rows, :], w, preferred_element_type=F32)
        o_ref[r:r + rows, :] = ALPHA * x_ref[r:r + rows, :] + g_ref[...] * m


def _resid_matmul(a, w, x, mod, gate_idx, *, tm, tn, hold_rows=False, chunks=1, name):
    m, k = a.shape
    n = w.shape[1]
    nb = n // tn
    a_spec = _held_rows_spec(tm, k) if hold_rows else pl.BlockSpec((tm, k), lambda i, j: (i, 0))
    return pl.pallas_call(
        functools.partial(_resid_mm_kernel, chunks=chunks),
        out_shape=jax.ShapeDtypeStruct((m, n), F32),
        grid=(m // tm, nb),
        in_specs=[a_spec,
                  pl.BlockSpec((k, tn), lambda i, j: (0, j)),
                  pl.BlockSpec((tm, tn), lambda i, j: (i, j)),
                  pl.BlockSpec((1, tn), lambda i, j: (0, gate_idx * nb + j))],
        out_specs=pl.BlockSpec((tm, tn), lambda i, j: (i, j)),
        compiler_params=_params(("parallel", "parallel"),
                                _nbytes((k, tn), w.dtype) + 2 * _nbytes((tm, tn), F32)
                                + (0 if hold_rows else _nbytes((tm, k), BF16)),
                                _cast_bytes((k, tn), w.dtype) + 2 * _nbytes((tm, tn), F32)
                                + (_nbytes((tm, k), BF16) if hold_rows else 0)),
        name=name,
    )(a, w, x, mod)


def _ln_ln_mod_kernel(r_ref, g_ref, b_ref, sh_ref, sc_ref, x_ref, u_ref):
    x1 = _ln_rows(r_ref[...]) * g_ref[...] + b_ref[...]
    x_ref[...] = x1
    u_ref[...] = (_ln_rows(x1) * (1.0 + sc_ref[...]) + sh_ref[...]).astype(u_ref.dtype)


def _ln_ln_mod(r, gain, bias, mod, shift_idx, scale_idx, *, tm=256):
    s, d = r.shape
    row = pl.BlockSpec((tm, d), lambda i: (i, 0))
    vec = pl.BlockSpec((1, d), lambda i: (0, 0))
    return pl.pallas_call(
        _ln_ln_mod_kernel,
        out_shape=(jax.ShapeDtypeStruct((s, d), F32), jax.ShapeDtypeStruct((s, d), BF16)),
        grid=(s // tm,),
        in_specs=[row, vec, vec,
                  pl.BlockSpec((1, d), lambda i: (0, shift_idx)),
                  pl.BlockSpec((1, d), lambda i: (0, scale_idx))],
        out_specs=(row, row),
        compiler_params=_params(("parallel",),
                                2 * _nbytes((tm, d), F32) + _nbytes((tm, d), BF16),
                                4 * _nbytes((tm, d), F32)),
        name="ln1_ln_mod",
    )(r, gain, bias, mod, mod)


def _ln_affine_kernel(r_ref, g_ref, b_ref, o_ref):
    o_ref[...] = _ln_rows(r_ref[...]) * g_ref[...] + b_ref[...]


def _ln_affine(r, gain, bias, *, tm=256):
    s, d = r.shape
    row = pl.BlockSpec((tm, d), lambda i: (i, 0))
    vec = pl.BlockSpec((1, d), lambda i: (0, 0))
    return pl.pallas_call(
        _ln_affine_kernel,
        out_shape=jax.ShapeDtypeStruct((s, d), F32),
        grid=(s // tm,),
        in_specs=[row, vec, vec],
        out_specs=row,
        compiler_params=_params(("parallel",), 2 * _nbytes((tm, d), F32),
                                3 * _nbytes((tm, d), F32)),
        name="ln2",
    )(r, gain, bias)


def _ffn_up_kernel(u_ref, wg_ref, wu_ref, o_ref):
    u = u_ref[...]
    g = jnp.dot(u, _as_bf16(wg_ref[...]), preferred_element_type=F32)
    up = jnp.dot(u, _as_bf16(wu_ref[...]), preferred_element_type=F32)
    o_ref[...] = (g * jax.nn.sigmoid(g) * up).astype(o_ref.dtype)


def _ffn_up(u, w_gate_up, *, tm=2048, tn=256):
    s, d = u.shape
    hidden = w_gate_up.shape[1] // 2
    nb = hidden // tn
    return pl.pallas_call(
        _ffn_up_kernel,
        out_shape=jax.ShapeDtypeStruct((s, hidden), BF16),
        grid=(s // tm, nb),
        in_specs=[_held_rows_spec(tm, d),
                  pl.BlockSpec((d, tn), lambda i, j: (0, j)),
                  pl.BlockSpec((d, tn), lambda i, j: (0, nb + j))],
        out_specs=pl.BlockSpec((tm, tn), lambda i, j: (i, j)),
        compiler_params=_params(("parallel", "parallel"),
                                2 * _nbytes((d, tn), w_gate_up.dtype) + _nbytes((tm, tn), BF16),
                                _nbytes((tm, d), BF16) + 2 * _nbytes((d, tn), BF16)
                                + 5 * _nbytes((tm, tn), F32)),
        name="ffn_up",
    )(u, w_gate_up, w_gate_up)


def kernel(x, c, w_ada, b_ada, w_in, b_forget, w_attn_out, w_pool, pool_scale, w_out,
           ln1_g, ln1_b, w_gate_up, w_down, ln2_g, ln2_b):
    batch, seq, d = x.shape
    assert batch == 1 and w_ada.shape[0] == DEPTH == 1
    fox_w = w_attn_out.shape[1]
    heads = fox_w // FOX_HEAD_DIM
    pool_w = w_pool.shape[1] * w_pool.shape[2]
    o_f = 3 * fox_w
    o_p = o_f + heads
    tk = 512

    xs = x[0]
    w_in_t = jnp.swapaxes(w_in, 1, 2)[0]
    w_f = jnp.pad(w_in_t[o_f:o_p], ((0, LANES - heads), (0, 0))).astype(BF16)
    b_f = jnp.pad(b_forget, ((0, 0), (0, LANES - heads)))

    mod = _ada(c.reshape(d, 1), w_ada[0], b_ada)

    u1, e, c_blk = _ln_mod_fcum(xs, mod, 0, 1, w_f, b_f, heads=heads, tm=tk)
    cb = c_blk[:, 0, :heads].T
    tq = 2 * tk
    qt, sq_q = _matmul_tn(u1, w_in_t, 0, fox_w, tm=tq, tn=512, tk=tq, scale=FOX_HEAD_DIM ** -0.5 * LOG2E,
                          group_norms=True, name="proj_qt")
    k, sq_k = _matmul_nt(u1, w_in_t, fox_w, fox_w, BF16, tm=tq, tn=512, group_norms=True, name="proj_k")
    per_head = lambda sq: jnp.sqrt(sq[:, :, 0, :512 // LANES].reshape(seq // tq, heads))
    thr = (SKIP_LOG2 + 2.0 * NORM_SLACK * per_head(sq_q) * jnp.max(per_head(sq_k), axis=0)).T
    vt = _matmul_tn(u1, w_in_t, 2 * fox_w, fox_w, tm=1024, tn=512, tk=tk, name="proj_vt")
    p = _matmul_nt(u1, w_in_t, o_p, pool_w, F32, tm=1024, tn=512, name="proj_pool")
    gates = _matmul_nt(u1, w_in_t, o_p + pool_w, 2 * d, BF16, tm=1024, tn=512, sigmoid=True,
                       name="proj_gates")
    attn, w_down_bf16 = _attention(cb, thr, qt, k, e, vt, w_down[0], heads=heads, tk=tk)
    mix = _mix(attn, w_attn_out[0], p, w_pool[0], pool_scale, gates)
    r1 = _resid_matmul(mix, w_out[0], xs, mod, 2, tm=2048, tn=256, hold_rows=True, chunks=4, name="out_proj")
    x1, u2 = _ln_ln_mod(r1, ln1_g, ln1_b, mod, 3, 4)

    act = _ffn_up(u2, w_gate_up[0])
    r2 = _resid_matmul(act, w_down_bf16, x1, mod, 5, tm=512, tn=512, name="ffn_down")
    out = _ln_affine(r2, ln2_g, ln2_b)
    return out[None]
```

```python
import functools

import jax
import jax.numpy as jnp
from jax import lax
from jax.experimental import pallas as pl
from jax.experimental.pallas import tpu as pltpu

F32 = jnp.float32
BF16 = jnp.bfloat16

FOX_HEAD_DIM = 128
POOL_WINDOWS = (2, 4, 8, 16)
POOL_HALO = 16
DEPTH = 1
ALPHA = (2 * DEPTH) ** 0.25
LN_EPS = 1e-5
LOG2E = 1.4426950408889634
LANES = 128
SUBLANES = 8
BIAS_PIECES = 3
ONES_ROWS = 16
MIX_CHUNKS = 4
FFN_CHUNKS = 4
SKIP_LOG2 = 48.0
NORM_SLACK = 1.02

V7X_VMEM_BYTES = 64 * 1024 * 1024
VMEM_CAP_BYTES = V7X_VMEM_BYTES - 6 * 1024 * 1024

_NT = (((1,), (1,)), ((), ()))
_TN = (((0,), (0,)), ((), ()))


def _nbytes(shape, dtype):
    n = jnp.dtype(dtype).itemsize
    for s in shape:
        n *= s
    return n


def _cast_bytes(shape, dtype):
    return 0 if dtype == BF16 else _nbytes(shape, BF16)


def _params(semantics, pipelined_bytes, resident_bytes=0):
    need = 2 * pipelined_bytes + resident_bytes
    assert need <= VMEM_CAP_BYTES, (need, VMEM_CAP_BYTES)
    return pltpu.CompilerParams(dimension_semantics=semantics,
                                vmem_limit_bytes=min(need + (4 << 20), VMEM_CAP_BYTES))


def _held_rows_spec(tm, k):
    return pl.BlockSpec((tm, k), lambda i, j: (i, 0), pipeline_mode=pl.Buffered(1))


def _ln_rows(x):
    mu = jnp.mean(x, axis=-1, keepdims=True)
    xc = x - mu
    var = jnp.mean(xc * xc, axis=-1, keepdims=True)
    return xc * lax.rsqrt(var + LN_EPS)


def _as_bf16(w):
    return w if w.dtype == BF16 else w.astype(BF16)


def _split3(v):
    hi = v.astype(BF16)
    r1 = v - hi.astype(F32)
    mid = r1.astype(BF16)
    lo = (r1 - mid.astype(F32)).astype(BF16)
    return hi, mid, lo


def _ada_kernel(c_ref, w_ref, b_ref, o_ref):
    k = pl.program_id(1)

    @pl.when(k == 0)
    def _():
        o_ref[...] = b_ref[...]

    cc = c_ref[...]
    s = cc * jax.nn.sigmoid(cc)
    o_ref[...] += jnp.sum(s * w_ref[...], axis=0, keepdims=True)


def _ada(c_col, w_ada, b_ada, *, tk=512, tn=4096):
    d, n = w_ada.shape
    return pl.pallas_call(
        _ada_kernel,
        out_shape=jax.ShapeDtypeStruct((1, n), F32),
        grid=(n // tn, d // tk),
        in_specs=[pl.BlockSpec((tk, 1), lambda j, k: (k, 0)),
                  pl.BlockSpec((tk, tn), lambda j, k: (k, j)),
                  pl.BlockSpec((1, tn), lambda j, k: (0, j))],
        out_specs=pl.BlockSpec((1, tn), lambda j, k: (0, j)),
        compiler_params=_params(("parallel", "arbitrary"),
                                _nbytes((tk, tn), F32) + _nbytes((tk, LANES), F32),
                                _nbytes((tk, tn), F32)),
        name="ada",
    )(c_col, w_ada, b_ada)


def _ln_mod_fcum_kernel(x_ref, sh_ref, sc_ref, wf_ref, b_ref, u_ref, e_ref, c_ref,
                        carry_ref, f_ref, *, tm, rows, heads):
    @pl.when(pl.program_id(0) == 0)
    def _():
        carry_ref[...] = jnp.zeros_like(carry_ref)

    gain = 1.0 + sc_ref[...]
    shift = sh_ref[...]
    for r in range(0, tm, rows):
        u = (_ln_rows(x_ref[r:r + rows, :]) * gain + shift).astype(BF16)
        u_ref[r:r + rows, :] = u
        f_ref[r:r + rows, :] = lax.dot_general(u, wf_ref[...], _NT, preferred_element_type=F32)
    f = f_ref[...] + b_ref[...]
    ls = (jnp.minimum(f, 0.0) - jnp.log1p(jnp.exp(-jnp.abs(f)))) * LOG2E
    row = lax.broadcasted_iota(jnp.int32, (tm, tm), 0)
    col = lax.broadcasted_iota(jnp.int32, (tm, tm), 1)
    tri = jnp.where(col <= row, 1.0, 0.0).astype(BF16)
    cs = sum(jnp.dot(tri, piece, preferred_element_type=F32) for piece in _split3(ls))
    carry = carry_ref[...]
    c_ref[...] = carry
    carry_ref[...] = carry + cs[tm - 1:tm, :]
    head_lane = lax.broadcasted_iota(jnp.int32, (tm, LANES), 1) < heads
    e = sum(pltpu.roll(jnp.where(head_lane, piece.astype(F32), 0.0), p * heads, 1) if p else
            jnp.where(head_lane, piece.astype(F32), 0.0)
            for p, piece in enumerate(_split3(-cs)))
    e_ref[...] = e.astype(e_ref.dtype)


def _ln_mod_fcum(x, mod, shift_idx, scale_idx, wf, b_row, *, heads, tm, rows=128):
    s, d = x.shape
    nb = s // tm
    assert BIAS_PIECES * heads <= LANES
    vec = lambda idx: pl.BlockSpec((1, d), lambda i: (0, idx))
    const = lambda shape: pl.BlockSpec(shape, lambda i: (0, 0))
    return pl.pallas_call(
        functools.partial(_ln_mod_fcum_kernel, tm=tm, rows=rows, heads=heads),
        out_shape=(jax.ShapeDtypeStruct((s, d), BF16),
                   jax.ShapeDtypeStruct((s, LANES), BF16),
                   jax.ShapeDtypeStruct((nb, 1, LANES), F32)),
        grid=(nb,),
        in_specs=[pl.BlockSpec((tm, d), lambda i: (i, 0)), vec(shift_idx), vec(scale_idx),
                  const((LANES, d)), const((1, LANES))],
        out_specs=(pl.BlockSpec((tm, d), lambda i: (i, 0)),
                   pl.BlockSpec((tm, LANES), lambda i: (i, 0)),
                   pl.BlockSpec((None, 1, LANES), lambda i: (i, 0, 0))),
        scratch_shapes=[pltpu.VMEM((1, LANES), F32), pltpu.VMEM((tm, LANES), F32)],
        compiler_params=_params(("arbitrary",),
                                _nbytes((tm, d), F32) + _nbytes((tm, d), BF16)
                                + _nbytes((tm, LANES), BF16) + _nbytes((d, LANES), BF16),
                                4 * _nbytes((rows, d), F32) + 2 * _nbytes((tm, tm), F32)),
        name="ln_mod_fcum",
    )(x, mod, mod, wf, b_row)


def _wt_spec(tn, k, row0):
    assert row0 % SUBLANES == 0 and tn % SUBLANES == 0
    return pl.BlockSpec((pl.Element(tn), pl.Element(k)),
                        lambda i, j: ((row0 // SUBLANES + j * (tn // SUBLANES)) * SUBLANES, 0))


def _mm_nt_kernel(a_ref, w_ref, o_ref, *rest, sigmoid):
    w = _as_bf16(w_ref[...])
    tm = a_ref.shape[0]
    chunk = tm // 4 if sigmoid or rest else tm
    lane = lax.broadcasted_iota(jnp.int32, (1, LANES), 1)
    nrm = jnp.zeros((1, LANES), F32)
    for r in range(0, tm, chunk):
        acc = lax.dot_general(a_ref[r:r + chunk, :], w, _NT, preferred_element_type=F32)
        if sigmoid:
            acc = 0.5 * jnp.tanh(0.5 * acc) + 0.5
        o_ref[r:r + chunk, :] = acc.astype(o_ref.dtype)
        if rest:
            for g in range(acc.shape[1] // LANES):
                sq = jnp.sum(jnp.square(acc[:, g * LANES:(g + 1) * LANES]), axis=1, keepdims=True)
                nrm = jnp.maximum(nrm, jnp.where(lane == g, jnp.max(sq, axis=0, keepdims=True), 0.0))
    if rest:
        rest[0][...] = nrm


def _matmul_nt(a, wt, row0, n, out_dtype, *, tm, tn, sigmoid=False, group_norms=False, name):
    m, k = a.shape
    assert not (group_norms and sigmoid)
    out_shape = [jax.ShapeDtypeStruct((m, n), out_dtype)]
    out_specs = [pl.BlockSpec((tm, tn), lambda i, j: (i, j))]
    if group_norms:
        out_shape.append(jax.ShapeDtypeStruct((m // tm, n // tn, 1, LANES), F32))
        out_specs.append(pl.BlockSpec((None, None, 1, LANES), lambda i, j: (i, j, 0, 0)))
    out = pl.pallas_call(
        functools.partial(_mm_nt_kernel, sigmoid=sigmoid),
        out_shape=out_shape,
        grid=(m // tm, n // tn),
        in_specs=[pl.BlockSpec((tm, k), lambda i, j: (i, 0)), _wt_spec(tn, k, row0)],
        out_specs=out_specs,
        compiler_params=_params(("parallel", "parallel"),
                                _nbytes((tm, k), a.dtype) + _nbytes((tn, k), wt.dtype)
                                + _nbytes((tm, tn), out_dtype),
                                _cast_bytes((tn, k), wt.dtype) + 3 * _nbytes((tm, tn), F32)),
        name=name,
    )(a, wt)
    return out if group_norms else out[0]


def _mm_tn_kernel(a_ref, w_ref, o_ref, *rest, scale):
    acc_t = lax.dot_general(_as_bf16(w_ref[...]), a_ref[...], _NT,
                            preferred_element_type=F32)
    if scale is not None:
        acc_t = acc_t * scale
    tk = o_ref.shape[2]
    for kb in range(o_ref.shape[0]):
        o_ref[kb] = acc_t[:, kb * tk:(kb + 1) * tk].astype(o_ref.dtype)
    if rest:
        lane = lax.broadcasted_iota(jnp.int32, (1, LANES), 1)
        nrm = jnp.zeros((1, LANES), F32)
        for g in range(acc_t.shape[0] // LANES):
            sq = jnp.sum(jnp.square(acc_t[g * LANES:(g + 1) * LANES, :]), axis=0, keepdims=True)
            nrm = jnp.where(lane == g, jnp.max(sq, axis=1, keepdims=True), nrm)
        rest[0][...] = nrm


def _matmul_tn(a, wt, row0, n, *, tm, tn, tk, scale=None, group_norms=False, name):
    m, k = a.shape
    out_shape = [jax.ShapeDtypeStruct((m // tk, n, tk), BF16)]
    out_specs = [pl.BlockSpec((tm // tk, tn, tk), lambda i, j: (i, j, 0))]
    if group_norms:
        out_shape.append(jax.ShapeDtypeStruct((m // tm, n // tn, 1, LANES), F32))
        out_specs.append(pl.BlockSpec((None, None, 1, LANES), lambda i, j: (i, j, 0, 0)))
    out = pl.pallas_call(
        functools.partial(_mm_tn_kernel, scale=scale),
        out_shape=out_shape,
        grid=(m // tm, n // tn),
        in_specs=[pl.BlockSpec((tm, k), lambda i, j: (i, 0)), _wt_spec(tn, k, row0)],
        out_specs=out_specs,
        compiler_params=_params(("parallel", "parallel"),
                                _nbytes((tm, k), a.dtype) + _nbytes((tn, k), wt.dtype)
                                + _nbytes((tm, tn), BF16),
                                _cast_bytes((tn, k), wt.dtype) + 3 * _nbytes((tm, tn), F32)),
        name=name,
    )(a, wt)
    return out if group_norms else out[0]


def _attn_kernel(cb_ref, thr_ref, q_ref, k_ref, e_ref, vt_ref, w_ref, o_ref, wb_ref,
                 s0_ref, s1_ref, acc_ref, *, tk, tn, heads, side_blocks):
    h = pl.program_id(0)
    i = pl.program_id(1)

    @pl.when(h * pl.num_programs(1) + i < side_blocks)
    def _():
        wb_ref[...] = w_ref[...].astype(wb_ref.dtype)

    dh, tq = q_ref.shape
    nt = tq // tn
    half = nt // 2
    r = lax.broadcasted_iota(jnp.int32, (dh, tq), 0) - h
    ones_rows = sum(jnp.where(r == p * heads, 1.0, 0.0) for p in range(BIAS_PIECES))
    qt_aug = jnp.concatenate([q_ref[...], ones_rows.astype(BF16)], axis=0)
    ones_v = jnp.ones((acc_ref.shape[0] - dh, tk), BF16)
    c_q = cb_ref[h, 2 * i]
    thr = thr_ref[h, i]
    acc_ref[...] = jnp.zeros_like(acc_ref)

    def logits_to(buf_ref, j, first_tile=0):
        rows = pl.ds(pl.multiple_of(j * tk, tk), tk)
        k_aug = jnp.concatenate([k_ref[rows, :], e_ref[rows, :]], axis=1)
        cms = []
        for n in range(first_tile, nt):
            cols = slice(n * tn, (n + 1) * tn)
            s = jnp.dot(k_aug, qt_aug[:, cols], preferred_element_type=F32)
            buf_ref[:, cols] = s
            cms.append(jnp.max(s, axis=0, keepdims=True))
        return jnp.concatenate(cms, axis=1)

    def causal(buf_ref, n, first_tile):
        key = lax.broadcasted_iota(jnp.int32, (tk, tn), 0)
        qry = lax.broadcasted_iota(jnp.int32, (tk, tn), 1) + (n - first_tile) * tn
        return jnp.where(key <= qry, buf_ref[:, n * tn:(n + 1) * tn], -jnp.inf)

    def softmax_pv(buf_ref, j, cmax, m, first_tile=0, causal_tiles=0):
        lo = first_tile * tn
        c = c_q - cb_ref[h, j]
        m_old = m[:, lo:]
        m_new = jnp.maximum(m_old, cmax + c)
        a = jnp.exp2(m_old - m_new)
        off = m_new - c
        v_aug = jnp.concatenate([vt_ref[j], ones_v], axis=0)
        for n in range(first_tile, nt):
            cols = slice(n * tn, (n + 1) * tn)
            rel = slice(n * tn - lo, (n + 1) * tn - lo)
            s = causal(buf_ref, n, first_tile) if n < first_tile + causal_tiles else buf_ref[:, cols]
            p = jnp.exp2(s - off[:, rel]).astype(BF16)
            acc_ref[:, cols] = a[:, rel] * acc_ref[:, cols] + jnp.dot(
                v_aug, p, preferred_element_type=F32)
        if first_tile:
            m_new = jnp.concatenate([m[:, :lo], m_new], axis=1)
        return m_new

    def causal_max(buf_ref, first_tile):
        return [jnp.max(causal(buf_ref, n, first_tile), axis=0, keepdims=True)
                for n in range(first_tile, first_tile + half)]

    def pair(p, carry):
        m, cm0 = carry
        cm1 = logits_to(s1_ref, 2 * p + 1)
        m = softmax_pv(s0_ref, 2 * p, cm0, m)
        cm0 = logits_to(s0_ref, 2 * p + 2)
        m = softmax_pv(s1_ref, 2 * p + 1, cm1, m)
        return m, cm0

    first = lax.while_loop(lambda p: (p < i) & (cb_ref[h, 2 * p + 2] - c_q > thr),
                           lambda p: p + 1, jnp.int32(0))
    init = (jnp.full((1, tq), -jnp.inf, F32), logits_to(s0_ref, 2 * first))
    m, cm0 = lax.fori_loop(first, i, pair, init)
    logits_to(s1_ref, 2 * i + 1, first_tile=half)
    cm = jnp.concatenate(causal_max(s0_ref, 0) + [cm0[:, half * tn:]], axis=1)
    m = softmax_pv(s0_ref, 2 * i, cm, m, causal_tiles=half)
    cm = jnp.concatenate(causal_max(s1_ref, half), axis=1)
    softmax_pv(s1_ref, 2 * i + 1, cm, m, first_tile=half, causal_tiles=half)
    o_ref[...] = (acc_ref[:dh, :] / acc_ref[dh:dh + 1, :]).astype(o_ref.dtype)


def _attention(cb, thr, qt, k, e, vt, w_side, *, heads, tk, tn=256, side_blocks=32):
    s = k.shape[0]
    dh = FOX_HEAD_DIM
    tq = 2 * tk
    nq = s // tq
    side_cols = w_side.shape[1] // 2
    side_rows = 2 * w_side.shape[0] // side_blocks
    assert side_rows * side_blocks == 2 * w_side.shape[0] and side_rows % (2 * SUBLANES) == 0
    assert side_cols % LANES == 0 and side_blocks <= heads * nq

    def side(h, i, *_):
        blk = jnp.minimum(h * nq + i, side_blocks - 1)
        return blk // 2, blk % 2

    grid_spec = pltpu.PrefetchScalarGridSpec(
        num_scalar_prefetch=2,
        grid=(heads, nq),
        in_specs=[pl.BlockSpec((None, dh, tq), lambda h, i, *_: (i, h, 0)),
                  pl.BlockSpec((s, dh), lambda h, i, *_: (0, h)),
                  pl.BlockSpec((s, LANES), lambda h, i, *_: (0, 0)),
                  pl.BlockSpec((s // tk, dh, tk), lambda h, i, *_: (0, h, 0)),
                  pl.BlockSpec((side_rows, side_cols), side)],
        out_specs=[pl.BlockSpec((dh, tq), lambda h, i, *_: (h, i)),
                   pl.BlockSpec((side_rows, side_cols), side)],
        scratch_shapes=[pltpu.VMEM((tk, tq), F32), pltpu.VMEM((tk, tq), F32),
                        pltpu.VMEM((dh + ONES_ROWS, tq), F32)],
    )
    return pl.pallas_call(
        functools.partial(_attn_kernel, tk=tk, tn=tn, heads=heads, side_blocks=side_blocks),
        out_shape=[jax.ShapeDtypeStruct((heads * dh, s), BF16),
                   jax.ShapeDtypeStruct(w_side.shape, BF16)],
        grid_spec=grid_spec,
        compiler_params=_params(("arbitrary", "arbitrary"),
                                3 * _nbytes((s, dh), BF16) + 2 * _nbytes((tq, dh), BF16)
                                + _nbytes((side_rows, side_cols), F32)
                                + _nbytes((side_rows, side_cols), BF16),
                                6 * _nbytes((tk, tq), F32)),
        name="fox_attention",
    )(cb, thr, qt, k, e, vt, w_side)


def _mix_kernel(a_ref, wa_ref, halo_ref, p_ref, wp_ref, ps_ref, ga_ref, gp_ref, o_ref,
                wa_b, wp_b, *, tm):
    j = pl.program_id(0)
    i = pl.program_id(1)

    @pl.when(i == 0)
    def _():
        wa_b[...] = wa_ref[...].astype(BF16)
        wp_b[...] = wp_ref[...].astype(BF16)

    halo = jnp.where(i == 0, 0.0, halo_ref[...])
    ext = jnp.concatenate([halo, p_ref[...]], axis=0)
    acc = ext
    win = ext
    for g, w in enumerate(POOL_WINDOWS):
        acc = acc + pltpu.roll(acc, w // 2, 0)
        win = jnp.where(j == g, acc, win)
    t = i * tm + lax.broadcasted_iota(jnp.int32, (tm, 1), 0)
    cnt = jnp.minimum(t + 1, jnp.left_shift(2, j)).astype(F32)
    pooled = (win[POOL_HALO:] / cnt - ext[POOL_HALO:]).astype(BF16)

    rows = tm // MIX_CHUNKS
    for r in range(0, tm, rows):
        ya = lax.dot_general(a_ref[:, r:r + rows], wa_b[...], _TN,
                             preferred_element_type=F32)
        yp = jnp.dot(pooled[r:r + rows], wp_b[...], preferred_element_type=F32) * ps_ref[...]
        o_ref[r:r + rows, :] = (ga_ref[r:r + rows, :].astype(F32) * ya
                                + gp_ref[r:r + rows, :].astype(F32) * yp).astype(o_ref.dtype)


def _mix(attn_t, w_a, p, w_pool, pool_scale, gates, *, tm=512):
    fw, s = attn_t.shape
    groups, gd, tn = w_pool.shape
    assert POOL_WINDOWS == tuple(2 << g for g in range(groups))
    d = w_a.shape[1]
    hb = tm // POOL_HALO
    return pl.pallas_call(
        functools.partial(_mix_kernel, tm=tm),
        out_shape=jax.ShapeDtypeStruct((s, d), BF16),
        grid=(groups, s // tm),
        in_specs=[pl.BlockSpec((fw, tm), lambda j, i: (0, i)),
                  pl.BlockSpec((fw, tn), lambda j, i: (0, j)),
                  pl.BlockSpec((POOL_HALO, gd), lambda j, i: (jnp.maximum(i * hb - 1, 0), j)),
                  pl.BlockSpec((tm, gd), lambda j, i: (i, j)),
                  pl.BlockSpec((None, gd, tn), lambda j, i: (j, 0, 0)),
                  pl.BlockSpec((1, tn), lambda j, i: (0, j)),
                  pl.BlockSpec((tm, tn), lambda j, i: (i, j)),
                  pl.BlockSpec((tm, tn), lambda j, i: (i, groups + j))],
        out_specs=pl.BlockSpec((tm, tn), lambda j, i: (i, j)),
        scratch_shapes=[pltpu.VMEM((fw, tn), BF16), pltpu.VMEM((gd, tn), BF16)],
        compiler_params=_params(("parallel", "arbitrary"),
                                _nbytes((tm, fw), BF16) + _nbytes((fw, tn), w_a.dtype)
                                + _nbytes((tm + POOL_HALO, gd), F32) + _nbytes((gd, tn), w_pool.dtype)
                                + 3 * _nbytes((tm, tn), BF16),
                                _nbytes((fw + gd, tn), BF16) + 3 * _nbytes((tm, tn), F32)
                                + 4 * _nbytes((tm + POOL_HALO, gd), F32)),
        name="branch_mix",
    )(attn_t, w_a, p, p, w_pool, pool_scale, gates, gates)


def _resid_mm_kernel(a_ref, w_ref, x_ref, g_ref, o_ref, *, chunks):
    w = _as_bf16(w_ref[...])
    rows = a_ref.shape[0] // chunks
    for r in range(0, a_ref.shape[0], rows):
        m = jnp.dot(a_ref[r:r + rows, :], w, preferred_element_type=F32)
        o_ref[r:r + rows, :] = ALPHA * x_ref[r:r + rows, :] + g_ref[...] * m


def _resid_matmul(a, w, x, mod, gate_idx, *, tm, tn, hold_rows=False, chunks=1, name):
    m, k = a.shape
    n = w.shape[1]
    nb = n // tn
    a_spec = _held_rows_spec(tm, k) if hold_rows else pl.BlockSpec((tm, k), lambda i, j: (i, 0))
    return pl.pallas_call(
        functools.partial(_resid_mm_kernel, chunks=chunks),
        out_shape=jax.ShapeDtypeStruct((m, n), F32),
        grid=(m // tm, nb),
        in_specs=[a_spec,
                  pl.BlockSpec((k, tn), lambda i, j: (0, j)),
                  pl.BlockSpec((tm, tn), lambda i, j: (i, j)),
                  pl.BlockSpec((1, tn), lambda i, j: (0, gate_idx * nb + j))],
        out_specs=pl.BlockSpec((tm, tn), lambda i, j: (i, j)),
        compiler_params=_params(("parallel", "parallel"),
                                _nbytes((k, tn), w.dtype) + 2 * _nbytes((tm, tn), F32)
                                + (0 if hold_rows else _nbytes((tm, k), BF16)),
                                _cast_bytes((k, tn), w.dtype) + 2 * _nbytes((tm, tn), F32)
                                + (_nbytes((tm, k), BF16) if hold_rows else 0)),
        name=name,
    )(a, w, x, mod)


def _ln_ln_mod_kernel(r_ref, g_ref, b_ref, sh_ref, sc_ref, x_ref, u_ref):
    x1 = _ln_rows(r_ref[...]) * g_ref[...] + b_ref[...]
    x_ref[...] = x1
    u_ref[...] = (_ln_rows(x1) * (1.0 + sc_ref[...]) + sh_ref[...]).astype(u_ref.dtype)


def _ln_ln_mod(r, gain, bias, mod, shift_idx, scale_idx, *, tm=256):
    s, d = r.shape
    row = pl.BlockSpec((tm, d), lambda i: (i, 0))
    vec = pl.BlockSpec((1, d), lambda i: (0, 0))
    return pl.pallas_call(
        _ln_ln_mod_kernel,
        out_shape=(jax.ShapeDtypeStruct((s, d), F32), jax.ShapeDtypeStruct((s, d), BF16)),
        grid=(s // tm,),
        in_specs=[row, vec, vec,
                  pl.BlockSpec((1, d), lambda i: (0, shift_idx)),
                  pl.BlockSpec((1, d), lambda i: (0, scale_idx))],
        out_specs=(row, row),
        compiler_params=_params(("parallel",),
                                2 * _nbytes((tm, d), F32) + _nbytes((tm, d), BF16),
                                4 * _nbytes((tm, d), F32)),
        name="ln1_ln_mod",
    )(r, gain, bias, mod, mod)


def _ln_affine_kernel(r_ref, g_ref, b_ref, o_ref):
    o_ref[...] = _ln_rows(r_ref[...]) * g_ref[...] + b_ref[...]


def _ln_affine(r, gain, bias, *, tm=256):
    s, d = r.shape
    row = pl.BlockSpec((tm, d), lambda i: (i, 0))
    vec = pl.BlockSpec((1, d), lambda i: (0, 0))
    return pl.pallas_call(
        _ln_affine_kernel,
        out_shape=jax.ShapeDtypeStruct((s, d), F32),
        grid=(s // tm,),
        in_specs=[row, vec, vec],
        out_specs=row,
        compiler_params=_params(("parallel",), 2 * _nbytes((tm, d), F32),
                                3 * _nbytes((tm, d), F32)),
        name="ln2",
    )(r, gain, bias)


def _ffn_up_kernel(u_ref, wg_ref, wu_ref, o_ref):
    wg = _as_bf16(wg_ref[...])
    wu = _as_bf16(wu_ref[...])
    rows = u_ref.shape[0] // FFN_CHUNKS
    for r in range(0, u_ref.shape[0], rows):
        u = u_ref[r:r + rows, :]
        g = jnp.dot(u, wg, preferred_element_type=F32)
        up = jnp.dot(u, wu, preferred_element_type=F32)
        o_ref[r:r + rows, :] = (g * jax.nn.sigmoid(g) * up).astype(o_ref.dtype)


def _ffn_up(u, w_gate_up, *, tm=2048, tn=256):
    s, d = u.shape
    hidden = w_gate_up.shape[1] // 2
    nb = hidden // tn
    return pl.pallas_call(
        _ffn_up_kernel,
        out_shape=jax.ShapeDtypeStruct((s, hidden), BF16),
        grid=(s // tm, nb),
        in_specs=[_held_rows_spec(tm, d),
                  pl.BlockSpec((d, tn), lambda i, j: (0, j)),
                  pl.BlockSpec((d, tn), lambda i, j: (0, nb + j))],
        out_specs=pl.BlockSpec((tm, tn), lambda i, j: (i, j)),
        compiler_params=_params(("parallel", "parallel"),
                                2 * _nbytes((d, tn), w_gate_up.dtype) + _nbytes((tm, tn), BF16),
                                _nbytes((tm, d), BF16) + 2 * _nbytes((d, tn), BF16)
                                + 5 * _nbytes((tm, tn), F32)),
        name="ffn_up",
    )(u, w_gate_up, w_gate_up)


def kernel(x, c, w_ada, b_ada, w_in, b_forget, w_attn_out, w_pool, pool_scale, w_out,
           ln1_g, ln1_b, w_gate_up, w_down, ln2_g, ln2_b):
    batch, seq, d = x.shape
    assert batch == 1 and w_ada.shape[0] == DEPTH == 1
    fox_w = w_attn_out.shape[1]
    heads = fox_w // FOX_HEAD_DIM
    pool_w = w_pool.shape[1] * w_pool.shape[2]
    o_f = 3 * fox_w
    o_p = o_f + heads
    tk = 512

    xs = x[0]
    w_in_t = jnp.swapaxes(w_in, 1, 2)[0]
    w_f = jnp.pad(w_in_t[o_f:o_p], ((0, LANES - heads), (0, 0))).astype(BF16)
    b_f = jnp.pad(b_forget, ((0, 0), (0, LANES - heads)))

    mod = _ada(c.reshape(d, 1), w_ada[0], b_ada)

    u1, e, c_blk = _ln_mod_fcum(xs, mod, 0, 1, w_f, b_f, heads=heads, tm=tk)
    cb = c_blk[:, 0, :heads].T
    tq = 2 * tk
    qt, sq_q = _matmul_tn(u1, w_in_t, 0, fox_w, tm=tq, tn=512, tk=tq, scale=FOX_HEAD_DIM ** -0.5 * LOG2E,
                          group_norms=True, name="proj_qt")
    k, sq_k = _matmul_nt(u1, w_in_t, fox_w, fox_w, BF16, tm=tq, tn=512, group_norms=True, name="proj_k")
    per_head = lambda sq: jnp.sqrt(sq[:, :, 0, :512 // LANES].reshape(seq // tq, heads))
    thr = (SKIP_LOG2 + 2.0 * NORM_SLACK * per_head(sq_q) * jnp.max(per_head(sq_k), axis=0)).T
    vt = _matmul_tn(u1, w_in_t, 2 * fox_w, fox_w, tm=1024, tn=512, tk=tk, name="proj_vt")
    p = _matmul_nt(u1, w_in_t, o_p, pool_w, F32, tm=1024, tn=512, name="proj_pool")
    gates = _matmul_nt(u1, w_in_t, o_p + pool_w, 2 * d, BF16, tm=1024, tn=512, sigmoid=True,
                       name="proj_gates")
    attn, w_down_bf16 = _attention(cb, thr, qt, k, e, vt, w_down[0], heads=heads, tk=tk)
    mix = _mix(attn, w_attn_out[0], p, w_pool[0], pool_scale, gates)
    r1 = _resid_matmul(mix, w_out[0], xs, mod, 2, tm=2048, tn=256, hold_rows=True, chunks=4, name="out_proj")
    x1, u2 = _ln_ln_mod(r1, ln1_g, ln1_b, mod, 3, 4)

    act = _ffn_up(u2, w_gate_up[0])
    r2 = _resid_matmul(act, w_down_bf16, x1, mod, 5, tm=512, tn=512, name="ffn_down")
    out = _ln_affine(r2, ln2_g, ln2_b)
    return out[None]
```

```python
import functools

import jax
import jax.numpy as jnp
from jax import lax
from jax.experimental import pallas as pl
from jax.experimental.pallas import tpu as pltpu

F32 = jnp.float32
BF16 = jnp.bfloat16

FOX_HEAD_DIM = 128
POOL_WINDOWS = (2, 4, 8, 16)
POOL_HALO = 16
DEPTH = 1
ALPHA = (2 * DEPTH) ** 0.25
LN_EPS = 1e-5
LOG2E = 1.4426950408889634
LANES = 128
SUBLANES = 8
BIAS_PIECES = 3
ONES_ROWS = 16
FFN_CHUNKS = 4
SKIP_LOG2 = 48.0
NORM_SLACK = 1.02

V7X_VMEM_BYTES = 64 * 1024 * 1024
VMEM_CAP_BYTES = V7X_VMEM_BYTES - 6 * 1024 * 1024

_NT = (((1,), (1,)), ((), ()))
_TN = (((0,), (0,)), ((), ()))


def _nbytes(shape, dtype):
    n = jnp.dtype(dtype).itemsize
    for s in shape:
        n *= s
    return n


def _cast_bytes(shape, dtype):
    return 0 if dtype == BF16 else _nbytes(shape, BF16)


def _params(semantics, pipelined_bytes, resident_bytes=0):
    need = 2 * pipelined_bytes + resident_bytes
    assert need <= VMEM_CAP_BYTES, (need, VMEM_CAP_BYTES)
    return pltpu.CompilerParams(dimension_semantics=semantics,
                                vmem_limit_bytes=min(need + (4 << 20), VMEM_CAP_BYTES))


def _ln_rows(x):
    mu = jnp.mean(x, axis=-1, keepdims=True)
    xc = x - mu
    var = jnp.mean(xc * xc, axis=-1, keepdims=True)
    return xc * lax.rsqrt(var + LN_EPS)


def _as_bf16(w):
    return w if w.dtype == BF16 else w.astype(BF16)


def _split3(v):
    hi = v.astype(BF16)
    r1 = v - hi.astype(F32)
    mid = r1.astype(BF16)
    lo = (r1 - mid.astype(F32)).astype(BF16)
    return hi, mid, lo


def _ada_kernel(c_ref, w_ref, b_ref, o_ref):
    k = pl.program_id(1)

    @pl.when(k == 0)
    def _():
        o_ref[...] = b_ref[...]

    cc = c_ref[...]
    s = cc * jax.nn.sigmoid(cc)
    o_ref[...] += jnp.sum(s * w_ref[...], axis=0, keepdims=True)


def _ada(c_col, w_ada, b_ada, *, tk=512, tn=4096):
    d, n = w_ada.shape
    return pl.pallas_call(
        _ada_kernel,
        out_shape=jax.ShapeDtypeStruct((1, n), F32),
        grid=(n // tn, d // tk),
        in_specs=[pl.BlockSpec((tk, 1), lambda j, k: (k, 0)),
                  pl.BlockSpec((tk, tn), lambda j, k: (k, j)),
                  pl.BlockSpec((1, tn), lambda j, k: (0, j))],
        out_specs=pl.BlockSpec((1, tn), lambda j, k: (0, j)),
        compiler_params=_params(("parallel", "arbitrary"),
                                _nbytes((tk, tn), F32) + _nbytes((tk, LANES), F32),
                                _nbytes((tk, tn), F32)),
        name="ada",
    )(c_col, w_ada, b_ada)


def _ln_mod_fcum_kernel(x_ref, sh_ref, sc_ref, wf_ref, b_ref, u_ref, e_ref, c_ref,
                        carry_ref, f_ref, *, tm, rows, heads):
    @pl.when(pl.program_id(0) == 0)
    def _():
        carry_ref[...] = jnp.zeros_like(carry_ref)

    gain = 1.0 + sc_ref[...]
    shift = sh_ref[...]
    for r in range(0, tm, rows):
        u = (_ln_rows(x_ref[r:r + rows, :]) * gain + shift).astype(BF16)
        u_ref[r:r + rows, :] = u
        f_ref[r:r + rows, :] = lax.dot_general(u, wf_ref[...], _NT, preferred_element_type=F32)
    f = f_ref[...] + b_ref[...]
    ls = (jnp.minimum(f, 0.0) - jnp.log1p(jnp.exp(-jnp.abs(f)))) * LOG2E
    row = lax.broadcasted_iota(jnp.int32, (tm, tm), 0)
    col = lax.broadcasted_iota(jnp.int32, (tm, tm), 1)
    tri = jnp.where(col <= row, 1.0, 0.0).astype(BF16)
    cs = sum(jnp.dot(tri, piece, preferred_element_type=F32) for piece in _split3(ls))
    carry = carry_ref[...]
    c_ref[...] = carry
    carry_ref[...] = carry + cs[tm - 1:tm, :]
    head_lane = lax.broadcasted_iota(jnp.int32, (tm, LANES), 1) < heads
    e = sum(pltpu.roll(jnp.where(head_lane, piece.astype(F32), 0.0), p * heads, 1) if p else
            jnp.where(head_lane, piece.astype(F32), 0.0)
            for p, piece in enumerate(_split3(-cs)))
    e_ref[...] = e.astype(e_ref.dtype)


def _ln_mod_fcum(x, mod, shift_idx, scale_idx, wf, b_row, *, heads, tm, rows=128):
    s, d = x.shape
    nb = s // tm
    assert BIAS_PIECES * heads <= LANES
    vec = lambda idx: pl.BlockSpec((1, d), lambda i: (0, idx))
    const = lambda shape: pl.BlockSpec(shape, lambda i: (0, 0))
    return pl.pallas_call(
        functools.partial(_ln_mod_fcum_kernel, tm=tm, rows=rows, heads=heads),
        out_shape=(jax.ShapeDtypeStruct((s, d), BF16),
                   jax.ShapeDtypeStruct((s, LANES), BF16),
                   jax.ShapeDtypeStruct((nb, 1, LANES), F32)),
        grid=(nb,),
        in_specs=[pl.BlockSpec((tm, d), lambda i: (i, 0)), vec(shift_idx), vec(scale_idx),
                  const((LANES, d)), const((1, LANES))],
        out_specs=(pl.BlockSpec((tm, d), lambda i: (i, 0)),
                   pl.BlockSpec((tm, LANES), lambda i: (i, 0)),
                   pl.BlockSpec((None, 1, LANES), lambda i: (i, 0, 0))),
        scratch_shapes=[pltpu.VMEM((1, LANES), F32), pltpu.VMEM((tm, LANES), F32)],
        compiler_params=_params(("arbitrary",),
                                _nbytes((tm, d), F32) + _nbytes((tm, d), BF16)
                                + _nbytes((tm, LANES), BF16) + _nbytes((d, LANES), BF16),
                                4 * _nbytes((rows, d), F32) + 2 * _nbytes((tm, tm), F32)),
        name="ln_mod_fcum",
    )(x, mod, mod, wf, b_row)


def _wt_spec(tn, k, row0):
    assert row0 % SUBLANES == 0 and tn % SUBLANES == 0
    return pl.BlockSpec((pl.Element(tn), pl.Element(k)),
                        lambda i, j: ((row0 // SUBLANES + j * (tn // SUBLANES)) * SUBLANES, 0))


def _mm_nt_kernel(a_ref, w_ref, o_ref, *rest, sigmoid):
    w = _as_bf16(w_ref[...])
    tm = a_ref.shape[0]
    chunk = tm // 4 if sigmoid or rest else tm
    lane = lax.broadcasted_iota(jnp.int32, (1, LANES), 1)
    nrm = jnp.zeros((1, LANES), F32)
    for r in range(0, tm, chunk):
        acc = lax.dot_general(a_ref[r:r + chunk, :], w, _NT, preferred_element_type=F32)
        if sigmoid:
            acc = 0.5 * jnp.tanh(0.5 * acc) + 0.5
        o_ref[r:r + chunk, :] = acc.astype(o_ref.dtype)
        if rest:
            for g in range(acc.shape[1] // LANES):
                sq = jnp.sum(jnp.square(acc[:, g * LANES:(g + 1) * LANES]), axis=1, keepdims=True)
                nrm = jnp.maximum(nrm, jnp.where(lane == g, jnp.max(sq, axis=0, keepdims=True), 0.0))
    if rest:
        rest[0][...] = nrm


def _matmul_nt(a, wt, row0, n, out_dtype, *, tm, tn, sigmoid=False, group_norms=False, name):
    m, k = a.shape
    assert not (group_norms and sigmoid)
    out_shape = [jax.ShapeDtypeStruct((m, n), out_dtype)]
    out_specs = [pl.BlockSpec((tm, tn), lambda i, j: (i, j))]
    if group_norms:
        out_shape.append(jax.ShapeDtypeStruct((m // tm, n // tn, 1, LANES), F32))
        out_specs.append(pl.BlockSpec((None, None, 1, LANES), lambda i, j: (i, j, 0, 0)))
    out = pl.pallas_call(
        functools.partial(_mm_nt_kernel, sigmoid=sigmoid),
        out_shape=out_shape,
        grid=(m // tm, n // tn),
        in_specs=[pl.BlockSpec((tm, k), lambda i, j: (i, 0)), _wt_spec(tn, k, row0)],
        out_specs=out_specs,
        compiler_params=_params(("parallel", "parallel"),
                                _nbytes((tm, k), a.dtype) + _nbytes((tn, k), wt.dtype)
                                + _nbytes((tm, tn), out_dtype),
                                _cast_bytes((tn, k), wt.dtype) + 3 * _nbytes((tm, tn), F32)),
        name=name,
    )(a, wt)
    return out if group_norms else out[0]


def _mm_tn_kernel(a_ref, w_ref, o_ref, *rest, scale):
    acc_t = lax.dot_general(_as_bf16(w_ref[...]), a_ref[...], _NT,
                            preferred_element_type=F32)
    if scale is not None:
        acc_t = acc_t * scale
    tk = o_ref.shape[2]
    for kb in range(o_ref.shape[0]):
        o_ref[kb] = acc_t[:, kb * tk:(kb + 1) * tk].astype(o_ref.dtype)
    if rest:
        lane = lax.broadcasted_iota(jnp.int32, (1, LANES), 1)
        nrm = jnp.zeros((1, LANES), F32)
        for g in range(acc_t.shape[0] // LANES):
            sq = jnp.sum(jnp.square(acc_t[g * LANES:(g + 1) * LANES, :]), axis=0, keepdims=True)
            nrm = jnp.where(lane == g, jnp.max(sq, axis=1, keepdims=True), nrm)
        rest[0][...] = nrm


def _matmul_tn(a, wt, row0, n, *, tm, tn, tk, scale=None, group_norms=False, name):
    m, k = a.shape
    out_shape = [jax.ShapeDtypeStruct((m // tk, n, tk), BF16)]
    out_specs = [pl.BlockSpec((tm // tk, tn, tk), lambda i, j: (i, j, 0))]
    if group_norms:
        out_shape.append(jax.ShapeDtypeStruct((m // tm, n // tn, 1, LANES), F32))
        out_specs.append(pl.BlockSpec((None, None, 1, LANES), lambda i, j: (i, j, 0, 0)))
    out = pl.pallas_call(
        functools.partial(_mm_tn_kernel, scale=scale),
        out_shape=out_shape,
        grid=(m // tm, n // tn),
        in_specs=[pl.BlockSpec((tm, k), lambda i, j: (i, 0)), _wt_spec(tn, k, row0)],
        out_specs=out_specs,
        compiler_params=_params(("parallel", "parallel"),
                                _nbytes((tm, k), a.dtype) + _nbytes((tn, k), wt.dtype)
                                + _nbytes((tm, tn), BF16),
                                _cast_bytes((tn, k), wt.dtype) + 3 * _nbytes((tm, tn), F32)),
        name=name,
    )(a, wt)
    return out if group_norms else out[0]


def _attn_kernel(cb_ref, thr_ref, q_ref, k_ref, e_ref, vt_ref, w_ref, o_ref, wb_ref,
                 s0_ref, s1_ref, acc_ref, *, tk, tn, heads, side_blocks):
    h = pl.program_id(0)
    i = pl.program_id(1)

    @pl.when(h * pl.num_programs(1) + i < side_blocks)
    def _():
        wb_ref[...] = w_ref[...].astype(wb_ref.dtype)

    dh, tq = q_ref.shape
    nt = tq // tn
    half = nt // 2
    r = lax.broadcasted_iota(jnp.int32, (dh, tq), 0) - h
    ones_rows = sum(jnp.where(r == p * heads, 1.0, 0.0) for p in range(BIAS_PIECES))
    qt_aug = jnp.concatenate([q_ref[...], ones_rows.astype(BF16)], axis=0)
    ones_v = jnp.ones((acc_ref.shape[0] - dh, tk), BF16)
    c_q = cb_ref[h, 2 * i]
    thr = thr_ref[h, i]
    acc_ref[...] = jnp.zeros_like(acc_ref)

    def logits_to(buf_ref, j, first_tile=0):
        rows = pl.ds(pl.multiple_of(j * tk, tk), tk)
        k_aug = jnp.concatenate([k_ref[rows, :], e_ref[rows, :]], axis=1)
        cms = []
        for n in range(first_tile, nt):
            cols = slice(n * tn, (n + 1) * tn)
            s = jnp.dot(k_aug, qt_aug[:, cols], preferred_element_type=F32)
            buf_ref[:, cols] = s
            cms.append(jnp.max(s, axis=0, keepdims=True))
        return jnp.concatenate(cms, axis=1)

    def causal(buf_ref, n, first_tile):
        key = lax.broadcasted_iota(jnp.int32, (tk, tn), 0)
        qry = lax.broadcasted_iota(jnp.int32, (tk, tn), 1) + (n - first_tile) * tn
        return jnp.where(key <= qry, buf_ref[:, n * tn:(n + 1) * tn], -jnp.inf)

    def softmax_pv(buf_ref, j, cmax, m, first_tile=0, causal_tiles=0):
        lo = first_tile * tn
        c = c_q - cb_ref[h, j]
        m_old = m[:, lo:]
        m_new = jnp.maximum(m_old, cmax + c)
        a = jnp.exp2(m_old - m_new)
        off = m_new - c
        v_aug = jnp.concatenate([vt_ref[j], ones_v], axis=0)
        for n in range(first_tile, nt):
            cols = slice(n * tn, (n + 1) * tn)
            rel = slice(n * tn - lo, (n + 1) * tn - lo)
            s = causal(buf_ref, n, first_tile) if n < first_tile + causal_tiles else buf_ref[:, cols]
            p = jnp.exp2(s - off[:, rel]).astype(BF16)
            acc_ref[:, cols] = a[:, rel] * acc_ref[:, cols] + jnp.dot(
                v_aug, p, preferred_element_type=F32)
        if first_tile:
            m_new = jnp.concatenate([m[:, :lo], m_new], axis=1)
        return m_new

    def causal_max(buf_ref, first_tile):
        return [jnp.max(causal(buf_ref, n, first_tile), axis=0, keepdims=True)
                for n in range(first_tile, first_tile + half)]

    def pair(p, carry):
        m, cm0 = carry
        cm1 = logits_to(s1_ref, 2 * p + 1)
        m = softmax_pv(s0_ref, 2 * p, cm0, m)
        cm0 = logits_to(s0_ref, 2 * p + 2)
        m = softmax_pv(s1_ref, 2 * p + 1, cm1, m)
        return m, cm0

    first = lax.while_loop(lambda p: (p < i) & (cb_ref[h, 2 * p + 2] - c_q > thr),
                           lambda p: p + 1, jnp.int32(0))
    init = (jnp.full((1, tq), -jnp.inf, F32), logits_to(s0_ref, 2 * first))
    m, cm0 = lax.fori_loop(first, i, pair, init)
    logits_to(s1_ref, 2 * i + 1, first_tile=half)
    cm = jnp.concatenate(causal_max(s0_ref, 0) + [cm0[:, half * tn:]], axis=1)
    m = softmax_pv(s0_ref, 2 * i, cm, m, causal_tiles=half)
    cm = jnp.concatenate(causal_max(s1_ref, half), axis=1)
    softmax_pv(s1_ref, 2 * i + 1, cm, m, first_tile=half, causal_tiles=half)
    o_ref[...] = (acc_ref[:dh, :] / acc_ref[dh:dh + 1, :]).astype(o_ref.dtype)


def _attention(cb, thr, qt, k, e, vt, w_side, *, heads, tk, tn=256, side_blocks=32):
    s = k.shape[0]
    dh = FOX_HEAD_DIM
    tq = 2 * tk
    nq = s // tq
    side_cols = w_side.shape[1] // 2
    side_rows = 2 * w_side.shape[0] // side_blocks
    assert side_rows * side_blocks == 2 * w_side.shape[0] and side_rows % (2 * SUBLANES) == 0
    assert side_cols % LANES == 0 and side_blocks <= heads * nq

    def side(h, i, *_):
        blk = jnp.minimum(h * nq + i, side_blocks - 1)
        return blk // 2, blk % 2

    grid_spec = pltpu.PrefetchScalarGridSpec(
        num_scalar_prefetch=2,
        grid=(heads, nq),
        in_specs=[pl.BlockSpec((None, dh, tq), lambda h, i, *_: (i, h, 0)),
                  pl.BlockSpec((s, dh), lambda h, i, *_: (0, h)),
                  pl.BlockSpec((s, LANES), lambda h, i, *_: (0, 0)),
                  pl.BlockSpec((s // tk, dh, tk), lambda h, i, *_: (0, h, 0)),
                  pl.BlockSpec((side_rows, side_cols), side)],
        out_specs=[pl.BlockSpec((dh, tq), lambda h, i, *_: (h, i)),
                   pl.BlockSpec((side_rows, side_cols), side)],
        scratch_shapes=[pltpu.VMEM((tk, tq), F32), pltpu.VMEM((tk, tq), F32),
                        pltpu.VMEM((dh + ONES_ROWS, tq), F32)],
    )
    return pl.pallas_call(
        functools.partial(_attn_kernel, tk=tk, tn=tn, heads=heads, side_blocks=side_blocks),
        out_shape=[jax.ShapeDtypeStruct((heads * dh, s), BF16),
                   jax.ShapeDtypeStruct(w_side.shape, BF16)],
        grid_spec=grid_spec,
        compiler_params=_params(("arbitrary", "arbitrary"),
                                3 * _nbytes((s, dh), BF16) + 2 * _nbytes((tq, dh), BF16)
                                + _nbytes((side_rows, side_cols), F32)
                                + _nbytes((side_rows, side_cols), BF16),
                                6 * _nbytes((tk, tq), F32)),
        name="fox_attention",
    )(cb, thr, qt, k, e, vt, w_side)


def _mix_kernel(a_ref, wa_ref, halo_ref, p_ref, wp_ref, ps_ref, ga_ref, gp_ref, o_ref,
                wa_b, wp_b, *, tm):
    j = pl.program_id(0)
    i = pl.program_id(1)

    @pl.when(i == 0)
    def _():
        wa_b[...] = wa_ref[...].astype(BF16)
        wp_b[...] = wp_ref[...].astype(BF16)

    halo = jnp.where(i == 0, 0.0, halo_ref[...])
    ext = jnp.concatenate([halo, p_ref[...]], axis=0)
    acc = ext
    win = ext
    for g, w in enumerate(POOL_WINDOWS):
        acc = acc + pltpu.roll(acc, w // 2, 0)
        win = jnp.where(j == g, acc, win)
    t = i * tm + lax.broadcasted_iota(jnp.int32, (tm, 1), 0)
    cnt = jnp.minimum(t + 1, jnp.left_shift(2, j)).astype(F32)
    pooled = (win[POOL_HALO:] / cnt - ext[POOL_HALO:]).astype(BF16)

    ya = lax.dot_general(a_ref[...], wa_b[...], _TN, preferred_element_type=F32)
    yp = jnp.dot(pooled, wp_b[...], preferred_element_type=F32) * ps_ref[...]
    o_ref[...] = (ga_ref[...].astype(F32) * ya + gp_ref[...].astype(F32) * yp).astype(o_ref.dtype)


def _mix(attn_t, w_a, p, w_pool, pool_scale, gates, *, tm=512):
    fw, s = attn_t.shape
    groups, gd, tn = w_pool.shape
    assert POOL_WINDOWS == tuple(2 << g for g in range(groups))
    d = w_a.shape[1]
    hb = tm // POOL_HALO
    return pl.pallas_call(
        functools.partial(_mix_kernel, tm=tm),
        out_shape=jax.ShapeDtypeStruct((s, d), BF16),
        grid=(groups, s // tm),
        in_specs=[pl.BlockSpec((fw, tm), lambda j, i: (0, i)),
                  pl.BlockSpec((fw, tn), lambda j, i: (0, j)),
                  pl.BlockSpec((POOL_HALO, gd), lambda j, i: (jnp.maximum(i * hb - 1, 0), j)),
                  pl.BlockSpec((tm, gd), lambda j, i: (i, j)),
                  pl.BlockSpec((None, gd, tn), lambda j, i: (j, 0, 0)),
                  pl.BlockSpec((1, tn), lambda j, i: (0, j)),
                  pl.BlockSpec((tm, tn), lambda j, i: (i, j)),
                  pl.BlockSpec((tm, tn), lambda j, i: (i, groups + j))],
        out_specs=pl.BlockSpec((tm, tn), lambda j, i: (i, j)),
        scratch_shapes=[pltpu.VMEM((fw, tn), BF16), pltpu.VMEM((gd, tn), BF16)],
        compiler_params=_params(("parallel", "arbitrary"),
                                _nbytes((tm, fw), BF16) + _nbytes((fw, tn), w_a.dtype)
                                + _nbytes((tm + POOL_HALO, gd), F32) + _nbytes((gd, tn), w_pool.dtype)
                                + 3 * _nbytes((tm, tn), BF16),
                                _nbytes((fw + gd, tn), BF16) + 3 * _nbytes((tm, tn), F32)
                                + 4 * _nbytes((tm + POOL_HALO, gd), F32)),
        name="branch_mix",
    )(attn_t, w_a, p, p, w_pool, pool_scale, gates, gates)


def _resid_mm_kernel(a_ref, w_ref, x_ref, g_ref, o_ref, *, chunks):
    w = _as_bf16(w_ref[...])
    rows = a_ref.shape[0] // chunks
    for r in range(0, a_ref.shape[0], rows):
        m = jnp.dot(a_ref[r:r + rows, :], w, preferred_element_type=F32)
        o_ref[r:r + rows, :] = ALPHA * x_ref[r:r + rows, :] + g_ref[...] * m


def _resid_matmul(a, w, x, mod, gate_idx, *, tm, tn, chunks=1, name):
    m, k = a.shape
    n = w.shape[1]
    nb = n // tn
    return pl.pallas_call(
        functools.partial(_resid_mm_kernel, chunks=chunks),
        out_shape=jax.ShapeDtypeStruct((m, n), F32),
        grid=(m // tm, nb),
        in_specs=[pl.BlockSpec((tm, k), lambda i, j: (i, 0)),
                  pl.BlockSpec((k, tn), lambda i, j: (0, j)),
                  pl.BlockSpec((tm, tn), lambda i, j: (i, j)),
                  pl.BlockSpec((1, tn), lambda i, j: (0, gate_idx * nb + j))],
        out_specs=pl.BlockSpec((tm, tn), lambda i, j: (i, j)),
        compiler_params=_params(("parallel", "parallel"),
                                _nbytes((tm, k), BF16) + _nbytes((k, tn), w.dtype)
                                + 2 * _nbytes((tm, tn), F32),
                                _cast_bytes((k, tn), w.dtype) + 2 * _nbytes((tm, tn), F32) // chunks),
        name=name,
    )(a, w, x, mod)


def _ln_ln_mod_kernel(r_ref, g_ref, b_ref, sh_ref, sc_ref, x_ref, u_ref):
    x1 = _ln_rows(r_ref[...]) * g_ref[...] + b_ref[...]
    x_ref[...] = x1
    u_ref[...] = (_ln_rows(x1) * (1.0 + sc_ref[...]) + sh_ref[...]).astype(u_ref.dtype)


def _ln_ln_mod(r, gain, bias, mod, shift_idx, scale_idx, *, tm=256):
    s, d = r.shape
    row = pl.BlockSpec((tm, d), lambda i: (i, 0))
    vec = pl.BlockSpec((1, d), lambda i: (0, 0))
    return pl.pallas_call(
        _ln_ln_mod_kernel,
        out_shape=(jax.ShapeDtypeStruct((s, d), F32), jax.ShapeDtypeStruct((s, d), BF16)),
        grid=(s // tm,),
        in_specs=[row, vec, vec,
                  pl.BlockSpec((1, d), lambda i: (0, shift_idx)),
                  pl.BlockSpec((1, d), lambda i: (0, scale_idx))],
        out_specs=(row, row),
        compiler_params=_params(("parallel",),
                                2 * _nbytes((tm, d), F32) + _nbytes((tm, d), BF16),
                                4 * _nbytes((tm, d), F32)),
        name="ln1_ln_mod",
    )(r, gain, bias, mod, mod)


def _ln_affine_kernel(r_ref, g_ref, b_ref, o_ref):
    o_ref[...] = _ln_rows(r_ref[...]) * g_ref[...] + b_ref[...]


def _ln_affine(r, gain, bias, *, tm=256):
    s, d = r.shape
    row = pl.BlockSpec((tm, d), lambda i: (i, 0))
    vec = pl.BlockSpec((1, d), lambda i: (0, 0))
    return pl.pallas_call(
        _ln_affine_kernel,
        out_shape=jax.ShapeDtypeStruct((s, d), F32),
        grid=(s // tm,),
        in_specs=[row, vec, vec],
        out_specs=row,
        compiler_params=_params(("parallel",), 2 * _nbytes((tm, d), F32),
                                3 * _nbytes((tm, d), F32)),
        name="ln2",
    )(r, gain, bias)


def _ffn_up_kernel(u_ref, wg_ref, wu_ref, o_ref):
    wg = _as_bf16(wg_ref[...])
    wu = _as_bf16(wu_ref[...])
    rows = u_ref.shape[0] // FFN_CHUNKS
    for r in range(0, u_ref.shape[0], rows):
        u = u_ref[r:r + rows, :]
        g = jnp.dot(u, wg, preferred_element_type=F32)
        up = jnp.dot(u, wu, preferred_element_type=F32)
        o_ref[r:r + rows, :] = (g * jax.nn.sigmoid(g) * up).astype(o_ref.dtype)


def _ffn_up(u, w_gate_up, *, tm=2048, tn=256):
    s, d = u.shape
    hidden = w_gate_up.shape[1] // 2
    nb = hidden // tn
    return pl.pallas_call(
        _ffn_up_kernel,
        out_shape=jax.ShapeDtypeStruct((s, hidden), BF16),
        grid=(s // tm, nb),
        in_specs=[pl.BlockSpec((tm, d), lambda i, j: (i, 0)),
                  pl.BlockSpec((d, tn), lambda i, j: (0, j)),
                  pl.BlockSpec((d, tn), lambda i, j: (0, nb + j))],
        out_specs=pl.BlockSpec((tm, tn), lambda i, j: (i, j)),
        compiler_params=_params(("parallel", "parallel"),
                                _nbytes((tm, d), BF16) + 2 * _nbytes((d, tn), w_gate_up.dtype)
                                + _nbytes((tm, tn), BF16),
                                2 * _nbytes((d, tn), BF16) + 2 * _nbytes((tm, tn), F32)),
        name="ffn_up",
    )(u, w_gate_up, w_gate_up)


def kernel(x, c, w_ada, b_ada, w_in, b_forget, w_attn_out, w_pool, pool_scale, w_out,
           ln1_g, ln1_b, w_gate_up, w_down, ln2_g, ln2_b):
    batch, seq, d = x.shape
    assert batch == 1 and w_ada.shape[0] == DEPTH == 1
    fox_w = w_attn_out.shape[1]
    heads = fox_w // FOX_HEAD_DIM
    pool_w = w_pool.shape[1] * w_pool.shape[2]
    o_f = 3 * fox_w
    o_p = o_f + heads
    tk = 512

    xs = x[0]
    w_in_t = jnp.swapaxes(w_in, 1, 2)[0]
    w_f = jnp.pad(w_in_t[o_f:o_p], ((0, LANES - heads), (0, 0))).astype(BF16)
    b_f = jnp.pad(b_forget, ((0, 0), (0, LANES - heads)))

    mod = _ada(c.reshape(d, 1), w_ada[0], b_ada)

    u1, e, c_blk = _ln_mod_fcum(xs, mod, 0, 1, w_f, b_f, heads=heads, tm=tk)
    cb = c_blk[:, 0, :heads].T
    tq = 2 * tk
    qt, sq_q = _matmul_tn(u1, w_in_t, 0, fox_w, tm=tq, tn=512, tk=tq, scale=FOX_HEAD_DIM ** -0.5 * LOG2E,
                          group_norms=True, name="proj_qt")
    k, sq_k = _matmul_nt(u1, w_in_t, fox_w, fox_w, BF16, tm=tq, tn=512, group_norms=True, name="proj_k")
    per_head = lambda sq: jnp.sqrt(sq[:, :, 0, :512 // LANES].reshape(seq // tq, heads))
    thr = (SKIP_LOG2 + 2.0 * NORM_SLACK * per_head(sq_q) * jnp.max(per_head(sq_k), axis=0)).T
    vt = _matmul_tn(u1, w_in_t, 2 * fox_w, fox_w, tm=1024, tn=512, tk=tk, name="proj_vt")
    p = _matmul_nt(u1, w_in_t, o_p, pool_w, F32, tm=1024, tn=512, name="proj_pool")
    gates = _matmul_nt(u1, w_in_t, o_p + pool_w, 2 * d, BF16, tm=1024, tn=512, sigmoid=True,
                       name="proj_gates")
    attn, w_down_bf16 = _attention(cb, thr, qt, k, e, vt, w_down[0], heads=heads, tk=tk)
    mix = _mix(attn, w_attn_out[0], p, w_pool[0], pool_scale, gates)
    r1 = _resid_matmul(mix, w_out[0], xs, mod, 2, tm=2048, tn=256, chunks=4, name="out_proj")
    x1, u2 = _ln_ln_mod(r1, ln1_g, ln1_b, mod, 3, 4)

    act = _ffn_up(u2, w_gate_up[0])
    r2 = _resid_matmul(act, w_down_bf16, x1, mod, 5, tm=512, tn=512, name="ffn_down")
    out = _ln_affine(r2, ln2_g, ln2_b)
    return out[None]
```

```python
import functools

import jax
import jax.numpy as jnp
from jax import lax
from jax.experimental import pallas as pl
from jax.experimental.pallas import tpu as pltpu

F32 = jnp.float32
BF16 = jnp.bfloat16

FOX_HEAD_DIM = 128
POOL_WINDOWS = (2, 4, 8, 16)
POOL_HALO = 16
DEPTH = 1
ALPHA = (2 * DEPTH) ** 0.25
LN_EPS = 1e-5
LOG2E = 1.4426950408889634
LANES = 128
SUBLANES = 8
BIAS_PIECES = 3
ONES_ROWS = 16
MM_CHUNK_ROWS = 512
FFN_CHUNKS = 4
SKIP_LOG2 = 48.0
NORM_SLACK = 1.02

V7X_VMEM_BYTES = 64 * 1024 * 1024
VMEM_CAP_BYTES = V7X_VMEM_BYTES - 6 * 1024 * 1024

_NT = (((1,), (1,)), ((), ()))
_TN = (((0,), (0,)), ((), ()))


def _nbytes(shape, dtype):
    n = jnp.dtype(dtype).itemsize
    for s in shape:
        n *= s
    return n


def _cast_bytes(shape, dtype):
    return 0 if dtype == BF16 else _nbytes(shape, BF16)


def _params(semantics, pipelined_bytes, resident_bytes=0):
    need = 2 * pipelined_bytes + resident_bytes
    assert need <= VMEM_CAP_BYTES, (need, VMEM_CAP_BYTES)
    return pltpu.CompilerParams(dimension_semantics=semantics,
                                vmem_limit_bytes=min(need + (4 << 20), VMEM_CAP_BYTES))


def _ln_rows(x):
    mu = jnp.mean(x, axis=-1, keepdims=True)
    xc = x - mu
    var = jnp.mean(xc * xc, axis=-1, keepdims=True)
    return xc * lax.rsqrt(var + LN_EPS)


def _as_bf16(w):
    return w if w.dtype == BF16 else w.astype(BF16)


def _split3(v):
    hi = v.astype(BF16)
    r1 = v - hi.astype(F32)
    mid = r1.astype(BF16)
    lo = (r1 - mid.astype(F32)).astype(BF16)
    return hi, mid, lo


def _ada_kernel(c_ref, w_ref, b_ref, o_ref):
    k = pl.program_id(1)

    @pl.when(k == 0)
    def _():
        o_ref[...] = b_ref[...]

    cc = c_ref[...]
    s = cc * jax.nn.sigmoid(cc)
    o_ref[...] += jnp.sum(s * w_ref[...], axis=0, keepdims=True)


def _ada(c_col, w_ada, b_ada, *, tk=512, tn=4096):
    d, n = w_ada.shape
    return pl.pallas_call(
        _ada_kernel,
        out_shape=jax.ShapeDtypeStruct((1, n), F32),
        grid=(n // tn, d // tk),
        in_specs=[pl.BlockSpec((tk, 1), lambda j, k: (k, 0)),
                  pl.BlockSpec((tk, tn), lambda j, k: (k, j)),
                  pl.BlockSpec((1, tn), lambda j, k: (0, j))],
        out_specs=pl.BlockSpec((1, tn), lambda j, k: (0, j)),
        compiler_params=_params(("parallel", "arbitrary"),
                                _nbytes((tk, tn), F32) + _nbytes((tk, LANES), F32),
                                _nbytes((tk, tn), F32)),
        name="ada",
    )(c_col, w_ada, b_ada)


def _ln_mod_fcum_kernel(x_ref, sh_ref, sc_ref, wf_ref, b_ref, u_ref, e_ref, c_ref,
                        carry_ref, f_ref, *, tm, rows, heads):
    @pl.when(pl.program_id(0) == 0)
    def _():
        carry_ref[...] = jnp.zeros_like(carry_ref)

    gain = 1.0 + sc_ref[...]
    shift = sh_ref[...]
    for r in range(0, tm, rows):
        u = (_ln_rows(x_ref[r:r + rows, :]) * gain + shift).astype(BF16)
        u_ref[r:r + rows, :] = u
        f_ref[r:r + rows, :] = lax.dot_general(u, wf_ref[...], _NT, preferred_element_type=F32)
    f = f_ref[...] + b_ref[...]
    ls = (jnp.minimum(f, 0.0) - jnp.log1p(jnp.exp(-jnp.abs(f)))) * LOG2E
    row = lax.broadcasted_iota(jnp.int32, (tm, tm), 0)
    col = lax.broadcasted_iota(jnp.int32, (tm, tm), 1)
    tri = jnp.where(col <= row, 1.0, 0.0).astype(BF16)
    cs = sum(jnp.dot(tri, piece, preferred_element_type=F32) for piece in _split3(ls))
    carry = carry_ref[...]
    c_ref[...] = carry
    carry_ref[...] = carry + cs[tm - 1:tm, :]
    head_lane = lax.broadcasted_iota(jnp.int32, (tm, LANES), 1) < heads
    e = sum(pltpu.roll(jnp.where(head_lane, piece.astype(F32), 0.0), p * heads, 1) if p else
            jnp.where(head_lane, piece.astype(F32), 0.0)
            for p, piece in enumerate(_split3(-cs)))
    e_ref[...] = e.astype(e_ref.dtype)


def _ln_mod_fcum(x, mod, shift_idx, scale_idx, wf, b_row, *, heads, tm, rows=128):
    s, d = x.shape
    nb = s // tm
    assert BIAS_PIECES * heads <= LANES
    vec = lambda idx: pl.BlockSpec((1, d), lambda i: (0, idx))
    const = lambda shape: pl.BlockSpec(shape, lambda i: (0, 0))
    return pl.pallas_call(
        functools.partial(_ln_mod_fcum_kernel, tm=tm, rows=rows, heads=heads),
        out_shape=(jax.ShapeDtypeStruct((s, d), BF16),
                   jax.ShapeDtypeStruct((s, LANES), BF16),
                   jax.ShapeDtypeStruct((nb, 1, LANES), F32)),
        grid=(nb,),
        in_specs=[pl.BlockSpec((tm, d), lambda i: (i, 0)), vec(shift_idx), vec(scale_idx),
                  const((LANES, d)), const((1, LANES))],
        out_specs=(pl.BlockSpec((tm, d), lambda i: (i, 0)),
                   pl.BlockSpec((tm, LANES), lambda i: (i, 0)),
                   pl.BlockSpec((None, 1, LANES), lambda i: (i, 0, 0))),
        scratch_shapes=[pltpu.VMEM((1, LANES), F32), pltpu.VMEM((tm, LANES), F32)],
        compiler_params=_params(("arbitrary",),
                                _nbytes((tm, d), F32) + _nbytes((tm, d), BF16)
                                + _nbytes((tm, LANES), BF16) + _nbytes((d, LANES), BF16),
                                4 * _nbytes((rows, d), F32) + 2 * _nbytes((tm, tm), F32)),
        name="ln_mod_fcum",
    )(x, mod, mod, wf, b_row)


def _wt_spec(tn, k, row0):
    assert row0 % SUBLANES == 0 and tn % SUBLANES == 0
    return pl.BlockSpec((pl.Element(tn), pl.Element(k)),
                        lambda i, j: ((row0 // SUBLANES + j * (tn // SUBLANES)) * SUBLANES, 0))


def _mm_nt_kernel(a_ref, w_ref, o_ref, *rest, sigmoid):
    w = _as_bf16(w_ref[...])
    tm = a_ref.shape[0]
    chunk = min(tm, MM_CHUNK_ROWS)
    lane = lax.broadcasted_iota(jnp.int32, (1, LANES), 1)
    nrm = jnp.zeros((1, LANES), F32)
    for r in range(0, tm, chunk):
        acc = lax.dot_general(a_ref[r:r + chunk, :], w, _NT, preferred_element_type=F32)
        if sigmoid:
            acc = 0.5 * jnp.tanh(0.5 * acc) + 0.5
        o_ref[r:r + chunk, :] = acc.astype(o_ref.dtype)
        if rest:
            for g in range(acc.shape[1] // LANES):
                sq = jnp.sum(jnp.square(acc[:, g * LANES:(g + 1) * LANES]), axis=1, keepdims=True)
                nrm = jnp.maximum(nrm, jnp.where(lane == g, jnp.max(sq, axis=0, keepdims=True), 0.0))
    if rest:
        rest[0][...] = nrm


def _matmul_nt(a, wt, row0, n, out_dtype, *, tm, tn, sigmoid=False, group_norms=False, name):
    m, k = a.shape
    assert not (group_norms and sigmoid)
    out_shape = [jax.ShapeDtypeStruct((m, n), out_dtype)]
    out_specs = [pl.BlockSpec((tm, tn), lambda i, j: (i, j))]
    if group_norms:
        out_shape.append(jax.ShapeDtypeStruct((m // tm, n // tn, 1, LANES), F32))
        out_specs.append(pl.BlockSpec((None, None, 1, LANES), lambda i, j: (i, j, 0, 0)))
    out = pl.pallas_call(
        functools.partial(_mm_nt_kernel, sigmoid=sigmoid),
        out_shape=out_shape,
        grid=(m // tm, n // tn),
        in_specs=[pl.BlockSpec((tm, k), lambda i, j: (i, 0)), _wt_spec(tn, k, row0)],
        out_specs=out_specs,
        compiler_params=_params(("parallel", "parallel"),
                                _nbytes((tm, k), a.dtype) + _nbytes((tn, k), wt.dtype)
                                + _nbytes((tm, tn), out_dtype),
                                _cast_bytes((tn, k), wt.dtype) + 3 * _nbytes((tm, tn), F32)),
        name=name,
    )(a, wt)
    return out if group_norms else out[0]


def _mm_tn_kernel(a_ref, w_ref, o_ref, *rest, scale):
    acc_t = lax.dot_general(_as_bf16(w_ref[...]), a_ref[...], _NT,
                            preferred_element_type=F32)
    if scale is not None:
        acc_t = acc_t * scale
    tk = o_ref.shape[2]
    for kb in range(o_ref.shape[0]):
        o_ref[kb] = acc_t[:, kb * tk:(kb + 1) * tk].astype(o_ref.dtype)
    if rest:
        lane = lax.broadcasted_iota(jnp.int32, (1, LANES), 1)
        nrm = jnp.zeros((1, LANES), F32)
        for g in range(acc_t.shape[0] // LANES):
            sq = jnp.sum(jnp.square(acc_t[g * LANES:(g + 1) * LANES, :]), axis=0, keepdims=True)
            nrm = jnp.where(lane == g, jnp.max(sq, axis=1, keepdims=True), nrm)
        rest[0][...] = nrm


def _matmul_tn(a, wt, row0, n, *, tm, tn, tk, scale=None, group_norms=False, name):
    m, k = a.shape
    out_shape = [jax.ShapeDtypeStruct((m // tk, n, tk), BF16)]
    out_specs = [pl.BlockSpec((tm // tk, tn, tk), lambda i, j: (i, j, 0))]
    if group_norms:
        out_shape.append(jax.ShapeDtypeStruct((m // tm, n // tn, 1, LANES), F32))
        out_specs.append(pl.BlockSpec((None, None, 1, LANES), lambda i, j: (i, j, 0, 0)))
    out = pl.pallas_call(
        functools.partial(_mm_tn_kernel, scale=scale),
        out_shape=out_shape,
        grid=(m // tm, n // tn),
        in_specs=[pl.BlockSpec((tm, k), lambda i, j: (i, 0)), _wt_spec(tn, k, row0)],
        out_specs=out_specs,
        compiler_params=_params(("parallel", "parallel"),
                                _nbytes((tm, k), a.dtype) + _nbytes((tn, k), wt.dtype)
                                + _nbytes((tm, tn), BF16),
                                _cast_bytes((tn, k), wt.dtype) + 3 * _nbytes((tm, tn), F32)),
        name=name,
    )(a, wt)
    return out if group_norms else out[0]


def _attn_kernel(cb_ref, thr_ref, q_ref, k_ref, e_ref, vt_ref, w_ref, o_ref, wb_ref,
                 s0_ref, s1_ref, acc_ref, *, tk, tn, heads, side_blocks):
    h = pl.program_id(0)
    i = pl.program_id(1)

    @pl.when(h * pl.num_programs(1) + i < side_blocks)
    def _():
        wb_ref[...] = w_ref[...].astype(wb_ref.dtype)

    dh, tq = q_ref.shape
    nt = tq // tn
    half = nt // 2
    r = lax.broadcasted_iota(jnp.int32, (dh, tq), 0) - h
    ones_rows = sum(jnp.where(r == p * heads, 1.0, 0.0) for p in range(BIAS_PIECES))
    qt_aug = jnp.concatenate([q_ref[...], ones_rows.astype(BF16)], axis=0)
    ones_v = jnp.ones((acc_ref.shape[0] - dh, tk), BF16)
    c_q = cb_ref[h, 2 * i]
    thr = thr_ref[h, i]
    acc_ref[...] = jnp.zeros_like(acc_ref)

    def logits_to(buf_ref, j, first_tile=0):
        rows = pl.ds(pl.multiple_of(j * tk, tk), tk)
        k_aug = jnp.concatenate([k_ref[rows, :], e_ref[rows, :]], axis=1)
        cms = []
        for n in range(first_tile, nt):
            cols = slice(n * tn, (n + 1) * tn)
            s = jnp.dot(k_aug, qt_aug[:, cols], preferred_element_type=F32)
            buf_ref[:, cols] = s
            cms.append(jnp.max(s, axis=0, keepdims=True))
        return jnp.concatenate(cms, axis=1)

    def causal(buf_ref, n, first_tile):
        key = lax.broadcasted_iota(jnp.int32, (tk, tn), 0)
        qry = lax.broadcasted_iota(jnp.int32, (tk, tn), 1) + (n - first_tile) * tn
        return jnp.where(key <= qry, buf_ref[:, n * tn:(n + 1) * tn], -jnp.inf)

    def softmax_pv(buf_ref, j, cmax, m, first_tile=0, causal_tiles=0):
        lo = first_tile * tn
        c = c_q - cb_ref[h, j]
        m_old = m[:, lo:]
        m_new = jnp.maximum(m_old, cmax + c)
        a = jnp.exp2(m_old - m_new)
        off = m_new - c
        v_aug = jnp.concatenate([vt_ref[j], ones_v], axis=0)
        for n in range(first_tile, nt):
            cols = slice(n * tn, (n + 1) * tn)
            rel = slice(n * tn - lo, (n + 1) * tn - lo)
            s = causal(buf_ref, n, first_tile) if n < first_tile + causal_tiles else buf_ref[:, cols]
            p = jnp.exp2(s - off[:, rel]).astype(BF16)
            acc_ref[:, cols] = a[:, rel] * acc_ref[:, cols] + jnp.dot(
                v_aug, p, preferred_element_type=F32)
        if first_tile:
            m_new = jnp.concatenate([m[:, :lo], m_new], axis=1)
        return m_new

    def causal_max(buf_ref, first_tile):
        return [jnp.max(causal(buf_ref, n, first_tile), axis=0, keepdims=True)
                for n in range(first_tile, first_tile + half)]

    def pair(p, carry):
        m, cm0 = carry
        cm1 = logits_to(s1_ref, 2 * p + 1)
        m = softmax_pv(s0_ref, 2 * p, cm0, m)
        cm0 = logits_to(s0_ref, 2 * p + 2)
        m = softmax_pv(s1_ref, 2 * p + 1, cm1, m)
        return m, cm0

    first = lax.while_loop(lambda p: (p < i) & (cb_ref[h, 2 * p + 2] - c_q > thr),
                           lambda p: p + 1, jnp.int32(0))
    init = (jnp.full((1, tq), -jnp.inf, F32), logits_to(s0_ref, 2 * first))
    m, cm0 = lax.fori_loop(first, i, pair, init)
    logits_to(s1_ref, 2 * i + 1, first_tile=half)
    cm = jnp.concatenate(causal_max(s0_ref, 0) + [cm0[:, half * tn:]], axis=1)
    m = softmax_pv(s0_ref, 2 * i, cm, m, causal_tiles=half)
    cm = jnp.concatenate(causal_max(s1_ref, half), axis=1)
    softmax_pv(s1_ref, 2 * i + 1, cm, m, first_tile=half, causal_tiles=half)
    o_ref[...] = (acc_ref[:dh, :] / acc_ref[dh:dh + 1, :]).astype(o_ref.dtype)


def _attention(cb, thr, qt, k, e, vt, w_side, *, heads, tk, tn=256, side_blocks=32):
    s = k.shape[0]
    dh = FOX_HEAD_DIM
    tq = 2 * tk
    nq = s // tq
    side_cols = w_side.shape[1] // 2
    side_rows = 2 * w_side.shape[0] // side_blocks
    assert side_rows * side_blocks == 2 * w_side.shape[0] and side_rows % (2 * SUBLANES) == 0
    assert side_cols % LANES == 0 and side_blocks <= heads * nq

    def side(h, i, *_):
        blk = jnp.minimum(h * nq + i, side_blocks - 1)
        return blk // 2, blk % 2

    grid_spec = pltpu.PrefetchScalarGridSpec(
        num_scalar_prefetch=2,
        grid=(heads, nq),
        in_specs=[pl.BlockSpec((None, dh, tq), lambda h, i, *_: (i, h, 0)),
                  pl.BlockSpec((s, dh), lambda h, i, *_: (0, h)),
                  pl.BlockSpec((s, LANES), lambda h, i, *_: (0, 0)),
                  pl.BlockSpec((s // tk, dh, tk), lambda h, i, *_: (0, h, 0)),
                  pl.BlockSpec((side_rows, side_cols), side)],
        out_specs=[pl.BlockSpec((dh, tq), lambda h, i, *_: (h, i)),
                   pl.BlockSpec((side_rows, side_cols), side)],
        scratch_shapes=[pltpu.VMEM((tk, tq), F32), pltpu.VMEM((tk, tq), F32),
                        pltpu.VMEM((dh + ONES_ROWS, tq), F32)],
    )
    return pl.pallas_call(
        functools.partial(_attn_kernel, tk=tk, tn=tn, heads=heads, side_blocks=side_blocks),
        out_shape=[jax.ShapeDtypeStruct((heads * dh, s), BF16),
                   jax.ShapeDtypeStruct(w_side.shape, BF16)],
        grid_spec=grid_spec,
        compiler_params=_params(("arbitrary", "arbitrary"),
                                3 * _nbytes((s, dh), BF16) + 2 * _nbytes((tq, dh), BF16)
                                + _nbytes((side_rows, side_cols), F32)
                                + _nbytes((side_rows, side_cols), BF16),
                                6 * _nbytes((tk, tq), F32)),
        name="fox_attention",
    )(cb, thr, qt, k, e, vt, w_side)


def _mix_kernel(a_ref, wa_ref, halo_ref, p_ref, wp_ref, ps_ref, ga_ref, gp_ref, o_ref,
                wa_b, wp_b, *, tm):
    j = pl.program_id(0)
    i = pl.program_id(1)

    @pl.when(i == 0)
    def _():
        wa_b[...] = wa_ref[...].astype(BF16)
        wp_b[...] = wp_ref[...].astype(BF16)

    halo = jnp.where(i == 0, 0.0, halo_ref[...])
    ext = jnp.concatenate([halo, p_ref[...]], axis=0)
    acc = ext
    win = ext
    for g, w in enumerate(POOL_WINDOWS):
        acc = acc + pltpu.roll(acc, w // 2, 0)
        win = jnp.where(j == g, acc, win)
    t = i * tm + lax.broadcasted_iota(jnp.int32, (tm, 1), 0)
    cnt = jnp.minimum(t + 1, jnp.left_shift(2, j)).astype(F32)
    pooled = (win[POOL_HALO:] / cnt - ext[POOL_HALO:]).astype(BF16)

    ya = lax.dot_general(a_ref[...], wa_b[...], _TN, preferred_element_type=F32)
    yp = jnp.dot(pooled, wp_b[...], preferred_element_type=F32) * ps_ref[...]
    o_ref[...] = (ga_ref[...].astype(F32) * ya + gp_ref[...].astype(F32) * yp).astype(o_ref.dtype)


def _mix(attn_t, w_a, p, w_pool, pool_scale, gates, *, tm=512):
    fw, s = attn_t.shape
    groups, gd, tn = w_pool.shape
    assert POOL_WINDOWS == tuple(2 << g for g in range(groups))
    d = w_a.shape[1]
    hb = tm // POOL_HALO
    return pl.pallas_call(
        functools.partial(_mix_kernel, tm=tm),
        out_shape=jax.ShapeDtypeStruct((s, d), BF16),
        grid=(groups, s // tm),
        in_specs=[pl.BlockSpec((fw, tm), lambda j, i: (0, i)),
                  pl.BlockSpec((fw, tn), lambda j, i: (0, j)),
                  pl.BlockSpec((POOL_HALO, gd), lambda j, i: (jnp.maximum(i * hb - 1, 0), j)),
                  pl.BlockSpec((tm, gd), lambda j, i: (i, j)),
                  pl.BlockSpec((None, gd, tn), lambda j, i: (j, 0, 0)),
                  pl.BlockSpec((1, tn), lambda j, i: (0, j)),
                  pl.BlockSpec((tm, tn), lambda j, i: (i, j)),
                  pl.BlockSpec((tm, tn), lambda j, i: (i, groups + j))],
        out_specs=pl.BlockSpec((tm, tn), lambda j, i: (i, j)),
        scratch_shapes=[pltpu.VMEM((fw, tn), BF16), pltpu.VMEM((gd, tn), BF16)],
        compiler_params=_params(("parallel", "arbitrary"),
                                _nbytes((tm, fw), BF16) + _nbytes((fw, tn), w_a.dtype)
                                + _nbytes((tm + POOL_HALO, gd), F32) + _nbytes((gd, tn), w_pool.dtype)
                                + 3 * _nbytes((tm, tn), BF16),
                                _nbytes((fw + gd, tn), BF16) + 3 * _nbytes((tm, tn), F32)
                                + 4 * _nbytes((tm + POOL_HALO, gd), F32)),
        name="branch_mix",
    )(attn_t, w_a, p, p, w_pool, pool_scale, gates, gates)


def _resid_mm_kernel(a_ref, w_ref, x_ref, g_ref, o_ref, *, chunks):
    w = _as_bf16(w_ref[...])
    rows = a_ref.shape[0] // chunks
    for r in range(0, a_ref.shape[0], rows):
        m = jnp.dot(a_ref[r:r + rows, :], w, preferred_element_type=F32)
        o_ref[r:r + rows, :] = ALPHA * x_ref[r:r + rows, :] + g_ref[...] * m


def _resid_matmul(a, w, x, mod, gate_idx, *, tm, tn, chunks=1, name):
    m, k = a.shape
    n = w.shape[1]
    nb = n // tn
    return pl.pallas_call(
        functools.partial(_resid_mm_kernel, chunks=chunks),
        out_shape=jax.ShapeDtypeStruct((m, n), F32),
        grid=(m // tm, nb),
        in_specs=[pl.BlockSpec((tm, k), lambda i, j: (i, 0)),
                  pl.BlockSpec((k, tn), lambda i, j: (0, j)),
                  pl.BlockSpec((tm, tn), lambda i, j: (i, j)),
                  pl.BlockSpec((1, tn), lambda i, j: (0, gate_idx * nb + j))],
        out_specs=pl.BlockSpec((tm, tn), lambda i, j: (i, j)),
        compiler_params=_params(("parallel", "parallel"),
                                _nbytes((tm, k), BF16) + _nbytes((k, tn), w.dtype)
                                + 2 * _nbytes((tm, tn), F32),
                                _cast_bytes((k, tn), w.dtype) + 2 * _nbytes((tm, tn), F32) // chunks),
        name=name,
    )(a, w, x, mod)


def _ln_ln_mod_kernel(r_ref, g_ref, b_ref, sh_ref, sc_ref, x_ref, u_ref):
    x1 = _ln_rows(r_ref[...]) * g_ref[...] + b_ref[...]
    x_ref[...] = x1
    u_ref[...] = (_ln_rows(x1) * (1.0 + sc_ref[...]) + sh_ref[...]).astype(u_ref.dtype)


def _ln_ln_mod(r, gain, bias, mod, shift_idx, scale_idx, *, tm=256):
    s, d = r.shape
    row = pl.BlockSpec((tm, d), lambda i: (i, 0))
    vec = pl.BlockSpec((1, d), lambda i: (0, 0))
    return pl.pallas_call(
        _ln_ln_mod_kernel,
        out_shape=(jax.ShapeDtypeStruct((s, d), F32), jax.ShapeDtypeStruct((s, d), BF16)),
        grid=(s // tm,),
        in_specs=[row, vec, vec,
                  pl.BlockSpec((1, d), lambda i: (0, shift_idx)),
                  pl.BlockSpec((1, d), lambda i: (0, scale_idx))],
        out_specs=(row, row),
        compiler_params=_params(("parallel",),
                                2 * _nbytes((tm, d), F32) + _nbytes((tm, d), BF16),
                                4 * _nbytes((tm, d), F32)),
        name="ln1_ln_mod",
    )(r, gain, bias, mod, mod)


def _ln_affine_kernel(r_ref, g_ref, b_ref, o_ref):
    o_ref[...] = _ln_rows(r_ref[...]) * g_ref[...] + b_ref[...]


def _ln_affine(r, gain, bias, *, tm=256):
    s, d = r.shape
    row = pl.BlockSpec((tm, d), lambda i: (i, 0))
    vec = pl.BlockSpec((1, d), lambda i: (0, 0))
    return pl.pallas_call(
        _ln_affine_kernel,
        out_shape=jax.ShapeDtypeStruct((s, d), F32),
        grid=(s // tm,),
        in_specs=[row, vec, vec],
        out_specs=row,
        compiler_params=_params(("parallel",), 2 * _nbytes((tm, d), F32),
                                3 * _nbytes((tm, d), F32)),
        name="ln2",
    )(r, gain, bias)


def _ffn_up_kernel(u_ref, wg_ref, wu_ref, o_ref):
    wg = _as_bf16(wg_ref[...])
    wu = _as_bf16(wu_ref[...])
    rows = u_ref.shape[0] // FFN_CHUNKS
    for r in range(0, u_ref.shape[0], rows):
        u = u_ref[r:r + rows, :]
        g = jnp.dot(u, wg, preferred_element_type=F32)
        up = jnp.dot(u, wu, preferred_element_type=F32)
        o_ref[r:r + rows, :] = (g * jax.nn.sigmoid(g) * up).astype(o_ref.dtype)


def _ffn_up(u, w_gate_up, *, tm=2048, tn=256):
    s, d = u.shape
    hidden = w_gate_up.shape[1] // 2
    nb = hidden // tn
    return pl.pallas_call(
        _ffn_up_kernel,
        out_shape=jax.ShapeDtypeStruct((s, hidden), BF16),
        grid=(s // tm, nb),
        in_specs=[pl.BlockSpec((tm, d), lambda i, j: (i, 0)),
                  pl.BlockSpec((d, tn), lambda i, j: (0, j)),
                  pl.BlockSpec((d, tn), lambda i, j: (0, nb + j))],
        out_specs=pl.BlockSpec((tm, tn), lambda i, j: (i, j)),
        compiler_params=_params(("parallel", "parallel"),
                                _nbytes((tm, d), BF16) + 2 * _nbytes((d, tn), w_gate_up.dtype)
                                + _nbytes((tm, tn), BF16),
                                2 * _nbytes((d, tn), BF16) + 2 * _nbytes((tm, tn), F32)),
        name="ffn_up",
    )(u, w_gate_up, w_gate_up)


def kernel(x, c, w_ada, b_ada, w_in, b_forget, w_attn_out, w_pool, pool_scale, w_out,
           ln1_g, ln1_b, w_gate_up, w_down, ln2_g, ln2_b):
    batch, seq, d = x.shape
    assert batch == 1 and w_ada.shape[0] == DEPTH == 1
    fox_w = w_attn_out.shape[1]
    heads = fox_w // FOX_HEAD_DIM
    pool_w = w_pool.shape[1] * w_pool.shape[2]
    o_f = 3 * fox_w
    o_p = o_f + heads
    tk = 512

    xs = x[0]
    w_in_t = jnp.swapaxes(w_in, 1, 2)[0]
    w_f = jnp.pad(w_in_t[o_f:o_p], ((0, LANES - heads), (0, 0))).astype(BF16)
    b_f = jnp.pad(b_forget, ((0, 0), (0, LANES - heads)))

    mod = _ada(c.reshape(d, 1), w_ada[0], b_ada)

    u1, e, c_blk = _ln_mod_fcum(xs, mod, 0, 1, w_f, b_f, heads=heads, tm=tk)
    cb = c_blk[:, 0, :heads].T
    tq = 2 * tk
    tm, tn = 2048, 256
    qt, sq_q = _matmul_tn(u1, w_in_t, 0, fox_w, tm=tm, tn=tn, tk=tq, scale=FOX_HEAD_DIM ** -0.5 * LOG2E,
                          group_norms=True, name="proj_qt")
    k, sq_k = _matmul_nt(u1, w_in_t, fox_w, fox_w, BF16, tm=tm, tn=tn, group_norms=True, name="proj_k")
    per_head = lambda sq: jnp.sqrt(sq[:, :, 0, :tn // LANES].reshape(seq // tm, heads))
    qn = jnp.repeat(per_head(sq_q), tm // tq, axis=0)
    thr = (SKIP_LOG2 + 2.0 * NORM_SLACK * qn * jnp.max(per_head(sq_k), axis=0)).T
    vt = _matmul_tn(u1, w_in_t, 2 * fox_w, fox_w, tm=tm, tn=tn, tk=tk, name="proj_vt")
    p = _matmul_nt(u1, w_in_t, o_p, pool_w, F32, tm=tm, tn=tn, name="proj_pool")
    gates = _matmul_nt(u1, w_in_t, o_p + pool_w, 2 * d, BF16, tm=tm, tn=tn, sigmoid=True,
                       name="proj_gates")
    attn, w_down_bf16 = _attention(cb, thr, qt, k, e, vt, w_down[0], heads=heads, tk=tk)
    mix = _mix(attn, w_attn_out[0], p, w_pool[0], pool_scale, gates)
    r1 = _resid_matmul(mix, w_out[0], xs, mod, 2, tm=2048, tn=256, chunks=4, name="out_proj")
    x1, u2 = _ln_ln_mod(r1, ln1_g, ln1_b, mod, 3, 4)

    act = _ffn_up(u2, w_gate_up[0])
    r2 = _resid_matmul(act, w_down_bf16, x1, mod, 5, tm=512, tn=512, name="ffn_down")
    out = _ln_affine(r2, ln2_g, ln2_b)
    return out[None]
```

```python
import functools

import jax
import jax.numpy as jnp
from jax import lax
from jax.experimental import pallas as pl
from jax.experimental.pallas import tpu as pltpu

F32 = jnp.float32
BF16 = jnp.bfloat16

FOX_HEAD_DIM = 128
POOL_WINDOWS = (2, 4, 8, 16)
POOL_HALO = 16
DEPTH = 1
ALPHA = (2 * DEPTH) ** 0.25
LN_EPS = 1e-5
LOG2E = 1.4426950408889634
LANES = 128
SUBLANES = 8
BIAS_PIECES = 3
ONES_ROWS = 16
FFN_CHUNKS = 4
Q_PER_STEP = 2
SKIP_LOG2 = 48.0
NORM_SLACK = 1.02

V7X_VMEM_BYTES = 64 * 1024 * 1024
VMEM_CAP_BYTES = V7X_VMEM_BYTES - 6 * 1024 * 1024

_NT = (((1,), (1,)), ((), ()))
_TN = (((0,), (0,)), ((), ()))


def _nbytes(shape, dtype):
    n = jnp.dtype(dtype).itemsize
    for s in shape:
        n *= s
    return n


def _cast_bytes(shape, dtype):
    return 0 if dtype == BF16 else _nbytes(shape, BF16)


def _params(semantics, pipelined_bytes, resident_bytes=0):
    need = 2 * pipelined_bytes + resident_bytes
    assert need <= VMEM_CAP_BYTES, (need, VMEM_CAP_BYTES)
    return pltpu.CompilerParams(dimension_semantics=semantics,
                                vmem_limit_bytes=min(need + (4 << 20), VMEM_CAP_BYTES))


def _ln_rows(x):
    mu = jnp.mean(x, axis=-1, keepdims=True)
    xc = x - mu
    var = jnp.mean(xc * xc, axis=-1, keepdims=True)
    return xc * lax.rsqrt(var + LN_EPS)


def _as_bf16(w):
    return w if w.dtype == BF16 else w.astype(BF16)


def _split3(v):
    hi = v.astype(BF16)
    r1 = v - hi.astype(F32)
    mid = r1.astype(BF16)
    lo = (r1 - mid.astype(F32)).astype(BF16)
    return hi, mid, lo


def _ada_kernel(c_ref, w_ref, b_ref, o_ref):
    k = pl.program_id(1)

    @pl.when(k == 0)
    def _():
        o_ref[...] = b_ref[...]

    cc = c_ref[...]
    s = cc * jax.nn.sigmoid(cc)
    o_ref[...] += jnp.sum(s * w_ref[...], axis=0, keepdims=True)


def _ada(c_col, w_ada, b_ada, *, tk=512, tn=4096):
    d, n = w_ada.shape
    return pl.pallas_call(
        _ada_kernel,
        out_shape=jax.ShapeDtypeStruct((1, n), F32),
        grid=(n // tn, d // tk),
        in_specs=[pl.BlockSpec((tk, 1), lambda j, k: (k, 0)),
                  pl.BlockSpec((tk, tn), lambda j, k: (k, j)),
                  pl.BlockSpec((1, tn), lambda j, k: (0, j))],
        out_specs=pl.BlockSpec((1, tn), lambda j, k: (0, j)),
        compiler_params=_params(("parallel", "arbitrary"),
                                _nbytes((tk, tn), F32) + _nbytes((tk, LANES), F32),
                                _nbytes((tk, tn), F32)),
        name="ada",
    )(c_col, w_ada, b_ada)


def _ln_mod_fcum_kernel(x_ref, sh_ref, sc_ref, wf_ref, b_ref, u_ref, e_ref, c_ref,
                        carry_ref, f_ref, *, tm, rows, heads):
    @pl.when(pl.program_id(0) == 0)
    def _():
        carry_ref[...] = jnp.zeros_like(carry_ref)

    gain = 1.0 + sc_ref[...]
    shift = sh_ref[...]
    for r in range(0, tm, rows):
        u = (_ln_rows(x_ref[r:r + rows, :]) * gain + shift).astype(BF16)
        u_ref[r:r + rows, :] = u
        f_ref[r:r + rows, :] = lax.dot_general(u, wf_ref[...], _NT, preferred_element_type=F32)
    f = f_ref[...] + b_ref[...]
    ls = (jnp.minimum(f, 0.0) - jnp.log1p(jnp.exp(-jnp.abs(f)))) * LOG2E
    row = lax.broadcasted_iota(jnp.int32, (tm, tm), 0)
    col = lax.broadcasted_iota(jnp.int32, (tm, tm), 1)
    tri = jnp.where(col <= row, 1.0, 0.0).astype(BF16)
    cs = sum(jnp.dot(tri, piece, preferred_element_type=F32) for piece in _split3(ls))
    carry = carry_ref[...]
    c_ref[...] = carry
    carry_ref[...] = carry + cs[tm - 1:tm, :]
    head_lane = lax.broadcasted_iota(jnp.int32, (tm, LANES), 1) < heads
    e = sum(pltpu.roll(jnp.where(head_lane, piece.astype(F32), 0.0), p * heads, 1) if p else
            jnp.where(head_lane, piece.astype(F32), 0.0)
            for p, piece in enumerate(_split3(-cs)))
    e_ref[...] = e.astype(e_ref.dtype)


def _ln_mod_fcum(x, mod, shift_idx, scale_idx, wf, b_row, *, heads, tm, rows=128):
    s, d = x.shape
    nb = s // tm
    assert BIAS_PIECES * heads <= LANES
    vec = lambda idx: pl.BlockSpec((1, d), lambda i: (0, idx))
    const = lambda shape: pl.BlockSpec(shape, lambda i: (0, 0))
    return pl.pallas_call(
        functools.partial(_ln_mod_fcum_kernel, tm=tm, rows=rows, heads=heads),
        out_shape=(jax.ShapeDtypeStruct((s, d), BF16),
                   jax.ShapeDtypeStruct((s, LANES), BF16),
                   jax.ShapeDtypeStruct((nb, 1, LANES), F32)),
        grid=(nb,),
        in_specs=[pl.BlockSpec((tm, d), lambda i: (i, 0)), vec(shift_idx), vec(scale_idx),
                  const((LANES, d)), const((1, LANES))],
        out_specs=(pl.BlockSpec((tm, d), lambda i: (i, 0)),
                   pl.BlockSpec((tm, LANES), lambda i: (i, 0)),
                   pl.BlockSpec((None, 1, LANES), lambda i: (i, 0, 0))),
        scratch_shapes=[pltpu.VMEM((1, LANES), F32), pltpu.VMEM((tm, LANES), F32)],
        compiler_params=_params(("arbitrary",),
                                _nbytes((tm, d), F32) + _nbytes((tm, d), BF16)
                                + _nbytes((tm, LANES), BF16) + _nbytes((d, LANES), BF16),
                                4 * _nbytes((rows, d), F32) + 2 * _nbytes((tm, tm), F32)),
        name="ln_mod_fcum",
    )(x, mod, mod, wf, b_row)


def _wt_spec(tn, k, row0):
    assert row0 % SUBLANES == 0 and tn % SUBLANES == 0
    return pl.BlockSpec((pl.Element(tn), pl.Element(k)),
                        lambda i, j: ((row0 // SUBLANES + j * (tn // SUBLANES)) * SUBLANES, 0))


def _mm_nt_kernel(a_ref, w_ref, o_ref, *rest, sigmoid):
    w = _as_bf16(w_ref[...])
    tm = a_ref.shape[0]
    chunk = tm // 4 if sigmoid or rest else tm
    lane = lax.broadcasted_iota(jnp.int32, (1, LANES), 1)
    nrm = jnp.zeros((1, LANES), F32)
    for r in range(0, tm, chunk):
        acc = lax.dot_general(a_ref[r:r + chunk, :], w, _NT, preferred_element_type=F32)
        if sigmoid:
            acc = 0.5 * jnp.tanh(0.5 * acc) + 0.5
        o_ref[r:r + chunk, :] = acc.astype(o_ref.dtype)
        if rest:
            for g in range(acc.shape[1] // LANES):
                sq = jnp.sum(jnp.square(acc[:, g * LANES:(g + 1) * LANES]), axis=1, keepdims=True)
                nrm = jnp.maximum(nrm, jnp.where(lane == g, jnp.max(sq, axis=0, keepdims=True), 0.0))
    if rest:
        rest[0][...] = nrm


def _matmul_nt(a, wt, row0, n, out_dtype, *, tm, tn, sigmoid=False, group_norms=False, name):
    m, k = a.shape
    assert not (group_norms and sigmoid)
    out_shape = [jax.ShapeDtypeStruct((m, n), out_dtype)]
    out_specs = [pl.BlockSpec((tm, tn), lambda i, j: (i, j))]
    if group_norms:
        out_shape.append(jax.ShapeDtypeStruct((m // tm, n // tn, 1, LANES), F32))
        out_specs.append(pl.BlockSpec((None, None, 1, LANES), lambda i, j: (i, j, 0, 0)))
    out = pl.pallas_call(
        functools.partial(_mm_nt_kernel, sigmoid=sigmoid),
        out_shape=out_shape,
        grid=(m // tm, n // tn),
        in_specs=[pl.BlockSpec((tm, k), lambda i, j: (i, 0)), _wt_spec(tn, k, row0)],
        out_specs=out_specs,
        compiler_params=_params(("parallel", "parallel"),
                                _nbytes((tm, k), a.dtype) + _nbytes((tn, k), wt.dtype)
                                + _nbytes((tm, tn), out_dtype),
                                _cast_bytes((tn, k), wt.dtype) + 3 * _nbytes((tm, tn), F32)),
        name=name,
    )(a, wt)
    return out if group_norms else out[0]


def _mm_tn_kernel(a_ref, w_ref, o_ref, *rest, scale):
    acc_t = lax.dot_general(_as_bf16(w_ref[...]), a_ref[...], _NT,
                            preferred_element_type=F32)
    if scale is not None:
        acc_t = acc_t * scale
    tk = o_ref.shape[2]
    for kb in range(o_ref.shape[0]):
        o_ref[kb] = acc_t[:, kb * tk:(kb + 1) * tk].astype(o_ref.dtype)
    if rest:
        lane = lax.broadcasted_iota(jnp.int32, (1, LANES), 1)
        nrm = jnp.zeros((1, LANES), F32)
        for g in range(acc_t.shape[0] // LANES):
            sq = jnp.sum(jnp.square(acc_t[g * LANES:(g + 1) * LANES, :]), axis=0, keepdims=True)
            nrm = jnp.where(lane == g, jnp.max(sq, axis=1, keepdims=True), nrm)
        rest[0][...] = nrm


def _matmul_tn(a, wt, row0, n, *, tm, tn, tk, scale=None, group_norms=False, name):
    m, k = a.shape
    out_shape = [jax.ShapeDtypeStruct((m // tk, n, tk), BF16)]
    out_specs = [pl.BlockSpec((tm // tk, tn, tk), lambda i, j: (i, j, 0))]
    if group_norms:
        out_shape.append(jax.ShapeDtypeStruct((m // tm, n // tn, 1, LANES), F32))
        out_specs.append(pl.BlockSpec((None, None, 1, LANES), lambda i, j: (i, j, 0, 0)))
    out = pl.pallas_call(
        functools.partial(_mm_tn_kernel, scale=scale),
        out_shape=out_shape,
        grid=(m // tm, n // tn),
        in_specs=[pl.BlockSpec((tm, k), lambda i, j: (i, 0)), _wt_spec(tn, k, row0)],
        out_specs=out_specs,
        compiler_params=_params(("parallel", "parallel"),
                                _nbytes((tm, k), a.dtype) + _nbytes((tn, k), wt.dtype)
                                + _nbytes((tm, tn), BF16),
                                _cast_bytes((tn, k), wt.dtype) + 3 * _nbytes((tm, tn), F32)),
        name=name,
    )(a, wt)
    return out if group_norms else out[0]


def _attn_kernel(cb_ref, thr_ref, q_ref, k_ref, e_ref, vt_ref, w_ref, o_ref, wb_ref, *scratch,
                 tk, tn, heads, side_blocks):
    h = pl.program_id(0)

    @pl.when(h * pl.num_programs(1) + pl.program_id(1) < side_blocks)
    def _():
        wb_ref[...] = w_ref[...].astype(wb_ref.dtype)

    blocks = [Q_PER_STEP * pl.program_id(1) + sub for sub in range(Q_PER_STEP)]
    firsts = [lax.while_loop(lambda p, i=i: (p < i) & (cb_ref[h, 2 * p + 2] - cb_ref[h, 2 * i]
                                                        > thr_ref[h, i]),
                             lambda p: p + 1, jnp.int32(0)) for i in blocks]
    for sub, (i, first) in enumerate(zip(blocks, firsts)):
        _attn_block(cb_ref, q_ref.at[sub], k_ref, e_ref, vt_ref,
                    o_ref.at[:, sub * q_ref.shape[2]:(sub + 1) * q_ref.shape[2]],
                    *scratch[3 * sub:3 * sub + 3], h=h, i=i, first=first, tk=tk, tn=tn, heads=heads)


def _attn_block(cb_ref, q_ref, k_ref, e_ref, vt_ref, o_ref, s0_ref, s1_ref, acc_ref, *,
                h, i, first, tk, tn, heads):
    dh, tq = q_ref.shape
    nt = tq // tn
    half = nt // 2
    r = lax.broadcasted_iota(jnp.int32, (dh, tq), 0) - h
    ones_rows = sum(jnp.where(r == p * heads, 1.0, 0.0) for p in range(BIAS_PIECES))
    qt_aug = jnp.concatenate([q_ref[...], ones_rows.astype(BF16)], axis=0)
    ones_v = jnp.ones((acc_ref.shape[0] - dh, tk), BF16)
    c_q = cb_ref[h, 2 * i]
    acc_ref[...] = jnp.zeros_like(acc_ref)

    def logits_to(buf_ref, j, first_tile=0):
        rows = pl.ds(pl.multiple_of(j * tk, tk), tk)
        k_aug = jnp.concatenate([k_ref[rows, :], e_ref[rows, :]], axis=1)
        cms = []
        for n in range(first_tile, nt):
            cols = slice(n * tn, (n + 1) * tn)
            s = jnp.dot(k_aug, qt_aug[:, cols], preferred_element_type=F32)
            buf_ref[:, cols] = s
            cms.append(jnp.max(s, axis=0, keepdims=True))
        return jnp.concatenate(cms, axis=1)

    def causal(buf_ref, n, first_tile):
        key = lax.broadcasted_iota(jnp.int32, (tk, tn), 0)
        qry = lax.broadcasted_iota(jnp.int32, (tk, tn), 1) + (n - first_tile) * tn
        return jnp.where(key <= qry, buf_ref[:, n * tn:(n + 1) * tn], -jnp.inf)

    def softmax_pv(buf_ref, j, cmax, m, first_tile=0, causal_tiles=0):
        lo = first_tile * tn
        c = c_q - cb_ref[h, j]
        m_old = m[:, lo:]
        m_new = jnp.maximum(m_old, cmax + c)
        a = jnp.exp2(m_old - m_new)
        off = m_new - c
        v_aug = jnp.concatenate([vt_ref[j], ones_v], axis=0)
        for n in range(first_tile, nt):
            cols = slice(n * tn, (n + 1) * tn)
            rel = slice(n * tn - lo, (n + 1) * tn - lo)
            s = causal(buf_ref, n, first_tile) if n < first_tile + causal_tiles else buf_ref[:, cols]
            p = jnp.exp2(s - off[:, rel]).astype(BF16)
            acc_ref[:, cols] = a[:, rel] * acc_ref[:, cols] + jnp.dot(
                v_aug, p, preferred_element_type=F32)
        if first_tile:
            m_new = jnp.concatenate([m[:, :lo], m_new], axis=1)
        return m_new

    def causal_max(buf_ref, first_tile):
        return [jnp.max(causal(buf_ref, n, first_tile), axis=0, keepdims=True)
                for n in range(first_tile, first_tile + half)]

    def pair(p, carry):
        m, cm0 = carry
        cm1 = logits_to(s1_ref, 2 * p + 1)
        m = softmax_pv(s0_ref, 2 * p, cm0, m)
        cm0 = logits_to(s0_ref, 2 * p + 2)
        m = softmax_pv(s1_ref, 2 * p + 1, cm1, m)
        return m, cm0

    init =(jnp.full((1, tq), -jnp.inf, F32), logits_to(s0_ref, 2 * first))
    m, cm0 = lax.fori_loop(first, i, pair, init)
    logits_to(s1_ref, 2 * i + 1, first_tile=half)
    cm = jnp.concatenate(causal_max(s0_ref, 0) + [cm0[:, half * tn:]], axis=1)
    m = softmax_pv(s0_ref, 2 * i, cm, m, causal_tiles=half)
    cm = jnp.concatenate(causal_max(s1_ref, half), axis=1)
    softmax_pv(s1_ref, 2 * i + 1, cm, m, first_tile=half, causal_tiles=half)
    o_ref[...] = (acc_ref[:dh, :] / acc_ref[dh:dh + 1, :]).astype(o_ref.dtype)


def _attention(cb, thr, qt, k, e, vt, w_side, *, heads, tk, tn=256, side_blocks=32):
    s = k.shape[0]
    dh = FOX_HEAD_DIM
    tq = 2 * tk
    steps = s // (Q_PER_STEP * tq)
    side_cols = w_side.shape[1] // 2
    side_rows = 2 * w_side.shape[0] // side_blocks
    assert side_rows * side_blocks == 2 * w_side.shape[0] and side_rows % (2 * SUBLANES) == 0
    assert side_cols % LANES == 0 and side_blocks <= heads * steps

    def side(h, i, *_):
        blk = jnp.minimum(h * steps + i, side_blocks - 1)
        return blk // 2, blk % 2

    grid_spec = pltpu.PrefetchScalarGridSpec(
        num_scalar_prefetch=2,
        grid=(heads, steps),
        in_specs=[pl.BlockSpec((Q_PER_STEP, dh, tq), lambda h, i, *_: (i, h, 0)),
                  pl.BlockSpec((s, dh), lambda h, i, *_: (0, h)),
                  pl.BlockSpec((s, LANES), lambda h, i, *_: (0, 0)),
                  pl.BlockSpec((s // tk, dh, tk), lambda h, i, *_: (0, h, 0)),
                  pl.BlockSpec((side_rows, side_cols), side)],
        out_specs=[pl.BlockSpec((dh, Q_PER_STEP * tq), lambda h, i, *_: (h, i)),
                   pl.BlockSpec((side_rows, side_cols), side)],
        scratch_shapes=Q_PER_STEP * [pltpu.VMEM((tk, tq), F32), pltpu.VMEM((tk, tq), F32),
                                     pltpu.VMEM((dh + ONES_ROWS, tq), F32)],
    )
    return pl.pallas_call(
        functools.partial(_attn_kernel, tk=tk, tn=tn, heads=heads, side_blocks=side_blocks),
        out_shape=[jax.ShapeDtypeStruct((heads * dh, s), BF16),
                   jax.ShapeDtypeStruct(w_side.shape, BF16)],
        grid_spec=grid_spec,
        compiler_params=_params(("arbitrary", "arbitrary"),
                                3 * _nbytes((s, dh), BF16) + 2 * Q_PER_STEP * _nbytes((tq, dh), BF16)
                                + _nbytes((side_rows, side_cols), F32)
                                + _nbytes((side_rows, side_cols), BF16),
                                (3 * Q_PER_STEP + 3) * _nbytes((tk, tq), F32)),
        name="fox_attention",
    )(cb, thr, qt, k, e, vt, w_side)


def _mix_kernel(a_ref, wa_ref, halo_ref, p_ref, wp_ref, ps_ref, ga_ref, gp_ref, o_ref,
                wa_b, wp_b, *, tm):
    j = pl.program_id(0)
    i = pl.program_id(1)

    @pl.when(i == 0)
    def _():
        wa_b[...] = wa_ref[...].astype(BF16)
        wp_b[...] = wp_ref[...].astype(BF16)

    halo = jnp.where(i == 0, 0.0, halo_ref[...])
    ext = jnp.concatenate([halo, p_ref[...]], axis=0)
    acc = ext
    win = ext
    for g, w in enumerate(POOL_WINDOWS):
        acc = acc + pltpu.roll(acc, w // 2, 0)
        win = jnp.where(j == g, acc, win)
    t = i * tm + lax.broadcasted_iota(jnp.int32, (tm, 1), 0)
    cnt = jnp.minimum(t + 1, jnp.left_shift(2, j)).astype(F32)
    pooled = (win[POOL_HALO:] / cnt - ext[POOL_HALO:]).astype(BF16)

    ya = lax.dot_general(a_ref[...], wa_b[...], _TN, preferred_element_type=F32)
    yp = jnp.dot(pooled, wp_b[...], preferred_element_type=F32) * ps_ref[...]
    o_ref[...] = (ga_ref[...].astype(F32) * ya + gp_ref[...].astype(F32) * yp).astype(o_ref.dtype)


def _mix(attn_t, w_a, p, w_pool, pool_scale, gates, *, tm=512):
    fw, s = attn_t.shape
    groups, gd, tn = w_pool.shape
    assert POOL_WINDOWS == tuple(2 << g for g in range(groups))
    d = w_a.shape[1]
    hb = tm // POOL_HALO
    return pl.pallas_call(
        functools.partial(_mix_kernel, tm=tm),
        out_shape=jax.ShapeDtypeStruct((s, d), BF16),
        grid=(groups, s // tm),
        in_specs=[pl.BlockSpec((fw, tm), lambda j, i: (0, i)),
                  pl.BlockSpec((fw, tn), lambda j, i: (0, j)),
                  pl.BlockSpec((POOL_HALO, gd), lambda j, i: (jnp.maximum(i * hb - 1, 0), j)),
                  pl.BlockSpec((tm, gd), lambda j, i: (i, j)),
                  pl.BlockSpec((None, gd, tn), lambda j, i: (j, 0, 0)),
                  pl.BlockSpec((1, tn), lambda j, i: (0, j)),
                  pl.BlockSpec((tm, tn), lambda j, i: (i, j)),
                  pl.BlockSpec((tm, tn), lambda j, i: (i, groups + j))],
        out_specs=pl.BlockSpec((tm, tn), lambda j, i: (i, j)),
        scratch_shapes=[pltpu.VMEM((fw, tn), BF16), pltpu.VMEM((gd, tn), BF16)],
        compiler_params=_params(("parallel", "arbitrary"),
                                _nbytes((tm, fw), BF16) + _nbytes((fw, tn), w_a.dtype)
                                + _nbytes((tm + POOL_HALO, gd), F32) + _nbytes((gd, tn), w_pool.dtype)
                                + 3 * _nbytes((tm, tn), BF16),
                                _nbytes((fw + gd, tn), BF16) + 3 * _nbytes((tm, tn), F32)
                                + 4 * _nbytes((tm + POOL_HALO, gd), F32)),
        name="branch_mix",
    )(attn_t, w_a, p, p, w_pool, pool_scale, gates, gates)


def _resid_mm_kernel(a_ref, w_ref, x_ref, g_ref, o_ref, *, chunks):
    w = _as_bf16(w_ref[...])
    rows = a_ref.shape[0] // chunks
    for r in range(0, a_ref.shape[0], rows):
        m = jnp.dot(a_ref[r:r + rows, :], w, preferred_element_type=F32)
        o_ref[r:r + rows, :] = ALPHA * x_ref[r:r + rows, :] + g_ref[...] * m


def _resid_matmul(a, w, x, mod, gate_idx, *, tm, tn, chunks=1, name):
    m, k = a.shape
    n = w.shape[1]
    nb = n // tn
    return pl.pallas_call(
        functools.partial(_resid_mm_kernel, chunks=chunks),
        out_shape=jax.ShapeDtypeStruct((m, n), F32),
        grid=(m // tm, nb),
        in_specs=[pl.BlockSpec((tm, k), lambda i, j: (i, 0)),
                  pl.BlockSpec((k, tn), lambda i, j: (0, j)),
                  pl.BlockSpec((tm, tn), lambda i, j: (i, j)),
                  pl.BlockSpec((1, tn), lambda i, j: (0, gate_idx * nb + j))],
        out_specs=pl.BlockSpec((tm, tn), lambda i, j: (i, j)),
        compiler_params=_params(("parallel", "parallel"),
                                _nbytes((tm, k), BF16) + _nbytes((k, tn), w.dtype)
                                + 2 * _nbytes((tm, tn), F32),
                                _cast_bytes((k, tn), w.dtype) + 2 * _nbytes((tm, tn), F32) // chunks),
        name=name,
    )(a, w, x, mod)


def _ln_ln_mod_kernel(r_ref, g_ref, b_ref, sh_ref, sc_ref, x_ref, u_ref):
    x1 = _ln_rows(r_ref[...]) * g_ref[...] + b_ref[...]
    x_ref[...] = x1
    u_ref[...] = (_ln_rows(x1) * (1.0 + sc_ref[...]) + sh_ref[...]).astype(u_ref.dtype)


def _ln_ln_mod(r, gain, bias, mod, shift_idx, scale_idx, *, tm=256):
    s, d = r.shape
    row = pl.BlockSpec((tm, d), lambda i: (i, 0))
    vec = pl.BlockSpec((1, d), lambda i: (0, 0))
    return pl.pallas_call(
        _ln_ln_mod_kernel,
        out_shape=(jax.ShapeDtypeStruct((s, d), F32), jax.ShapeDtypeStruct((s, d), BF16)),
        grid=(s // tm,),
        in_specs=[row, vec, vec,
                  pl.BlockSpec((1, d), lambda i: (0, shift_idx)),
                  pl.BlockSpec((1, d), lambda i: (0, scale_idx))],
        out_specs=(row, row),
        compiler_params=_params(("parallel",),
                                2 * _nbytes((tm, d), F32) + _nbytes((tm, d), BF16),
                                4 * _nbytes((tm, d), F32)),
        name="ln1_ln_mod",
    )(r, gain, bias, mod, mod)


def _ln_affine_kernel(r_ref, g_ref, b_ref, o_ref):
    o_ref[...] = _ln_rows(r_ref[...]) * g_ref[...] + b_ref[...]


def _ln_affine(r, gain, bias, *, tm=256):
    s, d = r.shape
    row = pl.BlockSpec((tm, d), lambda i: (i, 0))
    vec = pl.BlockSpec((1, d), lambda i: (0, 0))
    return pl.pallas_call(
        _ln_affine_kernel,
        out_shape=jax.ShapeDtypeStruct((s, d), F32),
        grid=(s // tm,),
        in_specs=[row, vec, vec],
        out_specs=row,
        compiler_params=_params(("parallel",), 2 * _nbytes((tm, d), F32),
                                3 * _nbytes((tm, d), F32)),
        name="ln2",
    )(r, gain, bias)


def _ffn_up_kernel(u_ref, wg_ref, wu_ref, o_ref):
    wg = _as_bf16(wg_ref[...])
    wu = _as_bf16(wu_ref[...])
    rows = u_ref.shape[0] // FFN_CHUNKS
    for r in range(0, u_ref.shape[0], rows):
        u = u_ref[r:r + rows, :]
        g = jnp.dot(u, wg, preferred_element_type=F32)
        up = jnp.dot(u, wu, preferred_element_type=F32)
        o_ref[r:r + rows, :] = (g * jax.nn.sigmoid(g) * up).astype(o_ref.dtype)


def _ffn_up(u, w_gate_up, *, tm=2048, tn=256):
    s, d = u.shape
    hidden = w_gate_up.shape[1] // 2
    nb = hidden // tn
    return pl.pallas_call(
        _ffn_up_kernel,
        out_shape=jax.ShapeDtypeStruct((s, hidden), BF16),
        grid=(s // tm, nb),
        in_specs=[pl.BlockSpec((tm, d), lambda i, j: (i, 0)),
                  pl.BlockSpec((d, tn), lambda i, j: (0, j)),
                  pl.BlockSpec((d, tn), lambda i, j: (0, nb + j))],
        out_specs=pl.BlockSpec((tm, tn), lambda i, j: (i, j)),
        compiler_params=_params(("parallel", "parallel"),
                                _nbytes((tm, d), BF16) + 2 * _nbytes((d, tn), w_gate_up.dtype)
                                + _nbytes((tm, tn), BF16),
                                2 * _nbytes((d, tn), BF16) + 2 * _nbytes((tm, tn), F32)),
        name="ffn_up",
    )(u, w_gate_up, w_gate_up)


def kernel(x, c, w_ada, b_ada, w_in, b_forget, w_attn_out, w_pool, pool_scale, w_out,
           ln1_g, ln1_b, w_gate_up, w_down, ln2_g, ln2_b):
    batch, seq, d = x.shape
    assert batch == 1 and w_ada.shape[0] == DEPTH == 1
    fox_w = w_attn_out.shape[1]
    heads = fox_w // FOX_HEAD_DIM
    pool_w = w_pool.shape[1] * w_pool.shape[2]
    o_f = 3 * fox_w
    o_p = o_f + heads
    tk = 512

    xs = x[0]
    w_in_t = jnp.swapaxes(w_in, 1, 2)[0]
    w_f = jnp.pad(w_in_t[o_f:o_p], ((0, LANES - heads), (0, 0))).astype(BF16)
    b_f = jnp.pad(b_forget, ((0, 0), (0, LANES - heads)))

    mod = _ada(c.reshape(d, 1), w_ada[0], b_ada)

    u1, e, c_blk = _ln_mod_fcum(xs, mod, 0, 1, w_f, b_f, heads=heads, tm=tk)
    cb = c_blk[:, 0, :heads].T
    tq = 2 * tk
    qt, sq_q = _matmul_tn(u1, w_in_t, 0, fox_w, tm=tq, tn=512, tk=tq, scale=FOX_HEAD_DIM ** -0.5 * LOG2E,
                          group_norms=True, name="proj_qt")
    k, sq_k = _matmul_nt(u1, w_in_t, fox_w, fox_w, BF16, tm=tq, tn=512, group_norms=True, name="proj_k")
    per_head = lambda sq: jnp.sqrt(sq[:, :, 0, :512 // LANES].reshape(seq // tq, heads))
    thr = (SKIP_LOG2 + 2.0 * NORM_SLACK * per_head(sq_q) * jnp.max(per_head(sq_k), axis=0)).T
    vt = _matmul_tn(u1, w_in_t, 2 * fox_w, fox_w, tm=1024, tn=512, tk=tk, name="proj_vt")
    p = _matmul_nt(u1, w_in_t, o_p, pool_w, F32, tm=1024, tn=512, name="proj_pool")
    gates = _matmul_nt(u1, w_in_t, o_p + pool_w, 2 * d, BF16, tm=1024, tn=512, sigmoid=True,
                       name="proj_gates")
    attn, w_down_bf16 = _attention(cb, thr, qt, k, e, vt, w_down[0], heads=heads, tk=tk)
    mix = _mix(attn, w_attn_out[0], p, w_pool[0], pool_scale, gates)
    r1 = _resid_matmul(mix, w_out[0], xs, mod, 2, tm=2048, tn=256, chunks=4, name="out_proj")
    x1, u2 = _ln_ln_mod(r1, ln1_g, ln1_b, mod, 3, 4)

    act = _ffn_up(u2, w_gate_up[0])
    r2 = _resid_matmul(act, w_down_bf16, x1, mod, 5, tm=512, tn=512, name="ffn_down")
    out = _ln_affine(r2, ln2_g, ln2_b)
    return out[None]
```

```python
import functools

import jax
import jax.numpy as jnp
from jax import lax
from jax.experimental import pallas as pl
from jax.experimental.pallas import tpu as pltpu

F32 = jnp.float32
BF16 = jnp.bfloat16

FOX_HEAD_DIM = 128
POOL_WINDOWS = (2, 4, 8, 16)
POOL_HALO = 16
DEPTH = 1
ALPHA = (2 * DEPTH) ** 0.25
LN_EPS = 1e-5
LOG2E = 1.4426950408889634
LANES = 128
SUBLANES = 8
BIAS_PIECES = 3
ONES_ROWS = 16
FFN_CHUNKS = 4
Q_PER_STEP = 4
SKIP_LOG2 = 48.0
NORM_SLACK = 1.02

V7X_VMEM_BYTES = 64 * 1024 * 1024
VMEM_CAP_BYTES = V7X_VMEM_BYTES - 6 * 1024 * 1024

_NT = (((1,), (1,)), ((), ()))
_TN = (((0,), (0,)), ((), ()))


def _nbytes(shape, dtype):
    n = jnp.dtype(dtype).itemsize
    for s in shape:
        n *= s
    return n


def _cast_bytes(shape, dtype):
    return 0 if dtype == BF16 else _nbytes(shape, BF16)


def _params(semantics, pipelined_bytes, resident_bytes=0):
    need = 2 * pipelined_bytes + resident_bytes
    assert need <= VMEM_CAP_BYTES, (need, VMEM_CAP_BYTES)
    return pltpu.CompilerParams(dimension_semantics=semantics,
                                vmem_limit_bytes=min(need + (4 << 20), VMEM_CAP_BYTES))


def _ln_rows(x):
    mu = jnp.mean(x, axis=-1, keepdims=True)
    xc = x - mu
    var = jnp.mean(xc * xc, axis=-1, keepdims=True)
    return xc * lax.rsqrt(var + LN_EPS)


def _as_bf16(w):
    return w if w.dtype == BF16 else w.astype(BF16)


def _split3(v):
    hi = v.astype(BF16)
    r1 = v - hi.astype(F32)
    mid = r1.astype(BF16)
    lo = (r1 - mid.astype(F32)).astype(BF16)
    return hi, mid, lo


def _ada_kernel(c_ref, w_ref, b_ref, o_ref):
    k = pl.program_id(1)

    @pl.when(k == 0)
    def _():
        o_ref[...] = b_ref[...]

    cc = c_ref[...]
    s = cc * jax.nn.sigmoid(cc)
    o_ref[...] += jnp.sum(s * w_ref[...], axis=0, keepdims=True)


def _ada(c_col, w_ada, b_ada, *, tk=512, tn=4096):
    d, n = w_ada.shape
    return pl.pallas_call(
        _ada_kernel,
        out_shape=jax.ShapeDtypeStruct((1, n), F32),
        grid=(n // tn, d // tk),
        in_specs=[pl.BlockSpec((tk, 1), lambda j, k: (k, 0)),
                  pl.BlockSpec((tk, tn), lambda j, k: (k, j)),
                  pl.BlockSpec((1, tn), lambda j, k: (0, j))],
        out_specs=pl.BlockSpec((1, tn), lambda j, k: (0, j)),
        compiler_params=_params(("parallel", "arbitrary"),
                                _nbytes((tk, tn), F32) + _nbytes((tk, LANES), F32),
                                _nbytes((tk, tn), F32)),
        name="ada",
    )(c_col, w_ada, b_ada)


def _ln_mod_fcum_kernel(x_ref, sh_ref, sc_ref, wf_ref, b_ref, u_ref, e_ref, c_ref,
                        carry_ref, f_ref, *, tm, rows, heads):
    @pl.when(pl.program_id(0) == 0)
    def _():
        carry_ref[...] = jnp.zeros_like(carry_ref)

    gain = 1.0 + sc_ref[...]
    shift = sh_ref[...]
    for r in range(0, tm, rows):
        u = (_ln_rows(x_ref[r:r + rows, :]) * gain + shift).astype(BF16)
        u_ref[r:r + rows, :] = u
        f_ref[r:r + rows, :] = lax.dot_general(u, wf_ref[...], _NT, preferred_element_type=F32)
    f = f_ref[...] + b_ref[...]
    ls = (jnp.minimum(f, 0.0) - jnp.log1p(jnp.exp(-jnp.abs(f)))) * LOG2E
    row = lax.broadcasted_iota(jnp.int32, (tm, tm), 0)
    col = lax.broadcasted_iota(jnp.int32, (tm, tm), 1)
    tri = jnp.where(col <= row, 1.0, 0.0).astype(BF16)
    cs = sum(jnp.dot(tri, piece, preferred_element_type=F32) for piece in _split3(ls))
    carry = carry_ref[...]
    c_ref[...] = carry
    carry_ref[...] = carry + cs[tm - 1:tm, :]
    head_lane = lax.broadcasted_iota(jnp.int32, (tm, LANES), 1) < heads
    e = sum(pltpu.roll(jnp.where(head_lane, piece.astype(F32), 0.0), p * heads, 1) if p else
            jnp.where(head_lane, piece.astype(F32), 0.0)
            for p, piece in enumerate(_split3(-cs)))
    e_ref[...] = e.astype(e_ref.dtype)


def _ln_mod_fcum(x, mod, shift_idx, scale_idx, wf, b_row, *, heads, tm, rows=128):
    s, d = x.shape
    nb = s // tm
    assert BIAS_PIECES * heads <= LANES
    vec = lambda idx: pl.BlockSpec((1, d), lambda i: (0, idx))
    const = lambda shape: pl.BlockSpec(shape, lambda i: (0, 0))
    return pl.pallas_call(
        functools.partial(_ln_mod_fcum_kernel, tm=tm, rows=rows, heads=heads),
        out_shape=(jax.ShapeDtypeStruct((s, d), BF16),
                   jax.ShapeDtypeStruct((s, LANES), BF16),
                   jax.ShapeDtypeStruct((nb, 1, LANES), F32)),
        grid=(nb,),
        in_specs=[pl.BlockSpec((tm, d), lambda i: (i, 0)), vec(shift_idx), vec(scale_idx),
                  const((LANES, d)), const((1, LANES))],
        out_specs=(pl.BlockSpec((tm, d), lambda i: (i, 0)),
                   pl.BlockSpec((tm, LANES), lambda i: (i, 0)),
                   pl.BlockSpec((None, 1, LANES), lambda i: (i, 0, 0))),
        scratch_shapes=[pltpu.VMEM((1, LANES), F32), pltpu.VMEM((tm, LANES), F32)],
        compiler_params=_params(("arbitrary",),
                                _nbytes((tm, d), F32) + _nbytes((tm, d), BF16)
                                + _nbytes((tm, LANES), BF16) + _nbytes((d, LANES), BF16),
                                4 * _nbytes((rows, d), F32) + 2 * _nbytes((tm, tm), F32)),
        name="ln_mod_fcum",
    )(x, mod, mod, wf, b_row)


def _wt_spec(tn, k, row0):
    assert row0 % SUBLANES == 0 and tn % SUBLANES == 0
    return pl.BlockSpec((pl.Element(tn), pl.Element(k)),
                        lambda i, j: ((row0 // SUBLANES + j * (tn // SUBLANES)) * SUBLANES, 0))


def _mm_nt_kernel(a_ref, w_ref, o_ref, *rest, sigmoid):
    w = _as_bf16(w_ref[...])
    tm = a_ref.shape[0]
    chunk = tm // 4 if sigmoid or rest else tm
    lane = lax.broadcasted_iota(jnp.int32, (1, LANES), 1)
    nrm = jnp.zeros((1, LANES), F32)
    for r in range(0, tm, chunk):
        acc = lax.dot_general(a_ref[r:r + chunk, :], w, _NT, preferred_element_type=F32)
        if sigmoid:
            acc = 0.5 * jnp.tanh(0.5 * acc) + 0.5
        o_ref[r:r + chunk, :] = acc.astype(o_ref.dtype)
        if rest:
            for g in range(acc.shape[1] // LANES):
                sq = jnp.sum(jnp.square(acc[:, g * LANES:(g + 1) * LANES]), axis=1, keepdims=True)
                nrm = jnp.maximum(nrm, jnp.where(lane == g, jnp.max(sq, axis=0, keepdims=True), 0.0))
    if rest:
        rest[0][...] = nrm


def _matmul_nt(a, wt, row0, n, out_dtype, *, tm, tn, sigmoid=False, group_norms=False, name):
    m, k = a.shape
    assert not (group_norms and sigmoid)
    out_shape = [jax.ShapeDtypeStruct((m, n), out_dtype)]
    out_specs = [pl.BlockSpec((tm, tn), lambda i, j: (i, j))]
    if group_norms:
        out_shape.append(jax.ShapeDtypeStruct((m // tm, n // tn, 1, LANES), F32))
        out_specs.append(pl.BlockSpec((None, None, 1, LANES), lambda i, j: (i, j, 0, 0)))
    out = pl.pallas_call(
        functools.partial(_mm_nt_kernel, sigmoid=sigmoid),
        out_shape=out_shape,
        grid=(m // tm, n // tn),
        in_specs=[pl.BlockSpec((tm, k), lambda i, j: (i, 0)), _wt_spec(tn, k, row0)],
        out_specs=out_specs,
        compiler_params=_params(("parallel", "parallel"),
                                _nbytes((tm, k), a.dtype) + _nbytes((tn, k), wt.dtype)
                                + _nbytes((tm, tn), out_dtype),
                                _cast_bytes((tn, k), wt.dtype) + 3 * _nbytes((tm, tn), F32)),
        name=name,
    )(a, wt)
    return out if group_norms else out[0]


def _mm_tn_kernel(a_ref, w_ref, o_ref, *rest, scale):
    acc_t = lax.dot_general(_as_bf16(w_ref[...]), a_ref[...], _NT,
                            preferred_element_type=F32)
    if scale is not None:
        acc_t = acc_t * scale
    tk = o_ref.shape[2]
    for kb in range(o_ref.shape[0]):
        o_ref[kb] = acc_t[:, kb * tk:(kb + 1) * tk].astype(o_ref.dtype)
    if rest:
        lane = lax.broadcasted_iota(jnp.int32, (1, LANES), 1)
        nrm = jnp.zeros((1, LANES), F32)
        for g in range(acc_t.shape[0] // LANES):
            sq = jnp.sum(jnp.square(acc_t[g * LANES:(g + 1) * LANES, :]), axis=0, keepdims=True)
            nrm = jnp.where(lane == g, jnp.max(sq, axis=1, keepdims=True), nrm)
        rest[0][...] = nrm


def _matmul_tn(a, wt, row0, n, *, tm, tn, tk, scale=None, group_norms=False, name):
    m, k = a.shape
    out_shape = [jax.ShapeDtypeStruct((m // tk, n, tk), BF16)]
    out_specs = [pl.BlockSpec((tm // tk, tn, tk), lambda i, j: (i, j, 0))]
    if group_norms:
        out_shape.append(jax.ShapeDtypeStruct((m // tm, n // tn, 1, LANES), F32))
        out_specs.append(pl.BlockSpec((None, None, 1, LANES), lambda i, j: (i, j, 0, 0)))
    out = pl.pallas_call(
        functools.partial(_mm_tn_kernel, scale=scale),
        out_shape=out_shape,
        grid=(m // tm, n // tn),
        in_specs=[pl.BlockSpec((tm, k), lambda i, j: (i, 0)), _wt_spec(tn, k, row0)],
        out_specs=out_specs,
        compiler_params=_params(("parallel", "parallel"),
                                _nbytes((tm, k), a.dtype) + _nbytes((tn, k), wt.dtype)
                                + _nbytes((tm, tn), BF16),
                                _cast_bytes((tn, k), wt.dtype) + 3 * _nbytes((tm, tn), F32)),
        name=name,
    )(a, wt)
    return out if group_norms else out[0]


def _attn_kernel(cb_ref, thr_ref, q_ref, k_ref, e_ref, vt_ref, w_ref, o_ref, wb_ref, *scratch,
                 tk, tn, heads, side_blocks):
    h = pl.program_id(0)

    @pl.when(h * pl.num_programs(1) + pl.program_id(1) < side_blocks)
    def _():
        wb_ref[...] = w_ref[...].astype(wb_ref.dtype)

    blocks = [Q_PER_STEP * pl.program_id(1) + sub for sub in range(Q_PER_STEP)]
    firsts = [lax.while_loop(lambda p, i=i: (p < i) & (cb_ref[h, 2 * p + 2] - cb_ref[h, 2 * i]
                                                        > thr_ref[h, i]),
                             lambda p: p + 1, jnp.int32(0)) for i in blocks]
    for sub, (i, first) in enumerate(zip(blocks, firsts)):
        _attn_block(cb_ref, q_ref.at[sub], k_ref, e_ref, vt_ref,
                    o_ref.at[:, sub * q_ref.shape[2]:(sub + 1) * q_ref.shape[2]],
                    *scratch[3 * sub:3 * sub + 3], h=h, i=i, first=first, tk=tk, tn=tn, heads=heads)


def _attn_block(cb_ref, q_ref, k_ref, e_ref, vt_ref, o_ref, s0_ref, s1_ref, acc_ref, *,
                h, i, first, tk, tn, heads):
    dh, tq = q_ref.shape
    nt = tq // tn
    half = nt // 2
    r = lax.broadcasted_iota(jnp.int32, (dh, tq), 0) - h
    ones_rows = sum(jnp.where(r == p * heads, 1.0, 0.0) for p in range(BIAS_PIECES))
    qt_aug = jnp.concatenate([q_ref[...], ones_rows.astype(BF16)], axis=0)
    ones_v = jnp.ones((acc_ref.shape[0] - dh, tk), BF16)
    c_q = cb_ref[h, 2 * i]
    acc_ref[...] = jnp.zeros_like(acc_ref)

    def logits_to(buf_ref, j, first_tile=0):
        rows = pl.ds(pl.multiple_of(j * tk, tk), tk)
        k_aug = jnp.concatenate([k_ref[rows, :], e_ref[rows, :]], axis=1)
        cms = []
        for n in range(first_tile, nt):
            cols = slice(n * tn, (n + 1) * tn)
            s = jnp.dot(k_aug, qt_aug[:, cols], preferred_element_type=F32)
            buf_ref[:, cols] = s
            cms.append(jnp.max(s, axis=0, keepdims=True))
        return jnp.concatenate(cms, axis=1)

    def causal(buf_ref, n, first_tile):
        key = lax.broadcasted_iota(jnp.int32, (tk, tn), 0)
        qry = lax.broadcasted_iota(jnp.int32, (tk, tn), 1) + (n - first_tile) * tn
        return jnp.where(key <= qry, buf_ref[:, n * tn:(n + 1) * tn], -jnp.inf)

    def softmax_pv(buf_ref, j, cmax, m, first_tile=0, causal_tiles=0):
        lo = first_tile * tn
        c = c_q - cb_ref[h, j]
        m_old = m[:, lo:]
        m_new = jnp.maximum(m_old, cmax + c)
        a = jnp.exp2(m_old - m_new)
        off = m_new - c
        v_aug = jnp.concatenate([vt_ref[j], ones_v], axis=0)
        for n in range(first_tile, nt):
            cols = slice(n * tn, (n + 1) * tn)
            rel = slice(n * tn - lo, (n + 1) * tn - lo)
            s = causal(buf_ref, n, first_tile) if n < first_tile + causal_tiles else buf_ref[:, cols]
            p = jnp.exp2(s - off[:, rel]).astype(BF16)
            acc_ref[:, cols] = a[:, rel] * acc_ref[:, cols] + jnp.dot(
                v_aug, p, preferred_element_type=F32)
        if first_tile:
            m_new = jnp.concatenate([m[:, :lo], m_new], axis=1)
        return m_new

    def causal_max(buf_ref, first_tile):
        return [jnp.max(causal(buf_ref, n, first_tile), axis=0, keepdims=True)
                for n in range(first_tile, first_tile + half)]

    def pair(p, carry):
        m, cm0 = carry
        cm1 = logits_to(s1_ref, 2 * p + 1)
        m = softmax_pv(s0_ref, 2 * p, cm0, m)
        cm0 = logits_to(s0_ref, 2 * p + 2)
        m = softmax_pv(s1_ref, 2 * p + 1, cm1, m)
        return m, cm0

    init = (jnp.full((1, tq), -jnp.inf, F32), logits_to(s0_ref, 2 * first))
    m, cm0 = lax.fori_loop(first, i, pair, init)
    logits_to(s1_ref, 2 * i + 1, first_tile=half)
    cm = jnp.concatenate(causal_max(s0_ref, 0) + [cm0[:, half * tn:]], axis=1)
    m = softmax_pv(s0_ref, 2 * i, cm, m, causal_tiles=half)
    cm = jnp.concatenate(causal_max(s1_ref, half), axis=1)
    softmax_pv(s1_ref, 2 * i + 1, cm, m, first_tile=half, causal_tiles=half)
    o_ref[...] = (acc_ref[:dh, :] / acc_ref[dh:dh + 1, :]).astype(o_ref.dtype)


def _attention(cb, thr, qt, k, e, vt, w_side, *, heads, tk, tn=256, side_blocks=32):
    s = k.shape[0]
    dh = FOX_HEAD_DIM
    tq = 2 * tk
    steps = s // (Q_PER_STEP * tq)
    side_cols = w_side.shape[1] // 2
    side_rows = 2 * w_side.shape[0] // side_blocks
    assert side_rows * side_blocks == 2 * w_side.shape[0] and side_rows % (2 * SUBLANES) == 0
    assert side_cols % LANES == 0 and side_blocks <= heads * steps

    def side(h, i, *_):
        blk = jnp.minimum(h * steps + i, side_blocks - 1)
        return blk // 2, blk % 2

    grid_spec = pltpu.PrefetchScalarGridSpec(
        num_scalar_prefetch=2,
        grid=(heads, steps),
        in_specs=[pl.BlockSpec((Q_PER_STEP, dh, tq), lambda h, i, *_: (i, h, 0)),
                  pl.BlockSpec((s, dh), lambda h, i, *_: (0, h)),
                  pl.BlockSpec((s, LANES), lambda h, i, *_: (0, 0)),
                  pl.BlockSpec((s // tk, dh, tk), lambda h, i, *_: (0, h, 0)),
                  pl.BlockSpec((side_rows, side_cols), side)],
        out_specs=[pl.BlockSpec((dh, Q_PER_STEP * tq), lambda h, i, *_: (h, i)),
                   pl.BlockSpec((side_rows, side_cols), side)],
        scratch_shapes=Q_PER_STEP * [pltpu.VMEM((tk, tq), F32), pltpu.VMEM((tk, tq), F32),
                                     pltpu.VMEM((dh + ONES_ROWS, tq), F32)],
    )
    return pl.pallas_call(
        functools.partial(_attn_kernel, tk=tk, tn=tn, heads=heads, side_blocks=side_blocks),
        out_shape=[jax.ShapeDtypeStruct((heads * dh, s), BF16),
                   jax.ShapeDtypeStruct(w_side.shape, BF16)],
        grid_spec=grid_spec,
        compiler_params=_params(("arbitrary", "arbitrary"),
                                3 * _nbytes((s, dh), BF16) + 2 * Q_PER_STEP * _nbytes((tq, dh), BF16)
                                + _nbytes((side_rows, side_cols), F32)
                                + _nbytes((side_rows, side_cols), BF16),
                                Q_PER_STEP * (2 * _nbytes((tk, tq), F32) + _nbytes((dh + ONES_ROWS, tq), F32))
                                + 3 * _nbytes((tk, tq), F32)),
        name="fox_attention",
    )(cb, thr, qt, k, e, vt, w_side)


def _mix_kernel(a_ref, wa_ref, halo_ref, p_ref, wp_ref, ps_ref, ga_ref, gp_ref, o_ref,
                wa_b, wp_b, *, tm):
    j = pl.program_id(0)
    i = pl.program_id(1)

    @pl.when(i == 0)
    def _():
        wa_b[...] = wa_ref[...].astype(BF16)
        wp_b[...] = wp_ref[...].astype(BF16)

    halo = jnp.where(i == 0, 0.0, halo_ref[...])
    ext = jnp.concatenate([halo, p_ref[...]], axis=0)
    acc = ext
    win = ext
    for g, w in enumerate(POOL_WINDOWS):
        acc = acc + pltpu.roll(acc, w // 2, 0)
        win = jnp.where(j == g, acc, win)
    t = i * tm + lax.broadcasted_iota(jnp.int32, (tm, 1), 0)
    cnt = jnp.minimum(t + 1, jnp.left_shift(2, j)).astype(F32)
    pooled = (win[POOL_HALO:] / cnt - ext[POOL_HALO:]).astype(BF16)

    ya = lax.dot_general(a_ref[...], wa_b[...], _TN, preferred_element_type=F32)
    yp = jnp.dot(pooled, wp_b[...], preferred_element_type=F32) * ps_ref[...]
    o_ref[...] = (ga_ref[...].astype(F32) * ya + gp_ref[...].astype(F32) * yp).astype(o_ref.dtype)


def _mix(attn_t, w_a, p, w_pool, pool_scale, gates, *, tm=512):
    fw, s = attn_t.shape
    groups, gd, tn = w_pool.shape
    assert POOL_WINDOWS == tuple(2 << g for g in range(groups))
    d = w_a.shape[1]
    hb = tm // POOL_HALO
    return pl.pallas_call(
        functools.partial(_mix_kernel, tm=tm),
        out_shape=jax.ShapeDtypeStruct((s, d), BF16),
        grid=(groups, s // tm),
        in_specs=[pl.BlockSpec((fw, tm), lambda j, i: (0, i)),
                  pl.BlockSpec((fw, tn), lambda j, i: (0, j)),
                  pl.BlockSpec((POOL_HALO, gd), lambda j, i: (jnp.maximum(i * hb - 1, 0), j)),
                  pl.BlockSpec((tm, gd), lambda j, i: (i, j)),
                  pl.BlockSpec((None, gd, tn), lambda j, i: (j, 0, 0)),
                  pl.BlockSpec((1, tn), lambda j, i: (0, j)),
                  pl.BlockSpec((tm, tn), lambda j, i: (i, j)),
                  pl.BlockSpec((tm, tn), lambda j, i: (i, groups + j))],
        out_specs=pl.BlockSpec((tm, tn), lambda j, i: (i, j)),
        scratch_shapes=[pltpu.VMEM((fw, tn), BF16), pltpu.VMEM((gd, tn), BF16)],
        compiler_params=_params(("parallel", "arbitrary"),
                                _nbytes((tm, fw), BF16) + _nbytes((fw, tn), w_a.dtype)
                                + _nbytes((tm + POOL_HALO, gd), F32) + _nbytes((gd, tn), w_pool.dtype)
                                + 3 * _nbytes((tm, tn), BF16),
                                _nbytes((fw + gd, tn), BF16) + 3 * _nbytes((tm, tn), F32)
                                + 4 * _nbytes((tm + POOL_HALO, gd), F32)),
        name="branch_mix",
    )(attn_t, w_a, p, p, w_pool, pool_scale, gates, gates)


def _resid_mm_kernel(a_ref, w_ref, x_ref, g_ref, o_ref, *, chunks):
    w = _as_bf16(w_ref[...])
    rows = a_ref.shape[0] // chunks
    for r in range(0, a_ref.shape[0], rows):
        m = jnp.dot(a_ref[r:r + rows, :], w, preferred_element_type=F32)
        o_ref[r:r + rows, :] = ALPHA * x_ref[r:r + rows, :] + g_ref[...] * m


def _resid_matmul(a, w, x, mod, gate_idx, *, tm, tn, chunks=1, name):
    m, k = a.shape
    n = w.shape[1]
    nb = n // tn
    return pl.pallas_call(
        functools.partial(_resid_mm_kernel, chunks=chunks),
        out_shape=jax.ShapeDtypeStruct((m, n), F32),
        grid=(m // tm, nb),
        in_specs=[pl.BlockSpec((tm, k), lambda i, j: (i, 0)),
                  pl.BlockSpec((k, tn), lambda i, j: (0, j)),
                  pl.BlockSpec((tm, tn), lambda i, j: (i, j)),
                  pl.BlockSpec((1, tn), lambda i, j: (0, gate_idx * nb + j))],
        out_specs=pl.BlockSpec((tm, tn), lambda i, j: (i, j)),
        compiler_params=_params(("parallel", "parallel"),
                                _nbytes((tm, k), BF16) + _nbytes((k, tn), w.dtype)
                                + 2 * _nbytes((tm, tn), F32),
                                _cast_bytes((k, tn), w.dtype) + 2 * _nbytes((tm, tn), F32) // chunks),
        name=name,
    )(a, w, x, mod)


def _ln_ln_mod_kernel(r_ref, g_ref, b_ref, sh_ref, sc_ref, x_ref, u_ref):
    x1 = _ln_rows(r_ref[...]) * g_ref[...] + b_ref[...]
    x_ref[...] = x1
    u_ref[...] = (_ln_rows(x1) * (1.0 + sc_ref[...]) + sh_ref[...]).astype(u_ref.dtype)


def _ln_ln_mod(r, gain, bias, mod, shift_idx, scale_idx, *, tm=256):
    s, d = r.shape
    row = pl.BlockSpec((tm, d), lambda i: (i, 0))
    vec = pl.BlockSpec((1, d), lambda i: (0, 0))
    return pl.pallas_call(
        _ln_ln_mod_kernel,
        out_shape=(jax.ShapeDtypeStruct((s, d), F32), jax.ShapeDtypeStruct((s, d), BF16)),
        grid=(s // tm,),
        in_specs=[row, vec, vec,
                  pl.BlockSpec((1, d), lambda i: (0, shift_idx)),
                  pl.BlockSpec((1, d), lambda i: (0, scale_idx))],
        out_specs=(row, row),
        compiler_params=_params(("parallel",),
                                2 * _nbytes((tm, d), F32) + _nbytes((tm, d), BF16),
                                4 * _nbytes((tm, d), F32)),
        name="ln1_ln_mod",
    )(r, gain, bias, mod, mod)


def _ln_affine_kernel(r_ref, g_ref, b_ref, o_ref):
    o_ref[...] = _ln_rows(r_ref[...]) * g_ref[...] + b_ref[...]


def _ln_affine(r, gain, bias, *, tm=256):
    s, d = r.shape
    row = pl.BlockSpec((tm, d), lambda i: (i, 0))
    vec = pl.BlockSpec((1, d), lambda i: (0, 0))
    return pl.pallas_call(
        _ln_affine_kernel,
        out_shape=jax.ShapeDtypeStruct((s, d), F32),
        grid=(s // tm,),
        in_specs=[row, vec, vec],
        out_specs=row,
        compiler_params=_params(("parallel",), 2 * _nbytes((tm, d), F32),
                                3 * _nbytes((tm, d), F32)),
        name="ln2",
    )(r, gain, bias)


def _ffn_up_kernel(u_ref, wg_ref, wu_ref, o_ref):
    wg = _as_bf16(wg_ref[...])
    wu = _as_bf16(wu_ref[...])
    rows = u_ref.shape[0] // FFN_CHUNKS
    for r in range(0, u_ref.shape[0], rows):
        u = u_ref[r:r + rows, :]
        g = jnp.dot(u, wg, preferred_element_type=F32)
        up = jnp.dot(u, wu, preferred_element_type=F32)
        o_ref[r:r + rows, :] = (g * jax.nn.sigmoid(g) * up).astype(o_ref.dtype)


def _ffn_up(u, w_gate_up, *, tm=2048, tn=256):
    s, d = u.shape
    hidden = w_gate_up.shape[1] // 2
    nb = hidden // tn
    return pl.pallas_call(
        _ffn_up_kernel,
        out_shape=jax.ShapeDtypeStruct((s, hidden), BF16),
        grid=(s // tm, nb),
        in_specs=[pl.BlockSpec((tm, d), lambda i, j: (i, 0)),
                  pl.BlockSpec((d, tn), lambda i, j: (0, j)),
                  pl.BlockSpec((d, tn), lambda i, j: (0, nb + j))],
        out_specs=pl.BlockSpec((tm, tn), lambda i, j: (i, j)),
        compiler_params=_params(("parallel", "parallel"),
                                _nbytes((tm, d), BF16) + 2 * _nbytes((d, tn), w_gate_up.dtype)
                                + _nbytes((tm, tn), BF16),
                                2 * _nbytes((d, tn), BF16) + 2 * _nbytes((tm, tn), F32)),
        name="ffn_up",
    )(u, w_gate_up, w_gate_up)


def kernel(x, c, w_ada, b_ada, w_in, b_forget, w_attn_out, w_pool, pool_scale, w_out,
           ln1_g, ln1_b, w_gate_up, w_down, ln2_g, ln2_b):
    batch, seq, d = x.shape
    assert batch == 1 and w_ada.shape[0] == DEPTH == 1
    fox_w = w_attn_out.shape[1]
    heads = fox_w // FOX_HEAD_DIM
    pool_w = w_pool.shape[1] * w_pool.shape[2]
    o_f = 3 * fox_w
    o_p = o_f + heads
    tk = 512

    xs = x[0]
    w_in_t = jnp.swapaxes(w_in, 1, 2)[0]
    w_f = jnp.pad(w_in_t[o_f:o_p], ((0, LANES - heads), (0, 0))).astype(BF16)
    b_f = jnp.pad(b_forget, ((0, 0), (0, LANES - heads)))

    mod = _ada(c.reshape(d, 1), w_ada[0], b_ada)

    u1, e, c_blk = _ln_mod_fcum(xs, mod, 0, 1, w_f, b_f, heads=heads, tm=tk)
    cb = c_blk[:, 0, :heads].T
    tq = 2 * tk
    qt, sq_q = _matmul_tn(u1, w_in_t, 0, fox_w, tm=tq, tn=512, tk=tq, scale=FOX_HEAD_DIM ** -0.5 * LOG2E,
                          group_norms=True, name="proj_qt")
    k, sq_k = _matmul_nt(u1, w_in_t, fox_w, fox_w, BF16, tm=tq, tn=512, group_norms=True, name="proj_k")
    per_head = lambda sq: jnp.sqrt(sq[:, :, 0, :512 // LANES].reshape(seq // tq, heads))
    thr = (SKIP_LOG2 + 2.0 * NORM_SLACK * per_head(sq_q) * jnp.max(per_head(sq_k), axis=0)).T
    vt = _matmul_tn(u1, w_in_t, 2 * fox_w, fox_w, tm=1024, tn=512, tk=tk, name="proj_vt")
    p = _matmul_nt(u1, w_in_t, o_p, pool_w, F32, tm=1024, tn=512, name="proj_pool")
    gates = _matmul_nt(u1, w_in_t, o_p + pool_w, 2 * d, BF16, tm=1024, tn=512, sigmoid=True,
                       name="proj_gates")
    attn, w_down_bf16 = _attention(cb, thr, qt, k, e, vt, w_down[0], heads=heads, tk=tk)
    mix = _mix(attn, w_attn_out[0], p, w_pool[0], pool_scale, gates)
    r1 = _resid_matmul(mix, w_out[0], xs, mod, 2, tm=2048, tn=256, chunks=4, name="out_proj")
    x1, u2 = _ln_ln_mod(r1, ln1_g, ln1_b, mod, 3, 4)

    act = _ffn_up(u2, w_gate_up[0])
    r2 = _resid_matmul(act, w_down_bf16, x1, mod, 5, tm=512, tn=512, name="ffn_down")
    out = _ln_affine(r2, ln2_g, ln2_b)
    return out[None]
```

```python
import functools

import jax
import jax.numpy as jnp
from jax import lax
from jax.experimental import pallas as pl
from jax.experimental.pallas import tpu as pltpu

F32 = jnp.float32
BF16 = jnp.bfloat16

FOX_HEAD_DIM = 128
POOL_WINDOWS = (2, 4, 8, 16)
POOL_HALO = 16
DEPTH = 1
ALPHA = (2 * DEPTH) ** 0.25
LN_EPS = 1e-5
LOG2E = 1.4426950408889634
LANES = 128
SUBLANES = 8
BIAS_PIECES = 3
ONES_ROWS = 16
FFN_CHUNKS = 4
Q_PER_STEP = 4
SKIP_LOG2 = 48.0
NORM_SLACK = 1.02

V7X_VMEM_BYTES = 64 * 1024 * 1024
VMEM_CAP_BYTES = V7X_VMEM_BYTES - 6 * 1024 * 1024

_NT = (((1,), (1,)), ((), ()))
_TN = (((0,), (0,)), ((), ()))


def _nbytes(shape, dtype):
    n = jnp.dtype(dtype).itemsize
    for s in shape:
        n *= s
    return n


def _cast_bytes(shape, dtype):
    return 0 if dtype == BF16 else _nbytes(shape, BF16)


def _params(semantics, pipelined_bytes, resident_bytes=0):
    need = 2 * pipelined_bytes + resident_bytes
    assert need <= VMEM_CAP_BYTES, (need, VMEM_CAP_BYTES)
    return pltpu.CompilerParams(dimension_semantics=semantics,
                                vmem_limit_bytes=min(need + (4 << 20), VMEM_CAP_BYTES))


def _ln_rows(x):
    mu = jnp.mean(x, axis=-1, keepdims=True)
    xc = x - mu
    var = jnp.mean(xc * xc, axis=-1, keepdims=True)
    return xc * lax.rsqrt(var + LN_EPS)


def _as_bf16(w):
    return w if w.dtype == BF16 else w.astype(BF16)


def _split3(v):
    hi = v.astype(BF16)
    r1 = v - hi.astype(F32)
    mid = r1.astype(BF16)
    lo = (r1 - mid.astype(F32)).astype(BF16)
    return hi, mid, lo


def _ada_kernel(c_ref, w_ref, b_ref, o_ref):
    k = pl.program_id(1)

    @pl.when(k == 0)
    def _():
        o_ref[...] = b_ref[...]

    cc = c_ref[...]
    s = cc * jax.nn.sigmoid(cc)
    o_ref[...] += jnp.sum(s * w_ref[...], axis=0, keepdims=True)


def _ada(c_col, w_ada, b_ada, *, tk=512, tn=4096):
    d, n = w_ada.shape
    return pl.pallas_call(
        _ada_kernel,
        out_shape=jax.ShapeDtypeStruct((1, n), F32),
        grid=(n // tn, d // tk),
        in_specs=[pl.BlockSpec((tk, 1), lambda j, k: (k, 0)),
                  pl.BlockSpec((tk, tn), lambda j, k: (k, j)),
                  pl.BlockSpec((1, tn), lambda j, k: (0, j))],
        out_specs=pl.BlockSpec((1, tn), lambda j, k: (0, j)),
        compiler_params=_params(("parallel", "arbitrary"),
                                _nbytes((tk, tn), F32) + _nbytes((tk, LANES), F32),
                                _nbytes((tk, tn), F32)),
        name="ada",
    )(c_col, w_ada, b_ada)


def _ln_mod_fcum_kernel(x_ref, sh_ref, sc_ref, wf_ref, b_ref, u_ref, e_ref, c_ref,
                        carry_ref, f_ref, *, tm, rows, heads):
    @pl.when(pl.program_id(0) == 0)
    def _():
        carry_ref[...] = jnp.zeros_like(carry_ref)

    gain = 1.0 + sc_ref[...]
    shift = sh_ref[...]
    for r in range(0, tm, rows):
        u = (_ln_rows(x_ref[r:r + rows, :]) * gain + shift).astype(BF16)
        u_ref[r:r + rows, :] = u
        f_ref[r:r + rows, :] = lax.dot_general(u, wf_ref[...], _NT, preferred_element_type=F32)
    f = f_ref[...] + b_ref[...]
    ls = (jnp.minimum(f, 0.0) - jnp.log1p(jnp.exp(-jnp.abs(f)))) * LOG2E
    row = lax.broadcasted_iota(jnp.int32, (tm, tm), 0)
    col = lax.broadcasted_iota(jnp.int32, (tm, tm), 1)
    tri = jnp.where(col <= row, 1.0, 0.0).astype(BF16)
    cs = sum(jnp.dot(tri, piece, preferred_element_type=F32) for piece in _split3(ls))
    carry = carry_ref[...]
    c_ref[...] = carry
    carry_ref[...] = carry + cs[tm - 1:tm, :]
    head_lane = lax.broadcasted_iota(jnp.int32, (tm, LANES), 1) < heads
    e = sum(pltpu.roll(jnp.where(head_lane, piece.astype(F32), 0.0), p * heads, 1) if p else
            jnp.where(head_lane, piece.astype(F32), 0.0)
            for p, piece in enumerate(_split3(-cs)))
    e_ref[...] = e.astype(e_ref.dtype)


def _ln_mod_fcum(x, mod, shift_idx, scale_idx, wf, b_row, *, heads, tm, rows=128):
    s, d = x.shape
    nb = s // tm
    assert BIAS_PIECES * heads <= LANES
    vec = lambda idx: pl.BlockSpec((1, d), lambda i: (0, idx))
    const = lambda shape: pl.BlockSpec(shape, lambda i: (0, 0))
    return pl.pallas_call(
        functools.partial(_ln_mod_fcum_kernel, tm=tm, rows=rows, heads=heads),
        out_shape=(jax.ShapeDtypeStruct((s, d), BF16),
                   jax.ShapeDtypeStruct((s, LANES), BF16),
                   jax.ShapeDtypeStruct((nb, 1, LANES), F32)),
        grid=(nb,),
        in_specs=[pl.BlockSpec((tm, d), lambda i: (i, 0)), vec(shift_idx), vec(scale_idx),
                  const((LANES, d)), const((1, LANES))],
        out_specs=(pl.BlockSpec((tm, d), lambda i: (i, 0)),
                   pl.BlockSpec((tm, LANES), lambda i: (i, 0)),
                   pl.BlockSpec((None, 1, LANES), lambda i: (i, 0, 0))),
        scratch_shapes=[pltpu.VMEM((1, LANES), F32), pltpu.VMEM((tm, LANES), F32)],
        compiler_params=_params(("arbitrary",),
                                _nbytes((tm, d), F32) + _nbytes((tm, d), BF16)
                                + _nbytes((tm, LANES), BF16) + _nbytes((d, LANES), BF16),
                                4 * _nbytes((rows, d), F32) + 2 * _nbytes((tm, tm), F32)),
        name="ln_mod_fcum",
    )(x, mod, mod, wf, b_row)


def _wt_spec(tn, k, row0):
    assert row0 % SUBLANES == 0 and tn % SUBLANES == 0
    return pl.BlockSpec((pl.Element(tn), pl.Element(k)),
                        lambda i, j: ((row0 // SUBLANES + j * (tn // SUBLANES)) * SUBLANES, 0))


def _mm_nt_kernel(a_ref, w_ref, o_ref, *rest, sigmoid):
    w = _as_bf16(w_ref[...])
    tm = a_ref.shape[0]
    chunk = tm // 4 if sigmoid or rest else tm
    lane = lax.broadcasted_iota(jnp.int32, (1, LANES), 1)
    nrm = jnp.zeros((1, LANES), F32)
    for r in range(0, tm, chunk):
        acc = lax.dot_general(a_ref[r:r + chunk, :], w, _NT, preferred_element_type=F32)
        if sigmoid:
            acc = 0.5 * jnp.tanh(0.5 * acc) + 0.5
        o_ref[r:r + chunk, :] = acc.astype(o_ref.dtype)
        if rest:
            for g in range(acc.shape[1] // LANES):
                sq = jnp.sum(jnp.square(acc[:, g * LANES:(g + 1) * LANES]), axis=1, keepdims=True)
                nrm = jnp.maximum(nrm, jnp.where(lane == g, jnp.max(sq, axis=0, keepdims=True), 0.0))
    if rest:
        rest[0][...] = nrm


def _matmul_nt(a, wt, row0, n, out_dtype, *, tm, tn, sigmoid=False, group_norms=False, name):
    m, k = a.shape
    assert not (group_norms and sigmoid)
    out_shape = [jax.ShapeDtypeStruct((m, n), out_dtype)]
    out_specs = [pl.BlockSpec((tm, tn), lambda i, j: (i, j))]
    if group_norms:
        out_shape.append(jax.ShapeDtypeStruct((m // tm, n // tn, 1, LANES), F32))
        out_specs.append(pl.BlockSpec((None, None, 1, LANES), lambda i, j: (i, j, 0, 0)))
    out = pl.pallas_call(
        functools.partial(_mm_nt_kernel, sigmoid=sigmoid),
        out_shape=out_shape,
        grid=(m // tm, n // tn),
        in_specs=[pl.BlockSpec((tm, k), lambda i, j: (i, 0)), _wt_spec(tn, k, row0)],
        out_specs=out_specs,
        compiler_params=_params(("parallel", "parallel"),
                                _nbytes((tm, k), a.dtype) + _nbytes((tn, k), wt.dtype)
                                + _nbytes((tm, tn), out_dtype),
                                _cast_bytes((tn, k), wt.dtype) + 3 * _nbytes((tm, tn), F32)),
        name=name,
    )(a, wt)
    return out if group_norms else out[0]


def _mm_tn_kernel(a_ref, w_ref, o_ref, *rest, scale):
    acc_t = lax.dot_general(_as_bf16(w_ref[...]), a_ref[...], _NT,
                            preferred_element_type=F32)
    if scale is not None:
        acc_t = acc_t * scale
    tk = o_ref.shape[2]
    for kb in range(o_ref.shape[0]):
        o_ref[kb] = acc_t[:, kb * tk:(kb + 1) * tk].astype(o_ref.dtype)
    if rest:
        lane = lax.broadcasted_iota(jnp.int32, (1, LANES), 1)
        nrm = jnp.zeros((1, LANES), F32)
        for g in range(acc_t.shape[0] // LANES):
            sq = jnp.sum(jnp.square(acc_t[g * LANES:(g + 1) * LANES, :]), axis=0, keepdims=True)
            nrm = jnp.where(lane == g, jnp.max(sq, axis=1, keepdims=True), nrm)
        rest[0][...] = nrm


def _matmul_tn(a, wt, row0, n, *, tm, tn, tk, scale=None, group_norms=False, name):
    m, k = a.shape
    out_shape = [jax.ShapeDtypeStruct((m // tk, n, tk), BF16)]
    out_specs = [pl.BlockSpec((tm // tk, tn, tk), lambda i, j: (i, j, 0))]
    if group_norms:
        out_shape.append(jax.ShapeDtypeStruct((m // tm, n // tn, 1, LANES), F32))
        out_specs.append(pl.BlockSpec((None, None, 1, LANES), lambda i, j: (i, j, 0, 0)))
    out = pl.pallas_call(
        functools.partial(_mm_tn_kernel, scale=scale),
        out_shape=out_shape,
        grid=(m // tm, n // tn),
        in_specs=[pl.BlockSpec((tm, k), lambda i, j: (i, 0)), _wt_spec(tn, k, row0)],
        out_specs=out_specs,
        compiler_params=_params(("parallel", "parallel"),
                                _nbytes((tm, k), a.dtype) + _nbytes((tn, k), wt.dtype)
                                + _nbytes((tm, tn), BF16),
                                _cast_bytes((tn, k), wt.dtype) + 3 * _nbytes((tm, tn), F32)),
        name=name,
    )(a, wt)
    return out if group_norms else out[0]


def _attn_kernel(cb_ref, thr_ref, q_ref, k_ref, e_ref, vt_ref, w_ref, o_ref, wb_ref, *scratch,
                 tk, tn, heads, side_blocks):
    h = pl.program_id(0)

    @pl.when(h * pl.num_programs(1) + pl.program_id(1) < side_blocks)
    def _():
        wb_ref[...] = w_ref[...].astype(wb_ref.dtype)

    blocks = [Q_PER_STEP * pl.program_id(1) + sub for sub in range(Q_PER_STEP)]
    firsts = [lax.while_loop(lambda p, i=i: (p < i) & (cb_ref[h, 2 * p + 2] - cb_ref[h, 2 * i]
                                                        > thr_ref[h, i]),
                             lambda p: p + 1, jnp.int32(0)) for i in blocks]
    for sub, (i, first) in enumerate(zip(blocks, firsts)):
        _attn_block(cb_ref, q_ref.at[sub], k_ref, e_ref, vt_ref,
                    o_ref.at[:, sub * q_ref.shape[2]:(sub + 1) * q_ref.shape[2]],
                    *scratch[3 * sub:3 * sub + 3], h=h, i=i, first=first, tk=tk, tn=tn, heads=heads)


def _attn_block(cb_ref, q_ref, k_ref, e_ref, vt_ref, o_ref, s0_ref, s1_ref, acc_ref, *,
                h, i, first, tk, tn, heads):
    dh, tq = q_ref.shape
    nt = tq // tn
    half = nt // 2
    r = lax.broadcasted_iota(jnp.int32, (dh, tq), 0) - h
    ones_rows = sum(jnp.where(r == p * heads, 1.0, 0.0) for p in range(BIAS_PIECES))
    qt_aug = jnp.concatenate([q_ref[...], ones_rows.astype(BF16)], axis=0)
    ones_v = jnp.ones((acc_ref.shape[0] - dh, tk), BF16)
    c_q = cb_ref[h, 2 * i]
    acc_ref[...] = jnp.zeros_like(acc_ref)

    def logits_to(buf_ref, j, first_tile=0):
        rows = pl.ds(pl.multiple_of(j * tk, tk), tk)
        k_aug = jnp.concatenate([k_ref[rows, :], e_ref[rows, :]], axis=1)
        cms = []
        for n in range(first_tile, nt):
            cols = slice(n * tn, (n + 1) * tn)
            s = jnp.dot(k_aug, qt_aug[:, cols], preferred_element_type=F32)
            buf_ref[:, cols] = s
            cms.append(jnp.max(s, axis=0, keepdims=True))
        return jnp.concatenate(cms, axis=1)

    def causal(buf_ref, n, first_tile):
        key = lax.broadcasted_iota(jnp.int32, (tk, tn), 0)
        qry = lax.broadcasted_iota(jnp.int32, (tk, tn), 1) + (n - first_tile) * tn
        return jnp.where(key <= qry, buf_ref[:, n * tn:(n + 1) * tn], -jnp.inf)

    def softmax_pv(buf_ref, j, cmax, m, first_tile=0, causal_tiles=0):
        lo = first_tile * tn
        c = c_q - cb_ref[h, j]
        m_old = m[:, lo:]
        m_new = jnp.maximum(m_old, cmax + c)
        a = jnp.exp2(m_old - m_new)
        off = m_new - c
        v_aug = jnp.concatenate([vt_ref[j], ones_v], axis=0)
        for n in range(first_tile, nt):
            cols = slice(n * tn, (n + 1) * tn)
            rel = slice(n * tn - lo, (n + 1) * tn - lo)
            s = causal(buf_ref, n, first_tile) if n < first_tile + causal_tiles else buf_ref[:, cols]
            p = jnp.exp2(s - off[:, rel]).astype(BF16)
            acc_ref[:, cols] = a[:, rel] * acc_ref[:, cols] + jnp.dot(
                v_aug, p, preferred_element_type=F32)
        if first_tile:
            m_new = jnp.concatenate([m[:, :lo], m_new], axis=1)
        return m_new

    def causal_max(buf_ref, first_tile):
        return [jnp.max(causal(buf_ref, n, first_tile), axis=0, keepdims=True)
                for n in range(first_tile, first_tile + half)]

    def pair(p, carry):
        m, cm0 = carry
        cm1 = logits_to(s1_ref, 2 * p + 1)
        m = softmax_pv(s0_ref, 2 * p, cm0, m)
        cm0 = logits_to(s0_ref, 2 * p + 2)
        m = softmax_pv(s1_ref, 2 * p + 1, cm1, m)
        return m, cm0

    def two_pairs(q, carry):
        p = start + 2 * q
        return pair(p + 1, pair(p, carry))

    start = first + (i - first) % 2
    carry = (jnp.full((1, tq), -jnp.inf, F32), logits_to(s0_ref, 2 * first))
    carry = lax.fori_loop(first, start, pair, carry)
    m, cm0 = lax.fori_loop(0, (i - start) // 2, two_pairs, carry)
    logits_to(s1_ref, 2 * i + 1, first_tile=half)
    cm = jnp.concatenate(causal_max(s0_ref, 0) + [cm0[:, half * tn:]], axis=1)
    m = softmax_pv(s0_ref, 2 * i, cm, m, causal_tiles=half)
    cm = jnp.concatenate(causal_max(s1_ref, half), axis=1)
    softmax_pv(s1_ref, 2 * i + 1, cm, m, first_tile=half, causal_tiles=half)
    o_ref[...] = (acc_ref[:dh, :] / acc_ref[dh:dh + 1, :]).astype(o_ref.dtype)


def _attention(cb, thr, qt, k, e, vt, w_side, *, heads, tk, tn=256, side_blocks=32):
    s = k.shape[0]
    dh = FOX_HEAD_DIM
    tq = 2 * tk
    steps = s // (Q_PER_STEP * tq)
    side_cols = w_side.shape[1] // 2
    side_rows = 2 * w_side.shape[0] // side_blocks
    assert side_rows * side_blocks == 2 * w_side.shape[0] and side_rows % (2 * SUBLANES) == 0
    assert side_cols % LANES == 0 and side_blocks <= heads * steps

    def side(h, i, *_):
        blk = jnp.minimum(h * steps + i, side_blocks - 1)
        return blk // 2, blk % 2

    grid_spec = pltpu.PrefetchScalarGridSpec(
        num_scalar_prefetch=2,
        grid=(heads, steps),
        in_specs=[pl.BlockSpec((Q_PER_STEP, dh, tq), lambda h, i, *_: (i, h, 0)),
                  pl.BlockSpec((s, dh), lambda h, i, *_: (0, h)),
                  pl.BlockSpec((s, LANES), lambda h, i, *_: (0, 0)),
                  pl.BlockSpec((s // tk, dh, tk), lambda h, i, *_: (0, h, 0)),
                  pl.BlockSpec((side_rows, side_cols), side)],
        out_specs=[pl.BlockSpec((dh, Q_PER_STEP * tq), lambda h, i, *_: (h, i)),
                   pl.BlockSpec((side_rows, side_cols), side)],
        scratch_shapes=Q_PER_STEP * [pltpu.VMEM((tk, tq), F32), pltpu.VMEM((tk, tq), F32),
                                     pltpu.VMEM((dh + ONES_ROWS, tq), F32)],
    )
    return pl.pallas_call(
        functools.partial(_attn_kernel, tk=tk, tn=tn, heads=heads, side_blocks=side_blocks),
        out_shape=[jax.ShapeDtypeStruct((heads * dh, s), BF16),
                   jax.ShapeDtypeStruct(w_side.shape, BF16)],
        grid_spec=grid_spec,
        compiler_params=_params(("arbitrary", "arbitrary"),
                                3 * _nbytes((s, dh), BF16) + 2 * Q_PER_STEP * _nbytes((tq, dh), BF16)
                                + _nbytes((side_rows, side_cols), F32)
                                + _nbytes((side_rows, side_cols), BF16),
                                Q_PER_STEP * (2 * _nbytes((tk, tq), F32) + _nbytes((dh + ONES_ROWS, tq), F32))
                                + 3 * _nbytes((tk, tq), F32)),
        name="fox_attention",
    )(cb, thr, qt, k, e, vt, w_side)


def _mix_kernel(a_ref, wa_ref, halo_ref, p_ref, wp_ref, ps_ref, ga_ref, gp_ref, o_ref,
                wa_b, wp_b, *, tm):
    j = pl.program_id(0)
    i = pl.program_id(1)

    @pl.when(i == 0)
    def _():
        wa_b[...] = wa_ref[...].astype(BF16)
        wp_b[...] = wp_ref[...].astype(BF16)

    halo = jnp.where(i == 0, 0.0, halo_ref[...])
    ext = jnp.concatenate([halo, p_ref[...]], axis=0)
    acc = ext
    win = ext
    for g, w in enumerate(POOL_WINDOWS):
        acc = acc + pltpu.roll(acc, w // 2, 0)
        win = jnp.where(j == g, acc, win)
    t = i * tm + lax.broadcasted_iota(jnp.int32, (tm, 1), 0)
    cnt = jnp.minimum(t + 1, jnp.left_shift(2, j)).astype(F32)
    pooled = (win[POOL_HALO:] / cnt - ext[POOL_HALO:]).astype(BF16)

    ya = lax.dot_general(a_ref[...], wa_b[...], _TN, preferred_element_type=F32)
    yp = jnp.dot(pooled, wp_b[...], preferred_element_type=F32) * ps_ref[...]
    o_ref[...] = (ga_ref[...].astype(F32) * ya + gp_ref[...].astype(F32) * yp).astype(o_ref.dtype)


def _mix(attn_t, w_a, p, w_pool, pool_scale, gates, *, tm=512):
    fw, s = attn_t.shape
    groups, gd, tn = w_pool.shape
    assert POOL_WINDOWS == tuple(2 << g for g in range(groups))
    d = w_a.shape[1]
    hb = tm // POOL_HALO
    return pl.pallas_call(
        functools.partial(_mix_kernel, tm=tm),
        out_shape=jax.ShapeDtypeStruct((s, d), BF16),
        grid=(groups, s // tm),
        in_specs=[pl.BlockSpec((fw, tm), lambda j, i: (0, i)),
                  pl.BlockSpec((fw, tn), lambda j, i: (0, j)),
                  pl.BlockSpec((POOL_HALO, gd), lambda j, i: (jnp.maximum(i * hb - 1, 0), j)),
                  pl.BlockSpec((tm, gd), lambda j, i: (i, j)),
                  pl.BlockSpec((None, gd, tn), lambda j, i: (j, 0, 0)),
                  pl.BlockSpec((1, tn), lambda j, i: (0, j)),
                  pl.BlockSpec((tm, tn), lambda j, i: (i, j)),
                  pl.BlockSpec((tm, tn), lambda j, i: (i, groups + j))],
        out_specs=pl.BlockSpec((tm, tn), lambda j, i: (i, j)),
        scratch_shapes=[pltpu.VMEM((fw, tn), BF16), pltpu.VMEM((gd, tn), BF16)],
        compiler_params=_params(("parallel", "arbitrary"),
                                _nbytes((tm, fw), BF16) + _nbytes((fw, tn), w_a.dtype)
                                + _nbytes((tm + POOL_HALO, gd), F32) + _nbytes((gd, tn), w_pool.dtype)
                                + 3 * _nbytes((tm, tn), BF16),
                                _nbytes((fw + gd, tn), BF16) + 3 * _nbytes((tm, tn), F32)
                                + 4 * _nbytes((tm + POOL_HALO, gd), F32)),
        name="branch_mix",
    )(attn_t, w_a, p, p, w_pool, pool_scale, gates, gates)


def _resid_mm_kernel(a_ref, w_ref, x_ref, g_ref, o_ref, *, chunks):
    w = _as_bf16(w_ref[...])
    rows = a_ref.shape[0] // chunks
    for r in range(0, a_ref.shape[0], rows):
        m = jnp.dot(a_ref[r:r + rows, :], w, preferred_element_type=F32)
        o_ref[r:r + rows, :] = ALPHA * x_ref[r:r + rows, :] + g_ref[...] * m


def _resid_matmul(a, w, x, mod, gate_idx, *, tm, tn, chunks=1, name):
    m, k = a.shape
    n = w.shape[1]
    nb = n // tn
    return pl.pallas_call(
        functools.partial(_resid_mm_kernel, chunks=chunks),
        out_shape=jax.ShapeDtypeStruct((m, n), F32),
        grid=(m // tm, nb),
        in_specs=[pl.BlockSpec((tm, k), lambda i, j: (i, 0)),
                  pl.BlockSpec((k, tn), lambda i, j: (0, j)),
                  pl.BlockSpec((tm, tn), lambda i, j: (i, j)),
                  pl.BlockSpec((1, tn), lambda i, j: (0, gate_idx * nb + j))],
        out_specs=pl.BlockSpec((tm, tn), lambda i, j: (i, j)),
        compiler_params=_params(("parallel", "parallel"),
                                _nbytes((tm, k), BF16) + _nbytes((k, tn), w.dtype)
                                + 2 * _nbytes((tm, tn), F32),
                                _cast_bytes((k, tn), w.dtype) + 2 * _nbytes((tm, tn), F32) // chunks),
        name=name,
    )(a, w, x, mod)


def _ln_ln_mod_kernel(r_ref, g_ref, b_ref, sh_ref, sc_ref, x_ref, u_ref):
    x1 = _ln_rows(r_ref[...]) * g_ref[...] + b_ref[...]
    x_ref[...] = x1
    u_ref[...] = (_ln_rows(x1) * (1.0 + sc_ref[...]) + sh_ref[...]).astype(u_ref.dtype)


def _ln_ln_mod(r, gain, bias, mod, shift_idx, scale_idx, *, tm=256):
    s, d = r.shape
    row = pl.BlockSpec((tm, d), lambda i: (i, 0))
    vec = pl.BlockSpec((1, d), lambda i: (0, 0))
    return pl.pallas_call(
        _ln_ln_mod_kernel,
        out_shape=(jax.ShapeDtypeStruct((s, d), F32), jax.ShapeDtypeStruct((s, d), BF16)),
        grid=(s // tm,),
        in_specs=[row, vec, vec,
                  pl.BlockSpec((1, d), lambda i: (0, shift_idx)),
                  pl.BlockSpec((1, d), lambda i: (0, scale_idx))],
        out_specs=(row, row),
        compiler_params=_params(("parallel",),
                                2 * _nbytes((tm, d), F32) + _nbytes((tm, d), BF16),
                                4 * _nbytes((tm, d), F32)),
        name="ln1_ln_mod",
    )(r, gain, bias, mod, mod)


def _ln_affine_kernel(r_ref, g_ref, b_ref, o_ref):
    o_ref[...] = _ln_rows(r_ref[...]) * g_ref[...] + b_ref[...]


def _ln_affine(r, gain, bias, *, tm=512):
    s, d = r.shape
    row = pl.BlockSpec((tm, d), lambda i: (i, 0))
    vec = pl.BlockSpec((1, d), lambda i: (0, 0))
    return pl.pallas_call(
        _ln_affine_kernel,
        out_shape=jax.ShapeDtypeStruct((s, d), F32),
        grid=(s // tm,),
        in_specs=[row, vec, vec],
        out_specs=row,
        compiler_params=_params(("parallel",), 2 * _nbytes((tm, d), F32),
                                3 * _nbytes((tm, d), F32)),
        name="ln2",
    )(r, gain, bias)


def _ffn_up_kernel(u_ref, wg_ref, wu_ref, o_ref):
    wg = _as_bf16(wg_ref[...])
    wu = _as_bf16(wu_ref[...])
    rows = u_ref.shape[0] // FFN_CHUNKS
    for r in range(0, u_ref.shape[0], rows):
        u = u_ref[r:r + rows, :]
        g = jnp.dot(u, wg, preferred_element_type=F32)
        up = jnp.dot(u, wu, preferred_element_type=F32)
        o_ref[r:r + rows, :] = (g * jax.nn.sigmoid(g) * up).astype(o_ref.dtype)


def _ffn_up(u, w_gate_up, *, tm=2048, tn=256):
    s, d = u.shape
    hidden = w_gate_up.shape[1] // 2
    nb = hidden // tn
    return pl.pallas_call(
        _ffn_up_kernel,
        out_shape=jax.ShapeDtypeStruct((s, hidden), BF16),
        grid=(s // tm, nb),
        in_specs=[pl.BlockSpec((tm, d), lambda i, j: (i, 0)),
                  pl.BlockSpec((d, tn), lambda i, j: (0, j)),
                  pl.BlockSpec((d, tn), lambda i, j: (0, nb + j))],
        out_specs=pl.BlockSpec((tm, tn), lambda i, j: (i, j)),
        compiler_params=_params(("parallel", "parallel"),
                                _nbytes((tm, d), BF16) + 2 * _nbytes((d, tn), w_gate_up.dtype)
                                + _nbytes((tm, tn), BF16),
                                2 * _nbytes((d, tn), BF16) + 2 * _nbytes((tm, tn), F32)),
        name="ffn_up",
    )(u, w_gate_up, w_gate_up)


def kernel(x, c, w_ada, b_ada, w_in, b_forget, w_attn_out, w_pool, pool_scale, w_out,
           ln1_g, ln1_b, w_gate_up, w_down, ln2_g, ln2_b):
    batch, seq, d = x.shape
    assert batch == 1 and w_ada.shape[0] == DEPTH == 1
    fox_w = w_attn_out.shape[1]
    heads = fox_w // FOX_HEAD_DIM
    pool_w = w_pool.shape[1] * w_pool.shape[2]
    o_f = 3 * fox_w
    o_p = o_f + heads
    tk = 512

    xs = x[0]
    w_in_t = jnp.swapaxes(w_in, 1, 2)[0]
    w_f = jnp.pad(w_in_t[o_f:o_p], ((0, LANES - heads), (0, 0))).astype(BF16)
    b_f = jnp.pad(b_forget, ((0, 0), (0, LANES - heads)))

    mod = _ada(c.reshape(d, 1), w_ada[0], b_ada)

    u1, e, c_blk = _ln_mod_fcum(xs, mod, 0, 1, w_f, b_f, heads=heads, tm=tk)
    cb = c_blk[:, 0, :heads].T
    tq = 2 * tk
    qt, sq_q = _matmul_tn(u1, w_in_t, 0, fox_w, tm=tq, tn=512, tk=tq, scale=FOX_HEAD_DIM ** -0.5 * LOG2E,
                          group_norms=True, name="proj_qt")
    k, sq_k = _matmul_nt(u1, w_in_t, fox_w, fox_w, BF16, tm=tq, tn=512, group_norms=True, name="proj_k")
    per_head = lambda sq: jnp.sqrt(sq[:, :, 0, :512 // LANES].reshape(seq // tq, heads))
    thr = (SKIP_LOG2 + 2.0 * NORM_SLACK * per_head(sq_q) * jnp.max(per_head(sq_k), axis=0)).T
    vt = _matmul_tn(u1, w_in_t, 2 * fox_w, fox_w, tm=1024, tn=512, tk=tk, name="proj_vt")
    p = _matmul_nt(u1, w_in_t, o_p, pool_w, F32, tm=1024, tn=512, name="proj_pool")
    gates = _matmul_nt(u1, w_in_t, o_p + pool_w, 2 * d, BF16, tm=1024, tn=512, sigmoid=True,
                       name="proj_gates")
    attn, w_down_bf16 = _attention(cb, thr, qt, k, e, vt, w_down[0], heads=heads, tk=tk)
    mix = _mix(attn, w_attn_out[0], p, w_pool[0], pool_scale, gates)
    r1 = _resid_matmul(mix, w_out[0], xs, mod, 2, tm=2048, tn=256, chunks=4, name="out_proj")
    x1, u2 = _ln_ln_mod(r1, ln1_g, ln1_b, mod, 3, 4)

    act = _ffn_up(u2, w_gate_up[0])
    r2 = _resid_matmul(act, w_down_bf16, x1, mod, 5, tm=512, tn=512, name="ffn_down")
    out = _ln_affine(r2, ln2_g, ln2_b)
    return out[None]
```

```python
import functools

import jax
import jax.numpy as jnp
from jax import lax
from jax.experimental import pallas as pl
from jax.experimental.pallas import tpu as pltpu

F32 = jnp.float32
BF16 = jnp.bfloat16

FOX_HEAD_DIM = 128
POOL_WINDOWS = (2, 4, 8, 16)
POOL_HALO = 16
DEPTH = 1
ALPHA = (2 * DEPTH) ** 0.25
LN_EPS = 1e-5
LOG2E = 1.4426950408889634
LANES = 128
SUBLANES = 8
BIAS_PIECES = 3
ONES_ROWS = 16
MIX_CHUNKS = 4
FFN_CHUNKS = 4
Q_PER_STEP = 4
SKIP_LOG2 = 48.0
NORM_SLACK = 1.02

V7X_VMEM_BYTES = 64 * 1024 * 1024
VMEM_CAP_BYTES = V7X_VMEM_BYTES - 6 * 1024 * 1024

_NT = (((1,), (1,)), ((), ()))
_TN = (((0,), (0,)), ((), ()))


def _nbytes(shape, dtype):
    n = jnp.dtype(dtype).itemsize
    for s in shape:
        n *= s
    return n


def _cast_bytes(shape, dtype):
    return 0 if dtype == BF16 else _nbytes(shape, BF16)


def _params(semantics, pipelined_bytes, resident_bytes=0):
    need = 2 * pipelined_bytes + resident_bytes
    assert need <= VMEM_CAP_BYTES, (need, VMEM_CAP_BYTES)
    return pltpu.CompilerParams(dimension_semantics=semantics,
                                vmem_limit_bytes=min(need + (4 << 20), VMEM_CAP_BYTES))


def _ln_rows(x):
    mu = jnp.mean(x, axis=-1, keepdims=True)
    xc = x - mu
    var = jnp.mean(xc * xc, axis=-1, keepdims=True)
    return xc * lax.rsqrt(var + LN_EPS)


def _as_bf16(w):
    return w if w.dtype == BF16 else w.astype(BF16)


def _split3(v):
    hi = v.astype(BF16)
    r1 = v - hi.astype(F32)
    mid = r1.astype(BF16)
    lo = (r1 - mid.astype(F32)).astype(BF16)
    return hi, mid, lo


def _ada_kernel(c_ref, w_ref, b_ref, o_ref):
    k = pl.program_id(1)

    @pl.when(k == 0)
    def _():
        o_ref[...] = b_ref[...]

    cc = c_ref[...]
    s = cc * jax.nn.sigmoid(cc)
    o_ref[...] += jnp.sum(s * w_ref[...], axis=0, keepdims=True)


def _ada(c_col, w_ada, b_ada, *, tk=512, tn=4096):
    d, n = w_ada.shape
    return pl.pallas_call(
        _ada_kernel,
        out_shape=jax.ShapeDtypeStruct((1, n), F32),
        grid=(n // tn, d // tk),
        in_specs=[pl.BlockSpec((tk, 1), lambda j, k: (k, 0)),
                  pl.BlockSpec((tk, tn), lambda j, k: (k, j)),
                  pl.BlockSpec((1, tn), lambda j, k: (0, j))],
        out_specs=pl.BlockSpec((1, tn), lambda j, k: (0, j)),
        compiler_params=_params(("parallel", "arbitrary"),
                                _nbytes((tk, tn), F32) + _nbytes((tk, LANES), F32),
                                _nbytes((tk, tn), F32)),
        name="ada",
    )(c_col, w_ada, b_ada)


def _ln_mod_fcum_kernel(x_ref, sh_ref, sc_ref, wf_ref, b_ref, u_ref, e_ref, c_ref,
                        carry_ref, f_ref, *, tm, rows, heads):
    @pl.when(pl.program_id(0) == 0)
    def _():
        carry_ref[...] = jnp.zeros_like(carry_ref)

    gain = 1.0 + sc_ref[...]
    shift = sh_ref[...]
    for r in range(0, tm, rows):
        u = (_ln_rows(x_ref[r:r + rows, :]) * gain + shift).astype(BF16)
        u_ref[r:r + rows, :] = u
        f_ref[r:r + rows, :] = lax.dot_general(u, wf_ref[...], _NT, preferred_element_type=F32)
    f = f_ref[...] + b_ref[...]
    ls = (jnp.minimum(f, 0.0) - jnp.log1p(jnp.exp(-jnp.abs(f)))) * LOG2E
    row = lax.broadcasted_iota(jnp.int32, (tm, tm), 0)
    col = lax.broadcasted_iota(jnp.int32, (tm, tm), 1)
    tri = jnp.where(col <= row, 1.0, 0.0).astype(BF16)
    cs = sum(jnp.dot(tri, piece, preferred_element_type=F32) for piece in _split3(ls))
    carry = carry_ref[...]
    c_ref[...] = carry
    carry_ref[...] = carry + cs[tm - 1:tm, :]
    head_lane = lax.broadcasted_iota(jnp.int32, (tm, LANES), 1) < heads
    e = sum(pltpu.roll(jnp.where(head_lane, piece.astype(F32), 0.0), p * heads, 1) if p else
            jnp.where(head_lane, piece.astype(F32), 0.0)
            for p, piece in enumerate(_split3(-cs)))
    e_ref[...] = e.astype(e_ref.dtype)


def _ln_mod_fcum(x, mod, shift_idx, scale_idx, wf, b_row, *, heads, tm, rows=128):
    s, d = x.shape
    nb = s // tm
    assert BIAS_PIECES * heads <= LANES
    vec = lambda idx: pl.BlockSpec((1, d), lambda i: (0, idx))
    const = lambda shape: pl.BlockSpec(shape, lambda i: (0, 0))
    return pl.pallas_call(
        functools.partial(_ln_mod_fcum_kernel, tm=tm, rows=rows, heads=heads),
        out_shape=(jax.ShapeDtypeStruct((s, d), BF16),
                   jax.ShapeDtypeStruct((s, LANES), BF16),
                   jax.ShapeDtypeStruct((nb, 1, LANES), F32)),
        grid=(nb,),
        in_specs=[pl.BlockSpec((tm, d), lambda i: (i, 0)), vec(shift_idx), vec(scale_idx),
                  const((LANES, d)), const((1, LANES))],
        out_specs=(pl.BlockSpec((tm, d), lambda i: (i, 0)),
                   pl.BlockSpec((tm, LANES), lambda i: (i, 0)),
                   pl.BlockSpec((None, 1, LANES), lambda i: (i, 0, 0))),
        scratch_shapes=[pltpu.VMEM((1, LANES), F32), pltpu.VMEM((tm, LANES), F32)],
        compiler_params=_params(("arbitrary",),
                                _nbytes((tm, d), F32) + _nbytes((tm, d), BF16)
                                + _nbytes((tm, LANES), BF16) + _nbytes((d, LANES), BF16),
                                4 * _nbytes((rows, d), F32) + 2 * _nbytes((tm, tm), F32)),
        name="ln_mod_fcum",
    )(x, mod, mod, wf, b_row)


def _wt_spec(tn, k, row0):
    assert row0 % SUBLANES == 0 and tn % SUBLANES == 0
    return pl.BlockSpec((pl.Element(tn), pl.Element(k)),
                        lambda i, j: ((row0 // SUBLANES + j * (tn // SUBLANES)) * SUBLANES, 0))


def _mm_nt_kernel(a_ref, w_ref, o_ref, *rest, sigmoid):
    w = _as_bf16(w_ref[...])
    tm = a_ref.shape[0]
    chunk = tm // 4 if sigmoid or rest else tm
    lane = lax.broadcasted_iota(jnp.int32, (1, LANES), 1)
    nrm = jnp.zeros((1, LANES), F32)
    for r in range(0, tm, chunk):
        acc = lax.dot_general(a_ref[r:r + chunk, :], w, _NT, preferred_element_type=F32)
        if sigmoid:
            acc = 0.5 * jnp.tanh(0.5 * acc) + 0.5
        o_ref[r:r + chunk, :] = acc.astype(o_ref.dtype)
        if rest:
            for g in range(acc.shape[1] // LANES):
                sq = jnp.sum(jnp.square(acc[:, g * LANES:(g + 1) * LANES]), axis=1, keepdims=True)
                nrm = jnp.maximum(nrm, jnp.where(lane == g, jnp.max(sq, axis=0, keepdims=True), 0.0))
    if rest:
        rest[0][...] = nrm


def _matmul_nt(a, wt, row0, n, out_dtype, *, tm, tn, sigmoid=False, group_norms=False, name):
    m, k = a.shape
    assert not (group_norms and sigmoid)
    out_shape = [jax.ShapeDtypeStruct((m, n), out_dtype)]
    out_specs = [pl.BlockSpec((tm, tn), lambda i, j: (i, j))]
    if group_norms:
        out_shape.append(jax.ShapeDtypeStruct((m // tm, n // tn, 1, LANES), F32))
        out_specs.append(pl.BlockSpec((None, None, 1, LANES), lambda i, j: (i, j, 0, 0)))
    out = pl.pallas_call(
        functools.partial(_mm_nt_kernel, sigmoid=sigmoid),
        out_shape=out_shape,
        grid=(m // tm, n // tn),
        in_specs=[pl.BlockSpec((tm, k), lambda i, j: (i, 0)), _wt_spec(tn, k, row0)],
        out_specs=out_specs,
        compiler_params=_params(("parallel", "parallel"),
                                _nbytes((tm, k), a.dtype) + _nbytes((tn, k), wt.dtype)
                                + _nbytes((tm, tn), out_dtype),
                                _cast_bytes((tn, k), wt.dtype) + 3 * _nbytes((tm, tn), F32)),
        name=name,
    )(a, wt)
    return out if group_norms else out[0]


def _mm_tn_kernel(a_ref, w_ref, o_ref, *rest, scale):
    acc_t = lax.dot_general(_as_bf16(w_ref[...]), a_ref[...], _NT,
                            preferred_element_type=F32)
    if scale is not None:
        acc_t = acc_t * scale
    tk = o_ref.shape[2]
    for kb in range(o_ref.shape[0]):
        o_ref[kb] = acc_t[:, kb * tk:(kb + 1) * tk].astype(o_ref.dtype)
    if rest:
        lane = lax.broadcasted_iota(jnp.int32, (1, LANES), 1)
        nrm = jnp.zeros((1, LANES), F32)
        for g in range(acc_t.shape[0] // LANES):
            sq = jnp.sum(jnp.square(acc_t[g * LANES:(g + 1) * LANES, :]), axis=0, keepdims=True)
            nrm = jnp.where(lane == g, jnp.max(sq, axis=1, keepdims=True), nrm)
        rest[0][...] = nrm


def _matmul_tn(a, wt, row0, n, *, tm, tn, tk, scale=None, group_norms=False, name):
    m, k = a.shape
    out_shape = [jax.ShapeDtypeStruct((m // tk, n, tk), BF16)]
    out_specs = [pl.BlockSpec((tm // tk, tn, tk), lambda i, j: (i, j, 0))]
    if group_norms:
        out_shape.append(jax.ShapeDtypeStruct((m // tm, n // tn, 1, LANES), F32))
        out_specs.append(pl.BlockSpec((None, None, 1, LANES), lambda i, j: (i, j, 0, 0)))
    out = pl.pallas_call(
        functools.partial(_mm_tn_kernel, scale=scale),
        out_shape=out_shape,
        grid=(m // tm, n // tn),
        in_specs=[pl.BlockSpec((tm, k), lambda i, j: (i, 0)), _wt_spec(tn, k, row0)],
        out_specs=out_specs,
        compiler_params=_params(("parallel", "parallel"),
                                _nbytes((tm, k), a.dtype) + _nbytes((tn, k), wt.dtype)
                                + _nbytes((tm, tn), BF16),
                                _cast_bytes((tn, k), wt.dtype) + 3 * _nbytes((tm, tn), F32)),
        name=name,
    )(a, wt)
    return out if group_norms else out[0]


def _attn_kernel(cb_ref, thr_ref, q_ref, k_ref, e_ref, vt_ref, w_ref, o_ref, wb_ref, *scratch,
                 tk, tn, heads, side_blocks):
    h = pl.program_id(0)

    @pl.when(h * pl.num_programs(1) + pl.program_id(1) < side_blocks)
    def _():
        wb_ref[...] = w_ref[...].astype(wb_ref.dtype)

    blocks = [Q_PER_STEP * pl.program_id(1) + sub for sub in range(Q_PER_STEP)]
    firsts = [lax.while_loop(lambda p, i=i: (p < i) & (cb_ref[h, 2 * p + 2] - cb_ref[h, 2 * i]
                                                        > thr_ref[h, i]),
                             lambda p: p + 1, jnp.int32(0)) for i in blocks]
    for sub, (i, first) in enumerate(zip(blocks, firsts)):
        _attn_block(cb_ref, q_ref.at[sub], k_ref, e_ref, vt_ref,
                    o_ref.at[:, sub * q_ref.shape[2]:(sub + 1) * q_ref.shape[2]],
                    *scratch[3 * sub:3 * sub + 3], h=h, i=i, first=first, tk=tk, tn=tn, heads=heads)


def _attn_block(cb_ref, q_ref, k_ref, e_ref, vt_ref, o_ref, s0_ref, s1_ref, acc_ref, *,
                h, i, first, tk, tn, heads):
    dh, tq = q_ref.shape
    nt = tq // tn
    half = nt // 2
    r = lax.broadcasted_iota(jnp.int32, (dh, tq), 0) - h
    ones_rows = sum(jnp.where(r == p * heads, 1.0, 0.0) for p in range(BIAS_PIECES))
    qt_aug = jnp.concatenate([q_ref[...], ones_rows.astype(BF16)], axis=0)
    ones_v = jnp.ones((acc_ref.shape[0] - dh, tk), BF16)
    c_q = cb_ref[h, 2 * i]
    acc_ref[...] = jnp.zeros_like(acc_ref)

    def logits_to(buf_ref, j, first_tile=0):
        rows = pl.ds(pl.multiple_of(j * tk, tk), tk)
        k_aug = jnp.concatenate([k_ref[rows, :], e_ref[rows, :]], axis=1)
        cms = []
        for n in range(first_tile, nt):
            cols = slice(n * tn, (n + 1) * tn)
            s = jnp.dot(k_aug, qt_aug[:, cols], preferred_element_type=F32)
            buf_ref[:, cols] = s
            cms.append(jnp.max(s, axis=0, keepdims=True))
        return jnp.concatenate(cms, axis=1)

    def causal(buf_ref, n, first_tile):
        key = lax.broadcasted_iota(jnp.int32, (tk, tn), 0)
        qry = lax.broadcasted_iota(jnp.int32, (tk, tn), 1) + (n - first_tile) * tn
        return jnp.where(key <= qry, buf_ref[:, n * tn:(n + 1) * tn], -jnp.inf)

    def softmax_pv(buf_ref, j, cmax, m, first_tile=0, causal_tiles=0):
        lo = first_tile * tn
        c = c_q - cb_ref[h, j]
        m_old = m[:, lo:]
        m_new = jnp.maximum(m_old, cmax + c)
        a = jnp.exp2(m_old - m_new)
        off = m_new - c
        v_aug = jnp.concatenate([vt_ref[j], ones_v], axis=0)
        for n in range(first_tile, nt):
            cols = slice(n * tn, (n + 1) * tn)
            rel = slice(n * tn - lo, (n + 1) * tn - lo)
            s = causal(buf_ref, n, first_tile) if n < first_tile + causal_tiles else buf_ref[:, cols]
            p = jnp.exp2(s - off[:, rel]).astype(BF16)
            acc_ref[:, cols] = a[:, rel] * acc_ref[:, cols] + jnp.dot(
                v_aug, p, preferred_element_type=F32)
        if first_tile:
            m_new = jnp.concatenate([m[:, :lo], m_new], axis=1)
        return m_new

    def causal_max(buf_ref, first_tile):
        return [jnp.max(causal(buf_ref, n, first_tile), axis=0, keepdims=True)
                for n in range(first_tile, first_tile + half)]

    def pair(p, carry):
        m, cm0 = carry
        cm1 = logits_to(s1_ref, 2 * p + 1)
        m = softmax_pv(s0_ref, 2 * p, cm0, m)
        cm0 = logits_to(s0_ref, 2 * p + 2)
        m = softmax_pv(s1_ref, 2 * p + 1, cm1, m)
        return m, cm0

    def two_pairs(q, carry):
        p = start + 2 * q
        return pair(p + 1, pair(p, carry))

    start = first + (i - first) % 2
    carry = (jnp.full((1, tq), -jnp.inf, F32), logits_to(s0_ref, 2 * first))
    carry = lax.fori_loop(first, start, pair, carry)
    m, cm0 = lax.fori_loop(0, (i - start) // 2, two_pairs, carry)
    logits_to(s1_ref, 2 * i + 1, first_tile=half)
    cm = jnp.concatenate(causal_max(s0_ref, 0) + [cm0[:, half * tn:]], axis=1)
    m = softmax_pv(s0_ref, 2 * i, cm, m, causal_tiles=half)
    cm = jnp.concatenate(causal_max(s1_ref, half), axis=1)
    softmax_pv(s1_ref, 2 * i + 1, cm, m, first_tile=half, causal_tiles=half)
    o_ref[...] = (acc_ref[:dh, :] / acc_ref[dh:dh + 1, :]).astype(o_ref.dtype)


def _attention(cb, thr, qt, k, e, vt, w_side, *, heads, tk, tn=256, side_blocks=32):
    s = k.shape[0]
    dh = FOX_HEAD_DIM
    tq = 2 * tk
    steps = s // (Q_PER_STEP * tq)
    side_cols = w_side.shape[1] // 2
    side_rows = 2 * w_side.shape[0] // side_blocks
    assert side_rows * side_blocks == 2 * w_side.shape[0] and side_rows % (2 * SUBLANES) == 0
    assert side_cols % LANES == 0 and side_blocks <= heads * steps

    def side(h, i, *_):
        blk = jnp.minimum(h * steps + i, side_blocks - 1)
        return blk // 2, blk % 2

    grid_spec = pltpu.PrefetchScalarGridSpec(
        num_scalar_prefetch=2,
        grid=(heads, steps),
        in_specs=[pl.BlockSpec((Q_PER_STEP, dh, tq), lambda h, i, *_: (i, h, 0)),
                  pl.BlockSpec((s, dh), lambda h, i, *_: (0, h)),
                  pl.BlockSpec((s, LANES), lambda h, i, *_: (0, 0)),
                  pl.BlockSpec((s // tk, dh, tk), lambda h, i, *_: (0, h, 0)),
                  pl.BlockSpec((side_rows, side_cols), side)],
        out_specs=[pl.BlockSpec((dh, Q_PER_STEP * tq), lambda h, i, *_: (h, i)),
                   pl.BlockSpec((side_rows, side_cols), side)],
        scratch_shapes=Q_PER_STEP * [pltpu.VMEM((tk, tq), F32), pltpu.VMEM((tk, tq), F32),
                                     pltpu.VMEM((dh + ONES_ROWS, tq), F32)],
    )
    return pl.pallas_call(
        functools.partial(_attn_kernel, tk=tk, tn=tn, heads=heads, side_blocks=side_blocks),
        out_shape=[jax.ShapeDtypeStruct((heads * dh, s), BF16),
                   jax.ShapeDtypeStruct(w_side.shape, BF16)],
        grid_spec=grid_spec,
        compiler_params=_params(("arbitrary", "arbitrary"),
                                3 * _nbytes((s, dh), BF16) + 2 * Q_PER_STEP * _nbytes((tq, dh), BF16)
                                + _nbytes((side_rows, side_cols), F32)
                                + _nbytes((side_rows, side_cols), BF16),
                                Q_PER_STEP * (2 * _nbytes((tk, tq), F32) + _nbytes((dh + ONES_ROWS, tq), F32))
                                + 3 * _nbytes((tk, tq), F32)),
        name="fox_attention",
    )(cb, thr, qt, k, e, vt, w_side)


def _mix_kernel(a_ref, wa_ref, halo_ref, p_ref, wp_ref, ps_ref, ga_ref, gp_ref, o_ref,
                wa_b, wp_b, *, tm):
    j = pl.program_id(0)
    i = pl.program_id(1)

    @pl.when(i == 0)
    def _():
        wa_b[...] = wa_ref[...].astype(BF16)
        wp_b[...] = wp_ref[...].astype(BF16)

    halo = jnp.where(i == 0, 0.0, halo_ref[...])
    ext = jnp.concatenate([halo, p_ref[...]], axis=0)
    acc = ext
    win = ext
    for g, w in enumerate(POOL_WINDOWS):
        acc = acc + pltpu.roll(acc, w // 2, 0)
        win = jnp.where(j == g, acc, win)
    t = i * tm + lax.broadcasted_iota(jnp.int32, (tm, 1), 0)
    cnt = jnp.minimum(t + 1, jnp.left_shift(2, j)).astype(F32)
    pooled = (win[POOL_HALO:] / cnt - ext[POOL_HALO:]).astype(BF16)

    a_t = a_ref[...]
    cols = o_ref.shape[1] // MIX_CHUNKS
    for c in range(0, o_ref.shape[1], cols):
        ya = lax.dot_general(a_t, wa_b[:, c:c + cols], _TN, preferred_element_type=F32)
        yp = jnp.dot(pooled, wp_b[:, c:c + cols], preferred_element_type=F32) * ps_ref[:, c:c + cols]
        o_ref[:, c:c + cols] = (ga_ref[:, c:c + cols].astype(F32) * ya
                                + gp_ref[:, c:c + cols].astype(F32) * yp).astype(o_ref.dtype)


def _mix(attn_t, w_a, p, w_pool, pool_scale, gates, *, tm=512):
    fw, s = attn_t.shape
    groups, gd, tn = w_pool.shape
    assert POOL_WINDOWS == tuple(2 << g for g in range(groups))
    d = w_a.shape[1]
    hb = tm // POOL_HALO
    return pl.pallas_call(
        functools.partial(_mix_kernel, tm=tm),
        out_shape=jax.ShapeDtypeStruct((s, d), BF16),
        grid=(groups, s // tm),
        in_specs=[pl.BlockSpec((fw, tm), lambda j, i: (0, i)),
                  pl.BlockSpec((fw, tn), lambda j, i: (0, j)),
                  pl.BlockSpec((POOL_HALO, gd), lambda j, i: (jnp.maximum(i * hb - 1, 0), j)),
                  pl.BlockSpec((tm, gd), lambda j, i: (i, j)),
                  pl.BlockSpec((None, gd, tn), lambda j, i: (j, 0, 0)),
                  pl.BlockSpec((1, tn), lambda j, i: (0, j)),
                  pl.BlockSpec((tm, tn), lambda j, i: (i, j)),
                  pl.BlockSpec((tm, tn), lambda j, i: (i, groups + j))],
        out_specs=pl.BlockSpec((tm, tn), lambda j, i: (i, j)),
        scratch_shapes=[pltpu.VMEM((fw, tn), BF16), pltpu.VMEM((gd, tn), BF16)],
        compiler_params=_params(("parallel", "arbitrary"),
                                _nbytes((tm, fw), BF16) + _nbytes((fw, tn), w_a.dtype)
                                + _nbytes((tm + POOL_HALO, gd), F32) + _nbytes((gd, tn), w_pool.dtype)
                                + 3 * _nbytes((tm, tn), BF16),
                                _nbytes((fw + gd, tn), BF16) + 3 * _nbytes((tm, tn), F32)
                                + 4 * _nbytes((tm + POOL_HALO, gd), F32)),
        name="branch_mix",
    )(attn_t, w_a, p, p, w_pool, pool_scale, gates, gates)


def _resid_mm_kernel(a_ref, w_ref, x_ref, g_ref, o_ref, *, chunks):
    w = _as_bf16(w_ref[...])
    rows = a_ref.shape[0] // chunks
    for r in range(0, a_ref.shape[0], rows):
        m = jnp.dot(a_ref[r:r + rows, :], w, preferred_element_type=F32)
        o_ref[r:r + rows, :] = ALPHA * x_ref[r:r + rows, :] + g_ref[...] * m


def _resid_matmul(a, w, x, mod, gate_idx, *, tm, tn, chunks=1, name):
    m, k = a.shape
    n = w.shape[1]
    nb = n // tn
    return pl.pallas_call(
        functools.partial(_resid_mm_kernel, chunks=chunks),
        out_shape=jax.ShapeDtypeStruct((m, n), F32),
        grid=(m // tm, nb),
        in_specs=[pl.BlockSpec((tm, k), lambda i, j: (i, 0)),
                  pl.BlockSpec((k, tn), lambda i, j: (0, j)),
                  pl.BlockSpec((tm, tn), lambda i, j: (i, j)),
                  pl.BlockSpec((1, tn), lambda i, j: (0, gate_idx * nb + j))],
        out_specs=pl.BlockSpec((tm, tn), lambda i, j: (i, j)),
        compiler_params=_params(("parallel", "parallel"),
                                _nbytes((tm, k), BF16) + _nbytes((k, tn), w.dtype)
                                + 2 * _nbytes((tm, tn), F32),
                                _cast_bytes((k, tn), w.dtype) + 2 * _nbytes((tm, tn), F32) // chunks),
        name=name,
    )(a, w, x, mod)


def _ln_ln_mod_kernel(r_ref, g_ref, b_ref, sh_ref, sc_ref, x_ref, u_ref):
    x1 = _ln_rows(r_ref[...]) * g_ref[...] + b_ref[...]
    x_ref[...] = x1
    u_ref[...] = (_ln_rows(x1) * (1.0 + sc_ref[...]) + sh_ref[...]).astype(u_ref.dtype)


def _ln_ln_mod(r, gain, bias, mod, shift_idx, scale_idx, *, tm=256):
    s, d = r.shape
    row = pl.BlockSpec((tm, d), lambda i: (i, 0))
    vec = pl.BlockSpec((1, d), lambda i: (0, 0))
    return pl.pallas_call(
        _ln_ln_mod_kernel,
        out_shape=(jax.ShapeDtypeStruct((s, d), F32), jax.ShapeDtypeStruct((s, d), BF16)),
        grid=(s // tm,),
        in_specs=[row, vec, vec,
                  pl.BlockSpec((1, d), lambda i: (0, shift_idx)),
                  pl.BlockSpec((1, d), lambda i: (0, scale_idx))],
        out_specs=(row, row),
        compiler_params=_params(("parallel",),
                                2 * _nbytes((tm, d), F32) + _nbytes((tm, d), BF16),
                                4 * _nbytes((tm, d), F32)),
        name="ln1_ln_mod",
    )(r, gain, bias, mod, mod)


def _ln_affine_kernel(r_ref, g_ref, b_ref, o_ref):
    o_ref[...] = _ln_rows(r_ref[...]) * g_ref[...] + b_ref[...]


def _ln_affine(r, gain, bias, *, tm=512):
    s, d = r.shape
    row = pl.BlockSpec((tm, d), lambda i: (i, 0))
    vec = pl.BlockSpec((1, d), lambda i: (0, 0))
    return pl.pallas_call(
        _ln_affine_kernel,
        out_shape=jax.ShapeDtypeStruct((s, d), F32),
        grid=(s // tm,),
        in_specs=[row, vec, vec],
        out_specs=row,
        compiler_params=_params(("parallel",), 2 * _nbytes((tm, d), F32),
                                3 * _nbytes((tm, d), F32)),
        name="ln2",
    )(r, gain, bias)


def _ffn_up_kernel(u_ref, wg_ref, wu_ref, o_ref):
    wg = _as_bf16(wg_ref[...])
    wu = _as_bf16(wu_ref[...])
    rows = u_ref.shape[0] // FFN_CHUNKS
    for r in range(0, u_ref.shape[0], rows):
        u = u_ref[r:r + rows, :]
        g = jnp.dot(u, wg, preferred_element_type=F32)
        up = jnp.dot(u, wu, preferred_element_type=F32)
        o_ref[r:r + rows, :] = (g * jax.nn.sigmoid(g) * up).astype(o_ref.dtype)


def _ffn_up(u, w_gate_up, *, tm=2048, tn=256):
    s, d = u.shape
    hidden = w_gate_up.shape[1] // 2
    nb = hidden // tn
    return pl.pallas_call(
        _ffn_up_kernel,
        out_shape=jax.ShapeDtypeStruct((s, hidden), BF16),
        grid=(s // tm, nb),
        in_specs=[pl.BlockSpec((tm, d), lambda i, j: (i, 0)),
                  pl.BlockSpec((d, tn), lambda i, j: (0, j)),
                  pl.BlockSpec((d, tn), lambda i, j: (0, nb + j))],
        out_specs=pl.BlockSpec((tm, tn), lambda i, j: (i, j)),
        compiler_params=_params(("parallel", "parallel"),
                                _nbytes((tm, d), BF16) + 2 * _nbytes((d, tn), w_gate_up.dtype)
                                + _nbytes((tm, tn), BF16),
                                2 * _nbytes((d, tn), BF16) + 2 * _nbytes((tm, tn), F32)),
        name="ffn_up",
    )(u, w_gate_up, w_gate_up)


def kernel(x, c, w_ada, b_ada, w_in, b_forget, w_attn_out, w_pool, pool_scale, w_out,
           ln1_g, ln1_b, w_gate_up, w_down, ln2_g, ln2_b):
    batch, seq, d = x.shape
    assert batch == 1 and w_ada.shape[0] == DEPTH == 1
    fox_w = w_attn_out.shape[1]
    heads = fox_w // FOX_HEAD_DIM
    pool_w = w_pool.shape[1] * w_pool.shape[2]
    o_f = 3 * fox_w
    o_p = o_f + heads
    tk = 512

    xs = x[0]
    w_in_t = jnp.swapaxes(w_in, 1, 2)[0]
    w_f = jnp.pad(w_in_t[o_f:o_p], ((0, LANES - heads), (0, 0))).astype(BF16)
    b_f = jnp.pad(b_forget, ((0, 0), (0, LANES - heads)))

    mod = _ada(c.reshape(d, 1), w_ada[0], b_ada)

    u1, e, c_blk = _ln_mod_fcum(xs, mod, 0, 1, w_f, b_f, heads=heads, tm=tk)
    cb = c_blk[:, 0, :heads].T
    tq = 2 * tk
    qt, sq_q = _matmul_tn(u1, w_in_t, 0, fox_w, tm=tq, tn=512, tk=tq, scale=FOX_HEAD_DIM ** -0.5 * LOG2E,
                          group_norms=True, name="proj_qt")
    k, sq_k = _matmul_nt(u1, w_in_t, fox_w, fox_w, BF16, tm=tq, tn=512, group_norms=True, name="proj_k")
    per_head = lambda sq: jnp.sqrt(sq[:, :, 0, :512 // LANES].reshape(seq // tq, heads))
    thr = (SKIP_LOG2 + 2.0 * NORM_SLACK * per_head(sq_q) * jnp.max(per_head(sq_k), axis=0)).T
    vt = _matmul_tn(u1, w_in_t, 2 * fox_w, fox_w, tm=1024, tn=512, tk=tk, name="proj_vt")
    p = _matmul_nt(u1, w_in_t, o_p, pool_w, F32, tm=1024, tn=512, name="proj_pool")
    gates = _matmul_nt(u1, w_in_t, o_p + pool_w, 2 * d, BF16, tm=1024, tn=512, sigmoid=True,
                       name="proj_gates")
    attn, w_down_bf16 = _attention(cb, thr, qt, k, e, vt, w_down[0], heads=heads, tk=tk)
    mix = _mix(attn, w_attn_out[0], p, w_pool[0], pool_scale, gates)
    r1 = _resid_matmul(mix, w_out[0], xs, mod, 2, tm=2048, tn=256, chunks=4, name="out_proj")
    x1, u2 = _ln_ln_mod(r1, ln1_g, ln1_b, mod, 3, 4)

    act = _ffn_up(u2, w_gate_up[0])
    r2 = _resid_matmul(act, w_down_bf16, x1, mod, 5, tm=512, tn=512, name="ffn_down")
    out = _ln_affine(r2, ln2_g, ln2_b)
    return out[None]
```

```python
import functools

import jax
import jax.numpy as jnp
from jax import lax
from jax.experimental import pallas as pl
from jax.experimental.pallas import tpu as pltpu

F32 = jnp.float32
BF16 = jnp.bfloat16

FOX_HEAD_DIM = 128
POOL_WINDOWS = (2, 4, 8, 16)
POOL_HALO = 16
DEPTH = 1
ALPHA = (2 * DEPTH) ** 0.25
LN_EPS = 1e-5
LOG2E = 1.4426950408889634
LANES = 128
SUBLANES = 8
BIAS_PIECES = 3
ONES_ROWS = 16
MIX_CHUNKS = 4
FFN_CHUNKS = 4
Q_PER_STEP = 4
SKIP_LOG2 = 48.0
NORM_SLACK = 1.02

V7X_VMEM_BYTES = 64 * 1024 * 1024
VMEM_CAP_BYTES = V7X_VMEM_BYTES - 6 * 1024 * 1024

_NT = (((1,), (1,)), ((), ()))
_TN = (((0,), (0,)), ((), ()))


def _nbytes(shape, dtype):
    n = jnp.dtype(dtype).itemsize
    for s in shape:
        n *= s
    return n


def _cast_bytes(shape, dtype):
    return 0 if dtype == BF16 else _nbytes(shape, BF16)


def _params(semantics, pipelined_bytes, resident_bytes=0):
    need = 2 * pipelined_bytes + resident_bytes
    assert need <= VMEM_CAP_BYTES, (need, VMEM_CAP_BYTES)
    return pltpu.CompilerParams(dimension_semantics=semantics,
                                vmem_limit_bytes=min(need + (4 << 20), VMEM_CAP_BYTES))


def _ln_rows(x):
    mu = jnp.mean(x, axis=-1, keepdims=True)
    xc = x - mu
    var = jnp.mean(xc * xc, axis=-1, keepdims=True)
    return xc * lax.rsqrt(var + LN_EPS)


def _as_bf16(w):
    return w if w.dtype == BF16 else w.astype(BF16)


def _split3(v):
    hi = v.astype(BF16)
    r1 = v - hi.astype(F32)
    mid = r1.astype(BF16)
    lo = (r1 - mid.astype(F32)).astype(BF16)
    return hi, mid, lo


def _ada_kernel(c_ref, w_ref, b_ref, o_ref):
    k = pl.program_id(1)

    @pl.when(k == 0)
    def _():
        o_ref[...] = b_ref[...]

    cc = c_ref[...]
    s = cc * jax.nn.sigmoid(cc)
    o_ref[...] += jnp.sum(s * w_ref[...], axis=0, keepdims=True)


def _ada(c_col, w_ada, b_ada, *, tk=512, tn=4096):
    d, n = w_ada.shape
    return pl.pallas_call(
        _ada_kernel,
        out_shape=jax.ShapeDtypeStruct((1, n), F32),
        grid=(n // tn, d // tk),
        in_specs=[pl.BlockSpec((tk, 1), lambda j, k: (k, 0)),
                  pl.BlockSpec((tk, tn), lambda j, k: (k, j)),
                  pl.BlockSpec((1, tn), lambda j, k: (0, j))],
        out_specs=pl.BlockSpec((1, tn), lambda j, k: (0, j)),
        compiler_params=_params(("parallel", "arbitrary"),
                                _nbytes((tk, tn), F32) + _nbytes((tk, LANES), F32),
                                _nbytes((tk, tn), F32)),
        name="ada",
    )(c_col, w_ada, b_ada)


def _ln_mod_fcum_kernel(x_ref, sh_ref, sc_ref, wf_ref, b_ref, u_ref, e_ref, c_ref,
                        carry_ref, f_ref, *, tm, rows, heads):
    @pl.when(pl.program_id(0) == 0)
    def _():
        carry_ref[...] = jnp.zeros_like(carry_ref)

    gain = 1.0 + sc_ref[...]
    shift = sh_ref[...]
    for r in range(0, tm, rows):
        u = (_ln_rows(x_ref[r:r + rows, :]) * gain + shift).astype(BF16)
        u_ref[r:r + rows, :] = u
        f_ref[r:r + rows, :] = lax.dot_general(u, wf_ref[...], _NT, preferred_element_type=F32)
    f = f_ref[...] + b_ref[...]
    ls = (jnp.minimum(f, 0.0) - jnp.log1p(jnp.exp(-jnp.abs(f)))) * LOG2E
    row = lax.broadcasted_iota(jnp.int32, (tm, tm), 0)
    col = lax.broadcasted_iota(jnp.int32, (tm, tm), 1)
    tri = jnp.where(col <= row, 1.0, 0.0).astype(BF16)
    cs = sum(jnp.dot(tri, piece, preferred_element_type=F32) for piece in _split3(ls))
    carry = carry_ref[...]
    c_ref[...] = carry
    carry_ref[...] = carry + cs[tm - 1:tm, :]
    head_lane = lax.broadcasted_iota(jnp.int32, (tm, LANES), 1) < heads
    e = sum(pltpu.roll(jnp.where(head_lane, piece.astype(F32), 0.0), p * heads, 1) if p else
            jnp.where(head_lane, piece.astype(F32), 0.0)
            for p, piece in enumerate(_split3(-cs)))
    e_ref[...] = e.astype(e_ref.dtype)


def _ln_mod_fcum(x, mod, shift_idx, scale_idx, wf, b_row, *, heads, tm, rows=128):
    s, d = x.shape
    nb = s // tm
    assert BIAS_PIECES * heads <= LANES
    vec = lambda idx: pl.BlockSpec((1, d), lambda i: (0, idx))
    const = lambda shape: pl.BlockSpec(shape, lambda i: (0, 0))
    return pl.pallas_call(
        functools.partial(_ln_mod_fcum_kernel, tm=tm, rows=rows, heads=heads),
        out_shape=(jax.ShapeDtypeStruct((s, d), BF16),
                   jax.ShapeDtypeStruct((s, LANES), BF16),
                   jax.ShapeDtypeStruct((nb, 1, LANES), F32)),
        grid=(nb,),
        in_specs=[pl.BlockSpec((tm, d), lambda i: (i, 0)), vec(shift_idx), vec(scale_idx),
                  const((LANES, d)), const((1, LANES))],
        out_specs=(pl.BlockSpec((tm, d), lambda i: (i, 0)),
                   pl.BlockSpec((tm, LANES), lambda i: (i, 0)),
                   pl.BlockSpec((None, 1, LANES), lambda i: (i, 0, 0))),
        scratch_shapes=[pltpu.VMEM((1, LANES), F32), pltpu.VMEM((tm, LANES), F32)],
        compiler_params=_params(("arbitrary",),
                                _nbytes((tm, d), F32) + _nbytes((tm, d), BF16)
                                + _nbytes((tm, LANES), BF16) + _nbytes((d, LANES), BF16),
                                4 * _nbytes((rows, d), F32) + 2 * _nbytes((tm, tm), F32)),
        name="ln_mod_fcum",
    )(x, mod, mod, wf, b_row)


def _wt_spec(tn, k, row0):
    assert row0 % SUBLANES == 0 and tn % SUBLANES == 0
    return pl.BlockSpec((pl.Element(tn), pl.Element(k)),
                        lambda i, j: ((row0 // SUBLANES + j * (tn // SUBLANES)) * SUBLANES, 0))


def _mm_nt_kernel(a_ref, w_ref, o_ref, *rest, sigmoid):
    w = w_ref[...].T.astype(BF16)
    tm = a_ref.shape[0]
    chunk = tm // 4 if sigmoid or rest else tm
    lane = lax.broadcasted_iota(jnp.int32, (1, LANES), 1)
    nrm = jnp.zeros((1, LANES), F32)
    for r in range(0, tm, chunk):
        acc = jnp.dot(a_ref[r:r + chunk, :], w, preferred_element_type=F32)
        if sigmoid:
            acc = 0.5 * jnp.tanh(0.5 * acc) + 0.5
        o_ref[r:r + chunk, :] = acc.astype(o_ref.dtype)
        if rest:
            for g in range(acc.shape[1] // LANES):
                sq = jnp.sum(jnp.square(acc[:, g * LANES:(g + 1) * LANES]), axis=1, keepdims=True)
                nrm = jnp.maximum(nrm, jnp.where(lane == g, jnp.max(sq, axis=0, keepdims=True), 0.0))
    if rest:
        rest[0][...] = nrm


def _matmul_nt(a, wt, row0, n, out_dtype, *, tm, tn, sigmoid=False, group_norms=False, name):
    m, k = a.shape
    assert not (group_norms and sigmoid)
    out_shape = [jax.ShapeDtypeStruct((m, n), out_dtype)]
    out_specs = [pl.BlockSpec((tm, tn), lambda i, j: (i, j))]
    if group_norms:
        out_shape.append(jax.ShapeDtypeStruct((m // tm, n // tn, 1, LANES), F32))
        out_specs.append(pl.BlockSpec((None, None, 1, LANES), lambda i, j: (i, j, 0, 0)))
    out = pl.pallas_call(
        functools.partial(_mm_nt_kernel, sigmoid=sigmoid),
        out_shape=out_shape,
        grid=(m // tm, n // tn),
        in_specs=[pl.BlockSpec((tm, k), lambda i, j: (i, 0)), _wt_spec(tn, k, row0)],
        out_specs=out_specs,
        compiler_params=_params(("parallel", "parallel"),
                                _nbytes((tm, k), a.dtype) + _nbytes((tn, k), wt.dtype)
                                + _nbytes((tm, tn), out_dtype),
                                _cast_bytes((tn, k), wt.dtype) + 3 * _nbytes((tm, tn), F32)),
        name=name,
    )(a, wt)
    return out if group_norms else out[0]


def _mm_tn_kernel(a_ref, w_ref, o_ref, *rest, scale):
    acc_t = lax.dot_general(_as_bf16(w_ref[...]), a_ref[...], _NT,
                            preferred_element_type=F32)
    if scale is not None:
        acc_t = acc_t * scale
    tk = o_ref.shape[2]
    for kb in range(o_ref.shape[0]):
        o_ref[kb] = acc_t[:, kb * tk:(kb + 1) * tk].astype(o_ref.dtype)
    if rest:
        lane = lax.broadcasted_iota(jnp.int32, (1, LANES), 1)
        nrm = jnp.zeros((1, LANES), F32)
        for g in range(acc_t.shape[0] // LANES):
            sq = jnp.sum(jnp.square(acc_t[g * LANES:(g + 1) * LANES, :]), axis=0, keepdims=True)
            nrm = jnp.where(lane == g, jnp.max(sq, axis=1, keepdims=True), nrm)
        rest[0][...] = nrm


def _matmul_tn(a, wt, row0, n, *, tm, tn, tk, scale=None, group_norms=False, name):
    m, k = a.shape
    out_shape = [jax.ShapeDtypeStruct((m // tk, n, tk), BF16)]
    out_specs = [pl.BlockSpec((tm // tk, tn, tk), lambda i, j: (i, j, 0))]
    if group_norms:
        out_shape.append(jax.ShapeDtypeStruct((m // tm, n // tn, 1, LANES), F32))
        out_specs.append(pl.BlockSpec((None, None, 1, LANES), lambda i, j: (i, j, 0, 0)))
    out = pl.pallas_call(
        functools.partial(_mm_tn_kernel, scale=scale),
        out_shape=out_shape,
        grid=(m // tm, n // tn),
        in_specs=[pl.BlockSpec((tm, k), lambda i, j: (i, 0)), _wt_spec(tn, k, row0)],
        out_specs=out_specs,
        compiler_params=_params(("parallel", "parallel"),
                                _nbytes((tm, k), a.dtype) + _nbytes((tn, k), wt.dtype)
                                + _nbytes((tm, tn), BF16),
                                _cast_bytes((tn, k), wt.dtype) + 3 * _nbytes((tm, tn), F32)),
        name=name,
    )(a, wt)
    return out if group_norms else out[0]


def _attn_kernel(cb_ref, thr_ref, q_ref, k_ref, e_ref, vt_ref, w_ref, o_ref, wb_ref, *scratch,
                 tk, tn, heads, side_blocks):
    h = pl.program_id(0)

    @pl.when(h * pl.num_programs(1) + pl.program_id(1) < side_blocks)
    def _():
        wb_ref[...] = w_ref[...].astype(wb_ref.dtype)

    blocks = [Q_PER_STEP * pl.program_id(1) + sub for sub in range(Q_PER_STEP)]
    firsts = [lax.while_loop(lambda p, i=i: (p < i) & (cb_ref[h, 2 * p + 2] - cb_ref[h, 2 * i]
                                                        > thr_ref[h, i]),
                             lambda p: p + 1, jnp.int32(0)) for i in blocks]
    for sub, (i, first) in enumerate(zip(blocks, firsts)):
        _attn_block(cb_ref, q_ref.at[sub], k_ref, e_ref, vt_ref,
                    o_ref.at[:, sub * q_ref.shape[2]:(sub + 1) * q_ref.shape[2]],
                    *scratch[3 * sub:3 * sub + 3], h=h, i=i, first=first, tk=tk, tn=tn, heads=heads)


def _attn_block(cb_ref, q_ref, k_ref, e_ref, vt_ref, o_ref, s0_ref, s1_ref, acc_ref, *,
                h, i, first, tk, tn, heads):
    dh, tq = q_ref.shape
    nt = tq // tn
    half = nt // 2
    r = lax.broadcasted_iota(jnp.int32, (dh, tq), 0) - h
    ones_rows = sum(jnp.where(r == p * heads, 1.0, 0.0) for p in range(BIAS_PIECES))
    qt_aug = jnp.concatenate([q_ref[...], ones_rows.astype(BF16)], axis=0)
    ones_v = jnp.ones((acc_ref.shape[0] - dh, tk), BF16)
    c_q = cb_ref[h, 2 * i]
    acc_ref[...] = jnp.zeros_like(acc_ref)

    def logits_to(buf_ref, j, first_tile=0):
        rows = pl.ds(pl.multiple_of(j * tk, tk), tk)
        k_aug = jnp.concatenate([k_ref[rows, :], e_ref[rows, :]], axis=1)
        cms = []
        for n in range(first_tile, nt):
            cols = slice(n * tn, (n + 1) * tn)
            s = jnp.dot(k_aug, qt_aug[:, cols], preferred_element_type=F32)
            buf_ref[:, cols] = s
            cms.append(jnp.max(s, axis=0, keepdims=True))
        return jnp.concatenate(cms, axis=1)

    def causal(buf_ref, n, first_tile):
        key = lax.broadcasted_iota(jnp.int32, (tk, tn), 0)
        qry = lax.broadcasted_iota(jnp.int32, (tk, tn), 1) + (n - first_tile) * tn
        return jnp.where(key <= qry, buf_ref[:, n * tn:(n + 1) * tn], -jnp.inf)

    def softmax_pv(buf_ref, j, cmax, m, first_tile=0, causal_tiles=0):
        lo = first_tile * tn
        c = c_q - cb_ref[h, j]
        m_old = m[:, lo:]
        m_new = jnp.maximum(m_old, cmax + c)
        a = jnp.exp2(m_old - m_new)
        off = m_new - c
        v_aug = jnp.concatenate([vt_ref[j], ones_v], axis=0)
        for n in range(first_tile, nt):
            cols = slice(n * tn, (n + 1) * tn)
            rel = slice(n * tn - lo, (n + 1) * tn - lo)
            s = causal(buf_ref, n, first_tile) if n < first_tile + causal_tiles else buf_ref[:, cols]
            p = jnp.exp2(s - off[:, rel]).astype(BF16)
            acc_ref[:, cols] = a[:, rel] * acc_ref[:, cols] + jnp.dot(
                v_aug, p, preferred_element_type=F32)
        if first_tile:
            m_new = jnp.concatenate([m[:, :lo], m_new], axis=1)
        return m_new

    def causal_max(buf_ref, first_tile):
        return [jnp.max(causal(buf_ref, n, first_tile), axis=0, keepdims=True)
                for n in range(first_tile, first_tile + half)]

    def pair(p, carry):
        m, cm0 = carry
        cm1 = logits_to(s1_ref, 2 * p + 1)
        m = softmax_pv(s0_ref, 2 * p, cm0, m)
        cm0 = logits_to(s0_ref, 2 * p + 2)
        m = softmax_pv(s1_ref, 2 * p + 1, cm1, m)
        return m, cm0

    def two_pairs(q, carry):
        p = start + 2 * q
        return pair(p + 1, pair(p, carry))

    start = first + (i - first) % 2
    carry = (jnp.full((1, tq), -jnp.inf, F32), logits_to(s0_ref, 2 * first))
    carry = lax.fori_loop(first, start, pair, carry)
    m, cm0 = lax.fori_loop(0, (i - start) // 2, two_pairs, carry)
    logits_to(s1_ref, 2 * i + 1, first_tile=half)
    cm = jnp.concatenate(causal_max(s0_ref, 0) + [cm0[:, half * tn:]], axis=1)
    m = softmax_pv(s0_ref, 2 * i, cm, m, causal_tiles=half)
    cm = jnp.concatenate(causal_max(s1_ref, half), axis=1)
    softmax_pv(s1_ref, 2 * i + 1, cm, m, first_tile=half, causal_tiles=half)
    o_ref[...] = (acc_ref[:dh, :] / acc_ref[dh:dh + 1, :]).astype(o_ref.dtype)


def _attention(cb, thr, qt, k, e, vt, w_side, *, heads, tk, tn=256, side_blocks=32):
    s = k.shape[0]
    dh = FOX_HEAD_DIM
    tq = 2 * tk
    steps = s // (Q_PER_STEP * tq)
    side_cols = w_side.shape[1] // 2
    side_rows = 2 * w_side.shape[0] // side_blocks
    assert side_rows * side_blocks == 2 * w_side.shape[0] and side_rows % (2 * SUBLANES) == 0
    assert side_cols % LANES == 0 and side_blocks <= heads * steps

    def side(h, i, *_):
        blk = jnp.minimum(h * steps + i, side_blocks - 1)
        return blk // 2, blk % 2

    grid_spec = pltpu.PrefetchScalarGridSpec(
        num_scalar_prefetch=2,
        grid=(heads, steps),
        in_specs=[pl.BlockSpec((Q_PER_STEP, dh, tq), lambda h, i, *_: (i, h, 0)),
                  pl.BlockSpec((s, dh), lambda h, i, *_: (0, h)),
                  pl.BlockSpec((s, LANES), lambda h, i, *_: (0, 0)),
                  pl.BlockSpec((s // tk, dh, tk), lambda h, i, *_: (0, h, 0)),
                  pl.BlockSpec((side_rows, side_cols), side)],
        out_specs=[pl.BlockSpec((dh, Q_PER_STEP * tq), lambda h, i, *_: (h, i)),
                   pl.BlockSpec((side_rows, side_cols), side)],
        scratch_shapes=Q_PER_STEP * [pltpu.VMEM((tk, tq), F32), pltpu.VMEM((tk, tq), F32),
                                     pltpu.VMEM((dh + ONES_ROWS, tq), F32)],
    )
    return pl.pallas_call(
        functools.partial(_attn_kernel, tk=tk, tn=tn, heads=heads, side_blocks=side_blocks),
        out_shape=[jax.ShapeDtypeStruct((heads * dh, s), BF16),
                   jax.ShapeDtypeStruct(w_side.shape, BF16)],
        grid_spec=grid_spec,
        compiler_params=_params(("arbitrary", "arbitrary"),
                                3 * _nbytes((s, dh), BF16) + 2 * Q_PER_STEP * _nbytes((tq, dh), BF16)
                                + _nbytes((side_rows, side_cols), F32)
                                + _nbytes((side_rows, side_cols), BF16),
                                Q_PER_STEP * (2 * _nbytes((tk, tq), F32) + _nbytes((dh + ONES_ROWS, tq), F32))
                                + 3 * _nbytes((tk, tq), F32)),
        name="fox_attention",
    )(cb, thr, qt, k, e, vt, w_side)


def _mix_kernel(a_ref, wa_ref, halo_ref, p_ref, wp_ref, ps_ref, ga_ref, gp_ref, o_ref,
                wa_b, wp_b, *, tm):
    j = pl.program_id(0)
    i = pl.program_id(1)

    @pl.when(i == 0)
    def _():
        wa_b[...] = wa_ref[...].astype(BF16)
        wp_b[...] = wp_ref[...].astype(BF16)

    halo = jnp.where(i == 0, 0.0, halo_ref[...])
    ext = jnp.concatenate([halo, p_ref[...]], axis=0)
    acc = ext
    win = ext
    for g, w in enumerate(POOL_WINDOWS):
        acc = acc + pltpu.roll(acc, w // 2, 0)
        win = jnp.where(j == g, acc, win)
    t = i * tm + lax.broadcasted_iota(jnp.int32, (tm, 1), 0)
    cnt = jnp.minimum(t + 1, jnp.left_shift(2, j)).astype(F32)
    pooled = (win[POOL_HALO:] / cnt - ext[POOL_HALO:]).astype(BF16)

    a_t = a_ref[...]
    cols = o_ref.shape[1] // MIX_CHUNKS
    for c in range(0, o_ref.shape[1], cols):
        ya = lax.dot_general(a_t, wa_b[:, c:c + cols], _TN, preferred_element_type=F32)
        yp = jnp.dot(pooled, wp_b[:, c:c + cols], preferred_element_type=F32) * ps_ref[:, c:c + cols]
        o_ref[:, c:c + cols] = (ga_ref[:, c:c + cols].astype(F32) * ya
                                + gp_ref[:, c:c + cols].astype(F32) * yp).astype(o_ref.dtype)


def _mix(attn_t, w_a, p, w_pool, pool_scale, gates, *, tm=512):
    fw, s = attn_t.shape
    groups, gd, tn = w_pool.shape
    assert POOL_WINDOWS == tuple(2 << g for g in range(groups))
    d = w_a.shape[1]
    hb = tm // POOL_HALO
    return pl.pallas_call(
        functools.partial(_mix_kernel, tm=tm),
        out_shape=jax.ShapeDtypeStruct((s, d), BF16),
        grid=(groups, s // tm),
        in_specs=[pl.BlockSpec((fw, tm), lambda j, i: (0, i)),
                  pl.BlockSpec((fw, tn), lambda j, i: (0, j)),
                  pl.BlockSpec((POOL_HALO, gd), lambda j, i: (jnp.maximum(i * hb - 1, 0), j)),
                  pl.BlockSpec((tm, gd), lambda j, i: (i, j)),
                  pl.BlockSpec((None, gd, tn), lambda j, i: (j, 0, 0)),
                  pl.BlockSpec((1, tn), lambda j, i: (0, j)),
                  pl.BlockSpec((tm, tn), lambda j, i: (i, j)),
                  pl.BlockSpec((tm, tn), lambda j, i: (i, groups + j))],
        out_specs=pl.BlockSpec((tm, tn), lambda j, i: (i, j)),
        scratch_shapes=[pltpu.VMEM((fw, tn), BF16), pltpu.VMEM((gd, tn), BF16)],
        compiler_params=_params(("parallel", "arbitrary"),
                                _nbytes((tm, fw), BF16) + _nbytes((fw, tn), w_a.dtype)
                                + _nbytes((tm + POOL_HALO, gd), F32) + _nbytes((gd, tn), w_pool.dtype)
                                + 3 * _nbytes((tm, tn), BF16),
                                _nbytes((fw + gd, tn), BF16) + 3 * _nbytes((tm, tn), F32)
                                + 4 * _nbytes((tm + POOL_HALO, gd), F32)),
        name="branch_mix",
    )(attn_t, w_a, p, p, w_pool, pool_scale, gates, gates)


def _resid_mm_kernel(a_ref, w_ref, x_ref, g_ref, o_ref, *, chunks):
    w = _as_bf16(w_ref[...])
    rows = a_ref.shape[0] // chunks
    for r in range(0, a_ref.shape[0], rows):
        m = jnp.dot(a_ref[r:r + rows, :], w, preferred_element_type=F32)
        o_ref[r:r + rows, :] = ALPHA * x_ref[r:r + rows, :] + g_ref[...] * m


def _resid_matmul(a, w, x, mod, gate_idx, *, tm, tn, chunks=1, name):
    m, k = a.shape
    n = w.shape[1]
    nb = n // tn
    return pl.pallas_call(
        functools.partial(_resid_mm_kernel, chunks=chunks),
        out_shape=jax.ShapeDtypeStruct((m, n), F32),
        grid=(m // tm, nb),
        in_specs=[pl.BlockSpec((tm, k), lambda i, j: (i, 0)),
                  pl.BlockSpec((k, tn), lambda i, j: (0, j)),
                  pl.BlockSpec((tm, tn), lambda i, j: (i, j)),
                  pl.BlockSpec((1, tn), lambda i, j: (0, gate_idx * nb + j))],
        out_specs=pl.BlockSpec((tm, tn), lambda i, j: (i, j)),
        compiler_params=_params(("parallel", "parallel"),
                                _nbytes((tm, k), BF16) + _nbytes((k, tn), w.dtype)
                                + 2 * _nbytes((tm, tn), F32),
                                _cast_bytes((k, tn), w.dtype) + 2 * _nbytes((tm, tn), F32) // chunks),
        name=name,
    )(a, w, x, mod)


def _ln_ln_mod_kernel(r_ref, g_ref, b_ref, sh_ref, sc_ref, x_ref, u_ref):
    x1 = _ln_rows(r_ref[...]) * g_ref[...] + b_ref[...]
    x_ref[...] = x1
    u_ref[...] = (_ln_rows(x1) * (1.0 + sc_ref[...]) + sh_ref[...]).astype(u_ref.dtype)


def _ln_ln_mod(r, gain, bias, mod, shift_idx, scale_idx, *, tm=256):
    s, d = r.shape
    row = pl.BlockSpec((tm, d), lambda i: (i, 0))
    vec = pl.BlockSpec((1, d), lambda i: (0, 0))
    return pl.pallas_call(
        _ln_ln_mod_kernel,
        out_shape=(jax.ShapeDtypeStruct((s, d), F32), jax.ShapeDtypeStruct((s, d), BF16)),
        grid=(s // tm,),
        in_specs=[row, vec, vec,
                  pl.BlockSpec((1, d), lambda i: (0, shift_idx)),
                  pl.BlockSpec((1, d), lambda i: (0, scale_idx))],
        out_specs=(row, row),
        compiler_params=_params(("parallel",),
                                2 * _nbytes((tm, d), F32) + _nbytes((tm, d), BF16),
                                4 * _nbytes((tm, d), F32)),
        name="ln1_ln_mod",
    )(r, gain, bias, mod, mod)


def _ln_affine_kernel(r_ref, g_ref, b_ref, o_ref):
    o_ref[...] = _ln_rows(r_ref[...]) * g_ref[...] + b_ref[...]


def _ln_affine(r, gain, bias, *, tm=512):
    s, d = r.shape
    row = pl.BlockSpec((tm, d), lambda i: (i, 0))
    vec = pl.BlockSpec((1, d), lambda i: (0, 0))
    return pl.pallas_call(
        _ln_affine_kernel,
        out_shape=jax.ShapeDtypeStruct((s, d), F32),
        grid=(s // tm,),
        in_specs=[row, vec, vec],
        out_specs=row,
        compiler_params=_params(("parallel",), 2 * _nbytes((tm, d), F32),
                                3 * _nbytes((tm, d), F32)),
        name="ln2",
    )(r, gain, bias)


def _ffn_up_kernel(u_ref, wg_ref, wu_ref, o_ref):
    wg = _as_bf16(wg_ref[...])
    wu = _as_bf16(wu_ref[...])
    rows = u_ref.shape[0] // FFN_CHUNKS
    for r in range(0, u_ref.shape[0], rows):
        u = u_ref[r:r + rows, :]
        g = jnp.dot(u, wg, preferred_element_type=F32)
        up = jnp.dot(u, wu, preferred_element_type=F32)
        o_ref[r:r + rows, :] = (g * jax.nn.sigmoid(g) * up).astype(o_ref.dtype)


def _ffn_up(u, w_gate_up, *, tm=2048, tn=256):
    s, d = u.shape
    hidden = w_gate_up.shape[1] // 2
    nb = hidden // tn
    return pl.pallas_call(
        _ffn_up_kernel,
        out_shape=jax.ShapeDtypeStruct((s, hidden), BF16),
        grid=(s // tm, nb),
        in_specs=[pl.BlockSpec((tm, d), lambda i, j: (i, 0)),
                  pl.BlockSpec((d, tn), lambda i, j: (0, j)),
                  pl.BlockSpec((d, tn), lambda i, j: (0, nb + j))],
        out_specs=pl.BlockSpec((tm, tn), lambda i, j: (i, j)),
        compiler_params=_params(("parallel", "parallel"),
                                _nbytes((tm, d), BF16) + 2 * _nbytes((d, tn), w_gate_up.dtype)
                                + _nbytes((tm, tn), BF16),
                                2 * _nbytes((d, tn), BF16) + 2 * _nbytes((tm, tn), F32)),
        name="ffn_up",
    )(u, w_gate_up, w_gate_up)


def kernel(x, c, w_ada, b_ada, w_in, b_forget, w_attn_out, w_pool, pool_scale, w_out,
           ln1_g, ln1_b, w_gate_up, w_down, ln2_g, ln2_b):
    batch, seq, d = x.shape
    assert batch == 1 and w_ada.shape[0] == DEPTH == 1
    fox_w = w_attn_out.shape[1]
    heads = fox_w // FOX_HEAD_DIM
    pool_w = w_pool.shape[1] * w_pool.shape[2]
    o_f = 3 * fox_w
    o_p = o_f + heads
    tk = 512

    xs = x[0]
    w_in_t = jnp.swapaxes(w_in, 1, 2)[0]
    w_f = jnp.pad(w_in_t[o_f:o_p], ((0, LANES - heads), (0, 0))).astype(BF16)
    b_f = jnp.pad(b_forget, ((0, 0), (0, LANES - heads)))

    mod = _ada(c.reshape(d, 1), w_ada[0], b_ada)

    u1, e, c_blk = _ln_mod_fcum(xs, mod, 0, 1, w_f, b_f, heads=heads, tm=tk)
    cb = c_blk[:, 0, :heads].T
    tq = 2 * tk
    qt, sq_q = _matmul_tn(u1, w_in_t, 0, fox_w, tm=tq, tn=512, tk=tq, scale=FOX_HEAD_DIM ** -0.5 * LOG2E,
                          group_norms=True, name="proj_qt")
    k, sq_k = _matmul_nt(u1, w_in_t, fox_w, fox_w, BF16, tm=tq, tn=512, group_norms=True, name="proj_k")
    per_head = lambda sq: jnp.sqrt(sq[:, :, 0, :512 // LANES].reshape(seq // tq, heads))
    thr = (SKIP_LOG2 + 2.0 * NORM_SLACK * per_head(sq_q) * jnp.max(per_head(sq_k), axis=0)).T
    vt = _matmul_tn(u1, w_in_t, 2 * fox_w, fox_w, tm=1024, tn=512, tk=tk, name="proj_vt")
    p = _matmul_nt(u1, w_in_t, o_p, pool_w, F32, tm=1024, tn=512, name="proj_pool")
    gates = _matmul_nt(u1, w_in_t, o_p + pool_w, 2 * d, BF16, tm=1024, tn=512, sigmoid=True,
                       name="proj_gates")
    attn, w_down_bf16 = _attention(cb, thr, qt, k, e, vt, w_down[0], heads=heads, tk=tk)
    mix = _mix(attn, w_attn_out[0], p, w_pool[0], pool_scale, gates)
    r1 = _resid_matmul(mix, w_out[0], xs, mod, 2, tm=2048, tn=256, chunks=4, name="out_proj")
    x1, u2 = _ln_ln_mod(r1, ln1_g, ln1_b, mod, 3, 4)

    act = _ffn_up(u2, w_gate_up[0])
    r2 = _resid_matmul(act, w_down_bf16, x1, mod, 5, tm=512, tn=512, name="ffn_down")
    out = _ln_affine(r2, ln2_g, ln2_b)
    return out[None]
```

```python
import functools

import jax
import jax.numpy as jnp
from jax import lax
from jax.experimental import pallas as pl
from jax.experimental.pallas import tpu as pltpu

F32 = jnp.float32
BF16 = jnp.bfloat16

FOX_HEAD_DIM = 128
POOL_WINDOWS = (2, 4, 8, 16)
POOL_HALO = 16
DEPTH = 1
ALPHA = (2 * DEPTH) ** 0.25
LN_EPS = 1e-5
LOG2E = 1.4426950408889634
LANES = 128
SUBLANES = 8
BIAS_PIECES = 3
ONES_ROWS = 16
MM_CHUNKS = 4
MIX_CHUNKS = 4
FFN_CHUNKS = 4
Q_PER_STEP = 4
SKIP_LOG2 = 48.0
NORM_SLACK = 1.02

V7X_VMEM_BYTES = 64 * 1024 * 1024
VMEM_CAP_BYTES = V7X_VMEM_BYTES - 6 * 1024 * 1024

_NT = (((1,), (1,)), ((), ()))
_TN = (((0,), (0,)), ((), ()))


def _nbytes(shape, dtype):
    n = jnp.dtype(dtype).itemsize
    for s in shape:
        n *= s
    return n


def _cast_bytes(shape, dtype):
    return 0 if dtype == BF16 else _nbytes(shape, BF16)


def _params(semantics, pipelined_bytes, resident_bytes=0):
    need = 2 * pipelined_bytes + resident_bytes
    assert need <= VMEM_CAP_BYTES, (need, VMEM_CAP_BYTES)
    return pltpu.CompilerParams(dimension_semantics=semantics,
                                vmem_limit_bytes=min(need + (4 << 20), VMEM_CAP_BYTES))


def _ln_rows(x):
    mu = jnp.mean(x, axis=-1, keepdims=True)
    xc = x - mu
    var = jnp.mean(xc * xc, axis=-1, keepdims=True)
    return xc * lax.rsqrt(var + LN_EPS)


def _as_bf16(w):
    return w if w.dtype == BF16 else w.astype(BF16)


def _split3(v):
    hi = v.astype(BF16)
    r1 = v - hi.astype(F32)
    mid = r1.astype(BF16)
    lo = (r1 - mid.astype(F32)).astype(BF16)
    return hi, mid, lo


def _ada_kernel(c_ref, w_ref, b_ref, o_ref):
    k = pl.program_id(1)

    @pl.when(k == 0)
    def _():
        o_ref[...] = b_ref[...]

    cc = c_ref[...]
    s = cc * jax.nn.sigmoid(cc)
    o_ref[...] += jnp.sum(s * w_ref[...], axis=0, keepdims=True)


def _ada(c_col, w_ada, b_ada, *, tk=512, tn=4096):
    d, n = w_ada.shape
    return pl.pallas_call(
        _ada_kernel,
        out_shape=jax.ShapeDtypeStruct((1, n), F32),
        grid=(n // tn, d // tk),
        in_specs=[pl.BlockSpec((tk, 1), lambda j, k: (k, 0)),
                  pl.BlockSpec((tk, tn), lambda j, k: (k, j)),
                  pl.BlockSpec((1, tn), lambda j, k: (0, j))],
        out_specs=pl.BlockSpec((1, tn), lambda j, k: (0, j)),
        compiler_params=_params(("parallel", "arbitrary"),
                                _nbytes((tk, tn), F32) + _nbytes((tk, LANES), F32),
                                _nbytes((tk, tn), F32)),
        name="ada",
    )(c_col, w_ada, b_ada)


def _ln_mod_fcum_kernel(x_ref, sh_ref, sc_ref, wf_ref, b_ref, u_ref, e_ref, c_ref,
                        carry_ref, f_ref, *, tm, rows, heads):
    @pl.when(pl.program_id(0) == 0)
    def _():
        carry_ref[...] = jnp.zeros_like(carry_ref)

    gain = 1.0 + sc_ref[...]
    shift = sh_ref[...]
    for r in range(0, tm, rows):
        u = (_ln_rows(x_ref[r:r + rows, :]) * gain + shift).astype(BF16)
        u_ref[r:r + rows, :] = u
        f_ref[r:r + rows, :] = lax.dot_general(u, wf_ref[...], _NT, preferred_element_type=F32)
    f = f_ref[...] + b_ref[...]
    ls = (jnp.minimum(f, 0.0) - jnp.log1p(jnp.exp(-jnp.abs(f)))) * LOG2E
    row = lax.broadcasted_iota(jnp.int32, (tm, tm), 0)
    col = lax.broadcasted_iota(jnp.int32, (tm, tm), 1)
    tri = jnp.where(col <= row, 1.0, 0.0).astype(BF16)
    cs = sum(jnp.dot(tri, piece, preferred_element_type=F32) for piece in _split3(ls))
    carry = carry_ref[...]
    c_ref[...] = carry
    carry_ref[...] = carry + cs[tm - 1:tm, :]
    head_lane = lax.broadcasted_iota(jnp.int32, (tm, LANES), 1) < heads
    e = sum(pltpu.roll(jnp.where(head_lane, piece.astype(F32), 0.0), p * heads, 1) if p else
            jnp.where(head_lane, piece.astype(F32), 0.0)
            for p, piece in enumerate(_split3(-cs)))
    e_ref[...] = e.astype(e_ref.dtype)


def _ln_mod_fcum(x, mod, shift_idx, scale_idx, wf, b_row, *, heads, tm, rows=128):
    s, d = x.shape
    nb = s // tm
    assert BIAS_PIECES * heads <= LANES
    vec = lambda idx: pl.BlockSpec((1, d), lambda i: (0, idx))
    const = lambda shape: pl.BlockSpec(shape, lambda i: (0, 0))
    return pl.pallas_call(
        functools.partial(_ln_mod_fcum_kernel, tm=tm, rows=rows, heads=heads),
        out_shape=(jax.ShapeDtypeStruct((s, d), BF16),
                   jax.ShapeDtypeStruct((s, LANES), BF16),
                   jax.ShapeDtypeStruct((nb, 1, LANES), F32)),
        grid=(nb,),
        in_specs=[pl.BlockSpec((tm, d), lambda i: (i, 0)), vec(shift_idx), vec(scale_idx),
                  const((LANES, d)), const((1, LANES))],
        out_specs=(pl.BlockSpec((tm, d), lambda i: (i, 0)),
                   pl.BlockSpec((tm, LANES), lambda i: (i, 0)),
                   pl.BlockSpec((None, 1, LANES), lambda i: (i, 0, 0))),
        scratch_shapes=[pltpu.VMEM((1, LANES), F32), pltpu.VMEM((tm, LANES), F32)],
        compiler_params=_params(("arbitrary",),
                                _nbytes((tm, d), F32) + _nbytes((tm, d), BF16)
                                + _nbytes((tm, LANES), BF16) + _nbytes((d, LANES), BF16),
                                4 * _nbytes((rows, d), F32) + 2 * _nbytes((tm, tm), F32)),
        name="ln_mod_fcum",
    )(x, mod, mod, wf, b_row)


def _wt_spec(tn, k, row0):
    assert row0 % SUBLANES == 0 and tn % SUBLANES == 0
    return pl.BlockSpec((pl.Element(tn), pl.Element(k)),
                        lambda i, j: ((row0 // SUBLANES + j * (tn // SUBLANES)) * SUBLANES, 0))


def _proj_kpg_kernel(a_ref, w_ref, k_ref, nrm_ref, p_ref, g_ref, *, nk, npool):
    j = pl.program_id(1)
    w = _as_bf16(w_ref[...])
    tm = a_ref.shape[0]
    chunk = tm // MM_CHUNKS

    def tiles():
        for r in range(0, tm, chunk):
            yield slice(r, r + chunk), lax.dot_general(a_ref[r:r + chunk, :], w, _NT,
                                                       preferred_element_type=F32)

    @pl.when(j < nk)
    def _():
        lane = lax.broadcasted_iota(jnp.int32, (1, LANES), 1)
        nrm = jnp.zeros((1, LANES), F32)
        for rows, acc in tiles():
            k_ref[rows, :] = acc.astype(k_ref.dtype)
            for g in range(acc.shape[1] // LANES):
                sq = jnp.sum(jnp.square(acc[:, g * LANES:(g + 1) * LANES]), axis=1, keepdims=True)
                nrm = jnp.maximum(nrm, jnp.where(lane == g, jnp.max(sq, axis=0, keepdims=True), 0.0))
        nrm_ref[...] = nrm

    @pl.when((j >= nk) & (j < nk + npool))
    def _():
        for rows, acc in tiles():
            p_ref[rows, :] = acc

    @pl.when(j >= nk + npool)
    def _():
        for rows, acc in tiles():
            g_ref[rows, :] = (0.5 * jnp.tanh(0.5 * acc) + 0.5).astype(g_ref.dtype)


def _proj_kpg(a, wt, k_row0, k_n, pg_row0, pool_n, gate_n, *, tm, tn):
    m, kdim = a.shape
    nk, npool, ngate = k_n // tn, pool_n // tn, gate_n // tn
    assert k_row0 % SUBLANES == 0 and pg_row0 % SUBLANES == 0 and tn % SUBLANES == 0

    def w_rows(i, j):
        blk = jnp.where(j < nk, k_row0 // SUBLANES + j * (tn // SUBLANES),
                        pg_row0 // SUBLANES + (j - nk) * (tn // SUBLANES))
        return blk * SUBLANES, 0

    k_idx = lambda i, j: (i, jnp.minimum(j, nk - 1))
    return pl.pallas_call(
        functools.partial(_proj_kpg_kernel, nk=nk, npool=npool),
        out_shape=[jax.ShapeDtypeStruct((m, k_n), BF16),
                   jax.ShapeDtypeStruct((m // tm, nk, 1, LANES), F32),
                   jax.ShapeDtypeStruct((m, pool_n), F32),
                   jax.ShapeDtypeStruct((m, gate_n), BF16)],
        grid=(m // tm, nk + npool + ngate),
        in_specs=[pl.BlockSpec((tm, kdim), lambda i, j: (i, 0), pipeline_mode=pl.Buffered(1)),
                  pl.BlockSpec((pl.Element(tn), pl.Element(kdim)), w_rows)],
        out_specs=[pl.BlockSpec((tm, tn), k_idx),
                   pl.BlockSpec((None, None, 1, LANES), lambda i, j: (*k_idx(i, j), 0, 0)),
                   pl.BlockSpec((tm, tn), lambda i, j: (i, jnp.clip(j - nk, 0, npool - 1))),
                   pl.BlockSpec((tm, tn), lambda i, j: (i, jnp.maximum(j - nk - npool, 0)))],
        compiler_params=_params(("arbitrary", "arbitrary"),
                                _nbytes((tn, kdim), wt.dtype) + 2 * _nbytes((tm, tn), BF16)
                                + _nbytes((tm, tn), F32),
                                _nbytes((tm, kdim), a.dtype) + _cast_bytes((tn, kdim), wt.dtype)
                                + 3 * _nbytes((tm // MM_CHUNKS, tn), F32)),
        name="proj_kpg",
    )(a, wt)


def _mm_tn_kernel(a_ref, w_ref, o_ref, *rest, scale):
    acc_t = lax.dot_general(_as_bf16(w_ref[...]), a_ref[...], _NT,
                            preferred_element_type=F32)
    if scale is not None:
        acc_t = acc_t * scale
    tk = o_ref.shape[2]
    for kb in range(o_ref.shape[0]):
        o_ref[kb] = acc_t[:, kb * tk:(kb + 1) * tk].astype(o_ref.dtype)
    if rest:
        lane = lax.broadcasted_iota(jnp.int32, (1, LANES), 1)
        nrm = jnp.zeros((1, LANES), F32)
        for g in range(acc_t.shape[0] // LANES):
            sq = jnp.sum(jnp.square(acc_t[g * LANES:(g + 1) * LANES, :]), axis=0, keepdims=True)
            nrm = jnp.where(lane == g, jnp.max(sq, axis=1, keepdims=True), nrm)
        rest[0][...] = nrm


def _matmul_tn(a, wt, row0, n, *, tm, tn, tk, scale=None, group_norms=False, name):
    m, k = a.shape
    out_shape = [jax.ShapeDtypeStruct((m // tk, n, tk), BF16)]
    out_specs = [pl.BlockSpec((tm // tk, tn, tk), lambda i, j: (i, j, 0))]
    if group_norms:
        out_shape.append(jax.ShapeDtypeStruct((m // tm, n // tn, 1, LANES), F32))
        out_specs.append(pl.BlockSpec((None, None, 1, LANES), lambda i, j: (i, j, 0, 0)))
    out = pl.pallas_call(
        functools.partial(_mm_tn_kernel, scale=scale),
        out_shape=out_shape,
        grid=(m // tm, n // tn),
        in_specs=[pl.BlockSpec((tm, k), lambda i, j: (i, 0)), _wt_spec(tn, k, row0)],
        out_specs=out_specs,
        compiler_params=_params(("parallel", "parallel"),
                                _nbytes((tm, k), a.dtype) + _nbytes((tn, k), wt.dtype)
                                + _nbytes((tm, tn), BF16),
                                _cast_bytes((tn, k), wt.dtype) + 3 * _nbytes((tm, tn), F32)),
        name=name,
    )(a, wt)
    return out if group_norms else out[0]


def _attn_kernel(cb_ref, thr_ref, q_ref, k_ref, e_ref, vt_ref, w_ref, o_ref, wb_ref, *scratch,
                 tk, tn, heads, side_blocks):
    h = pl.program_id(0)

    @pl.when(h * pl.num_programs(1) + pl.program_id(1) < side_blocks)
    def _():
        wb_ref[...] = w_ref[...].astype(wb_ref.dtype)

    blocks = [Q_PER_STEP * pl.program_id(1) + sub for sub in range(Q_PER_STEP)]
    firsts = [lax.while_loop(lambda p, i=i: (p < i) & (cb_ref[h, 2 * p + 2] - cb_ref[h, 2 * i]
                                                        > thr_ref[h, i]),
                             lambda p: p + 1, jnp.int32(0)) for i in blocks]
    for sub, (i, first) in enumerate(zip(blocks, firsts)):
        _attn_block(cb_ref, q_ref.at[sub], k_ref, e_ref, vt_ref,
                    o_ref.at[:, sub * q_ref.shape[2]:(sub + 1) * q_ref.shape[2]],
                    *scratch[3 * sub:3 * sub + 3], h=h, i=i, first=first, tk=tk, tn=tn, heads=heads)


def _attn_block(cb_ref, q_ref, k_ref, e_ref, vt_ref, o_ref, s0_ref, s1_ref, acc_ref, *,
                h, i, first, tk, tn, heads):
    dh, tq = q_ref.shape
    nt = tq // tn
    half = nt // 2
    r = lax.broadcasted_iota(jnp.int32, (dh, tq), 0) - h
    ones_rows = sum(jnp.where(r == p * heads, 1.0, 0.0) for p in range(BIAS_PIECES))
    qt_aug = jnp.concatenate([q_ref[...], ones_rows.astype(BF16)], axis=0)
    ones_v = jnp.ones((acc_ref.shape[0] - dh, tk), BF16)
    c_q = cb_ref[h, 2 * i]
    acc_ref[...] = jnp.zeros_like(acc_ref)

    def logits_to(buf_ref, j, first_tile=0):
        rows = pl.ds(pl.multiple_of(j * tk, tk), tk)
        k_aug = jnp.concatenate([k_ref[rows, :], e_ref[rows, :]], axis=1)
        cms = []
        for n in range(first_tile, nt):
            cols = slice(n * tn, (n + 1) * tn)
            s = jnp.dot(k_aug, qt_aug[:, cols], preferred_element_type=F32)
            buf_ref[:, cols] = s
            cms.append(jnp.max(s, axis=0, keepdims=True))
        return jnp.concatenate(cms, axis=1)

    def causal(buf_ref, n, first_tile):
        key = lax.broadcasted_iota(jnp.int32, (tk, tn), 0)
        qry = lax.broadcasted_iota(jnp.int32, (tk, tn), 1) + (n - first_tile) * tn
        return jnp.where(key <= qry, buf_ref[:, n * tn:(n + 1) * tn], -jnp.inf)

    def softmax_pv(buf_ref, j, cmax, m, first_tile=0, causal_tiles=0):
        lo = first_tile * tn
        c = c_q - cb_ref[h, j]
        m_old = m[:, lo:]
        m_new = jnp.maximum(m_old, cmax + c)
        a = jnp.exp2(m_old - m_new)
        off = m_new - c
        v_aug = jnp.concatenate([vt_ref[j], ones_v], axis=0)
        for n in range(first_tile, nt):
            cols = slice(n * tn, (n + 1) * tn)
            rel = slice(n * tn - lo, (n + 1) * tn - lo)
            s = causal(buf_ref, n, first_tile) if n < first_tile + causal_tiles else buf_ref[:, cols]
            p = jnp.exp2(s - off[:, rel]).astype(BF16)
            acc_ref[:, cols] = a[:, rel] * acc_ref[:, cols] + jnp.dot(
                v_aug, p, preferred_element_type=F32)
        if first_tile:
            m_new = jnp.concatenate([m[:, :lo], m_new], axis=1)
        return m_new

    def causal_max(buf_ref, first_tile):
        return [jnp.max(causal(buf_ref, n, first_tile), axis=0, keepdims=True)
                for n in range(first_tile, first_tile + half)]

    def pair(p, carry):
        m, cm0 = carry
        cm1 = logits_to(s1_ref, 2 * p + 1)
        m = softmax_pv(s0_ref, 2 * p, cm0, m)
        cm0 = logits_to(s0_ref, 2 * p + 2)
        m = softmax_pv(s1_ref, 2 * p + 1, cm1, m)
        return m, cm0

    def two_pairs(q, carry):
        p = start + 2 * q
        return pair(p + 1, pair(p, carry))

    start = first + (i - first) % 2
    carry = (jnp.full((1, tq), -jnp.inf, F32), logits_to(s0_ref, 2 * first))
    carry = lax.fori_loop(first, start, pair, carry)
    m, cm0 = lax.fori_loop(0, (i - start) // 2, two_pairs, carry)
    logits_to(s1_ref, 2 * i + 1, first_tile=half)
    cm = jnp.concatenate(causal_max(s0_ref, 0) + [cm0[:, half * tn:]], axis=1)
    m = softmax_pv(s0_ref, 2 * i, cm, m, causal_tiles=half)
    cm = jnp.concatenate(causal_max(s1_ref, half), axis=1)
    softmax_pv(s1_ref, 2 * i + 1, cm, m, first_tile=half, causal_tiles=half)
    o_ref[...] = (acc_ref[:dh, :] / acc_ref[dh:dh + 1, :]).astype(o_ref.dtype)


def _attention(cb, thr, qt, k, e, vt, w_side, *, heads, tk, tn=256, side_blocks=32):
    s = k.shape[0]
    dh = FOX_HEAD_DIM
    tq = 2 * tk
    steps = s // (Q_PER_STEP * tq)
    side_cols = w_side.shape[1] // 2
    side_rows = 2 * w_side.shape[0] // side_blocks
    assert side_rows * side_blocks == 2 * w_side.shape[0] and side_rows % (2 * SUBLANES) == 0
    assert side_cols % LANES == 0 and side_blocks <= heads * steps

    def side(h, i, *_):
        blk = jnp.minimum(h * steps + i, side_blocks - 1)
        return blk // 2, blk % 2

    grid_spec = pltpu.PrefetchScalarGridSpec(
        num_scalar_prefetch=2,
        grid=(heads, steps),
        in_specs=[pl.BlockSpec((Q_PER_STEP, dh, tq), lambda h, i, *_: (i, h, 0)),
                  pl.BlockSpec((s, dh), lambda h, i, *_: (0, h)),
                  pl.BlockSpec((s, LANES), lambda h, i, *_: (0, 0)),
                  pl.BlockSpec((s // tk, dh, tk), lambda h, i, *_: (0, h, 0)),
                  pl.BlockSpec((side_rows, side_cols), side)],
        out_specs=[pl.BlockSpec((dh, Q_PER_STEP * tq), lambda h, i, *_: (h, i)),
                   pl.BlockSpec((side_rows, side_cols), side)],
        scratch_shapes=Q_PER_STEP * [pltpu.VMEM((tk, tq), F32), pltpu.VMEM((tk, tq), F32),
                                     pltpu.VMEM((dh + ONES_ROWS, tq), F32)],
    )
    return pl.pallas_call(
        functools.partial(_attn_kernel, tk=tk, tn=tn, heads=heads, side_blocks=side_blocks),
        out_shape=[jax.ShapeDtypeStruct((heads * dh, s), BF16),
                   jax.ShapeDtypeStruct(w_side.shape, BF16)],
        grid_spec=grid_spec,
        compiler_params=_params(("arbitrary", "arbitrary"),
                                3 * _nbytes((s, dh), BF16) + 2 * Q_PER_STEP * _nbytes((tq, dh), BF16)
                                + _nbytes((side_rows, side_cols), F32)
                                + _nbytes((side_rows, side_cols), BF16),
                                Q_PER_STEP * (2 * _nbytes((tk, tq), F32) + _nbytes((dh + ONES_ROWS, tq), F32))
                                + 3 * _nbytes((tk, tq), F32)),
        name="fox_attention",
    )(cb, thr, qt, k, e, vt, w_side)


def _mix_kernel(a_ref, wa_ref, halo_ref, p_ref, wp_ref, ps_ref, ga_ref, gp_ref, o_ref,
                wa_b, wp_b, *, tm):
    j = pl.program_id(0)
    i = pl.program_id(1)

    @pl.when(i == 0)
    def _():
        wa_b[...] = wa_ref[...].astype(BF16)
        wp_b[...] = wp_ref[...].astype(BF16)

    halo = jnp.where(i == 0, 0.0, halo_ref[...])
    ext = jnp.concatenate([halo, p_ref[...]], axis=0)
    acc = ext
    win = ext
    for g, w in enumerate(POOL_WINDOWS):
        acc = acc + pltpu.roll(acc, w // 2, 0)
        win = jnp.where(j == g, acc, win)
    t = i * tm + lax.broadcasted_iota(jnp.int32, (tm, 1), 0)
    cnt = jnp.minimum(t + 1, jnp.left_shift(2, j)).astype(F32)
    pooled = (win[POOL_HALO:] / cnt - ext[POOL_HALO:]).astype(BF16)

    a_t = a_ref[...]
    cols = o_ref.shape[1] // MIX_CHUNKS
    for c in range(0, o_ref.shape[1], cols):
        ya = lax.dot_general(a_t, wa_b[:, c:c + cols], _TN, preferred_element_type=F32)
        yp = jnp.dot(pooled, wp_b[:, c:c + cols], preferred_element_type=F32) * ps_ref[:, c:c + cols]
        o_ref[:, c:c + cols] = (ga_ref[:, c:c + cols].astype(F32) * ya
                                + gp_ref[:, c:c + cols].astype(F32) * yp).astype(o_ref.dtype)


def _mix(attn_t, w_a, p, w_pool, pool_scale, gates, *, tm=512):
    fw, s = attn_t.shape
    groups, gd, tn = w_pool.shape
    assert POOL_WINDOWS == tuple(2 << g for g in range(groups))
    d = w_a.shape[1]
    hb = tm // POOL_HALO
    return pl.pallas_call(
        functools.partial(_mix_kernel, tm=tm),
        out_shape=jax.ShapeDtypeStruct((s, d), BF16),
        grid=(groups, s // tm),
        in_specs=[pl.BlockSpec((fw, tm), lambda j, i: (0, i)),
                  pl.BlockSpec((fw, tn), lambda j, i: (0, j)),
                  pl.BlockSpec((POOL_HALO, gd), lambda j, i: (jnp.maximum(i * hb - 1, 0), j)),
                  pl.BlockSpec((tm, gd), lambda j, i: (i, j)),
                  pl.BlockSpec((None, gd, tn), lambda j, i: (j, 0, 0)),
                  pl.BlockSpec((1, tn), lambda j, i: (0, j)),
                  pl.BlockSpec((tm, tn), lambda j, i: (i, j)),
                  pl.BlockSpec((tm, tn), lambda j, i: (i, groups + j))],
        out_specs=pl.BlockSpec((tm, tn), lambda j, i: (i, j)),
        scratch_shapes=[pltpu.VMEM((fw, tn), BF16), pltpu.VMEM((gd, tn), BF16)],
        compiler_params=_params(("parallel", "arbitrary"),
                                _nbytes((tm, fw), BF16) + _nbytes((fw, tn), w_a.dtype)
                                + _nbytes((tm + POOL_HALO, gd), F32) + _nbytes((gd, tn), w_pool.dtype)
                                + 3 * _nbytes((tm, tn), BF16),
                                _nbytes((fw + gd, tn), BF16) + 3 * _nbytes((tm, tn), F32)
                                + 4 * _nbytes((tm + POOL_HALO, gd), F32)),
        name="branch_mix",
    )(attn_t, w_a, p, p, w_pool, pool_scale, gates, gates)


def _resid_mm_kernel(a_ref, w_ref, x_ref, g_ref, o_ref, *, chunks):
    w = _as_bf16(w_ref[...])
    rows = a_ref.shape[0] // chunks
    for r in range(0, a_ref.shape[0], rows):
        m = jnp.dot(a_ref[r:r + rows, :], w, preferred_element_type=F32)
        o_ref[r:r + rows, :] = ALPHA * x_ref[r:r + rows, :] + g_ref[...] * m


def _resid_matmul(a, w, x, mod, gate_idx, *, tm, tn, chunks=1, name):
    m, k = a.shape
    n = w.shape[1]
    nb = n // tn
    return pl.pallas_call(
        functools.partial(_resid_mm_kernel, chunks=chunks),
        out_shape=jax.ShapeDtypeStruct((m, n), F32),
        grid=(m // tm, nb),
        in_specs=[pl.BlockSpec((tm, k), lambda i, j: (i, 0)),
                  pl.BlockSpec((k, tn), lambda i, j: (0, j)),
                  pl.BlockSpec((tm, tn), lambda i, j: (i, j)),
                  pl.BlockSpec((1, tn), lambda i, j: (0, gate_idx * nb + j))],
        out_specs=pl.BlockSpec((tm, tn), lambda i, j: (i, j)),
        compiler_params=_params(("parallel", "parallel"),
                                _nbytes((tm, k), BF16) + _nbytes((k, tn), w.dtype)
                                + 2 * _nbytes((tm, tn), F32),
                                _cast_bytes((k, tn), w.dtype) + 2 * _nbytes((tm, tn), F32) // chunks),
        name=name,
    )(a, w, x, mod)


def _ln_ln_mod_kernel(r_ref, g_ref, b_ref, sh_ref, sc_ref, x_ref, u_ref):
    x1 = _ln_rows(r_ref[...]) * g_ref[...] + b_ref[...]
    x_ref[...] = x1
    u_ref[...] = (_ln_rows(x1) * (1.0 + sc_ref[...]) + sh_ref[...]).astype(u_ref.dtype)


def _ln_ln_mod(r, gain, bias, mod, shift_idx, scale_idx, *, tm=256):
    s, d = r.shape
    row = pl.BlockSpec((tm, d), lambda i: (i, 0))
    vec = pl.BlockSpec((1, d), lambda i: (0, 0))
    return pl.pallas_call(
        _ln_ln_mod_kernel,
        out_shape=(jax.ShapeDtypeStruct((s, d), F32), jax.ShapeDtypeStruct((s, d), BF16)),
        grid=(s // tm,),
        in_specs=[row, vec, vec,
                  pl.BlockSpec((1, d), lambda i: (0, shift_idx)),
                  pl.BlockSpec((1, d), lambda i: (0, scale_idx))],
        out_specs=(row, row),
        compiler_params=_params(("parallel",),
                                2 * _nbytes((tm, d), F32) + _nbytes((tm, d), BF16),
                                4 * _nbytes((tm, d), F32)),
        name="ln1_ln_mod",
    )(r, gain, bias, mod, mod)


def _ln_affine_kernel(r_ref, g_ref, b_ref, o_ref):
    o_ref[...] = _ln_rows(r_ref[...]) * g_ref[...] + b_ref[...]


def _ln_affine(r, gain, bias, *, tm=512):
    s, d = r.shape
    row = pl.BlockSpec((tm, d), lambda i: (i, 0))
    vec = pl.BlockSpec((1, d), lambda i: (0, 0))
    return pl.pallas_call(
        _ln_affine_kernel,
        out_shape=jax.ShapeDtypeStruct((s, d), F32),
        grid=(s // tm,),
        in_specs=[row, vec, vec],
        out_specs=row,
        compiler_params=_params(("parallel",), 2 * _nbytes((tm, d), F32),
                                3 * _nbytes((tm, d), F32)),
        name="ln2",
    )(r, gain, bias)


def _ffn_up_kernel(u_ref, wg_ref, wu_ref, o_ref):
    wg = _as_bf16(wg_ref[...])
    wu = _as_bf16(wu_ref[...])
    rows = u_ref.shape[0] // FFN_CHUNKS
    for r in range(0, u_ref.shape[0], rows):
        u = u_ref[r:r + rows, :]
        g = jnp.dot(u, wg, preferred_element_type=F32)
        up = jnp.dot(u, wu, preferred_element_type=F32)
        o_ref[r:r + rows, :] = (g * jax.nn.sigmoid(g) * up).astype(o_ref.dtype)


def _ffn_up(u, w_gate_up, *, tm=2048, tn=256):
    s, d = u.shape
    hidden = w_gate_up.shape[1] // 2
    nb = hidden // tn
    return pl.pallas_call(
        _ffn_up_kernel,
        out_shape=jax.ShapeDtypeStruct((s, hidden), BF16),
        grid=(s // tm, nb),
        in_specs=[pl.BlockSpec((tm, d), lambda i, j: (i, 0)),
                  pl.BlockSpec((d, tn), lambda i, j: (0, j)),
                  pl.BlockSpec((d, tn), lambda i, j: (0, nb + j))],
        out_specs=pl.BlockSpec((tm, tn), lambda i, j: (i, j)),
        compiler_params=_params(("parallel", "parallel"),
                                _nbytes((tm, d), BF16) + 2 * _nbytes((d, tn), w_gate_up.dtype)
                                + _nbytes((tm, tn), BF16),
                                2 * _nbytes((d, tn), BF16) + 2 * _nbytes((tm, tn), F32)),
        name="ffn_up",
    )(u, w_gate_up, w_gate_up)


def kernel(x, c, w_ada, b_ada, w_in, b_forget, w_attn_out, w_pool, pool_scale, w_out,
           ln1_g, ln1_b, w_gate_up, w_down, ln2_g, ln2_b):
    batch, seq, d = x.shape
    assert batch == 1 and w_ada.shape[0] == DEPTH == 1
    fox_w = w_attn_out.shape[1]
    heads = fox_w // FOX_HEAD_DIM
    pool_w = w_pool.shape[1] * w_pool.shape[2]
    o_f = 3 * fox_w
    o_p = o_f + heads
    tk = 512

    xs = x[0]
    w_in_t = jnp.swapaxes(w_in, 1, 2)[0]
    w_f = jnp.pad(w_in_t[o_f:o_p], ((0, LANES - heads), (0, 0))).astype(BF16)
    b_f = jnp.pad(b_forget, ((0, 0), (0, LANES - heads)))

    mod = _ada(c.reshape(d, 1), w_ada[0], b_ada)

    u1, e, c_blk = _ln_mod_fcum(xs, mod, 0, 1, w_f, b_f, heads=heads, tm=tk)
    cb = c_blk[:, 0, :heads].T
    tq = 2 * tk
    qt, sq_q = _matmul_tn(u1, w_in_t, 0, fox_w, tm=tq, tn=512, tk=tq, scale=FOX_HEAD_DIM ** -0.5 * LOG2E,
                          group_norms=True, name="proj_qt")
    k, sq_k, p, gates = _proj_kpg(u1, w_in_t, fox_w, fox_w, o_p, pool_w, 2 * d, tm=2048, tn=512)
    per_head = lambda sq: jnp.sqrt(sq[:, :, 0, :512 // LANES].reshape(sq.shape[0], heads))
    thr = (SKIP_LOG2 + 2.0 * NORM_SLACK * per_head(sq_q) * jnp.max(per_head(sq_k), axis=0)).T
    vt = _matmul_tn(u1, w_in_t, 2 * fox_w, fox_w, tm=1024, tn=512, tk=tk, name="proj_vt")
    attn, w_down_bf16 = _attention(cb, thr, qt, k, e, vt, w_down[0], heads=heads, tk=tk)
    mix = _mix(attn, w_attn_out[0], p, w_pool[0], pool_scale, gates)
    r1 = _resid_matmul(mix, w_out[0], xs, mod, 2, tm=2048, tn=256, chunks=4, name="out_proj")
    x1, u2 = _ln_ln_mod(r1, ln1_g, ln1_b, mod, 3, 4)

    act = _ffn_up(u2, w_gate_up[0])
    r2 = _resid_matmul(act, w_down_bf16, x1, mod, 5, tm=512, tn=512, name="ffn_down")
    out = _ln_affine(r2, ln2_g, ln2_b)
    return out[None]
```

```python
import functools

import jax
import jax.numpy as jnp
from jax import lax
from jax.experimental import pallas as pl
from jax.experimental.pallas import tpu as pltpu

F32 = jnp.float32
BF16 = jnp.bfloat16

FOX_HEAD_DIM = 128
POOL_WINDOWS = (2, 4, 8, 16)
POOL_HALO = 16
DEPTH = 1
ALPHA = (2 * DEPTH) ** 0.25
LN_EPS = 1e-5
LOG2E = 1.4426950408889634
LANES = 128
SUBLANES = 8
BIAS_PIECES = 3
ONES_ROWS = 16
MM_CHUNKS = 4
MIX_CHUNKS = 4
FFN_CHUNKS = 4
Q_PER_STEP = 4
SKIP_LOG2 = 48.0
NORM_SLACK = 1.02

V7X_VMEM_BYTES = 64 * 1024 * 1024
VMEM_CAP_BYTES = V7X_VMEM_BYTES - 6 * 1024 * 1024

_NT = (((1,), (1,)), ((), ()))
_TN = (((0,), (0,)), ((), ()))


def _nbytes(shape, dtype):
    n = jnp.dtype(dtype).itemsize
    for s in shape:
        n *= s
    return n


def _cast_bytes(shape, dtype):
    return 0 if dtype == BF16 else _nbytes(shape, BF16)


def _params(semantics, pipelined_bytes, resident_bytes=0):
    need = 2 * pipelined_bytes + resident_bytes
    assert need <= VMEM_CAP_BYTES, (need, VMEM_CAP_BYTES)
    return pltpu.CompilerParams(dimension_semantics=semantics,
                                vmem_limit_bytes=min(need + (4 << 20), VMEM_CAP_BYTES))


def _ln_rows(x):
    mu = jnp.mean(x, axis=-1, keepdims=True)
    xc = x - mu
    var = jnp.mean(xc * xc, axis=-1, keepdims=True)
    return xc * lax.rsqrt(var + LN_EPS)


def _as_bf16(w):
    return w if w.dtype == BF16 else w.astype(BF16)


def _split3(v):
    hi = v.astype(BF16)
    r1 = v - hi.astype(F32)
    mid = r1.astype(BF16)
    lo = (r1 - mid.astype(F32)).astype(BF16)
    return hi, mid, lo


def _ada_kernel(c_ref, w_ref, b_ref, o_ref):
    k = pl.program_id(1)

    @pl.when(k == 0)
    def _():
        o_ref[...] = b_ref[...]

    cc = c_ref[...]
    s = cc * jax.nn.sigmoid(cc)
    o_ref[...] += jnp.sum(s * w_ref[...], axis=0, keepdims=True)


def _ada(c_col, w_ada, b_ada, *, tk=512, tn=4096):
    d, n = w_ada.shape
    return pl.pallas_call(
        _ada_kernel,
        out_shape=jax.ShapeDtypeStruct((1, n), F32),
        grid=(n // tn, d // tk),
        in_specs=[pl.BlockSpec((tk, 1), lambda j, k: (k, 0)),
                  pl.BlockSpec((tk, tn), lambda j, k: (k, j)),
                  pl.BlockSpec((1, tn), lambda j, k: (0, j))],
        out_specs=pl.BlockSpec((1, tn), lambda j, k: (0, j)),
        compiler_params=_params(("parallel", "arbitrary"),
                                _nbytes((tk, tn), F32) + _nbytes((tk, LANES), F32),
                                _nbytes((tk, tn), F32)),
        name="ada",
    )(c_col, w_ada, b_ada)


def _ln_mod_fcum_kernel(x_ref, sh_ref, sc_ref, wf_ref, b_ref, u_ref, e_ref, c_ref,
                        carry_ref, f_ref, *, tm, rows, heads):
    @pl.when(pl.program_id(0) == 0)
    def _():
        carry_ref[...] = jnp.zeros_like(carry_ref)

    gain = 1.0 + sc_ref[...]
    shift = sh_ref[...]
    for r in range(0, tm, rows):
        u = (_ln_rows(x_ref[r:r + rows, :]) * gain + shift).astype(BF16)
        u_ref[r:r + rows, :] = u
        f_ref[r:r + rows, :] = lax.dot_general(u, wf_ref[...], _NT, preferred_element_type=F32)
    f = f_ref[...] + b_ref[...]
    ls = (jnp.minimum(f, 0.0) - jnp.log1p(jnp.exp(-jnp.abs(f)))) * LOG2E
    row = lax.broadcasted_iota(jnp.int32, (tm, tm), 0)
    col = lax.broadcasted_iota(jnp.int32, (tm, tm), 1)
    tri = jnp.where(col <= row, 1.0, 0.0).astype(BF16)
    cs = sum(jnp.dot(tri, piece, preferred_element_type=F32) for piece in _split3(ls))
    carry = carry_ref[...]
    c_ref[...] = carry
    carry_ref[...] = carry + cs[tm - 1:tm, :]
    head_lane = lax.broadcasted_iota(jnp.int32, (tm, LANES), 1) < heads
    e = sum(pltpu.roll(jnp.where(head_lane, piece.astype(F32), 0.0), p * heads, 1) if p else
            jnp.where(head_lane, piece.astype(F32), 0.0)
            for p, piece in enumerate(_split3(-cs)))
    e_ref[...] = e.astype(e_ref.dtype)


def _ln_mod_fcum(x, mod, shift_idx, scale_idx, wf, b_row, *, heads, tm, rows=128):
    s, d = x.shape
    nb = s // tm
    assert BIAS_PIECES * heads <= LANES
    vec = lambda idx: pl.BlockSpec((1, d), lambda i: (0, idx))
    const = lambda shape: pl.BlockSpec(shape, lambda i: (0, 0))
    return pl.pallas_call(
        functools.partial(_ln_mod_fcum_kernel, tm=tm, rows=rows, heads=heads),
        out_shape=(jax.ShapeDtypeStruct((s, d), BF16),
                   jax.ShapeDtypeStruct((s, LANES), BF16),
                   jax.ShapeDtypeStruct((nb, 1, LANES), F32)),
        grid=(nb,),
        in_specs=[pl.BlockSpec((tm, d), lambda i: (i, 0)), vec(shift_idx), vec(scale_idx),
                  const((LANES, d)), const((1, LANES))],
        out_specs=(pl.BlockSpec((tm, d), lambda i: (i, 0)),
                   pl.BlockSpec((tm, LANES), lambda i: (i, 0)),
                   pl.BlockSpec((None, 1, LANES), lambda i: (i, 0, 0))),
        scratch_shapes=[pltpu.VMEM((1, LANES), F32), pltpu.VMEM((tm, LANES), F32)],
        compiler_params=_params(("arbitrary",),
                                _nbytes((tm, d), F32) + _nbytes((tm, d), BF16)
                                + _nbytes((tm, LANES), BF16) + _nbytes((d, LANES), BF16),
                                4 * _nbytes((rows, d), F32) + 2 * _nbytes((tm, tm), F32)),
        name="ln_mod_fcum",
    )(x, mod, mod, wf, b_row)


def _wt_spec(tn, k, row0):
    assert row0 % SUBLANES == 0 and tn % SUBLANES == 0
    return pl.BlockSpec((pl.Element(tn), pl.Element(k)),
                        lambda i, j: ((row0 // SUBLANES + j * (tn // SUBLANES)) * SUBLANES, 0))


def _proj_kpg_kernel(a_ref, w_ref, k_ref, nrm_ref, p_ref, g_ref, *, nk, npool):
    j = pl.program_id(1)
    w = _as_bf16(w_ref[...])
    tm = a_ref.shape[0]
    chunk = tm // MM_CHUNKS

    def tiles():
        for r in range(0, tm, chunk):
            yield slice(r, r + chunk), lax.dot_general(a_ref[r:r + chunk, :], w, _NT,
                                                       preferred_element_type=F32)

    @pl.when(j < nk)
    def _():
        lane = lax.broadcasted_iota(jnp.int32, (1, LANES), 1)
        nrm = jnp.zeros((1, LANES), F32)
        for rows, acc in tiles():
            k_ref[rows, :] = acc.astype(k_ref.dtype)
            for g in range(acc.shape[1] // LANES):
                sq = jnp.sum(jnp.square(acc[:, g * LANES:(g + 1) * LANES]), axis=1, keepdims=True)
                nrm = jnp.maximum(nrm, jnp.where(lane == g, jnp.max(sq, axis=0, keepdims=True), 0.0))
        nrm_ref[...] = nrm

    @pl.when((j >= nk) & (j < nk + npool))
    def _():
        for rows, acc in tiles():
            p_ref[rows, :] = acc

    @pl.when(j >= nk + npool)
    def _():
        for rows, acc in tiles():
            g_ref[rows, :] = (0.5 * jnp.tanh(0.5 * acc) + 0.5).astype(g_ref.dtype)


def _proj_kpg(a, wt, k_row0, k_n, pg_row0, pool_n, gate_n, *, tm, tn):
    m, kdim = a.shape
    nk, npool, ngate = k_n // tn, pool_n // tn, gate_n // tn
    assert k_row0 % SUBLANES == 0 and pg_row0 % SUBLANES == 0 and tn % SUBLANES == 0

    def w_rows(i, j):
        blk = jnp.where(j < nk, k_row0 // SUBLANES + j * (tn // SUBLANES),
                        pg_row0 // SUBLANES + (j - nk) * (tn // SUBLANES))
        return blk * SUBLANES, 0

    k_idx = lambda i, j: (i, jnp.minimum(j, nk - 1))
    return pl.pallas_call(
        functools.partial(_proj_kpg_kernel, nk=nk, npool=npool),
        out_shape=[jax.ShapeDtypeStruct((m, k_n), BF16),
                   jax.ShapeDtypeStruct((m // tm, nk, 1, LANES), F32),
                   jax.ShapeDtypeStruct((m, pool_n), F32),
                   jax.ShapeDtypeStruct((m, gate_n), BF16)],
        grid=(m // tm, nk + npool + ngate),
        in_specs=[pl.BlockSpec((tm, kdim), lambda i, j: (i, 0), pipeline_mode=pl.Buffered(1)),
                  pl.BlockSpec((pl.Element(tn), pl.Element(kdim)), w_rows)],
        out_specs=[pl.BlockSpec((tm, tn), k_idx),
                   pl.BlockSpec((None, None, 1, LANES), lambda i, j: (*k_idx(i, j), 0, 0)),
                   pl.BlockSpec((tm, tn), lambda i, j: (i, jnp.clip(j - nk, 0, npool - 1))),
                   pl.BlockSpec((tm, tn), lambda i, j: (i, jnp.maximum(j - nk - npool, 0)))],
        compiler_params=_params(("arbitrary", "arbitrary"),
                                _nbytes((tn, kdim), wt.dtype) + 2 * _nbytes((tm, tn), BF16)
                                + _nbytes((tm, tn), F32),
                                _nbytes((tm, kdim), a.dtype) + _cast_bytes((tn, kdim), wt.dtype)
                                + 3 * _nbytes((tm // MM_CHUNKS, tn), F32)),
        name="proj_kpg",
    )(a, wt)


def _proj_qv_kernel(a_ref, w_ref, q_ref, nrm_ref, v_ref, *, nq_tiles, scale):
    j = pl.program_id(1)
    w = _as_bf16(w_ref[...])
    tm = a_ref.shape[0]
    chunk = tm // MM_CHUNKS

    def tiles():
        for r in range(0, tm, chunk):
            yield r, lax.dot_general(w, a_ref[r:r + chunk, :], _NT, preferred_element_type=F32)

    @pl.when(j < nq_tiles)
    def _():
        tq = q_ref.shape[2]
        lane = lax.broadcasted_iota(jnp.int32, (1, LANES), 1)
        nrm = jnp.zeros((1, LANES), F32)
        for r, acc_t in tiles():
            acc_t = acc_t * scale
            q_ref[r // tq, :, r % tq:r % tq + chunk] = acc_t.astype(q_ref.dtype)
            for g in range(acc_t.shape[0] // LANES):
                sq = jnp.sum(jnp.square(acc_t[g * LANES:(g + 1) * LANES, :]), axis=0, keepdims=True)
                nrm = jnp.maximum(nrm, jnp.where(lane == g, jnp.max(sq, axis=1, keepdims=True), 0.0))
        nrm_ref[...] = nrm

    @pl.when(j >= nq_tiles)
    def _():
        tk = v_ref.shape[2]
        for r, acc_t in tiles():
            v_ref[r // tk, :, r % tk:r % tk + chunk] = acc_t.astype(v_ref.dtype)


def _proj_qv(a, wt, q_row0, v_row0, n, *, tm, tn, tq, tk, scale):
    m, kdim = a.shape
    nt = n // tn
    assert q_row0 % SUBLANES == 0 and v_row0 % SUBLANES == 0 and tn % SUBLANES == 0
    assert (tm // MM_CHUNKS) <= min(tq, tk) and tm % tq == 0 and tm % tk == 0

    def w_rows(i, j):
        blk = jnp.where(j < nt, q_row0 // SUBLANES + j * (tn // SUBLANES),
                        v_row0 // SUBLANES + (j - nt) * (tn // SUBLANES))
        return blk * SUBLANES, 0

    q_idx = lambda i, j: (i, jnp.minimum(j, nt - 1))
    return pl.pallas_call(
        functools.partial(_proj_qv_kernel, nq_tiles=nt, scale=scale),
        out_shape=[jax.ShapeDtypeStruct((m // tq, n, tq), BF16),
                   jax.ShapeDtypeStruct((m // tm, nt, 1, LANES), F32),
                   jax.ShapeDtypeStruct((m // tk, n, tk), BF16)],
        grid=(m // tm, 2 * nt),
        in_specs=[pl.BlockSpec((tm, kdim), lambda i, j: (i, 0), pipeline_mode=pl.Buffered(1)),
                  pl.BlockSpec((pl.Element(tn), pl.Element(kdim)), w_rows)],
        out_specs=[pl.BlockSpec((tm // tq, tn, tq), lambda i, j: (*q_idx(i, j), 0)),
                   pl.BlockSpec((None, None, 1, LANES), lambda i, j: (*q_idx(i, j), 0, 0)),
                   pl.BlockSpec((tm // tk, tn, tk), lambda i, j: (i, jnp.maximum(j - nt, 0), 0))],
        compiler_params=_params(("arbitrary", "arbitrary"),
                                _nbytes((tn, kdim), wt.dtype) + 2 * _nbytes((tm, tn), BF16),
                                _nbytes((tm, kdim), a.dtype) + _cast_bytes((tn, kdim), wt.dtype)
                                + 3 * _nbytes((tm // MM_CHUNKS, tn), F32)),
        name="proj_qv",
    )(a, wt)


def _attn_kernel(cb_ref, thr_ref, q_ref, k_ref, e_ref, vt_ref, w_ref, o_ref, wb_ref, *scratch,
                 tk, tn, heads, side_blocks):
    h = pl.program_id(0)

    @pl.when(h * pl.num_programs(1) + pl.program_id(1) < side_blocks)
    def _():
        wb_ref[...] = w_ref[...].astype(wb_ref.dtype)

    blocks = [Q_PER_STEP * pl.program_id(1) + sub for sub in range(Q_PER_STEP)]
    firsts = [lax.while_loop(lambda p, i=i: (p < i) & (cb_ref[h, 2 * p + 2] - cb_ref[h, 2 * i]
                                                        > thr_ref[h, i]),
                             lambda p: p + 1, jnp.int32(0)) for i in blocks]
    for sub, (i, first) in enumerate(zip(blocks, firsts)):
        _attn_block(cb_ref, q_ref.at[sub], k_ref, e_ref, vt_ref,
                    o_ref.at[:, sub * q_ref.shape[2]:(sub + 1) * q_ref.shape[2]],
                    *scratch[3 * sub:3 * sub + 3], h=h, i=i, first=first, tk=tk, tn=tn, heads=heads)


def _attn_block(cb_ref, q_ref, k_ref, e_ref, vt_ref, o_ref, s0_ref, s1_ref, acc_ref, *,
                h, i, first, tk, tn, heads):
    dh, tq = q_ref.shape
    nt = tq // tn
    half = nt // 2
    r = lax.broadcasted_iota(jnp.int32, (dh, tq), 0) - h
    ones_rows = sum(jnp.where(r == p * heads, 1.0, 0.0) for p in range(BIAS_PIECES))
    qt_aug = jnp.concatenate([q_ref[...], ones_rows.astype(BF16)], axis=0)
    ones_v = jnp.ones((acc_ref.shape[0] - dh, tk), BF16)
    c_q = cb_ref[h, 2 * i]
    acc_ref[...] = jnp.zeros_like(acc_ref)

    def logits_to(buf_ref, j, first_tile=0):
        rows = pl.ds(pl.multiple_of(j * tk, tk), tk)
        k_aug = jnp.concatenate([k_ref[rows, :], e_ref[rows, :]], axis=1)
        cms = []
        for n in range(first_tile, nt):
            cols = slice(n * tn, (n + 1) * tn)
            s = jnp.dot(k_aug, qt_aug[:, cols], preferred_element_type=F32)
            buf_ref[:, cols] = s
            cms.append(jnp.max(s, axis=0, keepdims=True))
        return jnp.concatenate(cms, axis=1)

    def causal(buf_ref, n, first_tile):
        key = lax.broadcasted_iota(jnp.int32, (tk, tn), 0)
        qry = lax.broadcasted_iota(jnp.int32, (tk, tn), 1) + (n - first_tile) * tn
        return jnp.where(key <= qry, buf_ref[:, n * tn:(n + 1) * tn], -jnp.inf)

    def softmax_pv(buf_ref, j, cmax, m, first_tile=0, causal_tiles=0):
        lo = first_tile * tn
        c = c_q - cb_ref[h, j]
        m_old = m[:, lo:]
        m_new = jnp.maximum(m_old, cmax + c)
        a = jnp.exp2(m_old - m_new)
        off = m_new - c
        v_aug = jnp.concatenate([vt_ref[j], ones_v], axis=0)
        for n in range(first_tile, nt):
            cols = slice(n * tn, (n + 1) * tn)
            rel = slice(n * tn - lo, (n + 1) * tn - lo)
            s = causal(buf_ref, n, first_tile) if n < first_tile + causal_tiles else buf_ref[:, cols]
            p = jnp.exp2(s - off[:, rel]).astype(BF16)
            acc_ref[:, cols] = a[:, rel] * acc_ref[:, cols] + jnp.dot(
                v_aug, p, preferred_element_type=F32)
        if first_tile:
            m_new = jnp.concatenate([m[:, :lo], m_new], axis=1)
        return m_new

    def causal_max(buf_ref, first_tile):
        return [jnp.max(causal(buf_ref, n, first_tile), axis=0, keepdims=True)
                for n in range(first_tile, first_tile + half)]

    def pair(p, carry):
        m, cm0 = carry
        cm1 = logits_to(s1_ref, 2 * p + 1)
        m = softmax_pv(s0_ref, 2 * p, cm0, m)
        cm0 = logits_to(s0_ref, 2 * p + 2)
        m = softmax_pv(s1_ref, 2 * p + 1, cm1, m)
        return m, cm0

    def two_pairs(q, carry):
        p = start + 2 * q
        return pair(p + 1, pair(p, carry))

    start = first + (i - first) % 2
    carry = (jnp.full((1, tq), -jnp.inf, F32), logits_to(s0_ref, 2 * first))
    carry = lax.fori_loop(first, start, pair, carry)
    m, cm0 = lax.fori_loop(0, (i - start) // 2, two_pairs, carry)
    logits_to(s1_ref, 2 * i + 1, first_tile=half)
    cm = jnp.concatenate(causal_max(s0_ref, 0) + [cm0[:, half * tn:]], axis=1)
    m = softmax_pv(s0_ref, 2 * i, cm, m, causal_tiles=half)
    cm = jnp.concatenate(causal_max(s1_ref, half), axis=1)
    softmax_pv(s1_ref, 2 * i + 1, cm, m, first_tile=half, causal_tiles=half)
    o_ref[...] = (acc_ref[:dh, :] / acc_ref[dh:dh + 1, :]).astype(o_ref.dtype)


def _attention(cb, thr, qt, k, e, vt, w_side, *, heads, tk, tn=256, side_blocks=32):
    s = k.shape[0]
    dh = FOX_HEAD_DIM
    tq = 2 * tk
    steps = s // (Q_PER_STEP * tq)
    side_cols = w_side.shape[1] // 2
    side_rows = 2 * w_side.shape[0] // side_blocks
    assert side_rows * side_blocks == 2 * w_side.shape[0] and side_rows % (2 * SUBLANES) == 0
    assert side_cols % LANES == 0 and side_blocks <= heads * steps

    def side(h, i, *_):
        blk = jnp.minimum(h * steps + i, side_blocks - 1)
        return blk // 2, blk % 2

    grid_spec = pltpu.PrefetchScalarGridSpec(
        num_scalar_prefetch=2,
        grid=(heads, steps),
        in_specs=[pl.BlockSpec((Q_PER_STEP, dh, tq), lambda h, i, *_: (i, h, 0)),
                  pl.BlockSpec((s, dh), lambda h, i, *_: (0, h)),
                  pl.BlockSpec((s, LANES), lambda h, i, *_: (0, 0)),
                  pl.BlockSpec((s // tk, dh, tk), lambda h, i, *_: (0, h, 0)),
                  pl.BlockSpec((side_rows, side_cols), side)],
        out_specs=[pl.BlockSpec((dh, Q_PER_STEP * tq), lambda h, i, *_: (h, i)),
                   pl.BlockSpec((side_rows, side_cols), side)],
        scratch_shapes=Q_PER_STEP * [pltpu.VMEM((tk, tq), F32), pltpu.VMEM((tk, tq), F32),
                                     pltpu.VMEM((dh + ONES_ROWS, tq), F32)],
    )
    return pl.pallas_call(
        functools.partial(_attn_kernel, tk=tk, tn=tn, heads=heads, side_blocks=side_blocks),
        out_shape=[jax.ShapeDtypeStruct((heads * dh, s), BF16),
                   jax.ShapeDtypeStruct(w_side.shape, BF16)],
        grid_spec=grid_spec,
        compiler_params=_params(("arbitrary", "arbitrary"),
                                3 * _nbytes((s, dh), BF16) + 2 * Q_PER_STEP * _nbytes((tq, dh), BF16)
                                + _nbytes((side_rows, side_cols), F32)
                                + _nbytes((side_rows, side_cols), BF16),
                                Q_PER_STEP * (2 * _nbytes((tk, tq), F32) + _nbytes((dh + ONES_ROWS, tq), F32))
                                + 3 * _nbytes((tk, tq), F32)),
        name="fox_attention",
    )(cb, thr, qt, k, e, vt, w_side)


def _mix_kernel(a_ref, wa_ref, halo_ref, p_ref, wp_ref, ps_ref, ga_ref, gp_ref, o_ref,
                wa_b, wp_b, *, tm):
    j = pl.program_id(0)
    i = pl.program_id(1)

    @pl.when(i == 0)
    def _():
        wa_b[...] = wa_ref[...].astype(BF16)
        wp_b[...] = wp_ref[...].astype(BF16)

    halo = jnp.where(i == 0, 0.0, halo_ref[...])
    ext = jnp.concatenate([halo, p_ref[...]], axis=0)
    acc = ext
    win = ext
    for g, w in enumerate(POOL_WINDOWS):
        acc = acc + pltpu.roll(acc, w // 2, 0)
        win = jnp.where(j == g, acc, win)
    t = i * tm + lax.broadcasted_iota(jnp.int32, (tm, 1), 0)
    cnt = jnp.minimum(t + 1, jnp.left_shift(2, j)).astype(F32)
    pooled = (win[POOL_HALO:] / cnt - ext[POOL_HALO:]).astype(BF16)

    a_t = a_ref[...]
    cols = o_ref.shape[1] // MIX_CHUNKS
    for c in range(0, o_ref.shape[1], cols):
        ya = lax.dot_general(a_t, wa_b[:, c:c + cols], _TN, preferred_element_type=F32)
        yp = jnp.dot(pooled, wp_b[:, c:c + cols], preferred_element_type=F32) * ps_ref[:, c:c + cols]
        o_ref[:, c:c + cols] = (ga_ref[:, c:c + cols].astype(F32) * ya
                                + gp_ref[:, c:c + cols].astype(F32) * yp).astype(o_ref.dtype)


def _mix(attn_t, w_a, p, w_pool, pool_scale, gates, *, tm=512):
    fw, s = attn_t.shape
    groups, gd, tn = w_pool.shape
    assert POOL_WINDOWS == tuple(2 << g for g in range(groups))
    d = w_a.shape[1]
    hb = tm // POOL_HALO
    return pl.pallas_call(
        functools.partial(_mix_kernel, tm=tm),
        out_shape=jax.ShapeDtypeStruct((s, d), BF16),
        grid=(groups, s // tm),
        in_specs=[pl.BlockSpec((fw, tm), lambda j, i: (0, i)),
                  pl.BlockSpec((fw, tn), lambda j, i: (0, j)),
                  pl.BlockSpec((POOL_HALO, gd), lambda j, i: (jnp.maximum(i * hb - 1, 0), j)),
                  pl.BlockSpec((tm, gd), lambda j, i: (i, j)),
                  pl.BlockSpec((None, gd, tn), lambda j, i: (j, 0, 0)),
                  pl.BlockSpec((1, tn), lambda j, i: (0, j)),
                  pl.BlockSpec((tm, tn), lambda j, i: (i, j)),
                  pl.BlockSpec((tm, tn), lambda j, i: (i, groups + j))],
        out_specs=pl.BlockSpec((tm, tn), lambda j, i: (i, j)),
        scratch_shapes=[pltpu.VMEM((fw, tn), BF16), pltpu.VMEM((gd, tn), BF16)],
        compiler_params=_params(("parallel", "arbitrary"),
                                _nbytes((tm, fw), BF16) + _nbytes((fw, tn), w_a.dtype)
                                + _nbytes((tm + POOL_HALO, gd), F32) + _nbytes((gd, tn), w_pool.dtype)
                                + 3 * _nbytes((tm, tn), BF16),
                                _nbytes((fw + gd, tn), BF16) + 3 * _nbytes((tm, tn), F32)
                                + 4 * _nbytes((tm + POOL_HALO, gd), F32)),
        name="branch_mix",
    )(attn_t, w_a, p, p, w_pool, pool_scale, gates, gates)


def _resid_mm_kernel(a_ref, w_ref, x_ref, g_ref, o_ref, *, chunks):
    w = _as_bf16(w_ref[...])
    rows = a_ref.shape[0] // chunks
    for r in range(0, a_ref.shape[0], rows):
        m = jnp.dot(a_ref[r:r + rows, :], w, preferred_element_type=F32)
        o_ref[r:r + rows, :] = ALPHA * x_ref[r:r + rows, :] + g_ref[...] * m


def _resid_matmul(a, w, x, mod, gate_idx, *, tm, tn, chunks=1, name):
    m, k = a.shape
    n = w.shape[1]
    nb = n // tn
    return pl.pallas_call(
        functools.partial(_resid_mm_kernel, chunks=chunks),
        out_shape=jax.ShapeDtypeStruct((m, n), F32),
        grid=(m // tm, nb),
        in_specs=[pl.BlockSpec((tm, k), lambda i, j: (i, 0)),
                  pl.BlockSpec((k, tn), lambda i, j: (0, j)),
                  pl.BlockSpec((tm, tn), lambda i, j: (i, j)),
                  pl.BlockSpec((1, tn), lambda i, j: (0, gate_idx * nb + j))],
        out_specs=pl.BlockSpec((tm, tn), lambda i, j: (i, j)),
        compiler_params=_params(("parallel", "parallel"),
                                _nbytes((tm, k), BF16) + _nbytes((k, tn), w.dtype)
                                + 2 * _nbytes((tm, tn), F32),
                                _cast_bytes((k, tn), w.dtype) + 2 * _nbytes((tm, tn), F32) // chunks),
        name=name,
    )(a, w, x, mod)


def _ln_ln_mod_kernel(r_ref, g_ref, b_ref, sh_ref, sc_ref, x_ref, u_ref):
    x1 = _ln_rows(r_ref[...]) * g_ref[...] + b_ref[...]
    x_ref[...] = x1
    u_ref[...] = (_ln_rows(x1) * (1.0 + sc_ref[...]) + sh_ref[...]).astype(u_ref.dtype)


def _ln_ln_mod(r, gain, bias, mod, shift_idx, scale_idx, *, tm=256):
    s, d = r.shape
    row = pl.BlockSpec((tm, d), lambda i: (i, 0))
    vec = pl.BlockSpec((1, d), lambda i: (0, 0))
    return pl.pallas_call(
        _ln_ln_mod_kernel,
        out_shape=(jax.ShapeDtypeStruct((s, d), F32), jax.ShapeDtypeStruct((s, d), BF16)),
        grid=(s // tm,),
        in_specs=[row, vec, vec,
                  pl.BlockSpec((1, d), lambda i: (0, shift_idx)),
                  pl.BlockSpec((1, d), lambda i: (0, scale_idx))],
        out_specs=(row, row),
        compiler_params=_params(("parallel",),
                                2 * _nbytes((tm, d), F32) + _nbytes((tm, d), BF16),
                                4 * _nbytes((tm, d), F32)),
        name="ln1_ln_mod",
    )(r, gain, bias, mod, mod)


def _ln_affine_kernel(r_ref, g_ref, b_ref, o_ref):
    o_ref[...] = _ln_rows(r_ref[...]) * g_ref[...] + b_ref[...]


def _ln_affine(r, gain, bias, *, tm=512):
    s, d = r.shape
    row = pl.BlockSpec((tm, d), lambda i: (i, 0))
    vec = pl.BlockSpec((1, d), lambda i: (0, 0))
    return pl.pallas_call(
        _ln_affine_kernel,
        out_shape=jax.ShapeDtypeStruct((s, d), F32),
        grid=(s // tm,),
        in_specs=[row, vec, vec],
        out_specs=row,
        compiler_params=_params(("parallel",), 2 * _nbytes((tm, d), F32),
                                3 * _nbytes((tm, d), F32)),
        name="ln2",
    )(r, gain, bias)


def _ffn_up_kernel(u_ref, wg_ref, wu_ref, o_ref):
    wg = _as_bf16(wg_ref[...])
    wu = _as_bf16(wu_ref[...])
    rows = u_ref.shape[0] // FFN_CHUNKS
    for r in range(0, u_ref.shape[0], rows):
        u = u_ref[r:r + rows, :]
        g = jnp.dot(u, wg, preferred_element_type=F32)
        up = jnp.dot(u, wu, preferred_element_type=F32)
        o_ref[r:r + rows, :] = (g * jax.nn.sigmoid(g) * up).astype(o_ref.dtype)


def _ffn_up(u, w_gate_up, *, tm=2048, tn=256):
    s, d = u.shape
    hidden = w_gate_up.shape[1] // 2
    nb = hidden // tn
    return pl.pallas_call(
        _ffn_up_kernel,
        out_shape=jax.ShapeDtypeStruct((s, hidden), BF16),
        grid=(s // tm, nb),
        in_specs=[pl.BlockSpec((tm, d), lambda i, j: (i, 0)),
                  pl.BlockSpec((d, tn), lambda i, j: (0, j)),
                  pl.BlockSpec((d, tn), lambda i, j: (0, nb + j))],
        out_specs=pl.BlockSpec((tm, tn), lambda i, j: (i, j)),
        compiler_params=_params(("parallel", "parallel"),
                                _nbytes((tm, d), BF16) + 2 * _nbytes((d, tn), w_gate_up.dtype)
                                + _nbytes((tm, tn), BF16),
                                2 * _nbytes((d, tn), BF16) + 2 * _nbytes((tm, tn), F32)),
        name="ffn_up",
    )(u, w_gate_up, w_gate_up)


def kernel(x, c, w_ada, b_ada, w_in, b_forget, w_attn_out, w_pool, pool_scale, w_out,
           ln1_g, ln1_b, w_gate_up, w_down, ln2_g, ln2_b):
    batch, seq, d = x.shape
    assert batch == 1 and w_ada.shape[0] == DEPTH == 1
    fox_w = w_attn_out.shape[1]
    heads = fox_w // FOX_HEAD_DIM
    pool_w = w_pool.shape[1] * w_pool.shape[2]
    o_f = 3 * fox_w
    o_p = o_f + heads
    tk = 512

    xs = x[0]
    w_in_t = jnp.swapaxes(w_in, 1, 2)[0]
    w_f = jnp.pad(w_in_t[o_f:o_p], ((0, LANES - heads), (0, 0))).astype(BF16)
    b_f = jnp.pad(b_forget, ((0, 0), (0, LANES - heads)))

    mod = _ada(c.reshape(d, 1), w_ada[0], b_ada)

    u1, e, c_blk = _ln_mod_fcum(xs, mod, 0, 1, w_f, b_f, heads=heads, tm=tk)
    cb = c_blk[:, 0, :heads].T
    tq = 2 * tk
    tm = 2048
    qt, sq_q, vt = _proj_qv(u1, w_in_t, 0, 2 * fox_w, fox_w, tm=tm, tn=512, tq=tq, tk=tk,
                            scale=FOX_HEAD_DIM ** -0.5 * LOG2E)
    k, sq_k, p, gates = _proj_kpg(u1, w_in_t, fox_w, fox_w, o_p, pool_w, 2 * d, tm=tm, tn=512)
    per_head = lambda sq: jnp.sqrt(sq[:, :, 0, :512 // LANES].reshape(sq.shape[0], heads))
    qn = jnp.repeat(per_head(sq_q), tm // tq, axis=0)
    thr = (SKIP_LOG2 + 2.0 * NORM_SLACK * qn * jnp.max(per_head(sq_k), axis=0)).T
    attn, w_down_bf16 = _attention(cb, thr, qt, k, e, vt, w_down[0], heads=heads, tk=tk)
    mix = _mix(attn, w_attn_out[0], p, w_pool[0], pool_scale, gates)
    r1 = _resid_matmul(mix, w_out[0], xs, mod, 2, tm=2048, tn=256, chunks=4, name="out_proj")
    x1, u2 = _ln_ln_mod(r1, ln1_g, ln1_b, mod, 3, 4)

    act = _ffn_up(u2, w_gate_up[0])
    r2 = _resid_matmul(act, w_down_bf16, x1, mod, 5, tm=512, tn=512, name="ffn_down")
    out = _ln_affine(r2, ln2_g, ln2_b)
    return out[None]
```

```python
import functools

import jax
import jax.numpy as jnp
from jax import lax
from jax.experimental import pallas as pl
from jax.experimental.pallas import tpu as pltpu

F32 = jnp.float32
BF16 = jnp.bfloat16

FOX_HEAD_DIM = 128
POOL_WINDOWS = (2, 4, 8, 16)
POOL_HALO = 16
DEPTH = 1
ALPHA = (2 * DEPTH) ** 0.25
LN_EPS = 1e-5
LOG2E = 1.4426950408889634
LANES = 128
SUBLANES = 8
BIAS_PIECES = 3
ONES_ROWS = 16
MM_CHUNKS = 4
MIX_CHUNKS = 4
FFN_CHUNKS = 4
Q_PER_STEP = 4
SKIP_LOG2 = 48.0
NORM_SLACK = 1.02

V7X_VMEM_BYTES = 64 * 1024 * 1024
VMEM_CAP_BYTES = V7X_VMEM_BYTES - 2 * 1024 * 1024

_NT = (((1,), (1,)), ((), ()))
_TN = (((0,), (0,)), ((), ()))


def _nbytes(shape, dtype):
    n = jnp.dtype(dtype).itemsize
    for s in shape:
        n *= s
    return n


def _cast_bytes(shape, dtype):
    return 0 if dtype == BF16 else _nbytes(shape, BF16)


def _params(semantics, pipelined_bytes, resident_bytes=0):
    need = 2 * pipelined_bytes + resident_bytes
    assert need <= VMEM_CAP_BYTES, (need, VMEM_CAP_BYTES)
    return pltpu.CompilerParams(dimension_semantics=semantics,
                                vmem_limit_bytes=min(need + (4 << 20), VMEM_CAP_BYTES))


def _ln_rows(x):
    mu = jnp.mean(x, axis=-1, keepdims=True)
    xc = x - mu
    var = jnp.mean(xc * xc, axis=-1, keepdims=True)
    return xc * lax.rsqrt(var + LN_EPS)


def _as_bf16(w):
    return w if w.dtype == BF16 else w.astype(BF16)


def _split3(v):
    hi = v.astype(BF16)
    r1 = v - hi.astype(F32)
    mid = r1.astype(BF16)
    lo = (r1 - mid.astype(F32)).astype(BF16)
    return hi, mid, lo


def _ada_kernel(c_ref, w_ref, b_ref, o_ref):
    k = pl.program_id(1)

    @pl.when(k == 0)
    def _():
        o_ref[...] = b_ref[...]

    cc = c_ref[...]
    s = cc * jax.nn.sigmoid(cc)
    o_ref[...] += jnp.sum(s * w_ref[...], axis=0, keepdims=True)


def _ada(c_col, w_ada, b_ada, *, tk=512, tn=4096):
    d, n = w_ada.shape
    return pl.pallas_call(
        _ada_kernel,
        out_shape=jax.ShapeDtypeStruct((1, n), F32),
        grid=(n // tn, d // tk),
        in_specs=[pl.BlockSpec((tk, 1), lambda j, k: (k, 0)),
                  pl.BlockSpec((tk, tn), lambda j, k: (k, j)),
                  pl.BlockSpec((1, tn), lambda j, k: (0, j))],
        out_specs=pl.BlockSpec((1, tn), lambda j, k: (0, j)),
        compiler_params=_params(("parallel", "arbitrary"),
                                _nbytes((tk, tn), F32) + _nbytes((tk, LANES), F32),
                                _nbytes((tk, tn), F32)),
        name="ada",
    )(c_col, w_ada, b_ada)


def _ln_mod_fcum_kernel(x_ref, sh_ref, sc_ref, wf_ref, b_ref, u_ref, e_ref, c_ref,
                        carry_ref, f_ref, *, tm, rows, heads):
    @pl.when(pl.program_id(0) == 0)
    def _():
        carry_ref[...] = jnp.zeros_like(carry_ref)

    gain = 1.0 + sc_ref[...]
    shift = sh_ref[...]
    for r in range(0, tm, rows):
        u = (_ln_rows(x_ref[r:r + rows, :]) * gain + shift).astype(BF16)
        u_ref[r:r + rows, :] = u
        f_ref[r:r + rows, :] = lax.dot_general(u, wf_ref[...], _NT, preferred_element_type=F32)
    f = f_ref[...] + b_ref[...]
    ls = (jnp.minimum(f, 0.0) - jnp.log1p(jnp.exp(-jnp.abs(f)))) * LOG2E
    row = lax.broadcasted_iota(jnp.int32, (tm, tm), 0)
    col = lax.broadcasted_iota(jnp.int32, (tm, tm), 1)
    tri = jnp.where(col <= row, 1.0, 0.0).astype(BF16)
    cs = sum(jnp.dot(tri, piece, preferred_element_type=F32) for piece in _split3(ls))
    carry = carry_ref[...]
    c_ref[...] = carry
    carry_ref[...] = carry + cs[tm - 1:tm, :]
    head_lane = lax.broadcasted_iota(jnp.int32, (tm, LANES), 1) < heads
    e = sum(pltpu.roll(jnp.where(head_lane, piece.astype(F32), 0.0), p * heads, 1) if p else
            jnp.where(head_lane, piece.astype(F32), 0.0)
            for p, piece in enumerate(_split3(-cs)))
    e_ref[...] = e.astype(e_ref.dtype)


def _ln_mod_fcum(x, mod, shift_idx, scale_idx, wf, b_row, *, heads, tm, rows=128):
    s, d = x.shape
    nb = s // tm
    assert BIAS_PIECES * heads <= LANES
    vec = lambda idx: pl.BlockSpec((1, d), lambda i: (0, idx))
    const = lambda shape: pl.BlockSpec(shape, lambda i: (0, 0))
    return pl.pallas_call(
        functools.partial(_ln_mod_fcum_kernel, tm=tm, rows=rows, heads=heads),
        out_shape=(jax.ShapeDtypeStruct((s, d), BF16),
                   jax.ShapeDtypeStruct((s, LANES), BF16),
                   jax.ShapeDtypeStruct((nb, 1, LANES), F32)),
        grid=(nb,),
        in_specs=[pl.BlockSpec((tm, d), lambda i: (i, 0)), vec(shift_idx), vec(scale_idx),
                  const((LANES, d)), const((1, LANES))],
        out_specs=(pl.BlockSpec((tm, d), lambda i: (i, 0)),
                   pl.BlockSpec((tm, LANES), lambda i: (i, 0)),
                   pl.BlockSpec((None, 1, LANES), lambda i: (i, 0, 0))),
        scratch_shapes=[pltpu.VMEM((1, LANES), F32), pltpu.VMEM((tm, LANES), F32)],
        compiler_params=_params(("arbitrary",),
                                _nbytes((tm, d), F32) + _nbytes((tm, d), BF16)
                                + _nbytes((tm, LANES), BF16) + _nbytes((d, LANES), BF16),
                                4 * _nbytes((rows, d), F32) + 2 * _nbytes((tm, tm), F32)),
        name="ln_mod_fcum",
    )(x, mod, mod, wf, b_row)


def _wt_spec(tn, k, row0):
    assert row0 % SUBLANES == 0 and tn % SUBLANES == 0
    return pl.BlockSpec((pl.Element(tn), pl.Element(k)),
                        lambda i, j: ((row0 // SUBLANES + j * (tn // SUBLANES)) * SUBLANES, 0))


def _proj_kpg_kernel(a_ref, w_ref, k_ref, nrm_ref, p_ref, g_ref, *, nk, npool):
    j = pl.program_id(1)
    w = _as_bf16(w_ref[...])
    tm = a_ref.shape[0]
    chunk = tm // MM_CHUNKS

    def tiles():
        for r in range(0, tm, chunk):
            yield slice(r, r + chunk), lax.dot_general(a_ref[r:r + chunk, :], w, _NT,
                                                       preferred_element_type=F32)

    @pl.when(j < nk)
    def _():
        lane = lax.broadcasted_iota(jnp.int32, (1, LANES), 1)
        nrm = jnp.zeros((1, LANES), F32)
        for rows, acc in tiles():
            k_ref[rows, :] = acc.astype(k_ref.dtype)
            for g in range(acc.shape[1] // LANES):
                sq = jnp.sum(jnp.square(acc[:, g * LANES:(g + 1) * LANES]), axis=1, keepdims=True)
                nrm = jnp.maximum(nrm, jnp.where(lane == g, jnp.max(sq, axis=0, keepdims=True), 0.0))
        nrm_ref[...] = nrm

    @pl.when((j >= nk) & (j < nk + npool))
    def _():
        for rows, acc in tiles():
            p_ref[rows, :] = acc

    @pl.when(j >= nk + npool)
    def _():
        for rows, acc in tiles():
            g_ref[rows, :] = (0.5 * jnp.tanh(0.5 * acc) + 0.5).astype(g_ref.dtype)


def _proj_kpg(a, wt, k_row0, k_n, pg_row0, pool_n, gate_n, *, tm, tn):
    m, kdim = a.shape
    nk, npool, ngate = k_n // tn, pool_n // tn, gate_n // tn
    assert k_row0 % SUBLANES == 0 and pg_row0 % SUBLANES == 0 and tn % SUBLANES == 0

    def w_rows(i, j):
        blk = jnp.where(j < nk, k_row0 // SUBLANES + j * (tn // SUBLANES),
                        pg_row0 // SUBLANES + (j - nk) * (tn // SUBLANES))
        return blk * SUBLANES, 0

    k_idx = lambda i, j: (i, jnp.minimum(j, nk - 1))
    return pl.pallas_call(
        functools.partial(_proj_kpg_kernel, nk=nk, npool=npool),
        out_shape=[jax.ShapeDtypeStruct((m, k_n), BF16),
                   jax.ShapeDtypeStruct((m // tm, nk, 1, LANES), F32),
                   jax.ShapeDtypeStruct((m, pool_n), F32),
                   jax.ShapeDtypeStruct((m, gate_n), BF16)],
        grid=(m // tm, nk + npool + ngate),
        in_specs=[pl.BlockSpec((tm, kdim), lambda i, j: (i, 0), pipeline_mode=pl.Buffered(1)),
                  pl.BlockSpec((pl.Element(tn), pl.Element(kdim)), w_rows)],
        out_specs=[pl.BlockSpec((tm, tn), k_idx),
                   pl.BlockSpec((None, None, 1, LANES), lambda i, j: (*k_idx(i, j), 0, 0)),
                   pl.BlockSpec((tm, tn), lambda i, j: (i, jnp.clip(j - nk, 0, npool - 1))),
                   pl.BlockSpec((tm, tn), lambda i, j: (i, jnp.maximum(j - nk - npool, 0)))],
        compiler_params=_params(("arbitrary", "arbitrary"),
                                _nbytes((tn, kdim), wt.dtype) + 2 * _nbytes((tm, tn), BF16)
                                + _nbytes((tm, tn), F32),
                                _nbytes((tm, kdim), a.dtype) + _cast_bytes((tn, kdim), wt.dtype)
                                + 3 * _nbytes((tm // MM_CHUNKS, tn), F32)),
        name="proj_kpg",
    )(a, wt)


def _mm_tn_kernel(a_ref, w_ref, o_ref, *rest, scale):
    acc_t = lax.dot_general(_as_bf16(w_ref[...]), a_ref[...], _NT,
                            preferred_element_type=F32)
    if scale is not None:
        acc_t = acc_t * scale
    tk = o_ref.shape[2]
    for kb in range(o_ref.shape[0]):
        o_ref[kb] = acc_t[:, kb * tk:(kb + 1) * tk].astype(o_ref.dtype)
    if rest:
        lane = lax.broadcasted_iota(jnp.int32, (1, LANES), 1)
        nrm = jnp.zeros((1, LANES), F32)
        for g in range(acc_t.shape[0] // LANES):
            sq = jnp.sum(jnp.square(acc_t[g * LANES:(g + 1) * LANES, :]), axis=0, keepdims=True)
            nrm = jnp.where(lane == g, jnp.max(sq, axis=1, keepdims=True), nrm)
        rest[0][...] = nrm


def _matmul_tn(a, wt, row0, n, *, tm, tn, tk, scale=None, group_norms=False, name):
    m, k = a.shape
    out_shape = [jax.ShapeDtypeStruct((m // tk, n, tk), BF16)]
    out_specs = [pl.BlockSpec((tm // tk, tn, tk), lambda i, j: (i, j, 0))]
    if group_norms:
        out_shape.append(jax.ShapeDtypeStruct((m // tm, n // tn, 1, LANES), F32))
        out_specs.append(pl.BlockSpec((None, None, 1, LANES), lambda i, j: (i, j, 0, 0)))
    out = pl.pallas_call(
        functools.partial(_mm_tn_kernel, scale=scale),
        out_shape=out_shape,
        grid=(m // tm, n // tn),
        in_specs=[pl.BlockSpec((tm, k), lambda i, j: (i, 0)), _wt_spec(tn, k, row0)],
        out_specs=out_specs,
        compiler_params=_params(("parallel", "parallel"),
                                _nbytes((tm, k), a.dtype) + _nbytes((tn, k), wt.dtype)
                                + _nbytes((tm, tn), BF16),
                                _cast_bytes((tn, k), wt.dtype) + 3 * _nbytes((tm, tn), F32)),
        name=name,
    )(a, wt)
    return out if group_norms else out[0]


def _attn_kernel(cb_ref, thr_ref, q_ref, k_ref, e_ref, vt_ref, w_ref, o_ref, wb_ref, *scratch,
                 tk, tn, heads, side_blocks):
    h = pl.program_id(0)

    @pl.when(h * pl.num_programs(1) + pl.program_id(1) < side_blocks)
    def _():
        wb_ref[...] = w_ref[...].astype(wb_ref.dtype)

    blocks = [Q_PER_STEP * pl.program_id(1) + sub for sub in range(Q_PER_STEP)]
    firsts = [lax.while_loop(lambda p, i=i: (p < i) & (cb_ref[h, 2 * p + 2] - cb_ref[h, 2 * i]
                                                        > thr_ref[h, i]),
                             lambda p: p + 1, jnp.int32(0)) for i in blocks]
    for sub, (i, first) in enumerate(zip(blocks, firsts)):
        _attn_block(cb_ref, q_ref.at[sub], k_ref, e_ref, vt_ref,
                    o_ref.at[:, sub * q_ref.shape[2]:(sub + 1) * q_ref.shape[2]],
                    *scratch[3 * sub:3 * sub + 3], h=h, i=i, first=first, tk=tk, tn=tn, heads=heads)


def _attn_block(cb_ref, q_ref, k_ref, e_ref, vt_ref, o_ref, s0_ref, s1_ref, acc_ref, *,
                h, i, first, tk, tn, heads):
    dh, tq = q_ref.shape
    nt = tq // tn
    half = nt // 2
    r = lax.broadcasted_iota(jnp.int32, (dh, tq), 0) - h
    ones_rows = sum(jnp.where(r == p * heads, 1.0, 0.0) for p in range(BIAS_PIECES))
    qt_aug = jnp.concatenate([q_ref[...], ones_rows.astype(BF16)], axis=0)
    ones_v = jnp.ones((acc_ref.shape[0] - dh, tk), BF16)
    c_q = cb_ref[h, 2 * i]
    acc_ref[...] = jnp.zeros_like(acc_ref)

    def logits_to(buf_ref, j, first_tile=0):
        rows = pl.ds(pl.multiple_of(j * tk, tk), tk)
        k_aug = jnp.concatenate([k_ref[rows, :], e_ref[rows, :]], axis=1)
        cms = []
        for n in range(first_tile, nt):
            cols = slice(n * tn, (n + 1) * tn)
            s = jnp.dot(k_aug, qt_aug[:, cols], preferred_element_type=F32)
            buf_ref[:, cols] = s
            cms.append(jnp.max(s, axis=0, keepdims=True))
        return jnp.concatenate(cms, axis=1)

    def causal(buf_ref, n, first_tile):
        key = lax.broadcasted_iota(jnp.int32, (tk, tn), 0)
        qry = lax.broadcasted_iota(jnp.int32, (tk, tn), 1) + (n - first_tile) * tn
        return jnp.where(key <= qry, buf_ref[:, n * tn:(n + 1) * tn], -jnp.inf)

    def softmax_pv(buf_ref, j, cmax, m, first_tile=0, causal_tiles=0):
        lo = first_tile * tn
        c = c_q - cb_ref[h, j]
        m_old = m[:, lo:]
        m_new = jnp.maximum(m_old, cmax + c)
        a = jnp.exp2(m_old - m_new)
        off = m_new - c
        v_aug = jnp.concatenate([vt_ref[j], ones_v], axis=0)
        for n in range(first_tile, nt):
            cols = slice(n * tn, (n + 1) * tn)
            rel = slice(n * tn - lo, (n + 1) * tn - lo)
            s = causal(buf_ref, n, first_tile) if n < first_tile + causal_tiles else buf_ref[:, cols]
            p = jnp.exp2(s - off[:, rel]).astype(BF16)
            acc_ref[:, cols] = a[:, rel] * acc_ref[:, cols] + jnp.dot(
                v_aug, p, preferred_element_type=F32)
        if first_tile:
            m_new = jnp.concatenate([m[:, :lo], m_new], axis=1)
        return m_new

    def causal_max(buf_ref, first_tile):
        return [jnp.max(causal(buf_ref, n, first_tile), axis=0, keepdims=True)
                for n in range(first_tile, first_tile + half)]

    def pair(p, carry):
        m, cm0 = carry
        cm1 = logits_to(s1_ref, 2 * p + 1)
        m = softmax_pv(s0_ref, 2 * p, cm0, m)
        cm0 = logits_to(s0_ref, 2 * p + 2)
        m = softmax_pv(s1_ref, 2 * p + 1, cm1, m)
        return m, cm0

    def two_pairs(q, carry):
        p = start + 2 * q
        return pair(p + 1, pair(p, carry))

    start = first + (i - first) % 2
    carry = (jnp.full((1, tq), -jnp.inf, F32), logits_to(s0_ref, 2 * first))
    carry = lax.fori_loop(first, start, pair, carry)
    m, cm0 = lax.fori_loop(0, (i - start) // 2, two_pairs, carry)
    logits_to(s1_ref, 2 * i + 1, first_tile=half)
    cm = jnp.concatenate(causal_max(s0_ref, 0) + [cm0[:, half * tn:]], axis=1)
    m = softmax_pv(s0_ref, 2 * i, cm, m, causal_tiles=half)
    cm = jnp.concatenate(causal_max(s1_ref, half), axis=1)
    softmax_pv(s1_ref, 2 * i + 1, cm, m, first_tile=half, causal_tiles=half)
    o_ref[...] = (acc_ref[:dh, :] / acc_ref[dh:dh + 1, :]).astype(o_ref.dtype)


def _attention(cb, thr, qt, k, e, vt, w_side, *, heads, tk, tn=256, side_blocks=32):
    s = k.shape[0]
    dh = FOX_HEAD_DIM
    tq = 2 * tk
    steps = s // (Q_PER_STEP * tq)
    side_cols = w_side.shape[1] // 2
    side_rows = 2 * w_side.shape[0] // side_blocks
    assert side_rows * side_blocks == 2 * w_side.shape[0] and side_rows % (2 * SUBLANES) == 0
    assert side_cols % LANES == 0 and side_blocks <= heads * steps

    def side(h, i, *_):
        blk = jnp.minimum(h * steps + i, side_blocks - 1)
        return blk // 2, blk % 2

    grid_spec = pltpu.PrefetchScalarGridSpec(
        num_scalar_prefetch=2,
        grid=(heads, steps),
        in_specs=[pl.BlockSpec((Q_PER_STEP, dh, tq), lambda h, i, *_: (i, h, 0)),
                  pl.BlockSpec((s, dh), lambda h, i, *_: (0, h)),
                  pl.BlockSpec((s, LANES), lambda h, i, *_: (0, 0)),
                  pl.BlockSpec((s // tk, dh, tk), lambda h, i, *_: (0, h, 0)),
                  pl.BlockSpec((side_rows, side_cols), side)],
        out_specs=[pl.BlockSpec((dh, Q_PER_STEP * tq), lambda h, i, *_: (h, i)),
                   pl.BlockSpec((side_rows, side_cols), side)],
        scratch_shapes=Q_PER_STEP * [pltpu.VMEM((tk, tq), F32), pltpu.VMEM((tk, tq), F32),
                                     pltpu.VMEM((dh + ONES_ROWS, tq), F32)],
    )
    return pl.pallas_call(
        functools.partial(_attn_kernel, tk=tk, tn=tn, heads=heads, side_blocks=side_blocks),
        out_shape=[jax.ShapeDtypeStruct((heads * dh, s), BF16),
                   jax.ShapeDtypeStruct(w_side.shape, BF16)],
        grid_spec=grid_spec,
        compiler_params=_params(("arbitrary", "arbitrary"),
                                3 * _nbytes((s, dh), BF16) + 2 * Q_PER_STEP * _nbytes((tq, dh), BF16)
                                + _nbytes((side_rows, side_cols), F32)
                                + _nbytes((side_rows, side_cols), BF16),
                                Q_PER_STEP * (2 * _nbytes((tk, tq), F32) + _nbytes((dh + ONES_ROWS, tq), F32))
                                + 3 * _nbytes((tk, tq), F32)),
        name="fox_attention",
    )(cb, thr, qt, k, e, vt, w_side)


def _mix_kernel(a_ref, wa_ref, halo_ref, p_ref, wp_ref, ps_ref, ga_ref, gp_ref, o_ref,
                wa_b, wp_b, *, tm):
    j = pl.program_id(0)
    i = pl.program_id(1)

    @pl.when(i == 0)
    def _():
        wa_b[...] = wa_ref[...].astype(BF16)
        wp_b[...] = wp_ref[...].astype(BF16)

    halo = jnp.where(i == 0, 0.0, halo_ref[...])
    ext = jnp.concatenate([halo, p_ref[...]], axis=0)
    acc = ext
    win = ext
    for g, w in enumerate(POOL_WINDOWS):
        acc = acc + pltpu.roll(acc, w // 2, 0)
        win = jnp.where(j == g, acc, win)
    t = i * tm + lax.broadcasted_iota(jnp.int32, (tm, 1), 0)
    cnt = jnp.minimum(t + 1, jnp.left_shift(2, j)).astype(F32)
    pooled = (win[POOL_HALO:] / cnt - ext[POOL_HALO:]).astype(BF16)

    a_t = a_ref[...]
    cols = o_ref.shape[1] // MIX_CHUNKS
    for c in range(0, o_ref.shape[1], cols):
        ya = lax.dot_general(a_t, wa_b[:, c:c + cols], _TN, preferred_element_type=F32)
        yp = jnp.dot(pooled, wp_b[:, c:c + cols], preferred_element_type=F32) * ps_ref[:, c:c + cols]
        o_ref[:, c:c + cols] = (ga_ref[:, c:c + cols].astype(F32) * ya
                                + gp_ref[:, c:c + cols].astype(F32) * yp).astype(o_ref.dtype)


def _mix(attn_t, w_a, p, w_pool, pool_scale, gates, *, tm=512):
    fw, s = attn_t.shape
    groups, gd, tn = w_pool.shape
    assert POOL_WINDOWS == tuple(2 << g for g in range(groups))
    d = w_a.shape[1]
    hb = tm // POOL_HALO
    return pl.pallas_call(
        functools.partial(_mix_kernel, tm=tm),
        out_shape=jax.ShapeDtypeStruct((s, d), BF16),
        grid=(groups, s // tm),
        in_specs=[pl.BlockSpec((fw, tm), lambda j, i: (0, i)),
                  pl.BlockSpec((fw, tn), lambda j, i: (0, j)),
                  pl.BlockSpec((POOL_HALO, gd), lambda j, i: (jnp.maximum(i * hb - 1, 0), j)),
                  pl.BlockSpec((tm, gd), lambda j, i: (i, j)),
                  pl.BlockSpec((None, gd, tn), lambda j, i: (j, 0, 0)),
                  pl.BlockSpec((1, tn), lambda j, i: (0, j)),
                  pl.BlockSpec((tm, tn), lambda j, i: (i, j)),
                  pl.BlockSpec((tm, tn), lambda j, i: (i, groups + j))],
        out_specs=pl.BlockSpec((tm, tn), lambda j, i: (i, j)),
        scratch_shapes=[pltpu.VMEM((fw, tn), BF16), pltpu.VMEM((gd, tn), BF16)],
        compiler_params=_params(("parallel", "arbitrary"),
                                _nbytes((tm, fw), BF16) + _nbytes((fw, tn), w_a.dtype)
                                + _nbytes((tm + POOL_HALO, gd), F32) + _nbytes((gd, tn), w_pool.dtype)
                                + 3 * _nbytes((tm, tn), BF16),
                                _nbytes((fw + gd, tn), BF16) + 3 * _nbytes((tm, tn), F32)
                                + 4 * _nbytes((tm + POOL_HALO, gd), F32)),
        name="branch_mix",
    )(attn_t, w_a, p, p, w_pool, pool_scale, gates, gates)


def _resid_mm_kernel(a_ref, w_ref, x_ref, g_ref, o_ref, *, chunks):
    w = _as_bf16(w_ref[...])
    rows = a_ref.shape[0] // chunks
    for r in range(0, a_ref.shape[0], rows):
        m = jnp.dot(a_ref[r:r + rows, :], w, preferred_element_type=F32)
        o_ref[r:r + rows, :] = ALPHA * x_ref[r:r + rows, :] + g_ref[...] * m


def _resid_matmul(a, w, x, mod, gate_idx, *, tm, tn, chunks=1, name):
    m, k = a.shape
    n = w.shape[1]
    nb = n // tn
    return pl.pallas_call(
        functools.partial(_resid_mm_kernel, chunks=chunks),
        out_shape=jax.ShapeDtypeStruct((m, n), F32),
        grid=(m // tm, nb),
        in_specs=[pl.BlockSpec((tm, k), lambda i, j: (i, 0)),
                  pl.BlockSpec((k, tn), lambda i, j: (0, j)),
                  pl.BlockSpec((tm, tn), lambda i, j: (i, j)),
                  pl.BlockSpec((1, tn), lambda i, j: (0, gate_idx * nb + j))],
        out_specs=pl.BlockSpec((tm, tn), lambda i, j: (i, j)),
        compiler_params=_params(("parallel", "parallel"),
                                _nbytes((tm, k), BF16) + _nbytes((k, tn), w.dtype)
                                + 2 * _nbytes((tm, tn), F32),
                                _cast_bytes((k, tn), w.dtype) + 2 * _nbytes((tm, tn), F32) // chunks),
        name=name,
    )(a, w, x, mod)


def _ln_ln_mod_kernel(r_ref, g_ref, b_ref, sh_ref, sc_ref, x_ref, u_ref):
    x1 = _ln_rows(r_ref[...]) * g_ref[...] + b_ref[...]
    x_ref[...] = x1
    u_ref[...] = (_ln_rows(x1) * (1.0 + sc_ref[...]) + sh_ref[...]).astype(u_ref.dtype)


def _ln_ln_mod(r, gain, bias, mod, shift_idx, scale_idx, *, tm=256):
    s, d = r.shape
    row = pl.BlockSpec((tm, d), lambda i: (i, 0))
    vec = pl.BlockSpec((1, d), lambda i: (0, 0))
    return pl.pallas_call(
        _ln_ln_mod_kernel,
        out_shape=(jax.ShapeDtypeStruct((s, d), F32), jax.ShapeDtypeStruct((s, d), BF16)),
        grid=(s // tm,),
        in_specs=[row, vec, vec,
                  pl.BlockSpec((1, d), lambda i: (0, shift_idx)),
                  pl.BlockSpec((1, d), lambda i: (0, scale_idx))],
        out_specs=(row, row),
        compiler_params=_params(("parallel",),
                                2 * _nbytes((tm, d), F32) + _nbytes((tm, d), BF16),
                                4 * _nbytes((tm, d), F32)),
        name="ln1_ln_mod",
    )(r, gain, bias, mod, mod)


def _ln_affine_kernel(r_ref, g_ref, b_ref, o_ref):
    o_ref[...] = _ln_rows(r_ref[...]) * g_ref[...] + b_ref[...]


def _ln_affine(r, gain, bias, *, tm=512):
    s, d = r.shape
    row = pl.BlockSpec((tm, d), lambda i: (i, 0))
    vec = pl.BlockSpec((1, d), lambda i: (0, 0))
    return pl.pallas_call(
        _ln_affine_kernel,
        out_shape=jax.ShapeDtypeStruct((s, d), F32),
        grid=(s // tm,),
        in_specs=[row, vec, vec],
        out_specs=row,
        compiler_params=_params(("parallel",), 2 * _nbytes((tm, d), F32),
                                3 * _nbytes((tm, d), F32)),
        name="ln2",
    )(r, gain, bias)


def _ffn_up_kernel(u_ref, wg_ref, wu_ref, o_ref):
    wg = _as_bf16(wg_ref[...])
    wu = _as_bf16(wu_ref[...])
    rows = u_ref.shape[0] // FFN_CHUNKS
    for r in range(0, u_ref.shape[0], rows):
        u = u_ref[r:r + rows, :]
        g = jnp.dot(u, wg, preferred_element_type=F32)
        up = jnp.dot(u, wu, preferred_element_type=F32)
        o_ref[r:r + rows, :] = (g * jax.nn.sigmoid(g) * up).astype(o_ref.dtype)


def _ffn_up(u, w_gate_up, *, tm=2048, tn=256):
    s, d = u.shape
    hidden = w_gate_up.shape[1] // 2
    nb = hidden // tn
    return pl.pallas_call(
        _ffn_up_kernel,
        out_shape=jax.ShapeDtypeStruct((s, hidden), BF16),
        grid=(s // tm, nb),
        in_specs=[pl.BlockSpec((tm, d), lambda i, j: (i, 0)),
                  pl.BlockSpec((d, tn), lambda i, j: (0, j)),
                  pl.BlockSpec((d, tn), lambda i, j: (0, nb + j))],
        out_specs=pl.BlockSpec((tm, tn), lambda i, j: (i, j)),
        compiler_params=_params(("parallel", "parallel"),
                                _nbytes((tm, d), BF16) + 2 * _nbytes((d, tn), w_gate_up.dtype)
                                + _nbytes((tm, tn), BF16),
                                2 * _nbytes((d, tn), BF16) + 2 * _nbytes((tm, tn), F32)),
        name="ffn_up",
    )(u, w_gate_up, w_gate_up)


def kernel(x, c, w_ada, b_ada, w_in, b_forget, w_attn_out, w_pool, pool_scale, w_out,
           ln1_g, ln1_b, w_gate_up, w_down, ln2_g, ln2_b):
    batch, seq, d = x.shape
    assert batch == 1 and w_ada.shape[0] == DEPTH == 1
    fox_w = w_attn_out.shape[1]
    heads = fox_w // FOX_HEAD_DIM
    pool_w = w_pool.shape[1] * w_pool.shape[2]
    o_f = 3 * fox_w
    o_p = o_f + heads
    tk = 512

    xs = x[0]
    w_in_t = jnp.swapaxes(w_in, 1, 2)[0]
    w_f = jnp.pad(w_in_t[o_f:o_p], ((0, LANES - heads), (0, 0))).astype(BF16)
    b_f = jnp.pad(b_forget, ((0, 0), (0, LANES - heads)))

    mod = _ada(c.reshape(d, 1), w_ada[0], b_ada)

    u1, e, c_blk = _ln_mod_fcum(xs, mod, 0, 1, w_f, b_f, heads=heads, tm=tk)
    cb = c_blk[:, 0, :heads].T
    tq = 2 * tk
    qt, sq_q = _matmul_tn(u1, w_in_t, 0, fox_w, tm=tq, tn=512, tk=tq, scale=FOX_HEAD_DIM ** -0.5 * LOG2E,
                          group_norms=True, name="proj_qt")
    k, sq_k, p, gates = _proj_kpg(u1, w_in_t, fox_w, fox_w, o_p, pool_w, 2 * d, tm=2048, tn=512)
    per_head = lambda sq: jnp.sqrt(sq[:, :, 0, :512 // LANES].reshape(sq.shape[0], heads))
    thr = (SKIP_LOG2 + 2.0 * NORM_SLACK * per_head(sq_q) * jnp.max(per_head(sq_k), axis=0)).T
    vt = _matmul_tn(u1, w_in_t, 2 * fox_w, fox_w, tm=1024, tn=512, tk=tk, name="proj_vt")
    attn, w_down_bf16 = _attention(cb, thr, qt, k, e, vt, w_down[0], heads=heads, tk=tk)
    mix = _mix(attn, w_attn_out[0], p, w_pool[0], pool_scale, gates)
    r1 = _resid_matmul(mix, w_out[0], xs, mod, 2, tm=2048, tn=256, chunks=4, name="out_proj")
    x1, u2 = _ln_ln_mod(r1, ln1_g, ln1_b, mod, 3, 4)

    act = _ffn_up(u2, w_gate_up[0])
    r2 = _resid_matmul(act, w_down_bf16, x1, mod, 5, tm=1024, tn=256, chunks=2, name="ffn_down")
    out = _ln_affine(r2, ln2_g, ln2_b)
    return out[None]
```

```python
import functools

import jax
import jax.numpy as jnp
from jax import lax
from jax.experimental import pallas as pl
from jax.experimental.pallas import tpu as pltpu

F32 = jnp.float32
BF16 = jnp.bfloat16

FOX_HEAD_DIM = 128
POOL_WINDOWS = (2, 4, 8, 16)
POOL_HALO = 16
DEPTH = 1
ALPHA = (2 * DEPTH) ** 0.25
LN_EPS = 1e-5
LOG2E = 1.4426950408889634
LANES = 128
SUBLANES = 8
BIAS_PIECES = 3
ONES_ROWS = 16
MM_CHUNKS = 4
MIX_CHUNKS = 4
FFN_CHUNKS = 4
Q_PER_STEP = 4
SKIP_LOG2 = 48.0
NORM_SLACK = 1.02

V7X_VMEM_BYTES = 64 * 1024 * 1024
VMEM_CAP_BYTES = V7X_VMEM_BYTES - 6 * 1024 * 1024

_NT = (((1,), (1,)), ((), ()))
_TN = (((0,), (0,)), ((), ()))


def _nbytes(shape, dtype):
    n = jnp.dtype(dtype).itemsize
    for s in shape:
        n *= s
    return n


def _cast_bytes(shape, dtype):
    return 0 if dtype == BF16 else _nbytes(shape, BF16)


def _params(semantics, pipelined_bytes, resident_bytes=0):
    need = 2 * pipelined_bytes + resident_bytes
    assert need <= VMEM_CAP_BYTES, (need, VMEM_CAP_BYTES)
    return pltpu.CompilerParams(dimension_semantics=semantics,
                                vmem_limit_bytes=min(need + (4 << 20), VMEM_CAP_BYTES))


def _ln_rows(x):
    mu = jnp.mean(x, axis=-1, keepdims=True)
    xc = x - mu
    var = jnp.mean(xc * xc, axis=-1, keepdims=True)
    return xc * lax.rsqrt(var + LN_EPS)


def _as_bf16(w):
    return w if w.dtype == BF16 else w.astype(BF16)


def _split3(v):
    hi = v.astype(BF16)
    r1 = v - hi.astype(F32)
    mid = r1.astype(BF16)
    lo = (r1 - mid.astype(F32)).astype(BF16)
    return hi, mid, lo


def _ada_kernel(c_ref, w_ref, b_ref, o_ref):
    k = pl.program_id(1)

    @pl.when(k == 0)
    def _():
        o_ref[...] = b_ref[...]

    cc = c_ref[...]
    s = cc * jax.nn.sigmoid(cc)
    o_ref[...] += jnp.sum(s * w_ref[...], axis=0, keepdims=True)


def _ada(c_col, w_ada, b_ada, n, *, tk=512, tn=4096):
    d = w_ada.shape[0]
    return pl.pallas_call(
        _ada_kernel,
        out_shape=jax.ShapeDtypeStruct((1, n), F32),
        grid=(n // tn, d // tk),
        in_specs=[pl.BlockSpec((tk, 1), lambda j, k: (k, 0)),
                  pl.BlockSpec((tk, tn), lambda j, k: (k, j)),
                  pl.BlockSpec((1, tn), lambda j, k: (0, j))],
        out_specs=pl.BlockSpec((1, tn), lambda j, k: (0, j)),
        compiler_params=_params(("parallel", "arbitrary"),
                                _nbytes((tk, tn), F32) + _nbytes((tk, LANES), F32),
                                _nbytes((tk, tn), F32)),
        name="ada",
    )(c_col, w_ada, b_ada)


def _ln_mod_fcum_kernel(x_ref, sh_ref, sc_ref, wf_ref, b_ref, u_ref, e_ref, c_ref,
                        carry_ref, f_ref, *, tm, rows, heads):
    @pl.when(pl.program_id(0) == 0)
    def _():
        carry_ref[...] = jnp.zeros_like(carry_ref)

    gain = 1.0 + sc_ref[...]
    shift = sh_ref[...]
    for r in range(0, tm, rows):
        u = (_ln_rows(x_ref[r:r + rows, :]) * gain + shift).astype(BF16)
        u_ref[r:r + rows, :] = u
        f_ref[r:r + rows, :] = lax.dot_general(u, wf_ref[...], _NT, preferred_element_type=F32)
    f = f_ref[...] + b_ref[...]
    ls = (jnp.minimum(f, 0.0) - jnp.log1p(jnp.exp(-jnp.abs(f)))) * LOG2E
    row = lax.broadcasted_iota(jnp.int32, (tm, tm), 0)
    col = lax.broadcasted_iota(jnp.int32, (tm, tm), 1)
    tri = jnp.where(col <= row, 1.0, 0.0).astype(BF16)
    cs = sum(jnp.dot(tri, piece, preferred_element_type=F32) for piece in _split3(ls))
    carry = carry_ref[...]
    c_ref[...] = carry
    carry_ref[...] = carry + cs[tm - 1:tm, :]
    head_lane = lax.broadcasted_iota(jnp.int32, (tm, LANES), 1) < heads
    e = sum(pltpu.roll(jnp.where(head_lane, piece.astype(F32), 0.0), p * heads, 1) if p else
            jnp.where(head_lane, piece.astype(F32), 0.0)
            for p, piece in enumerate(_split3(-cs)))
    e_ref[...] = e.astype(e_ref.dtype)


def _ln_mod_fcum(x, mod, shift_idx, scale_idx, wf, b_row, *, heads, tm, rows=128):
    s, d = x.shape
    nb = s // tm
    assert BIAS_PIECES * heads <= LANES
    vec = lambda idx: pl.BlockSpec((1, d), lambda i: (0, idx))
    const = lambda shape: pl.BlockSpec(shape, lambda i: (0, 0))
    return pl.pallas_call(
        functools.partial(_ln_mod_fcum_kernel, tm=tm, rows=rows, heads=heads),
        out_shape=(jax.ShapeDtypeStruct((s, d), BF16),
                   jax.ShapeDtypeStruct((s, LANES), BF16),
                   jax.ShapeDtypeStruct((nb, 1, LANES), F32)),
        grid=(nb,),
        in_specs=[pl.BlockSpec((tm, d), lambda i: (i, 0)), vec(shift_idx), vec(scale_idx),
                  const((LANES, d)), const((1, LANES))],
        out_specs=(pl.BlockSpec((tm, d), lambda i: (i, 0)),
                   pl.BlockSpec((tm, LANES), lambda i: (i, 0)),
                   pl.BlockSpec((None, 1, LANES), lambda i: (i, 0, 0))),
        scratch_shapes=[pltpu.VMEM((1, LANES), F32), pltpu.VMEM((tm, LANES), F32)],
        compiler_params=_params(("arbitrary",),
                                _nbytes((tm, d), F32) + _nbytes((tm, d), BF16)
                                + _nbytes((tm, LANES), BF16) + _nbytes((d, LANES), BF16),
                                4 * _nbytes((rows, d), F32) + 2 * _nbytes((tm, tm), F32)),
        name="ln_mod_fcum",
    )(x, mod, mod, wf, b_row)


def _wt_spec(tn, k, row0):
    assert row0 % SUBLANES == 0 and tn % SUBLANES == 0
    return pl.BlockSpec((pl.Element(tn), pl.Element(k)),
                        lambda i, j: ((row0 // SUBLANES + j * (tn // SUBLANES)) * SUBLANES, 0))


def _proj_kpg_kernel(a_ref, w_ref, k_ref, nrm_ref, p_ref, g_ref, *, nk, npool):
    j = pl.program_id(1)
    w = _as_bf16(w_ref[...])
    tm = a_ref.shape[0]
    chunk = tm // MM_CHUNKS

    def tiles():
        for r in range(0, tm, chunk):
            yield slice(r, r + chunk), lax.dot_general(a_ref[r:r + chunk, :], w, _NT,
                                                       preferred_element_type=F32)

    @pl.when(j < nk)
    def _():
        lane = lax.broadcasted_iota(jnp.int32, (1, LANES), 1)
        nrm = jnp.zeros((1, LANES), F32)
        for rows, acc in tiles():
            k_ref[rows, :] = acc.astype(k_ref.dtype)
            for g in range(acc.shape[1] // LANES):
                sq = jnp.sum(jnp.square(acc[:, g * LANES:(g + 1) * LANES]), axis=1, keepdims=True)
                nrm = jnp.maximum(nrm, jnp.where(lane == g, jnp.max(sq, axis=0, keepdims=True), 0.0))
        nrm_ref[...] = nrm

    @pl.when((j >= nk) & (j < nk + npool))
    def _():
        for rows, acc in tiles():
            p_ref[rows, :] = acc

    @pl.when(j >= nk + npool)
    def _():
        for rows, acc in tiles():
            g_ref[rows, :] = (0.5 * jnp.tanh(0.5 * acc) + 0.5).astype(g_ref.dtype)


def _proj_kpg(a, wt, k_row0, k_n, pg_row0, pool_n, gate_n, *, tm, tn):
    m, kdim = a.shape
    nk, npool, ngate = k_n // tn, pool_n // tn, gate_n // tn
    assert k_row0 % SUBLANES == 0 and pg_row0 % SUBLANES == 0 and tn % SUBLANES == 0

    def w_rows(i, j):
        blk = jnp.where(j < nk, k_row0 // SUBLANES + j * (tn // SUBLANES),
                        pg_row0 // SUBLANES + (j - nk) * (tn // SUBLANES))
        return blk * SUBLANES, 0

    k_idx = lambda i, j: (i, jnp.minimum(j, nk - 1))
    return pl.pallas_call(
        functools.partial(_proj_kpg_kernel, nk=nk, npool=npool),
        out_shape=[jax.ShapeDtypeStruct((m, k_n), BF16),
                   jax.ShapeDtypeStruct((m // tm, nk, 1, LANES), F32),
                   jax.ShapeDtypeStruct((m, pool_n), F32),
                   jax.ShapeDtypeStruct((m, gate_n), BF16)],
        grid=(m // tm, nk + npool + ngate),
        in_specs=[pl.BlockSpec((tm, kdim), lambda i, j: (i, 0), pipeline_mode=pl.Buffered(1)),
                  pl.BlockSpec((pl.Element(tn), pl.Element(kdim)), w_rows)],
        out_specs=[pl.BlockSpec((tm, tn), k_idx),
                   pl.BlockSpec((None, None, 1, LANES), lambda i, j: (*k_idx(i, j), 0, 0)),
                   pl.BlockSpec((tm, tn), lambda i, j: (i, jnp.clip(j - nk, 0, npool - 1))),
                   pl.BlockSpec((tm, tn), lambda i, j: (i, jnp.maximum(j - nk - npool, 0)))],
        compiler_params=_params(("arbitrary", "arbitrary"),
                                _nbytes((tn, kdim), wt.dtype) + 2 * _nbytes((tm, tn), BF16)
                                + _nbytes((tm, tn), F32),
                                _nbytes((tm, kdim), a.dtype) + _cast_bytes((tn, kdim), wt.dtype)
                                + 3 * _nbytes((tm // MM_CHUNKS, tn), F32)),
        name="proj_kpg",
    )(a, wt)


def _mm_tn_kernel(a_ref, w_ref, o_ref, *rest, scale):
    acc_t = lax.dot_general(_as_bf16(w_ref[...]), a_ref[...], _NT,
                            preferred_element_type=F32)
    if scale is not None:
        acc_t = acc_t * scale
    tk = o_ref.shape[2]
    for kb in range(o_ref.shape[0]):
        o_ref[kb] = acc_t[:, kb * tk:(kb + 1) * tk].astype(o_ref.dtype)
    if rest:
        lane = lax.broadcasted_iota(jnp.int32, (1, LANES), 1)
        nrm = jnp.zeros((1, LANES), F32)
        for g in range(acc_t.shape[0] // LANES):
            sq = jnp.sum(jnp.square(acc_t[g * LANES:(g + 1) * LANES, :]), axis=0, keepdims=True)
            nrm = jnp.where(lane == g, jnp.max(sq, axis=1, keepdims=True), nrm)
        rest[0][...] = nrm


def _matmul_tn(a, wt, row0, n, *, tm, tn, tk, scale=None, group_norms=False, name):
    m, k = a.shape
    out_shape = [jax.ShapeDtypeStruct((m // tk, n, tk), BF16)]
    out_specs = [pl.BlockSpec((tm // tk, tn, tk), lambda i, j: (i, j, 0))]
    if group_norms:
        out_shape.append(jax.ShapeDtypeStruct((m // tm, n // tn, 1, LANES), F32))
        out_specs.append(pl.BlockSpec((None, None, 1, LANES), lambda i, j: (i, j, 0, 0)))
    out = pl.pallas_call(
        functools.partial(_mm_tn_kernel, scale=scale),
        out_shape=out_shape,
        grid=(m // tm, n // tn),
        in_specs=[pl.BlockSpec((tm, k), lambda i, j: (i, 0)), _wt_spec(tn, k, row0)],
        out_specs=out_specs,
        compiler_params=_params(("parallel", "parallel"),
                                _nbytes((tm, k), a.dtype) + _nbytes((tn, k), wt.dtype)
                                + _nbytes((tm, tn), BF16),
                                _cast_bytes((tn, k), wt.dtype) + 3 * _nbytes((tm, tn), F32)),
        name=name,
    )(a, wt)
    return out if group_norms else out[0]


def _attn_kernel(cb_ref, thr_ref, q_ref, k_ref, e_ref, vt_ref, w_ref, o_ref, wb_ref, *scratch,
                 tk, tn, heads, side_blocks):
    h = pl.program_id(0)

    @pl.when(h * pl.num_programs(1) + pl.program_id(1) < side_blocks)
    def _():
        wb_ref[...] = w_ref[...].astype(wb_ref.dtype)

    blocks = [Q_PER_STEP * pl.program_id(1) + sub for sub in range(Q_PER_STEP)]
    firsts = [lax.while_loop(lambda p, i=i: (p < i) & (cb_ref[h, 2 * p + 2] - cb_ref[h, 2 * i]
                                                        > thr_ref[h, i]),
                             lambda p: p + 1, jnp.int32(0)) for i in blocks]
    for sub, (i, first) in enumerate(zip(blocks, firsts)):
        _attn_block(cb_ref, q_ref.at[sub], k_ref, e_ref, vt_ref,
                    o_ref.at[:, sub * q_ref.shape[2]:(sub + 1) * q_ref.shape[2]],
                    *scratch[3 * sub:3 * sub + 3], h=h, i=i, first=first, tk=tk, tn=tn, heads=heads)


def _attn_block(cb_ref, q_ref, k_ref, e_ref, vt_ref, o_ref, s0_ref, s1_ref, acc_ref, *,
                h, i, first, tk, tn, heads):
    dh, tq = q_ref.shape
    nt = tq // tn
    half = nt // 2
    r = lax.broadcasted_iota(jnp.int32, (dh, tq), 0) - h
    ones_rows = sum(jnp.where(r == p * heads, 1.0, 0.0) for p in range(BIAS_PIECES))
    qt_aug = jnp.concatenate([q_ref[...], ones_rows.astype(BF16)], axis=0)
    ones_v = jnp.ones((acc_ref.shape[0] - dh, tk), BF16)
    c_q = cb_ref[h, 2 * i]
    acc_ref[...] = jnp.zeros_like(acc_ref)

    def logits_to(buf_ref, j, first_tile=0):
        rows = pl.ds(pl.multiple_of(j * tk, tk), tk)
        k_aug = jnp.concatenate([k_ref[rows, :], e_ref[rows, :]], axis=1)
        cms = []
        for n in range(first_tile, nt):
            cols = slice(n * tn, (n + 1) * tn)
            s = jnp.dot(k_aug, qt_aug[:, cols], preferred_element_type=F32)
            buf_ref[:, cols] = s
            cms.append(jnp.max(s, axis=0, keepdims=True))
        return jnp.concatenate(cms, axis=1)

    def causal(buf_ref, n, first_tile):
        key = lax.broadcasted_iota(jnp.int32, (tk, tn), 0)
        qry = lax.broadcasted_iota(jnp.int32, (tk, tn), 1) + (n - first_tile) * tn
        return jnp.where(key <= qry, buf_ref[:, n * tn:(n + 1) * tn], -jnp.inf)

    def softmax_pv(buf_ref, j, cmax, m, first_tile=0, causal_tiles=0):
        lo = first_tile * tn
        c = c_q - cb_ref[h, j]
        m_old = m[:, lo:]
        m_new = jnp.maximum(m_old, cmax + c)
        a = jnp.exp2(m_old - m_new)
        off = m_new - c
        v_aug = jnp.concatenate([vt_ref[j], ones_v], axis=0)
        for n in range(first_tile, nt):
            cols = slice(n * tn, (n + 1) * tn)
            rel = slice(n * tn - lo, (n + 1) * tn - lo)
            s = causal(buf_ref, n, first_tile) if n < first_tile + causal_tiles else buf_ref[:, cols]
            p = jnp.exp2(s - off[:, rel]).astype(BF16)
            acc_ref[:, cols] = a[:, rel] * acc_ref[:, cols] + jnp.dot(
                v_aug, p, preferred_element_type=F32)
        if first_tile:
            m_new = jnp.concatenate([m[:, :lo], m_new], axis=1)
        return m_new

    def causal_max(buf_ref, first_tile):
        return [jnp.max(causal(buf_ref, n, first_tile), axis=0, keepdims=True)
                for n in range(first_tile, first_tile + half)]

    def pair(p, carry):
        m, cm0 = carry
        cm1 = logits_to(s1_ref, 2 * p + 1)
        m = softmax_pv(s0_ref, 2 * p, cm0, m)
        cm0 = logits_to(s0_ref, 2 * p + 2)
        m = softmax_pv(s1_ref, 2 * p + 1, cm1, m)
        return m, cm0

    def two_pairs(q, carry):
        p = start + 2 * q
        return pair(p + 1, pair(p, carry))

    start = first + (i - first) % 2
    carry = (jnp.full((1, tq), -jnp.inf, F32), logits_to(s0_ref, 2 * first))
    carry = lax.fori_loop(first, start, pair, carry)
    m, cm0 = lax.fori_loop(0, (i - start) // 2, two_pairs, carry)
    logits_to(s1_ref, 2 * i + 1, first_tile=half)
    cm = jnp.concatenate(causal_max(s0_ref, 0) + [cm0[:, half * tn:]], axis=1)
    m = softmax_pv(s0_ref, 2 * i, cm, m, causal_tiles=half)
    cm = jnp.concatenate(causal_max(s1_ref, half), axis=1)
    softmax_pv(s1_ref, 2 * i + 1, cm, m, first_tile=half, causal_tiles=half)
    o_ref[...] = (acc_ref[:dh, :] / acc_ref[dh:dh + 1, :]).astype(o_ref.dtype)


def _attention(cb, thr, qt, k, e, vt, w_side, *, heads, tk, tn=256, side_blocks=32):
    s = k.shape[0]
    dh = FOX_HEAD_DIM
    tq = 2 * tk
    steps = s // (Q_PER_STEP * tq)
    side_cols = w_side.shape[1] // 2
    side_rows = 2 * w_side.shape[0] // side_blocks
    assert side_rows * side_blocks == 2 * w_side.shape[0] and side_rows % (2 * SUBLANES) == 0
    assert side_cols % LANES == 0 and side_blocks <= heads * steps

    def side(h, i, *_):
        blk = jnp.minimum(h * steps + i, side_blocks - 1)
        return blk // 2, blk % 2

    grid_spec = pltpu.PrefetchScalarGridSpec(
        num_scalar_prefetch=2,
        grid=(heads, steps),
        in_specs=[pl.BlockSpec((Q_PER_STEP, dh, tq), lambda h, i, *_: (i, h, 0)),
                  pl.BlockSpec((s, dh), lambda h, i, *_: (0, h)),
                  pl.BlockSpec((s, LANES), lambda h, i, *_: (0, 0)),
                  pl.BlockSpec((s // tk, dh, tk), lambda h, i, *_: (0, h, 0)),
                  pl.BlockSpec((side_rows, side_cols), side)],
        out_specs=[pl.BlockSpec((dh, Q_PER_STEP * tq), lambda h, i, *_: (h, i)),
                   pl.BlockSpec((side_rows, side_cols), side)],
        scratch_shapes=Q_PER_STEP * [pltpu.VMEM((tk, tq), F32), pltpu.VMEM((tk, tq), F32),
                                     pltpu.VMEM((dh + ONES_ROWS, tq), F32)],
    )
    return pl.pallas_call(
        functools.partial(_attn_kernel, tk=tk, tn=tn, heads=heads, side_blocks=side_blocks),
        out_shape=[jax.ShapeDtypeStruct((heads * dh, s), BF16),
                   jax.ShapeDtypeStruct(w_side.shape, BF16)],
        grid_spec=grid_spec,
        compiler_params=_params(("arbitrary", "arbitrary"),
                                3 * _nbytes((s, dh), BF16) + 2 * Q_PER_STEP * _nbytes((tq, dh), BF16)
                                + _nbytes((side_rows, side_cols), F32)
                                + _nbytes((side_rows, side_cols), BF16),
                                Q_PER_STEP * (2 * _nbytes((tk, tq), F32) + _nbytes((dh + ONES_ROWS, tq), F32))
                                + 3 * _nbytes((tk, tq), F32)),
        name="fox_attention",
    )(cb, thr, qt, k, e, vt, w_side)


def _mix_kernel(a_ref, wa_ref, halo_ref, p_ref, wp_ref, ps_ref, ga_ref, gp_ref, c_ref, wada_ref, bada_ref,
                o_ref, mod_ref, wa_b, wp_b, *, tm):
    j = pl.program_id(0)
    i = pl.program_id(1)

    cc = c_ref[...]
    mod_ref[...] = jnp.sum(cc * jax.nn.sigmoid(cc) * wada_ref[...], axis=0, keepdims=True) + bada_ref[...]

    @pl.when(i == 0)
    def _():
        wa_b[...] = wa_ref[...].astype(BF16)
        wp_b[...] = wp_ref[...].astype(BF16)

    halo = jnp.where(i == 0, 0.0, halo_ref[...])
    ext = jnp.concatenate([halo, p_ref[...]], axis=0)
    acc = ext
    win = ext
    for g, w in enumerate(POOL_WINDOWS):
        acc = acc + pltpu.roll(acc, w // 2, 0)
        win = jnp.where(j == g, acc, win)
    t = i * tm + lax.broadcasted_iota(jnp.int32, (tm, 1), 0)
    cnt = jnp.minimum(t + 1, jnp.left_shift(2, j)).astype(F32)
    pooled = (win[POOL_HALO:] / cnt - ext[POOL_HALO:]).astype(BF16)

    a_t = a_ref[...]
    cols = o_ref.shape[1] // MIX_CHUNKS
    for c in range(0, o_ref.shape[1], cols):
        ya = lax.dot_general(a_t, wa_b[:, c:c + cols], _TN, preferred_element_type=F32)
        yp = jnp.dot(pooled, wp_b[:, c:c + cols], preferred_element_type=F32) * ps_ref[:, c:c + cols]
        o_ref[:, c:c + cols] = (ga_ref[:, c:c + cols].astype(F32) * ya
                                + gp_ref[:, c:c + cols].astype(F32) * yp).astype(o_ref.dtype)


def _mix(attn_t, w_a, p, w_pool, pool_scale, gates, c_col, w_ada, b_ada, ada_col0, *, tm=512):
    fw, s = attn_t.shape
    groups, gd, tn = w_pool.shape
    assert POOL_WINDOWS == tuple(2 << g for g in range(groups))
    d = w_a.shape[1]
    hb = tm // POOL_HALO
    rows = s // tm
    ada_n = w_ada.shape[1] - ada_col0
    ada_tn = ada_n // (groups * rows)
    assert ada_tn * groups * rows == ada_n and ada_tn % LANES == 0 and ada_col0 % ada_tn == 0
    ada_blk = lambda j, i: (0, ada_col0 // ada_tn + j * rows + i)
    return pl.pallas_call(
        functools.partial(_mix_kernel, tm=tm),
        out_shape=(jax.ShapeDtypeStruct((s, d), BF16), jax.ShapeDtypeStruct((1, ada_n), F32)),
        grid=(groups, rows),
        in_specs=[pl.BlockSpec((fw, tm), lambda j, i: (0, i)),
                  pl.BlockSpec((fw, tn), lambda j, i: (0, j)),
                  pl.BlockSpec((POOL_HALO, gd), lambda j, i: (jnp.maximum(i * hb - 1, 0), j)),
                  pl.BlockSpec((tm, gd), lambda j, i: (i, j)),
                  pl.BlockSpec((None, gd, tn), lambda j, i: (j, 0, 0)),
                  pl.BlockSpec((1, tn), lambda j, i: (0, j)),
                  pl.BlockSpec((tm, tn), lambda j, i: (i, j)),
                  pl.BlockSpec((tm, tn), lambda j, i: (i, groups + j)),
                  pl.BlockSpec((d, 1), lambda j, i: (0, 0)),
                  pl.BlockSpec((d, ada_tn), ada_blk),
                  pl.BlockSpec((1, ada_tn), ada_blk)],
        out_specs=(pl.BlockSpec((tm, tn), lambda j, i: (i, j)),
                   pl.BlockSpec((1, ada_tn), lambda j, i: (0, j * rows + i))),
        scratch_shapes=[pltpu.VMEM((fw, tn), BF16), pltpu.VMEM((gd, tn), BF16)],
        compiler_params=_params(("parallel", "arbitrary"),
                                _nbytes((tm, fw), BF16) + _nbytes((fw, tn), w_a.dtype)
                                + _nbytes((tm + POOL_HALO, gd), F32) + _nbytes((gd, tn), w_pool.dtype)
                                + 3 * _nbytes((tm, tn), BF16)
                                + _nbytes((d, ada_tn), F32) + _nbytes((d, LANES), F32),
                                _nbytes((fw + gd, tn), BF16) + 3 * _nbytes((tm, tn // MIX_CHUNKS), F32)
                                + 4 * _nbytes((tm + POOL_HALO, gd), F32)),
        name="branch_mix",
    )(attn_t, w_a, p, p, w_pool, pool_scale, gates, gates, c_col, w_ada, b_ada)


def _resid_mm_kernel(a_ref, w_ref, x_ref, g_ref, o_ref, *, chunks):
    w = _as_bf16(w_ref[...])
    rows = a_ref.shape[0] // chunks
    for r in range(0, a_ref.shape[0], rows):
        m = jnp.dot(a_ref[r:r + rows, :], w, preferred_element_type=F32)
        o_ref[r:r + rows, :] = ALPHA * x_ref[r:r + rows, :] + g_ref[...] * m


def _resid_matmul(a, w, x, mod, gate_idx, *, tm, tn, chunks=1, name):
    m, k = a.shape
    n = w.shape[1]
    nb = n // tn
    return pl.pallas_call(
        functools.partial(_resid_mm_kernel, chunks=chunks),
        out_shape=jax.ShapeDtypeStruct((m, n), F32),
        grid=(m // tm, nb),
        in_specs=[pl.BlockSpec((tm, k), lambda i, j: (i, 0)),
                  pl.BlockSpec((k, tn), lambda i, j: (0, j)),
                  pl.BlockSpec((tm, tn), lambda i, j: (i, j)),
                  pl.BlockSpec((1, tn), lambda i, j: (0, gate_idx * nb + j))],
        out_specs=pl.BlockSpec((tm, tn), lambda i, j: (i, j)),
        compiler_params=_params(("parallel", "parallel"),
                                _nbytes((tm, k), BF16) + _nbytes((k, tn), w.dtype)
                                + 2 * _nbytes((tm, tn), F32),
                                _cast_bytes((k, tn), w.dtype) + 2 * _nbytes((tm, tn), F32) // chunks),
        name=name,
    )(a, w, x, mod)


def _ln_ln_mod_kernel(r_ref, g_ref, b_ref, sh_ref, sc_ref, x_ref, u_ref):
    x1 = _ln_rows(r_ref[...]) * g_ref[...] + b_ref[...]
    x_ref[...] = x1
    u_ref[...] = (_ln_rows(x1) * (1.0 + sc_ref[...]) + sh_ref[...]).astype(u_ref.dtype)


def _ln_ln_mod(r, gain, bias, mod, shift_idx, scale_idx, *, tm=256):
    s, d = r.shape
    row = pl.BlockSpec((tm, d), lambda i: (i, 0))
    vec = pl.BlockSpec((1, d), lambda i: (0, 0))
    return pl.pallas_call(
        _ln_ln_mod_kernel,
        out_shape=(jax.ShapeDtypeStruct((s, d), F32), jax.ShapeDtypeStruct((s, d), BF16)),
        grid=(s // tm,),
        in_specs=[row, vec, vec,
                  pl.BlockSpec((1, d), lambda i: (0, shift_idx)),
                  pl.BlockSpec((1, d), lambda i: (0, scale_idx))],
        out_specs=(row, row),
        compiler_params=_params(("parallel",),
                                2 * _nbytes((tm, d), F32) + _nbytes((tm, d), BF16),
                                4 * _nbytes((tm, d), F32)),
        name="ln1_ln_mod",
    )(r, gain, bias, mod, mod)


def _ln_affine_kernel(r_ref, g_ref, b_ref, o_ref):
    o_ref[...] = _ln_rows(r_ref[...]) * g_ref[...] + b_ref[...]


def _ln_affine(r, gain, bias, *, tm=512):
    s, d = r.shape
    row = pl.BlockSpec((tm, d), lambda i: (i, 0))
    vec = pl.BlockSpec((1, d), lambda i: (0, 0))
    return pl.pallas_call(
        _ln_affine_kernel,
        out_shape=jax.ShapeDtypeStruct((s, d), F32),
        grid=(s // tm,),
        in_specs=[row, vec, vec],
        out_specs=row,
        compiler_params=_params(("parallel",), 2 * _nbytes((tm, d), F32),
                                3 * _nbytes((tm, d), F32)),
        name="ln2",
    )(r, gain, bias)


def _ffn_up_kernel(u_ref, wg_ref, wu_ref, o_ref):
    wg = _as_bf16(wg_ref[...])
    wu = _as_bf16(wu_ref[...])
    rows = u_ref.shape[0] // FFN_CHUNKS
    for r in range(0, u_ref.shape[0], rows):
        u = u_ref[r:r + rows, :]
        g = jnp.dot(u, wg, preferred_element_type=F32)
        up = jnp.dot(u, wu, preferred_element_type=F32)
        o_ref[r:r + rows, :] = (g * jax.nn.sigmoid(g) * up).astype(o_ref.dtype)


def _ffn_up(u, w_gate_up, *, tm=2048, tn=256):
    s, d = u.shape
    hidden = w_gate_up.shape[1] // 2
    nb = hidden // tn
    return pl.pallas_call(
        _ffn_up_kernel,
        out_shape=jax.ShapeDtypeStruct((s, hidden), BF16),
        grid=(s // tm, nb),
        in_specs=[pl.BlockSpec((tm, d), lambda i, j: (i, 0)),
                  pl.BlockSpec((d, tn), lambda i, j: (0, j)),
                  pl.BlockSpec((d, tn), lambda i, j: (0, nb + j))],
        out_specs=pl.BlockSpec((tm, tn), lambda i, j: (i, j)),
        compiler_params=_params(("parallel", "parallel"),
                                _nbytes((tm, d), BF16) + 2 * _nbytes((d, tn), w_gate_up.dtype)
                                + _nbytes((tm, tn), BF16),
                                2 * _nbytes((d, tn), BF16) + 2 * _nbytes((tm, tn), F32)),
        name="ffn_up",
    )(u, w_gate_up, w_gate_up)


def kernel(x, c, w_ada, b_ada, w_in, b_forget, w_attn_out, w_pool, pool_scale, w_out,
           ln1_g, ln1_b, w_gate_up, w_down, ln2_g, ln2_b):
    batch, seq, d = x.shape
    assert batch == 1 and w_ada.shape[0] == DEPTH == 1
    fox_w = w_attn_out.shape[1]
    heads = fox_w // FOX_HEAD_DIM
    pool_w = w_pool.shape[1] * w_pool.shape[2]
    o_f = 3 * fox_w
    o_p = o_f + heads
    tk = 512

    xs = x[0]
    w_in_t = jnp.swapaxes(w_in, 1, 2)[0]
    w_f = jnp.pad(w_in_t[o_f:o_p], ((0, LANES - heads), (0, 0))).astype(BF16)
    b_f = jnp.pad(b_forget, ((0, 0), (0, LANES - heads)))

    c_col = c.reshape(d, 1)
    mod = _ada(c_col, w_ada[0], b_ada, 2 * d)

    u1, e, c_blk = _ln_mod_fcum(xs, mod, 0, 1, w_f, b_f, heads=heads, tm=tk)
    cb = c_blk[:, 0, :heads].T
    tq = 2 * tk
    qt, sq_q = _matmul_tn(u1, w_in_t, 0, fox_w, tm=tq, tn=512, tk=tq, scale=FOX_HEAD_DIM ** -0.5 * LOG2E,
                          group_norms=True, name="proj_qt")
    k, sq_k, p, gates = _proj_kpg(u1, w_in_t, fox_w, fox_w, o_p, pool_w, 2 * d, tm=2048, tn=512)
    per_head = lambda sq: jnp.sqrt(sq[:, :, 0, :512 // LANES].reshape(sq.shape[0], heads))
    thr = (SKIP_LOG2 + 2.0 * NORM_SLACK * per_head(sq_q) * jnp.max(per_head(sq_k), axis=0)).T
    vt = _matmul_tn(u1, w_in_t, 2 * fox_w, fox_w, tm=1024, tn=512, tk=tk, name="proj_vt")
    attn, w_down_bf16 = _attention(cb, thr, qt, k, e, vt, w_down[0], heads=heads, tk=tk)
    mix, mod2 = _mix(attn, w_attn_out[0], p, w_pool[0], pool_scale, gates, c_col, w_ada[0], b_ada, 2 * d)
    r1 = _resid_matmul(mix, w_out[0], xs, mod2, 0, tm=2048, tn=256, chunks=4, name="out_proj")
    x1, u2 = _ln_ln_mod(r1, ln1_g, ln1_b, mod2, 1, 2)

    act = _ffn_up(u2, w_gate_up[0])
    r2 = _resid_matmul(act, w_down_bf16, x1, mod2, 3, tm=512, tn=512, name="ffn_down")
    out = _ln_affine(r2, ln2_g, ln2_b)
    return out[None]
```

```python
import functools

import jax
import jax.numpy as jnp
from jax import lax
from jax.experimental import pallas as pl
from jax.experimental.pallas import tpu as pltpu

F32 = jnp.float32
BF16 = jnp.bfloat16

FOX_HEAD_DIM = 128
POOL_WINDOWS = (2, 4, 8, 16)
POOL_HALO = 16
DEPTH = 1
ALPHA = (2 * DEPTH) ** 0.25
LN_EPS = 1e-5
LOG2E = 1.4426950408889634
LANES = 128
SUBLANES = 8
BIAS_PIECES = 3
ONES_ROWS = 16
MM_CHUNKS = 4
MIX_CHUNKS = 4
FFN_CHUNKS = 4
Q_PER_STEP = 4
SKIP_LOG2 = 48.0
NORM_SLACK = 1.02

V7X_VMEM_BYTES = 64 * 1024 * 1024
VMEM_CAP_BYTES = V7X_VMEM_BYTES - 6 * 1024 * 1024

_NT = (((1,), (1,)), ((), ()))
_TN = (((0,), (0,)), ((), ()))


def _nbytes(shape, dtype):
    n = jnp.dtype(dtype).itemsize
    for s in shape:
        n *= s
    return n


def _cast_bytes(shape, dtype):
    return 0 if dtype == BF16 else _nbytes(shape, BF16)


def _params(semantics, pipelined_bytes, resident_bytes=0):
    need = 2 * pipelined_bytes + resident_bytes
    assert need <= VMEM_CAP_BYTES, (need, VMEM_CAP_BYTES)
    return pltpu.CompilerParams(dimension_semantics=semantics,
                                vmem_limit_bytes=min(need + (4 << 20), VMEM_CAP_BYTES))


def _ln_rows(x):
    mu = jnp.mean(x, axis=-1, keepdims=True)
    xc = x - mu
    var = jnp.mean(xc * xc, axis=-1, keepdims=True)
    return xc * lax.rsqrt(var + LN_EPS)


def _as_bf16(w):
    return w if w.dtype == BF16 else w.astype(BF16)


def _split3(v):
    hi = v.astype(BF16)
    r1 = v - hi.astype(F32)
    mid = r1.astype(BF16)
    lo = (r1 - mid.astype(F32)).astype(BF16)
    return hi, mid, lo


def _ada_kernel(c_ref, w_ref, b_ref, o_ref):
    k = pl.program_id(1)

    @pl.when(k == 0)
    def _():
        o_ref[...] = b_ref[...]

    cc = c_ref[...]
    s = cc * jax.nn.sigmoid(cc)
    o_ref[...] += jnp.sum(s * w_ref[...], axis=0, keepdims=True)


def _ada(c_col, w_ada, b_ada, n, *, tk=512, tn=4096):
    d = w_ada.shape[0]
    return pl.pallas_call(
        _ada_kernel,
        out_shape=jax.ShapeDtypeStruct((1, n), F32),
        grid=(n // tn, d // tk),
        in_specs=[pl.BlockSpec((tk, 1), lambda j, k: (k, 0)),
                  pl.BlockSpec((tk, tn), lambda j, k: (k, j)),
                  pl.BlockSpec((1, tn), lambda j, k: (0, j))],
        out_specs=pl.BlockSpec((1, tn), lambda j, k: (0, j)),
        compiler_params=_params(("parallel", "arbitrary"),
                                _nbytes((tk, tn), F32) + _nbytes((tk, LANES), F32),
                                _nbytes((tk, tn), F32)),
        name="ada",
    )(c_col, w_ada, b_ada)


def _ln_mod_fcum_kernel(x_ref, sh_ref, sc_ref, wf_ref, b_ref, u_ref, e_ref, c_ref,
                        carry_ref, f_ref, *, tm, rows, heads):
    @pl.when(pl.program_id(0) == 0)
    def _():
        carry_ref[...] = jnp.zeros_like(carry_ref)

    gain = 1.0 + sc_ref[...]
    shift = sh_ref[...]
    for r in range(0, tm, rows):
        u = (_ln_rows(x_ref[r:r + rows, :]) * gain + shift).astype(BF16)
        u_ref[r:r + rows, :] = u
        f_ref[r:r + rows, :] = lax.dot_general(u, wf_ref[...], _NT, preferred_element_type=F32)
    f = f_ref[...] + b_ref[...]
    ls = (jnp.minimum(f, 0.0) - jnp.log1p(jnp.exp(-jnp.abs(f)))) * LOG2E
    row = lax.broadcasted_iota(jnp.int32, (tm, tm), 0)
    col = lax.broadcasted_iota(jnp.int32, (tm, tm), 1)
    tri = jnp.where(col <= row, 1.0, 0.0).astype(BF16)
    cs = sum(jnp.dot(tri, piece, preferred_element_type=F32) for piece in _split3(ls))
    carry = carry_ref[...]
    c_ref[...] = carry
    carry_ref[...] = carry + cs[tm - 1:tm, :]
    head_lane = lax.broadcasted_iota(jnp.int32, (tm, LANES), 1) < heads
    e = sum(pltpu.roll(jnp.where(head_lane, piece.astype(F32), 0.0), p * heads, 1) if p else
            jnp.where(head_lane, piece.astype(F32), 0.0)
            for p, piece in enumerate(_split3(-cs)))
    e_ref[...] = e.astype(e_ref.dtype)


def _ln_mod_fcum(x, mod, shift_idx, scale_idx, wf, b_row, *, heads, tm, rows=128):
    s, d = x.shape
    nb = s // tm
    assert BIAS_PIECES * heads <= LANES
    vec = lambda idx: pl.BlockSpec((1, d), lambda i: (0, idx))
    const = lambda shape: pl.BlockSpec(shape, lambda i: (0, 0))
    return pl.pallas_call(
        functools.partial(_ln_mod_fcum_kernel, tm=tm, rows=rows, heads=heads),
        out_shape=(jax.ShapeDtypeStruct((s, d), BF16),
                   jax.ShapeDtypeStruct((s, LANES), BF16),
                   jax.ShapeDtypeStruct((nb, 1, LANES), F32)),
        grid=(nb,),
        in_specs=[pl.BlockSpec((tm, d), lambda i: (i, 0)), vec(shift_idx), vec(scale_idx),
                  const((LANES, d)), const((1, LANES))],
        out_specs=(pl.BlockSpec((tm, d), lambda i: (i, 0)),
                   pl.BlockSpec((tm, LANES), lambda i: (i, 0)),
                   pl.BlockSpec((None, 1, LANES), lambda i: (i, 0, 0))),
        scratch_shapes=[pltpu.VMEM((1, LANES), F32), pltpu.VMEM((tm, LANES), F32)],
        compiler_params=_params(("arbitrary",),
                                _nbytes((tm, d), F32) + _nbytes((tm, d), BF16)
                                + _nbytes((tm, LANES), BF16) + _nbytes((d, LANES), BF16),
                                4 * _nbytes((rows, d), F32) + 2 * _nbytes((tm, tm), F32)),
        name="ln_mod_fcum",
    )(x, mod, mod, wf, b_row)


def _wt_spec(tn, k, row0):
    assert row0 % SUBLANES == 0 and tn % SUBLANES == 0
    return pl.BlockSpec((pl.Element(tn), pl.Element(k)),
                        lambda i, j: ((row0 // SUBLANES + j * (tn // SUBLANES)) * SUBLANES, 0))


def _proj_kpg_kernel(a_ref, w_ref, k_ref, nrm_ref, p_ref, g_ref, *, nk, npool):
    j = pl.program_id(1)
    w = _as_bf16(w_ref[...])
    tm = a_ref.shape[0]
    chunk = tm // MM_CHUNKS

    def tiles():
        for r in range(0, tm, chunk):
            yield slice(r, r + chunk), lax.dot_general(a_ref[r:r + chunk, :], w, _NT,
                                                       preferred_element_type=F32)

    @pl.when(j < nk)
    def _():
        lane = lax.broadcasted_iota(jnp.int32, (1, LANES), 1)
        nrm = jnp.zeros((1, LANES), F32)
        for rows, acc in tiles():
            k_ref[rows, :] = acc.astype(k_ref.dtype)
            for g in range(acc.shape[1] // LANES):
                sq = jnp.sum(jnp.square(acc[:, g * LANES:(g + 1) * LANES]), axis=1, keepdims=True)
                nrm = jnp.maximum(nrm, jnp.where(lane == g, jnp.max(sq, axis=0, keepdims=True), 0.0))
        nrm_ref[...] = nrm

    @pl.when((j >= nk) & (j < nk + npool))
    def _():
        for rows, acc in tiles():
            p_ref[rows, :] = acc

    @pl.when(j >= nk + npool)
    def _():
        for rows, acc in tiles():
            g_ref[rows, :] = (0.5 * jnp.tanh(0.5 * acc) + 0.5).astype(g_ref.dtype)


def _proj_kpg(a, wt, k_row0, k_n, pg_row0, pool_n, gate_n, *, tm, tn):
    m, kdim = a.shape
    nk, npool, ngate = k_n // tn, pool_n // tn, gate_n // tn
    assert k_row0 % SUBLANES == 0 and pg_row0 % SUBLANES == 0 and tn % SUBLANES == 0

    def w_rows(i, j):
        blk = jnp.where(j < nk, k_row0 // SUBLANES + j * (tn // SUBLANES),
                        pg_row0 // SUBLANES + (j - nk) * (tn // SUBLANES))
        return blk * SUBLANES, 0

    k_idx = lambda i, j: (i, jnp.minimum(j, nk - 1))
    return pl.pallas_call(
        functools.partial(_proj_kpg_kernel, nk=nk, npool=npool),
        out_shape=[jax.ShapeDtypeStruct((m, k_n), BF16),
                   jax.ShapeDtypeStruct((m // tm, nk, 1, LANES), F32),
                   jax.ShapeDtypeStruct((m, pool_n), F32),
                   jax.ShapeDtypeStruct((m, gate_n), BF16)],
        grid=(m // tm, nk + npool + ngate),
        in_specs=[pl.BlockSpec((tm, kdim), lambda i, j: (i, 0), pipeline_mode=pl.Buffered(1)),
                  pl.BlockSpec((pl.Element(tn), pl.Element(kdim)), w_rows)],
        out_specs=[pl.BlockSpec((tm, tn), k_idx),
                   pl.BlockSpec((None, None, 1, LANES), lambda i, j: (*k_idx(i, j), 0, 0)),
                   pl.BlockSpec((tm, tn), lambda i, j: (i, jnp.clip(j - nk, 0, npool - 1))),
                   pl.BlockSpec((tm, tn), lambda i, j: (i, jnp.maximum(j - nk - npool, 0)))],
        compiler_params=_params(("arbitrary", "arbitrary"),
                                _nbytes((tn, kdim), wt.dtype) + 2 * _nbytes((tm, tn), BF16)
                                + _nbytes((tm, tn), F32),
                                _nbytes((tm, kdim), a.dtype) + _cast_bytes((tn, kdim), wt.dtype)
                                + 3 * _nbytes((tm // MM_CHUNKS, tn), F32)),
        name="proj_kpg",
    )(a, wt)


def _mm_tn_kernel(a_ref, w_ref, o_ref, *rest, scale):
    acc_t = lax.dot_general(_as_bf16(w_ref[...]), a_ref[...], _NT,
                            preferred_element_type=F32)
    if scale is not None:
        acc_t = acc_t * scale
    tk = o_ref.shape[2]
    for kb in range(o_ref.shape[0]):
        o_ref[kb] = acc_t[:, kb * tk:(kb + 1) * tk].astype(o_ref.dtype)
    if rest:
        lane = lax.broadcasted_iota(jnp.int32, (1, LANES), 1)
        nrm = jnp.zeros((1, LANES), F32)
        for g in range(acc_t.shape[0] // LANES):
            sq = jnp.sum(jnp.square(acc_t[g * LANES:(g + 1) * LANES, :]), axis=0, keepdims=True)
            nrm = jnp.where(lane == g, jnp.max(sq, axis=1, keepdims=True), nrm)
        rest[0][...] = nrm


def _matmul_tn(a, wt, row0, n, *, tm, tn, tk, scale=None, group_norms=False, name):
    m, k = a.shape
    out_shape = [jax.ShapeDtypeStruct((m // tk, n, tk), BF16)]
    out_specs = [pl.BlockSpec((tm // tk, tn, tk), lambda i, j: (i, j, 0))]
    if group_norms:
        out_shape.append(jax.ShapeDtypeStruct((m // tm, n // tn, 1, LANES), F32))
        out_specs.append(pl.BlockSpec((None, None, 1, LANES), lambda i, j: (i, j, 0, 0)))
    out = pl.pallas_call(
        functools.partial(_mm_tn_kernel, scale=scale),
        out_shape=out_shape,
        grid=(m // tm, n // tn),
        in_specs=[pl.BlockSpec((tm, k), lambda i, j: (i, 0)), _wt_spec(tn, k, row0)],
        out_specs=out_specs,
        compiler_params=_params(("parallel", "parallel"),
                                _nbytes((tm, k), a.dtype) + _nbytes((tn, k), wt.dtype)
                                + _nbytes((tm, tn), BF16),
                                _cast_bytes((tn, k), wt.dtype) + 3 * _nbytes((tm, tn), F32)),
        name=name,
    )(a, wt)
    return out if group_norms else out[0]


def _attn_kernel(cb_ref, thr_ref, q_ref, k_ref, e_ref, vt_ref, w_ref, o_ref, wb_ref, *scratch,
                 tk, tn, heads, side_blocks):
    h = pl.program_id(0)

    @pl.when(h * pl.num_programs(1) + pl.program_id(1) < side_blocks)
    def _():
        wb_ref[...] = w_ref[...].astype(wb_ref.dtype)

    blocks = [Q_PER_STEP * pl.program_id(1) + sub for sub in range(Q_PER_STEP)]
    firsts = [lax.while_loop(lambda p, i=i: (p < i) & (cb_ref[h, 2 * p + 2] - cb_ref[h, 2 * i]
                                                        > thr_ref[h, i]),
                             lambda p: p + 1, jnp.int32(0)) for i in blocks]
    for sub, (i, first) in enumerate(zip(blocks, firsts)):
        _attn_block(cb_ref, q_ref.at[sub], k_ref, e_ref, vt_ref,
                    o_ref.at[:, sub * q_ref.shape[2]:(sub + 1) * q_ref.shape[2]],
                    *scratch[3 * sub:3 * sub + 3], h=h, i=i, first=first, tk=tk, tn=tn, heads=heads)


def _attn_block(cb_ref, q_ref, k_ref, e_ref, vt_ref, o_ref, s0_ref, s1_ref, acc_ref, *,
                h, i, first, tk, tn, heads):
    dh, tq = q_ref.shape
    nt = tq // tn
    half = nt // 2
    r = lax.broadcasted_iota(jnp.int32, (dh, tq), 0) - h
    ones_rows = sum(jnp.where(r == p * heads, 1.0, 0.0) for p in range(BIAS_PIECES))
    qt_aug = jnp.concatenate([q_ref[...], ones_rows.astype(BF16)], axis=0)
    ones_v = jnp.ones((acc_ref.shape[0] - dh, tk), BF16)
    c_q = cb_ref[h, 2 * i]
    acc_ref[...] = jnp.zeros_like(acc_ref)

    def logits_to(buf_ref, j, first_tile=0):
        rows = pl.ds(pl.multiple_of(j * tk, tk), tk)
        k_aug = jnp.concatenate([k_ref[rows, :], e_ref[rows, :]], axis=1)
        cms = []
        for n in range(first_tile, nt):
            cols = slice(n * tn, (n + 1) * tn)
            s = jnp.dot(k_aug, qt_aug[:, cols], preferred_element_type=F32)
            buf_ref[:, cols] = s
            cms.append(jnp.max(s, axis=0, keepdims=True))
        return jnp.concatenate(cms, axis=1)

    def causal(buf_ref, n, first_tile):
        key = lax.broadcasted_iota(jnp.int32, (tk, tn), 0)
        qry = lax.broadcasted_iota(jnp.int32, (tk, tn), 1) + (n - first_tile) * tn
        return jnp.where(key <= qry, buf_ref[:, n * tn:(n + 1) * tn], -jnp.inf)

    def softmax_pv(buf_ref, j, cmax, m, first_tile=0, causal_tiles=0):
        lo = first_tile * tn
        c = c_q - cb_ref[h, j]
        m_old = m[:, lo:]
        m_new = jnp.maximum(m_old, cmax + c)
        a = jnp.exp2(m_old - m_new)
        off = m_new - c
        v_aug = jnp.concatenate([vt_ref[j], ones_v], axis=0)
        for n in range(first_tile, nt):
            cols = slice(n * tn, (n + 1) * tn)
            rel = slice(n * tn - lo, (n + 1) * tn - lo)
            s = causal(buf_ref, n, first_tile) if n < first_tile + causal_tiles else buf_ref[:, cols]
            p = jnp.exp2(s - off[:, rel]).astype(BF16)
            acc_ref[:, cols] = a[:, rel] * acc_ref[:, cols] + jnp.dot(
                v_aug, p, preferred_element_type=F32)
        if first_tile:
            m_new = jnp.concatenate([m[:, :lo], m_new], axis=1)
        return m_new

    def causal_max(buf_ref, first_tile):
        return [jnp.max(causal(buf_ref, n, first_tile), axis=0, keepdims=True)
                for n in range(first_tile, first_tile + half)]

    def pair(p, carry):
        m, cm0 = carry
        cm1 = logits_to(s1_ref, 2 * p + 1)
        m = softmax_pv(s0_ref, 2 * p, cm0, m)
        cm0 = logits_to(s0_ref, 2 * p + 2)
        m = softmax_pv(s1_ref, 2 * p + 1, cm1, m)
        return m, cm0

    def two_pairs(q, carry):
        p = start + 2 * q
        return pair(p + 1, pair(p, carry))

    start = first + (i - first) % 2
    carry = (jnp.full((1, tq), -jnp.inf, F32), logits_to(s0_ref, 2 * first))
    carry = lax.fori_loop(first, start, pair, carry)
    m, cm0 = lax.fori_loop(0, (i - start) // 2, two_pairs, carry)
    logits_to(s1_ref, 2 * i + 1, first_tile=half)
    cm = jnp.concatenate(causal_max(s0_ref, 0) + [cm0[:, half * tn:]], axis=1)
    m = softmax_pv(s0_ref, 2 * i, cm, m, causal_tiles=half)
    cm = jnp.concatenate(causal_max(s1_ref, half), axis=1)
    softmax_pv(s1_ref, 2 * i + 1, cm, m, first_tile=half, causal_tiles=half)
    o_ref[...] = (acc_ref[:dh, :] / acc_ref[dh:dh + 1, :]).astype(o_ref.dtype)


def _attention(cb, thr, qt, k, e, vt, w_side, *, heads, tk, tn=256, side_blocks=32):
    s = k.shape[0]
    dh = FOX_HEAD_DIM
    tq = 2 * tk
    steps = s // (Q_PER_STEP * tq)
    side_cols = w_side.shape[1] // 2
    side_rows = 2 * w_side.shape[0] // side_blocks
    assert side_rows * side_blocks == 2 * w_side.shape[0] and side_rows % (2 * SUBLANES) == 0
    assert side_cols % LANES == 0 and side_blocks <= heads * steps

    def side(h, i, *_):
        blk = jnp.minimum(h * steps + i, side_blocks - 1)
        return blk // 2, blk % 2

    grid_spec = pltpu.PrefetchScalarGridSpec(
        num_scalar_prefetch=2,
        grid=(heads, steps),
        in_specs=[pl.BlockSpec((Q_PER_STEP, dh, tq), lambda h, i, *_: (i, h, 0)),
                  pl.BlockSpec((s, dh), lambda h, i, *_: (0, h)),
                  pl.BlockSpec((s, LANES), lambda h, i, *_: (0, 0)),
                  pl.BlockSpec((s // tk, dh, tk), lambda h, i, *_: (0, h, 0)),
                  pl.BlockSpec((side_rows, side_cols), side)],
        out_specs=[pl.BlockSpec((dh, Q_PER_STEP * tq), lambda h, i, *_: (h, i)),
                   pl.BlockSpec((side_rows, side_cols), side)],
        scratch_shapes=Q_PER_STEP * [pltpu.VMEM((tk, tq), F32), pltpu.VMEM((tk, tq), F32),
                                     pltpu.VMEM((dh + ONES_ROWS, tq), F32)],
    )
    return pl.pallas_call(
        functools.partial(_attn_kernel, tk=tk, tn=tn, heads=heads, side_blocks=side_blocks),
        out_shape=[jax.ShapeDtypeStruct((heads * dh, s), BF16),
                   jax.ShapeDtypeStruct(w_side.shape, BF16)],
        grid_spec=grid_spec,
        compiler_params=_params(("arbitrary", "arbitrary"),
                                3 * _nbytes((s, dh), BF16) + 2 * Q_PER_STEP * _nbytes((tq, dh), BF16)
                                + _nbytes((side_rows, side_cols), F32)
                                + _nbytes((side_rows, side_cols), BF16),
                                Q_PER_STEP * (2 * _nbytes((tk, tq), F32) + _nbytes((dh + ONES_ROWS, tq), F32))
                                + 3 * _nbytes((tk, tq), F32)),
        name="fox_attention",
    )(cb, thr, qt, k, e, vt, w_side)


def _mix_kernel(a_ref, wa_ref, halo_ref, p_ref, wp_ref, ps_ref, ga_ref, gp_ref, c_ref, wada_ref, bada_ref,
                o_ref, mod_ref, wa_b, wp_b, *, tm):
    j = pl.program_id(0)
    i = pl.program_id(1)

    ada_rows = c_ref.shape[0] // MIX_CHUNKS

    def ada_piece(k):
        cc = c_ref[k * ada_rows:(k + 1) * ada_rows, :]
        return jnp.sum(cc * jax.nn.sigmoid(cc) * wada_ref[k * ada_rows:(k + 1) * ada_rows, :],
                       axis=0, keepdims=True)

    @pl.when(i == 0)
    def _():
        wa_b[...] = wa_ref[...].astype(BF16)
        wp_b[...] = wp_ref[...].astype(BF16)

    halo = jnp.where(i == 0, 0.0, halo_ref[...])
    ext = jnp.concatenate([halo, p_ref[...]], axis=0)
    acc = ext
    win = ext
    for g, w in enumerate(POOL_WINDOWS):
        acc = acc + pltpu.roll(acc, w // 2, 0)
        win = jnp.where(j == g, acc, win)
    t = i * tm + lax.broadcasted_iota(jnp.int32, (tm, 1), 0)
    cnt = jnp.minimum(t + 1, jnp.left_shift(2, j)).astype(F32)
    pooled = (win[POOL_HALO:] / cnt - ext[POOL_HALO:]).astype(BF16)

    a_t = a_ref[...]
    cols = o_ref.shape[1] // MIX_CHUNKS
    mod_acc = bada_ref[...]
    for n, c in enumerate(range(0, o_ref.shape[1], cols)):
        ya = lax.dot_general(a_t, wa_b[:, c:c + cols], _TN, preferred_element_type=F32)
        yp = jnp.dot(pooled, wp_b[:, c:c + cols], preferred_element_type=F32) * ps_ref[:, c:c + cols]
        mod_acc = mod_acc + ada_piece(n)
        o_ref[:, c:c + cols] = (ga_ref[:, c:c + cols].astype(F32) * ya
                                + gp_ref[:, c:c + cols].astype(F32) * yp).astype(o_ref.dtype)
    mod_ref[...] = mod_acc


def _mix(attn_t, w_a, p, w_pool, pool_scale, gates, c_col, w_ada, b_ada, ada_col0, *, tm=512):
    fw, s = attn_t.shape
    groups, gd, tn = w_pool.shape
    assert POOL_WINDOWS == tuple(2 << g for g in range(groups))
    d = w_a.shape[1]
    hb = tm // POOL_HALO
    rows = s // tm
    ada_n = w_ada.shape[1] - ada_col0
    ada_tn = ada_n // (groups * rows)
    assert ada_tn * groups * rows == ada_n and ada_tn % LANES == 0 and ada_col0 % ada_tn == 0
    ada_blk = lambda j, i: (0, ada_col0 // ada_tn + j * rows + i)
    return pl.pallas_call(
        functools.partial(_mix_kernel, tm=tm),
        out_shape=(jax.ShapeDtypeStruct((s, d), BF16), jax.ShapeDtypeStruct((1, ada_n), F32)),
        grid=(groups, rows),
        in_specs=[pl.BlockSpec((fw, tm), lambda j, i: (0, i)),
                  pl.BlockSpec((fw, tn), lambda j, i: (0, j)),
                  pl.BlockSpec((POOL_HALO, gd), lambda j, i: (jnp.maximum(i * hb - 1, 0), j)),
                  pl.BlockSpec((tm, gd), lambda j, i: (i, j)),
                  pl.BlockSpec((None, gd, tn), lambda j, i: (j, 0, 0)),
                  pl.BlockSpec((1, tn), lambda j, i: (0, j)),
                  pl.BlockSpec((tm, tn), lambda j, i: (i, j)),
                  pl.BlockSpec((tm, tn), lambda j, i: (i, groups + j)),
                  pl.BlockSpec((d, 1), lambda j, i: (0, 0)),
                  pl.BlockSpec((d, ada_tn), ada_blk),
                  pl.BlockSpec((1, ada_tn), ada_blk)],
        out_specs=(pl.BlockSpec((tm, tn), lambda j, i: (i, j)),
                   pl.BlockSpec((1, ada_tn), lambda j, i: (0, j * rows + i))),
        scratch_shapes=[pltpu.VMEM((fw, tn), BF16), pltpu.VMEM((gd, tn), BF16)],
        compiler_params=_params(("parallel", "arbitrary"),
                                _nbytes((tm, fw), BF16) + _nbytes((fw, tn), w_a.dtype)
                                + _nbytes((tm + POOL_HALO, gd), F32) + _nbytes((gd, tn), w_pool.dtype)
                                + 3 * _nbytes((tm, tn), BF16)
                                + _nbytes((d, ada_tn), F32) + _nbytes((d, LANES), F32),
                                _nbytes((fw + gd, tn), BF16) + 3 * _nbytes((tm, tn // MIX_CHUNKS), F32)
                                + 4 * _nbytes((tm + POOL_HALO, gd), F32)),
        name="branch_mix",
    )(attn_t, w_a, p, p, w_pool, pool_scale, gates, gates, c_col, w_ada, b_ada)


def _resid_mm_kernel(a_ref, w_ref, x_ref, g_ref, o_ref, *, chunks):
    w = _as_bf16(w_ref[...])
    rows = a_ref.shape[0] // chunks
    for r in range(0, a_ref.shape[0], rows):
        m = jnp.dot(a_ref[r:r + rows, :], w, preferred_element_type=F32)
        o_ref[r:r + rows, :] = ALPHA * x_ref[r:r + rows, :] + g_ref[...] * m


def _resid_matmul(a, w, x, mod, gate_idx, *, tm, tn, chunks=1, name):
    m, k = a.shape
    n = w.shape[1]
    nb = n // tn
    return pl.pallas_call(
        functools.partial(_resid_mm_kernel, chunks=chunks),
        out_shape=jax.ShapeDtypeStruct((m, n), F32),
        grid=(m // tm, nb),
        in_specs=[pl.BlockSpec((tm, k), lambda i, j: (i, 0)),
                  pl.BlockSpec((k, tn), lambda i, j: (0, j)),
                  pl.BlockSpec((tm, tn), lambda i, j: (i, j)),
                  pl.BlockSpec((1, tn), lambda i, j: (0, gate_idx * nb + j))],
        out_specs=pl.BlockSpec((tm, tn), lambda i, j: (i, j)),
        compiler_params=_params(("parallel", "parallel"),
                                _nbytes((tm, k), BF16) + _nbytes((k, tn), w.dtype)
                                + 2 * _nbytes((tm, tn), F32),
                                _cast_bytes((k, tn), w.dtype) + 2 * _nbytes((tm, tn), F32) // chunks),
        name=name,
    )(a, w, x, mod)


def _ln_ln_mod_kernel(r_ref, g_ref, b_ref, sh_ref, sc_ref, x_ref, u_ref):
    x1 = _ln_rows(r_ref[...]) * g_ref[...] + b_ref[...]
    x_ref[...] = x1
    u_ref[...] = (_ln_rows(x1) * (1.0 + sc_ref[...]) + sh_ref[...]).astype(u_ref.dtype)


def _ln_ln_mod(r, gain, bias, mod, shift_idx, scale_idx, *, tm=256):
    s, d = r.shape
    row = pl.BlockSpec((tm, d), lambda i: (i, 0))
    vec = pl.BlockSpec((1, d), lambda i: (0, 0))
    return pl.pallas_call(
        _ln_ln_mod_kernel,
        out_shape=(jax.ShapeDtypeStruct((s, d), F32), jax.ShapeDtypeStruct((s, d), BF16)),
        grid=(s // tm,),
        in_specs=[row, vec, vec,
                  pl.BlockSpec((1, d), lambda i: (0, shift_idx)),
                  pl.BlockSpec((1, d), lambda i: (0, scale_idx))],
        out_specs=(row, row),
        compiler_params=_params(("parallel",),
                                2 * _nbytes((tm, d), F32) + _nbytes((tm, d), BF16),
                                4 * _nbytes((tm, d), F32)),
        name="ln1_ln_mod",
    )(r, gain, bias, mod, mod)


def _ln_affine_kernel(r_ref, g_ref, b_ref, o_ref):
    o_ref[...] = _ln_rows(r_ref[...]) * g_ref[...] + b_ref[...]


def _ln_affine(r, gain, bias, *, tm=512):
    s, d = r.shape
    row = pl.BlockSpec((tm, d), lambda i: (i, 0))
    vec = pl.BlockSpec((1, d), lambda i: (0, 0))
    return pl.pallas_call(
        _ln_affine_kernel,
        out_shape=jax.ShapeDtypeStruct((s, d), F32),
        grid=(s // tm,),
        in_specs=[row, vec, vec],
        out_specs=row,
        compiler_params=_params(("parallel",), 2 * _nbytes((tm, d), F32),
                                3 * _nbytes((tm, d), F32)),
        name="ln2",
    )(r, gain, bias)


def _ffn_up_kernel(u_ref, wg_ref, wu_ref, o_ref):
    wg = _as_bf16(wg_ref[...])
    wu = _as_bf16(wu_ref[...])
    rows = u_ref.shape[0] // FFN_CHUNKS
    for r in range(0, u_ref.shape[0], rows):
        u = u_ref[r:r + rows, :]
        g = jnp.dot(u, wg, preferred_element_type=F32)
        up = jnp.dot(u, wu, preferred_element_type=F32)
        o_ref[r:r + rows, :] = (g * jax.nn.sigmoid(g) * up).astype(o_ref.dtype)


def _ffn_up(u, w_gate_up, *, tm=2048, tn=256):
    s, d = u.shape
    hidden = w_gate_up.shape[1] // 2
    nb = hidden // tn
    return pl.pallas_call(
        _ffn_up_kernel,
        out_shape=jax.ShapeDtypeStruct((s, hidden), BF16),
        grid=(s // tm, nb),
        in_specs=[pl.BlockSpec((tm, d), lambda i, j: (i, 0)),
                  pl.BlockSpec((d, tn), lambda i, j: (0, j)),
                  pl.BlockSpec((d, tn), lambda i, j: (0, nb + j))],
        out_specs=pl.BlockSpec((tm, tn), lambda i, j: (i, j)),
        compiler_params=_params(("parallel", "parallel"),
                                _nbytes((tm, d), BF16) + 2 * _nbytes((d, tn), w_gate_up.dtype)
                                + _nbytes((tm, tn), BF16),
                                2 * _nbytes((d, tn), BF16) + 2 * _nbytes((tm, tn), F32)),
        name="ffn_up",
    )(u, w_gate_up, w_gate_up)


def kernel(x, c, w_ada, b_ada, w_in, b_forget, w_attn_out, w_pool, pool_scale, w_out,
           ln1_g, ln1_b, w_gate_up, w_down, ln2_g, ln2_b):
    batch, seq, d = x.shape
    assert batch == 1 and w_ada.shape[0] == DEPTH == 1
    fox_w = w_attn_out.shape[1]
    heads = fox_w // FOX_HEAD_DIM
    pool_w = w_pool.shape[1] * w_pool.shape[2]
    o_f = 3 * fox_w
    o_p = o_f + heads
    tk = 512

    xs = x[0]
    w_in_t = jnp.swapaxes(w_in, 1, 2)[0]
    w_f = jnp.pad(w_in_t[o_f:o_p], ((0, LANES - heads), (0, 0))).astype(BF16)
    b_f = jnp.pad(b_forget, ((0, 0), (0, LANES - heads)))

    c_col = c.reshape(d, 1)
    mod = _ada(c_col, w_ada[0], b_ada, 2 * d)

    u1, e, c_blk = _ln_mod_fcum(xs, mod, 0, 1, w_f, b_f, heads=heads, tm=tk)
    cb = c_blk[:, 0, :heads].T
    tq = 2 * tk
    qt, sq_q = _matmul_tn(u1, w_in_t, 0, fox_w, tm=tq, tn=512, tk=tq, scale=FOX_HEAD_DIM ** -0.5 * LOG2E,
                          group_norms=True, name="proj_qt")
    k, sq_k, p, gates = _proj_kpg(u1, w_in_t, fox_w, fox_w, o_p, pool_w, 2 * d, tm=2048, tn=512)
    per_head = lambda sq: jnp.sqrt(sq[:, :, 0, :512 // LANES].reshape(sq.shape[0], heads))
    thr = (SKIP_LOG2 + 2.0 * NORM_SLACK * per_head(sq_q) * jnp.max(per_head(sq_k), axis=0)).T
    vt = _matmul_tn(u1, w_in_t, 2 * fox_w, fox_w, tm=1024, tn=512, tk=tk, name="proj_vt")
    attn, w_down_bf16 = _attention(cb, thr, qt, k, e, vt, w_down[0], heads=heads, tk=tk)
    mix, mod2 = _mix(attn, w_attn_out[0], p, w_pool[0], pool_scale, gates, c_col, w_ada[0], b_ada, 2 * d)
    r1 = _resid_matmul(mix, w_out[0], xs, mod2, 0, tm=2048, tn=256, chunks=4, name="out_proj")
    x1, u2 = _ln_ln_mod(r1, ln1_g, ln1_b, mod2, 1, 2)

    act = _ffn_up(u2, w_gate_up[0])
    r2 = _resid_matmul(act, w_down_bf16, x1, mod2, 3, tm=512, tn=512, name="ffn_down")
    out = _ln_affine(r2, ln2_g, ln2_b)
    return out[None]
```

```python
import functools

import jax
import jax.numpy as jnp
from jax import lax
from jax.experimental import pallas as pl
from jax.experimental.pallas import tpu as pltpu

F32 = jnp.float32
BF16 = jnp.bfloat16

FOX_HEAD_DIM = 128
POOL_WINDOWS = (2, 4, 8, 16)
POOL_HALO = 16
DEPTH = 1
ALPHA = (2 * DEPTH) ** 0.25
LN_EPS = 1e-5
LOG2E = 1.4426950408889634
LANES = 128
SUBLANES = 8
BIAS_PIECES = 3
ONES_ROWS = 16
MM_CHUNKS = 4
MIX_CHUNKS = 4
FFN_CHUNKS = 4
Q_PER_STEP = 4
SKIP_LOG2 = 48.0
NORM_SLACK = 1.02

V7X_VMEM_BYTES = 64 * 1024 * 1024
VMEM_CAP_BYTES = V7X_VMEM_BYTES - 6 * 1024 * 1024

_NT = (((1,), (1,)), ((), ()))
_TN = (((0,), (0,)), ((), ()))


def _nbytes(shape, dtype):
    n = jnp.dtype(dtype).itemsize
    for s in shape:
        n *= s
    return n


def _cast_bytes(shape, dtype):
    return 0 if dtype == BF16 else _nbytes(shape, BF16)


def _params(semantics, pipelined_bytes, resident_bytes=0):
    need = 2 * pipelined_bytes + resident_bytes
    assert need <= VMEM_CAP_BYTES, (need, VMEM_CAP_BYTES)
    return pltpu.CompilerParams(dimension_semantics=semantics,
                                vmem_limit_bytes=min(need + (4 << 20), VMEM_CAP_BYTES))


def _ln_rows(x):
    mu = jnp.mean(x, axis=-1, keepdims=True)
    xc = x - mu
    var = jnp.mean(xc * xc, axis=-1, keepdims=True)
    return xc * lax.rsqrt(var + LN_EPS)


def _as_bf16(w):
    return w if w.dtype == BF16 else w.astype(BF16)


def _split3(v):
    hi = v.astype(BF16)
    r1 = v - hi.astype(F32)
    mid = r1.astype(BF16)
    lo = (r1 - mid.astype(F32)).astype(BF16)
    return hi, mid, lo


def _ada_kernel(c_ref, w_ref, b_ref, o_ref):
    k = pl.program_id(1)

    @pl.when(k == 0)
    def _():
        o_ref[...] = b_ref[...]

    cc = c_ref[...]
    s = cc * jax.nn.sigmoid(cc)
    o_ref[...] += jnp.sum(s * w_ref[...], axis=0, keepdims=True)


def _ada(c_col, w_ada, b_ada, n, *, tk=512, tn=4096):
    d = w_ada.shape[0]
    return pl.pallas_call(
        _ada_kernel,
        out_shape=jax.ShapeDtypeStruct((1, n), F32),
        grid=(n // tn, d // tk),
        in_specs=[pl.BlockSpec((tk, 1), lambda j, k: (k, 0)),
                  pl.BlockSpec((tk, tn), lambda j, k: (k, j)),
                  pl.BlockSpec((1, tn), lambda j, k: (0, j))],
        out_specs=pl.BlockSpec((1, tn), lambda j, k: (0, j)),
        compiler_params=_params(("parallel", "arbitrary"),
                                _nbytes((tk, tn), F32) + _nbytes((tk, LANES), F32),
                                _nbytes((tk, tn), F32)),
        name="ada",
    )(c_col, w_ada, b_ada)


def _ln_mod_fcum_kernel(x_ref, sh_ref, sc_ref, wf_ref, b_ref, u_ref, e_ref, c_ref,
                        carry_ref, f_ref, *, tm, rows, heads):
    @pl.when(pl.program_id(0) == 0)
    def _():
        carry_ref[...] = jnp.zeros_like(carry_ref)

    gain = 1.0 + sc_ref[...]
    shift = sh_ref[...]
    for r in range(0, tm, rows):
        u = (_ln_rows(x_ref[r:r + rows, :]) * gain + shift).astype(BF16)
        u_ref[r:r + rows, :] = u
        f_ref[r:r + rows, :] = lax.dot_general(u, wf_ref[...], _NT, preferred_element_type=F32)
    f = f_ref[...] + b_ref[...]
    ls = (jnp.minimum(f, 0.0) - jnp.log1p(jnp.exp(-jnp.abs(f)))) * LOG2E
    row = lax.broadcasted_iota(jnp.int32, (tm, tm), 0)
    col = lax.broadcasted_iota(jnp.int32, (tm, tm), 1)
    tri = jnp.where(col <= row, 1.0, 0.0).astype(BF16)
    cs = sum(jnp.dot(tri, piece, preferred_element_type=F32) for piece in _split3(ls))
    carry = carry_ref[...]
    c_ref[...] = carry
    carry_ref[...] = carry + cs[tm - 1:tm, :]
    head_lane = lax.broadcasted_iota(jnp.int32, (tm, LANES), 1) < heads
    e = sum(pltpu.roll(jnp.where(head_lane, piece.astype(F32), 0.0), p * heads, 1) if p else
            jnp.where(head_lane, piece.astype(F32), 0.0)
            for p, piece in enumerate(_split3(-cs)))
    e_ref[...] = e.astype(e_ref.dtype)


def _ln_mod_fcum(x, mod, shift_idx, scale_idx, wf, b_row, *, heads, tm, rows=128):
    s, d = x.shape
    nb = s // tm
    assert BIAS_PIECES * heads <= LANES
    vec = lambda idx: pl.BlockSpec((1, d), lambda i: (0, idx))
    const = lambda shape: pl.BlockSpec(shape, lambda i: (0, 0))
    return pl.pallas_call(
        functools.partial(_ln_mod_fcum_kernel, tm=tm, rows=rows, heads=heads),
        out_shape=(jax.ShapeDtypeStruct((s, d), BF16),
                   jax.ShapeDtypeStruct((s, LANES), BF16),
                   jax.ShapeDtypeStruct((nb, 1, LANES), F32)),
        grid=(nb,),
        in_specs=[pl.BlockSpec((tm, d), lambda i: (i, 0)), vec(shift_idx), vec(scale_idx),
                  const((LANES, d)), const((1, LANES))],
        out_specs=(pl.BlockSpec((tm, d), lambda i: (i, 0)),
                   pl.BlockSpec((tm, LANES), lambda i: (i, 0)),
                   pl.BlockSpec((None, 1, LANES), lambda i: (i, 0, 0))),
        scratch_shapes=[pltpu.VMEM((1, LANES), F32), pltpu.VMEM((tm, LANES), F32)],
        compiler_params=_params(("arbitrary",),
                                _nbytes((tm, d), F32) + _nbytes((tm, d), BF16)
                                + _nbytes((tm, LANES), BF16) + _nbytes((d, LANES), BF16),
                                4 * _nbytes((rows, d), F32) + 2 * _nbytes((tm, tm), F32)),
        name="ln_mod_fcum",
    )(x, mod, mod, wf, b_row)


def _wt_spec(tn, k, row0):
    assert row0 % SUBLANES == 0 and tn % SUBLANES == 0
    return pl.BlockSpec((pl.Element(tn), pl.Element(k)),
                        lambda i, j: ((row0 // SUBLANES + j * (tn // SUBLANES)) * SUBLANES, 0))


def _proj_kpg_kernel(a_ref, w_ref, k_ref, nrm_ref, p_ref, g_ref, *, nk, npool):
    j = pl.program_id(1)
    w = _as_bf16(w_ref[...])
    tm = a_ref.shape[0]
    chunk = tm // MM_CHUNKS

    def tiles():
        for r in range(0, tm, chunk):
            yield slice(r, r + chunk), lax.dot_general(a_ref[r:r + chunk, :], w, _NT,
                                                       preferred_element_type=F32)

    @pl.when(j < nk)
    def _():
        lane = lax.broadcasted_iota(jnp.int32, (1, LANES), 1)
        nrm = jnp.zeros((1, LANES), F32)
        for rows, acc in tiles():
            k_ref[rows, :] = acc.astype(k_ref.dtype)
            for g in range(acc.shape[1] // LANES):
                sq = jnp.sum(jnp.square(acc[:, g * LANES:(g + 1) * LANES]), axis=1, keepdims=True)
                nrm = jnp.maximum(nrm, jnp.where(lane == g, jnp.max(sq, axis=0, keepdims=True), 0.0))
        nrm_ref[...] = nrm

    @pl.when((j >= nk) & (j < nk + npool))
    def _():
        for rows, acc in tiles():
            p_ref[rows, :] = acc

    @pl.when(j >= nk + npool)
    def _():
        for rows, acc in tiles():
            g_ref[rows, :] = (0.5 * jnp.tanh(0.5 * acc) + 0.5).astype(g_ref.dtype)


def _proj_kpg(a, wt, k_row0, k_n, pg_row0, pool_n, gate_n, *, tm, tn):
    m, kdim = a.shape
    nk, npool, ngate = k_n // tn, pool_n // tn, gate_n // tn
    assert k_row0 % SUBLANES == 0 and pg_row0 % SUBLANES == 0 and tn % SUBLANES == 0

    def w_rows(i, j):
        blk = jnp.where(j < nk, k_row0 // SUBLANES + j * (tn // SUBLANES),
                        pg_row0 // SUBLANES + (j - nk) * (tn // SUBLANES))
        return blk * SUBLANES, 0

    k_idx = lambda i, j: (i, jnp.minimum(j, nk - 1))
    return pl.pallas_call(
        functools.partial(_proj_kpg_kernel, nk=nk, npool=npool),
        out_shape=[jax.ShapeDtypeStruct((m, k_n), BF16),
                   jax.ShapeDtypeStruct((m // tm, nk, 1, LANES), F32),
                   jax.ShapeDtypeStruct((m, pool_n), F32),
                   jax.ShapeDtypeStruct((m, gate_n), BF16)],
        grid=(m // tm, nk + npool + ngate),
        in_specs=[pl.BlockSpec((tm, kdim), lambda i, j: (i, 0), pipeline_mode=pl.Buffered(1)),
                  pl.BlockSpec((pl.Element(tn), pl.Element(kdim)), w_rows)],
        out_specs=[pl.BlockSpec((tm, tn), k_idx),
                   pl.BlockSpec((None, None, 1, LANES), lambda i, j: (*k_idx(i, j), 0, 0)),
                   pl.BlockSpec((tm, tn), lambda i, j: (i, jnp.clip(j - nk, 0, npool - 1))),
                   pl.BlockSpec((tm, tn), lambda i, j: (i, jnp.maximum(j - nk - npool, 0)))],
        compiler_params=_params(("arbitrary", "arbitrary"),
                                _nbytes((tn, kdim), wt.dtype) + 2 * _nbytes((tm, tn), BF16)
                                + _nbytes((tm, tn), F32),
                                _nbytes((tm, kdim), a.dtype) + _cast_bytes((tn, kdim), wt.dtype)
                                + 3 * _nbytes((tm // MM_CHUNKS, tn), F32)),
        name="proj_kpg",
    )(a, wt)


def _mm_tn_kernel(a_ref, w_ref, o_ref, *rest, scale):
    acc_t = lax.dot_general(_as_bf16(w_ref[...]), a_ref[...], _NT,
                            preferred_element_type=F32)
    if scale is not None:
        acc_t = acc_t * scale
    tk = o_ref.shape[2]
    for kb in range(o_ref.shape[0]):
        o_ref[kb] = acc_t[:, kb * tk:(kb + 1) * tk].astype(o_ref.dtype)
    if rest:
        lane = lax.broadcasted_iota(jnp.int32, (1, LANES), 1)
        nrm = jnp.zeros((1, LANES), F32)
        for g in range(acc_t.shape[0] // LANES):
            sq = jnp.sum(jnp.square(acc_t[g * LANES:(g + 1) * LANES, :]), axis=0, keepdims=True)
            nrm = jnp.where(lane == g, jnp.max(sq, axis=1, keepdims=True), nrm)
        rest[0][...] = nrm


def _matmul_tn(a, wt, row0, n, *, tm, tn, tk, scale=None, group_norms=False, name):
    m, k = a.shape
    out_shape = [jax.ShapeDtypeStruct((m // tk, n, tk), BF16)]
    out_specs = [pl.BlockSpec((tm // tk, tn, tk), lambda i, j: (i, j, 0))]
    if group_norms:
        out_shape.append(jax.ShapeDtypeStruct((m // tm, n // tn, 1, LANES), F32))
        out_specs.append(pl.BlockSpec((None, None, 1, LANES), lambda i, j: (i, j, 0, 0)))
    out = pl.pallas_call(
        functools.partial(_mm_tn_kernel, scale=scale),
        out_shape=out_shape,
        grid=(m // tm, n // tn),
        in_specs=[pl.BlockSpec((tm, k), lambda i, j: (i, 0)), _wt_spec(tn, k, row0)],
        out_specs=out_specs,
        compiler_params=_params(("parallel", "parallel"),
                                _nbytes((tm, k), a.dtype) + _nbytes((tn, k), wt.dtype)
                                + _nbytes((tm, tn), BF16),
                                _cast_bytes((tn, k), wt.dtype) + 3 * _nbytes((tm, tn), F32)),
        name=name,
    )(a, wt)
    return out if group_norms else out[0]


def _attn_kernel(cb_ref, thr_ref, q_ref, k_ref, e_ref, vt_ref, w_ref, o_ref, wb_ref, *scratch,
                 tk, tn, heads, side_blocks):
    h = pl.program_id(0)

    blocks = [Q_PER_STEP * pl.program_id(1) + sub for sub in range(Q_PER_STEP)]
    firsts = [lax.while_loop(lambda p, i=i: (p < i) & (cb_ref[h, 2 * p + 2] - cb_ref[h, 2 * i]
                                                        > thr_ref[h, i]),
                             lambda p: p + 1, jnp.int32(0)) for i in blocks]
    for sub, (i, first) in enumerate(zip(blocks, firsts)):
        side_cols = w_ref.shape[1] // Q_PER_STEP
        _attn_block(cb_ref, q_ref.at[sub], k_ref, e_ref, vt_ref,
                    o_ref.at[:, sub * q_ref.shape[2]:(sub + 1) * q_ref.shape[2]],
                    w_ref.at[:, sub * side_cols:(sub + 1) * side_cols],
                    wb_ref.at[:, sub * side_cols:(sub + 1) * side_cols],
                    *scratch[3 * sub:3 * sub + 3], h=h, i=i, first=first, tk=tk, tn=tn, heads=heads)


def _attn_block(cb_ref, q_ref, k_ref, e_ref, vt_ref, o_ref, w_ref, wb_ref, s0_ref, s1_ref, acc_ref, *,
                h, i, first, tk, tn, heads):
    dh, tq = q_ref.shape
    nt = tq // tn
    half = nt // 2
    r = lax.broadcasted_iota(jnp.int32, (dh, tq), 0) - h
    ones_rows = sum(jnp.where(r == p * heads, 1.0, 0.0) for p in range(BIAS_PIECES))
    qt_aug = jnp.concatenate([q_ref[...], ones_rows.astype(BF16)], axis=0)
    ones_v = jnp.ones((acc_ref.shape[0] - dh, tk), BF16)
    c_q = cb_ref[h, 2 * i]
    acc_ref[...] = jnp.zeros_like(acc_ref)

    def logits_to(buf_ref, j, first_tile=0):
        rows = pl.ds(pl.multiple_of(j * tk, tk), tk)
        k_aug = jnp.concatenate([k_ref[rows, :], e_ref[rows, :]], axis=1)
        cms = []
        for n in range(first_tile, nt):
            cols = slice(n * tn, (n + 1) * tn)
            s = jnp.dot(k_aug, qt_aug[:, cols], preferred_element_type=F32)
            buf_ref[:, cols] = s
            cms.append(jnp.max(s, axis=0, keepdims=True))
        return jnp.concatenate(cms, axis=1)

    def causal(buf_ref, n, first_tile):
        key = lax.broadcasted_iota(jnp.int32, (tk, tn), 0)
        qry = lax.broadcasted_iota(jnp.int32, (tk, tn), 1) + (n - first_tile) * tn
        return jnp.where(key <= qry, buf_ref[:, n * tn:(n + 1) * tn], -jnp.inf)

    def softmax_pv(buf_ref, j, cmax, m, first_tile=0, causal_tiles=0):
        lo = first_tile * tn
        c = c_q - cb_ref[h, j]
        m_old = m[:, lo:]
        m_new = jnp.maximum(m_old, cmax + c)
        a = jnp.exp2(m_old - m_new)
        off = m_new - c
        v_aug = jnp.concatenate([vt_ref[j], ones_v], axis=0)
        for n in range(first_tile, nt):
            cols = slice(n * tn, (n + 1) * tn)
            rel = slice(n * tn - lo, (n + 1) * tn - lo)
            s = causal(buf_ref, n, first_tile) if n < first_tile + causal_tiles else buf_ref[:, cols]
            p = jnp.exp2(s - off[:, rel]).astype(BF16)
            acc_ref[:, cols] = a[:, rel] * acc_ref[:, cols] + jnp.dot(
                v_aug, p, preferred_element_type=F32)
        if first_tile:
            m_new = jnp.concatenate([m[:, :lo], m_new], axis=1)
        return m_new

    def causal_max(buf_ref, first_tile):
        return [jnp.max(causal(buf_ref, n, first_tile), axis=0, keepdims=True)
                for n in range(first_tile, first_tile + half)]

    def pair(p, carry):
        m, cm0 = carry
        cm1 = logits_to(s1_ref, 2 * p + 1)
        m = softmax_pv(s0_ref, 2 * p, cm0, m)
        cm0 = logits_to(s0_ref, 2 * p + 2)
        m = softmax_pv(s1_ref, 2 * p + 1, cm1, m)
        return m, cm0

    def two_pairs(q, carry):
        p = start + 2 * q
        return pair(p + 1, pair(p, carry))

    start = first + (i - first) % 2
    carry = (jnp.full((1, tq), -jnp.inf, F32), logits_to(s0_ref, 2 * first))
    wb_ref[...] = w_ref[...].astype(wb_ref.dtype)
    carry = lax.fori_loop(first, start, pair, carry)
    m, cm0 = lax.fori_loop(0, (i - start) // 2, two_pairs, carry)
    logits_to(s1_ref, 2 * i + 1, first_tile=half)
    cm = jnp.concatenate(causal_max(s0_ref, 0) + [cm0[:, half * tn:]], axis=1)
    m = softmax_pv(s0_ref, 2 * i, cm, m, causal_tiles=half)
    cm = jnp.concatenate(causal_max(s1_ref, half), axis=1)
    softmax_pv(s1_ref, 2 * i + 1, cm, m, first_tile=half, causal_tiles=half)
    o_ref[...] = (acc_ref[:dh, :] / acc_ref[dh:dh + 1, :]).astype(o_ref.dtype)


def _attention(cb, thr, qt, k, e, vt, w_side, *, heads, tk, tn=256, side_blocks=32):
    s = k.shape[0]
    dh = FOX_HEAD_DIM
    tq = 2 * tk
    steps = s // (Q_PER_STEP * tq)
    side_cols = w_side.shape[1] // 2
    side_rows = 2 * w_side.shape[0] // side_blocks
    assert side_rows * side_blocks == 2 * w_side.shape[0] and side_rows % (2 * SUBLANES) == 0
    assert side_cols % LANES == 0 and side_blocks == heads * steps

    def side(h, i, *_):
        blk = jnp.minimum(h * steps + i, side_blocks - 1)
        return blk // 2, blk % 2

    grid_spec = pltpu.PrefetchScalarGridSpec(
        num_scalar_prefetch=2,
        grid=(heads, steps),
        in_specs=[pl.BlockSpec((Q_PER_STEP, dh, tq), lambda h, i, *_: (i, h, 0)),
                  pl.BlockSpec((s, dh), lambda h, i, *_: (0, h)),
                  pl.BlockSpec((s, LANES), lambda h, i, *_: (0, 0)),
                  pl.BlockSpec((s // tk, dh, tk), lambda h, i, *_: (0, h, 0)),
                  pl.BlockSpec((side_rows, side_cols), side)],
        out_specs=[pl.BlockSpec((dh, Q_PER_STEP * tq), lambda h, i, *_: (h, i)),
                   pl.BlockSpec((side_rows, side_cols), side)],
        scratch_shapes=Q_PER_STEP * [pltpu.VMEM((tk, tq), F32), pltpu.VMEM((tk, tq), F32),
                                     pltpu.VMEM((dh + ONES_ROWS, tq), F32)],
    )
    return pl.pallas_call(
        functools.partial(_attn_kernel, tk=tk, tn=tn, heads=heads, side_blocks=side_blocks),
        out_shape=[jax.ShapeDtypeStruct((heads * dh, s), BF16),
                   jax.ShapeDtypeStruct(w_side.shape, BF16)],
        grid_spec=grid_spec,
        compiler_params=_params(("arbitrary", "arbitrary"),
                                3 * _nbytes((s, dh), BF16) + 2 * Q_PER_STEP * _nbytes((tq, dh), BF16)
                                + _nbytes((side_rows, side_cols), F32)
                                + _nbytes((side_rows, side_cols), BF16),
                                Q_PER_STEP * (2 * _nbytes((tk, tq), F32) + _nbytes((dh + ONES_ROWS, tq), F32))
                                + 3 * _nbytes((tk, tq), F32)),
        name="fox_attention",
    )(cb, thr, qt, k, e, vt, w_side)


def _mix_kernel(a_ref, wa_ref, halo_ref, p_ref, wp_ref, ps_ref, ga_ref, gp_ref, c_ref, wada_ref, bada_ref,
                o_ref, mod_ref, wa_b, wp_b, *, tm):
    j = pl.program_id(0)
    i = pl.program_id(1)

    ada_rows = c_ref.shape[0] // MIX_CHUNKS

    def ada_piece(k):
        cc = c_ref[k * ada_rows:(k + 1) * ada_rows, :]
        return jnp.sum(cc * jax.nn.sigmoid(cc) * wada_ref[k * ada_rows:(k + 1) * ada_rows, :],
                       axis=0, keepdims=True)

    @pl.when(i == 0)
    def _():
        wa_b[...] = wa_ref[...].astype(BF16)
        wp_b[...] = wp_ref[...].astype(BF16)

    halo = jnp.where(i == 0, 0.0, halo_ref[...])
    ext = jnp.concatenate([halo, p_ref[...]], axis=0)
    acc = ext
    win = ext
    for g, w in enumerate(POOL_WINDOWS):
        acc = acc + pltpu.roll(acc, w // 2, 0)
        win = jnp.where(j == g, acc, win)
    t = i * tm + lax.broadcasted_iota(jnp.int32, (tm, 1), 0)
    cnt = jnp.minimum(t + 1, jnp.left_shift(2, j)).astype(F32)
    pooled = (win[POOL_HALO:] / cnt - ext[POOL_HALO:]).astype(BF16)

    a_t = a_ref[...]
    cols = o_ref.shape[1] // MIX_CHUNKS
    mod_acc = bada_ref[...]
    for n, c in enumerate(range(0, o_ref.shape[1], cols)):
        ya = lax.dot_general(a_t, wa_b[:, c:c + cols], _TN, preferred_element_type=F32)
        yp = jnp.dot(pooled, wp_b[:, c:c + cols], preferred_element_type=F32) * ps_ref[:, c:c + cols]
        mod_acc = mod_acc + ada_piece(n)
        o_ref[:, c:c + cols] = (ga_ref[:, c:c + cols].astype(F32) * ya
                                + gp_ref[:, c:c + cols].astype(F32) * yp).astype(o_ref.dtype)
    mod_ref[...] = mod_acc


def _mix(attn_t, w_a, p, w_pool, pool_scale, gates, c_col, w_ada, b_ada, ada_col0, *, tm=512):
    fw, s = attn_t.shape
    groups, gd, tn = w_pool.shape
    assert POOL_WINDOWS == tuple(2 << g for g in range(groups))
    d = w_a.shape[1]
    hb = tm // POOL_HALO
    rows = s // tm
    ada_n = w_ada.shape[1] - ada_col0
    ada_tn = ada_n // (groups * rows)
    assert ada_tn * groups * rows == ada_n and ada_tn % LANES == 0 and ada_col0 % ada_tn == 0
    ada_blk = lambda j, i: (0, ada_col0 // ada_tn + j * rows + i)
    return pl.pallas_call(
        functools.partial(_mix_kernel, tm=tm),
        out_shape=(jax.ShapeDtypeStruct((s, d), BF16), jax.ShapeDtypeStruct((1, ada_n), F32)),
        grid=(groups, rows),
        in_specs=[pl.BlockSpec((fw, tm), lambda j, i: (0, i)),
                  pl.BlockSpec((fw, tn), lambda j, i: (0, j)),
                  pl.BlockSpec((POOL_HALO, gd), lambda j, i: (jnp.maximum(i * hb - 1, 0), j)),
                  pl.BlockSpec((tm, gd), lambda j, i: (i, j)),
                  pl.BlockSpec((None, gd, tn), lambda j, i: (j, 0, 0)),
                  pl.BlockSpec((1, tn), lambda j, i: (0, j)),
                  pl.BlockSpec((tm, tn), lambda j, i: (i, j)),
                  pl.BlockSpec((tm, tn), lambda j, i: (i, groups + j)),
                  pl.BlockSpec((d, 1), lambda j, i: (0, 0)),
                  pl.BlockSpec((d, ada_tn), ada_blk),
                  pl.BlockSpec((1, ada_tn), ada_blk)],
        out_specs=(pl.BlockSpec((tm, tn), lambda j, i: (i, j)),
                   pl.BlockSpec((1, ada_tn), lambda j, i: (0, j * rows + i))),
        scratch_shapes=[pltpu.VMEM((fw, tn), BF16), pltpu.VMEM((gd, tn), BF16)],
        compiler_params=_params(("parallel", "arbitrary"),
                                _nbytes((tm, fw), BF16) + _nbytes((fw, tn), w_a.dtype)
                                + _nbytes((tm + POOL_HALO, gd), F32) + _nbytes((gd, tn), w_pool.dtype)
                                + 3 * _nbytes((tm, tn), BF16)
                                + _nbytes((d, ada_tn), F32) + _nbytes((d, LANES), F32),
                                _nbytes((fw + gd, tn), BF16) + 3 * _nbytes((tm, tn // MIX_CHUNKS), F32)
                                + 4 * _nbytes((tm + POOL_HALO, gd), F32)),
        name="branch_mix",
    )(attn_t, w_a, p, p, w_pool, pool_scale, gates, gates, c_col, w_ada, b_ada)


def _resid_mm_kernel(a_ref, w_ref, x_ref, g_ref, o_ref, *, chunks):
    w = _as_bf16(w_ref[...])
    rows = a_ref.shape[0] // chunks
    for r in range(0, a_ref.shape[0], rows):
        m = jnp.dot(a_ref[r:r + rows, :], w, preferred_element_type=F32)
        o_ref[r:r + rows, :] = ALPHA * x_ref[r:r + rows, :] + g_ref[...] * m


def _resid_matmul(a, w, x, mod, gate_idx, *, tm, tn, chunks=1, name):
    m, k = a.shape
    n = w.shape[1]
    nb = n // tn
    return pl.pallas_call(
        functools.partial(_resid_mm_kernel, chunks=chunks),
        out_shape=jax.ShapeDtypeStruct((m, n), F32),
        grid=(m // tm, nb),
        in_specs=[pl.BlockSpec((tm, k), lambda i, j: (i, 0)),
                  pl.BlockSpec((k, tn), lambda i, j: (0, j)),
                  pl.BlockSpec((tm, tn), lambda i, j: (i, j)),
                  pl.BlockSpec((1, tn), lambda i, j: (0, gate_idx * nb + j))],
        out_specs=pl.BlockSpec((tm, tn), lambda i, j: (i, j)),
        compiler_params=_params(("parallel", "parallel"),
                                _nbytes((tm, k), BF16) + _nbytes((k, tn), w.dtype)
                                + 2 * _nbytes((tm, tn), F32),
                                _cast_bytes((k, tn), w.dtype) + 2 * _nbytes((tm, tn), F32) // chunks),
        name=name,
    )(a, w, x, mod)


def _ln_ln_mod_kernel(r_ref, g_ref, b_ref, sh_ref, sc_ref, x_ref, u_ref):
    x1 = _ln_rows(r_ref[...]) * g_ref[...] + b_ref[...]
    x_ref[...] = x1
    u_ref[...] = (_ln_rows(x1) * (1.0 + sc_ref[...]) + sh_ref[...]).astype(u_ref.dtype)


def _ln_ln_mod(r, gain, bias, mod, shift_idx, scale_idx, *, tm=256):
    s, d = r.shape
    row = pl.BlockSpec((tm, d), lambda i: (i, 0))
    vec = pl.BlockSpec((1, d), lambda i: (0, 0))
    return pl.pallas_call(
        _ln_ln_mod_kernel,
        out_shape=(jax.ShapeDtypeStruct((s, d), F32), jax.ShapeDtypeStruct((s, d), BF16)),
        grid=(s // tm,),
        in_specs=[row, vec, vec,
                  pl.BlockSpec((1, d), lambda i: (0, shift_idx)),
                  pl.BlockSpec((1, d), lambda i: (0, scale_idx))],
        out_specs=(row, row),
        compiler_params=_params(("parallel",),
                                2 * _nbytes((tm, d), F32) + _nbytes((tm, d), BF16),
                                4 * _nbytes((tm, d), F32)),
        name="ln1_ln_mod",
    )(r, gain, bias, mod, mod)


def _ln_affine_kernel(r_ref, g_ref, b_ref, o_ref):
    o_ref[...] = _ln_rows(r_ref[...]) * g_ref[...] + b_ref[...]


def _ln_affine(r, gain, bias, *, tm=512):
    s, d = r.shape
    row = pl.BlockSpec((tm, d), lambda i: (i, 0))
    vec = pl.BlockSpec((1, d), lambda i: (0, 0))
    return pl.pallas_call(
        _ln_affine_kernel,
        out_shape=jax.ShapeDtypeStruct((s, d), F32),
        grid=(s // tm,),
        in_specs=[row, vec, vec],
        out_specs=row,
        compiler_params=_params(("parallel",), 2 * _nbytes((tm, d), F32),
                                3 * _nbytes((tm, d), F32)),
        name="ln2",
    )(r, gain, bias)


def _ffn_up_kernel(u_ref, wg_ref, wu_ref, o_ref):
    wg = _as_bf16(wg_ref[...])
    wu = _as_bf16(wu_ref[...])
    rows = u_ref.shape[0] // FFN_CHUNKS
    for r in range(0, u_ref.shape[0], rows):
        u = u_ref[r:r + rows, :]
        g = jnp.dot(u, wg, preferred_element_type=F32)
        up = jnp.dot(u, wu, preferred_element_type=F32)
        o_ref[r:r + rows, :] = (g * jax.nn.sigmoid(g) * up).astype(o_ref.dtype)


def _ffn_up(u, w_gate_up, *, tm=2048, tn=256):
    s, d = u.shape
    hidden = w_gate_up.shape[1] // 2
    nb = hidden // tn
    return pl.pallas_call(
        _ffn_up_kernel,
        out_shape=jax.ShapeDtypeStruct((s, hidden), BF16),
        grid=(s // tm, nb),
        in_specs=[pl.BlockSpec((tm, d), lambda i, j: (i, 0)),
                  pl.BlockSpec((d, tn), lambda i, j: (0, j)),
                  pl.BlockSpec((d, tn), lambda i, j: (0, nb + j))],
        out_specs=pl.BlockSpec((tm, tn), lambda i, j: (i, j)),
        compiler_params=_params(("parallel", "parallel"),
                                _nbytes((tm, d), BF16) + 2 * _nbytes((d, tn), w_gate_up.dtype)
                                + _nbytes((tm, tn), BF16),
                                2 * _nbytes((d, tn), BF16) + 2 * _nbytes((tm, tn), F32)),
        name="ffn_up",
    )(u, w_gate_up, w_gate_up)


def kernel(x, c, w_ada, b_ada, w_in, b_forget, w_attn_out, w_pool, pool_scale, w_out,
           ln1_g, ln1_b, w_gate_up, w_down, ln2_g, ln2_b):
    batch, seq, d = x.shape
    assert batch == 1 and w_ada.shape[0] == DEPTH == 1
    fox_w = w_attn_out.shape[1]
    heads = fox_w // FOX_HEAD_DIM
    pool_w = w_pool.shape[1] * w_pool.shape[2]
    o_f = 3 * fox_w
    o_p = o_f + heads
    tk = 512

    xs = x[0]
    w_in_t = jnp.swapaxes(w_in, 1, 2)[0]
    w_f = jnp.pad(w_in_t[o_f:o_p], ((0, LANES - heads), (0, 0))).astype(BF16)
    b_f = jnp.pad(b_forget, ((0, 0), (0, LANES - heads)))

    c_col = c.reshape(d, 1)
    mod = _ada(c_col, w_ada[0], b_ada, 2 * d)

    u1, e, c_blk = _ln_mod_fcum(xs, mod, 0, 1, w_f, b_f, heads=heads, tm=tk)
    cb = c_blk[:, 0, :heads].T
    tq = 2 * tk
    qt, sq_q = _matmul_tn(u1, w_in_t, 0, fox_w, tm=tq, tn=512, tk=tq, scale=FOX_HEAD_DIM ** -0.5 * LOG2E,
                          group_norms=True, name="proj_qt")
    k, sq_k, p, gates = _proj_kpg(u1, w_in_t, fox_w, fox_w, o_p, pool_w, 2 * d, tm=2048, tn=512)
    per_head = lambda sq: jnp.sqrt(sq[:, :, 0, :512 // LANES].reshape(sq.shape[0], heads))
    thr = (SKIP_LOG2 + 2.0 * NORM_SLACK * per_head(sq_q) * jnp.max(per_head(sq_k), axis=0)).T
    vt = _matmul_tn(u1, w_in_t, 2 * fox_w, fox_w, tm=1024, tn=512, tk=tk, name="proj_vt")
    attn, w_down_bf16 = _attention(cb, thr, qt, k, e, vt, w_down[0], heads=heads, tk=tk)
    mix, mod2 = _mix(attn, w_attn_out[0], p, w_pool[0], pool_scale, gates, c_col, w_ada[0], b_ada, 2 * d)
    r1 = _resid_matmul(mix, w_out[0], xs, mod2, 0, tm=2048, tn=256, chunks=4, name="out_proj")
    x1, u2 = _ln_ln_mod(r1, ln1_g, ln1_b, mod2, 1, 2)

    act = _ffn_up(u2, w_gate_up[0])
    r2 = _resid_matmul(act, w_down_bf16, x1, mod2, 3, tm=512, tn=512, name="ffn_down")
    out = _ln_affine(r2, ln2_g, ln2_b)
    return out[None]
```
